```python
import math
import jax
import jax.numpy as jnp
from jax import lax
import numpy as np

D_MODEL = 1024
BATCH = 4
SEQ = 4096
DEPTH = 4

GRID_W = 64
CTX_LEN = 256
EPS = 1e-6
N_BRANCH = 3
BRANCH_W = D_MODEL // 2
N_MOD = 9
D_FF = ((8 * D_MODEL // 3 + 255) // 256) * 256
CONV_W = 4
CONV_LEFT = 2
ML_HEADS = 4
ML_DV = BRANCH_W // ML_HEADS
ML_DK = ML_DV // 2
ML_QK = ML_HEADS * ML_DK
ML_V = BRANCH_W
ML_CHUNK = 64
LRU_WIDTH = BRANCH_W
LRU_BLOCKS = 8
LRU_BLOCK = LRU_WIDTH // LRU_BLOCKS
LRU_C = 8.0
DN_HEADS = 4
DN_DV = BRANCH_W // DN_HEADS
DN_DK = DN_DV
DN_K = DN_HEADS * DN_DK
DN_V = BRANCH_W
DN_CHUNK = 64
IN_WIDTHS = (ML_QK, ML_QK, ML_V, ML_V, 4 * ML_HEADS,
             LRU_WIDTH, LRU_WIDTH,
             DN_K, DN_K, DN_V, DN_V, 4 * DN_HEADS,
             N_BRANCH * D_MODEL)
D_IN = sum(IN_WIDTHS)

kernel_name = "hybrid_mlstm_rglru_gdn_prefix_dit"


def rmsnorm(x, g):
    x32 = x.astype(jnp.float32)
    y = x32 * lax.rsqrt(jnp.mean(x32 * x32, axis=-1, keepdims=True) + EPS)
    return (y * g.astype(jnp.float32)).astype(x.dtype)


def l2norm(x):
    x32 = x.astype(jnp.float32)
    return (x32 * lax.rsqrt(jnp.sum(x32 * x32, axis=-1, keepdims=True) + EPS)).astype(x.dtype)


def modulate(x, g, shift, scale):
    return rmsnorm(x, g) * (1 + scale) + shift


def centred_conv(x, w):
    L = x.shape[1]
    xp = jnp.pad(x, ((0, 0), (CONV_LEFT, CONV_W - 1 - CONV_LEFT), (0, 0)))
    return sum(xp[:, j:j + L] * w[j] for j in range(CONV_W))


def to_chunks(a, T):
    B, L = a.shape[:2]
    return jnp.moveaxis(a.reshape((B, L // T, T) + a.shape[2:]), 1, 0)


def from_chunks(a):
    nc, B, T = a.shape[:3]
    return jnp.moveaxis(a, 0, 1).reshape((B, nc * T) + a.shape[3:])


def to_colmajor(a):
    B, S = a.shape[:2]
    rows = S // GRID_W
    a = a.reshape((B, rows, GRID_W) + a.shape[2:])
    return jnp.swapaxes(a, 1, 2).reshape((B, S) + a.shape[3:])


def from_colmajor(a):
    B, S = a.shape[:2]
    rows = S // GRID_W
    a = a.reshape((B, GRID_W, rows) + a.shape[2:])
    return jnp.swapaxes(a, 1, 2).reshape((B, S) + a.shape[3:])


def bidir_prefix(scan_fn, ctx_f, ctx_b, lat_f, lat_b, init):
    flip = lambda t: tuple(jnp.flip(a, axis=1) for a in t)
    st_f, yc_f = scan_fn(ctx_f, init)
    _, yl_f = scan_fn(lat_f, st_f)
    st_b, yc_b = scan_fn(flip(ctx_b), init)
    _, yl_b = scan_fn(flip(lat_b), st_b)
    return yc_f + jnp.flip(yc_b, axis=1), yl_f + jnp.flip(yl_b, axis=1)


def mlstm_scan(inp, state):
    dtype = inp[0].dtype
    q, k, v, ig, fg = (a.astype(jnp.float32) for a in inp)
    T = ML_CHUNK
    causal = jnp.tril(jnp.ones((T, T), dtype=bool))

    def step(carry, blk):
        C, n, m = carry
        qc, kc, vc, ic, fc = blk
        b = jnp.cumsum(jax.nn.log_sigmoid(fc), axis=1)
        logD = b[:, :, None, :] - b[:, None, :, :] + ic[:, None, :, :]
        logD = jnp.where(causal[None, :, :, None], logD, -jnp.inf)
        inter = b + m[:, None, :]
        m_t = jnp.maximum(inter, jnp.max(logD, axis=2))
        w_in = jnp.exp(inter - m_t)
        s = jnp.einsum('bthd,bshd->btsh', qc, kc) * jnp.exp(logD - m_t[:, :, None, :])
        num = jnp.einsum('btsh,bshv->bthv', s, vc) + w_in[..., None] * jnp.einsum('bthd,bhdv->bthv', qc, C)
        den = jnp.sum(s, axis=2) + w_in * jnp.einsum('bthd,bhd->bth', qc, n)
        h = num / jnp.maximum(jnp.abs(den), jnp.exp(-m_t))[..., None]
        bT = b[:, -1]
        log_ws = bT[:, None, :] - b + ic
        m_new = jnp.maximum(bT + m, jnp.max(log_ws, axis=1))
        ws = jnp.exp(log_ws - m_new[:, None, :])
        dec = jnp.exp(bT + m - m_new)
        C = dec[..., None, None] * C + jnp.einsum('bth,bthd,bthv->bhdv', ws, kc, vc)
        n = dec[..., None] * n + jnp.einsum('bth,bthd->bhd', ws, kc)
        return (C, n, m_new), h

    state, h = lax.scan(step, state, tuple(to_chunks(a, T) for a in (q, k, v, ig, fg)))
    return state, from_chunks(h).astype(dtype)


def _mlstm_prep(q, k, v, gt, gate_b):
    B, L, _ = q.shape
    q = q.reshape(B, L, ML_HEADS, ML_DK) * (ML_DK ** -0.5)
    k = k.reshape(B, L, ML_HEADS, ML_DK)
    v = v.reshape(B, L, ML_HEADS, ML_DV)
    gt = gt.reshape(B, L, 4, ML_HEADS) + gate_b
    return (q, k, v, gt[:, :, 0], gt[:, :, 2]), (q, k, v, gt[:, :, 1], gt[:, :, 3])


def _mlstm_out(h, o, norm_g):
    B, L = h.shape[:2]
    h = rmsnorm(h, norm_g.reshape(ML_HEADS, ML_DV)).reshape(B, L, ML_V)
    return h * jax.nn.sigmoid(o)


def mlstm_branch(pc, pl, gate_b, norm_g, with_ctx_out):
    fc, bc = _mlstm_prep(pc[0], pc[1], pc[2], pc[4], gate_b)
    fl, bl = _mlstm_prep(pl[0], pl[1], pl[2], pl[4], gate_b)
    B = pl[0].shape[0]
    init = (jnp.zeros((B, ML_HEADS, ML_DK, ML_DV), jnp.float32),
            jnp.zeros((B, ML_HEADS, ML_DK), jnp.float32),
            jnp.zeros((B, ML_HEADS), jnp.float32))
    hc, hl = bidir_prefix(mlstm_scan, fc, bc, fl, bl, init)
    yc = _mlstm_out(hc, pc[3], norm_g) if with_ctx_out else None
    return yc, _mlstm_out(hl, pl[3], norm_g)


def blockdiag(x, w):
    B, L, C = x.shape
    return jnp.einsum('blni,nij->blnj', x.reshape(B, L, LRU_BLOCKS, LRU_BLOCK), w).reshape(B, L, C)


def lru_scan(inp, h0):
    dtype = inp[1].dtype
    log_a, bx = (a.astype(jnp.float32) for a in inp)
    a = jnp.exp(log_a)
    bx = bx.at[:, 0].add(a[:, 0] * h0)
    comb = lambda l, r: (l[0] * r[0], r[0] * l[1] + r[1])
    _, h = lax.associative_scan(comb, (a, bx), axis=1)
    return h[:, -1], h.astype(dtype)


def _lru_prep(xb, conv_w, conv_b, w_a, b_a, w_x, b_x, lam):
    xc = centred_conv(xb, conv_w) + conv_b

    def direction(d):
        r = jax.nn.sigmoid(blockdiag(xc, w_a[d]) + b_a[d])
        i = jax.nn.sigmoid(blockdiag(xc, w_x[d]) + b_x[d])
        log_a = -LRU_C * jax.nn.softplus(-lam[d]) * r
        return (log_a, jnp.sqrt(-jnp.expm1(2 * log_a)) * (i * xc))

    return direction(0), direction(1)


def lru_branch(pc, pl, conv_w, conv_b, w_a, b_a, w_x, b_x, lam, with_ctx_out):
    fc, bc = _lru_prep(pc[0], conv_w, conv_b, w_a, b_a, w_x, b_x, lam)
    fl, bl = _lru_prep(pl[0], conv_w, conv_b, w_a, b_a, w_x, b_x, lam)
    init = jnp.zeros((pl[0].shape[0], LRU_WIDTH), jnp.float32)
    hc, hl = bidir_prefix(lru_scan, fc, bc, fl, bl, init)
    yc = hc * jax.nn.gelu(pc[1]) if with_ctx_out else None
    return yc, hl * jax.nn.gelu(pl[1])


def gdn_scan(inp, S0):
    dtype = inp[0].dtype
    q, k, v, g, beta = (a.astype(jnp.float32) for a in inp)
    T = DN_CHUNK
    incl = jnp.tril(jnp.ones((T, T), dtype=bool))
    strict = jnp.tril(jnp.ones((T, T), dtype=bool), -1)
    eye = jnp.eye(T, dtype=jnp.float32)

    def step(S, blk):
        qc, kc, vc, gc, bc = blk
        qc, kc, vc = (jnp.swapaxes(a, 1, 2) for a in (qc, kc, vc))
        gc, bc = jnp.swapaxes(gc, 1, 2), jnp.swapaxes(bc, 1, 2)
        G = jnp.cumsum(gc, axis=-1)
        diff = G[..., :, None] - G[..., None, :]
        gam = jnp.where(incl, jnp.exp(jnp.where(incl, diff, 0.0)), 0.0)
        A = jnp.where(strict, bc[..., :, None] * jnp.einsum('bhtd,bhsd->bhts', kc, kc) * gam, 0.0)
        rhs = jnp.concatenate([bc[..., None] * vc, bc[..., None] * kc * jnp.exp(G)[..., None]], axis=-1)
        sol = lax.linalg.triangular_solve(eye + A, rhs, left_side=True, lower=True, unit_diagonal=True)
        u, w = sol[..., :DN_DV], sol[..., DN_DV:]
        vnew = u - jnp.einsum('bhtk,bhkv->bhtv', w, S)
        o = (jnp.exp(G)[..., None] * jnp.einsum('bhtk,bhkv->bhtv', qc, S)
             + jnp.einsum('bhts,bhsv->bhtv', jnp.einsum('bhtk,bhsk->bhts', qc, kc) * gam, vnew))
        GT = G[..., -1]
        S = (jnp.exp(GT)[..., None, None] * S
             + jnp.einsum('bhs,bhsk,bhsv->bhkv', jnp.exp(GT[..., None] - G), kc, vnew))
        return S, jnp.swapaxes(o, 1, 2)

    S, o = lax.scan(step, S0, tuple(to_chunks(a, T) for a in (q, k, v, g, beta)))
    return S, from_chunks(o).astype(dtype)


def _gdn_prep(q, k, v, ba, conv_w, a_log, dt_bias):
    B, L, _ = q.shape
    qkv = jax.nn.silu(centred_conv(jnp.concatenate([q, k, v], axis=-1), conv_w))
    q, k, v = jnp.split(qkv, [DN_K, 2 * DN_K], axis=-1)
    q = l2norm(q.reshape(B, L, DN_HEADS, DN_DK)) * (DN_DK ** -0.5)
    k = l2norm(k.reshape(B, L, DN_HEADS, DN_DK))
    v = v.reshape(B, L, DN_HEADS, DN_DV)
    ba = ba.reshape(B, L, 4, DN_HEADS).astype(jnp.float32)
    beta = jax.nn.sigmoid(ba[:, :, 0:2])
    g = -jnp.exp(a_log.astype(jnp.float32)) * jax.nn.softplus(ba[:, :, 2:4] + dt_bias.astype(jnp.float32))
    return (q, k, v, g[:, :, 0], beta[:, :, 0]), (q, k, v, g[:, :, 1], beta[:, :, 1])


def _gdn_out(o, z, norm_g):
    B, L = o.shape[:2]
    z = z.reshape(B, L, DN_HEADS, DN_DV)
    return (rmsnorm(o, norm_g) * jax.nn.silu(z)).reshape(B, L, DN_V)


def dn_branch(pc, pl, conv_w, a_log, dt_bias, norm_g, with_ctx_out):
    fc, bc = _gdn_prep(pc[0], pc[1], pc[2], pc[4], conv_w, a_log, dt_bias)
    fl, bl = _gdn_prep(to_colmajor(pl[0]), to_colmajor(pl[1]), to_colmajor(pl[2]), to_colmajor(pl[4]),
                       conv_w, a_log, dt_bias)
    init = jnp.zeros((pl[0].shape[0], DN_HEADS, DN_DK, DN_DV), jnp.float32)
    oc, ol = bidir_prefix(gdn_scan, fc, bc, fl, bl, init)
    yc = _gdn_out(oc, pc[3], norm_g) if with_ctx_out else None
    return yc, _gdn_out(from_colmajor(ol), pl[3], norm_g)


def _split_in(p):
    idx = np.cumsum(IN_WIDTHS)[:-1].tolist()
    return jnp.split(p, idx, axis=-1)


def merge(ys, gate_cols, w_branch, w_out):
    B, L, _ = gate_cols.shape
    gates = jax.nn.sigmoid(gate_cols.reshape(B, L, N_BRANCH, D_MODEL))
    proj = jnp.einsum('blnc,ncd->blnd', jnp.stack(ys, axis=2), w_branch)
    return jnp.sum(gates * proj, axis=2) @ w_out


def mixer(hc, hl, w_in, ml_gate_b, ml_norm_g, lru_conv_w, lru_conv_b, lru_w_a, lru_b_a, lru_w_x, lru_b_x,
          lru_lambda, dn_conv_w, dn_a_log, dn_dt_bias, dn_norm_g, w_branch, w_out, with_ctx_out):
    sc = _split_in(hc @ w_in)
    sl = _split_in(hl @ w_in)
    ml_c, ml_l = mlstm_branch(sc[0:5], sl[0:5], ml_gate_b, ml_norm_g, with_ctx_out)
    lr_c, lr_l = lru_branch(sc[5:7], sl[5:7], lru_conv_w, lru_conv_b, lru_w_a, lru_b_a, lru_w_x, lru_b_x,
                            lru_lambda, with_ctx_out)
    dn_c, dn_l = dn_branch(sc[7:12], sl[7:12], dn_conv_w, dn_a_log, dn_dt_bias, dn_norm_g, with_ctx_out)
    yl = merge((ml_l, lr_l, dn_l), sl[12], w_branch, w_out)
    yc = merge((ml_c, lr_c, dn_c), sc[12], w_branch, w_out) if with_ctx_out else None
    return yc, yl


def ffn_sub(x, mod, j, g_pre, g_post, w_gu, w_down):
    h = modulate(x, g_pre, mod[:, 3 * j], mod[:, 3 * j + 1])
    gu = h @ w_gu
    y = (jax.nn.silu(gu[..., :D_FF]) * gu[..., D_FF:]) @ w_down
    return x + 0.5 * mod[:, 3 * j + 2] * rmsnorm(y, g_post)


def setup_inputs(seed: int = 0) -> dict:
    key = jax.random.key(seed)
    ks = jax.random.split(key, 32)
    f32 = jnp.float32
    nrm = lambda i, shape, s: jax.random.normal(ks[i], shape, f32) * s
    x = nrm(0, (BATCH, SEQ, D_MODEL), 1.0)
    c = nrm(1, (BATCH, D_MODEL), 1.0)
    ctx = nrm(2, (BATCH, CTX_LEN, D_MODEL), 1.0)
    c_ctx = nrm(3, (D_MODEL,), 1.0)
    w_mod = nrm(4, (DEPTH, D_MODEL, N_MOD * D_MODEL), D_MODEL ** -0.5)
    b_mod = nrm(5, (DEPTH, N_MOD * D_MODEL), 0.02)
    norm_g = 1.0 + nrm(6, (DEPTH, 6, D_MODEL), 0.02)
    ffn_w_gu = nrm(7, (DEPTH, 2, D_MODEL, 2 * D_FF), D_MODEL ** -0.5)
    ffn_w_down = nrm(8, (DEPTH, 2, D_FF, D_MODEL), D_FF ** -0.5)
    w_in = nrm(9, (DEPTH, D_MODEL, D_IN), D_MODEL ** -0.5)
    ig_b = -2.0 + nrm(10, (DEPTH, 2, ML_HEADS), 0.1)
    fg_b = jnp.linspace(3.0, 6.0, ML_HEADS, dtype=f32) + nrm(11, (DEPTH, 2, ML_HEADS), 0.1)
    ml_gate_b = jnp.concatenate([ig_b, fg_b], axis=1)
    ml_norm_g = 1.0 + nrm(12, (DEPTH, ML_V), 0.02)
    lru_conv_w = nrm(13, (DEPTH, CONV_W, LRU_WIDTH), CONV_W ** -0.5)
    lru_conv_b = nrm(14, (DEPTH, LRU_WIDTH), 0.02)
    lru_w_a = nrm(15, (DEPTH, 2, LRU_BLOCKS, LRU_BLOCK, LRU_BLOCK), LRU_BLOCK ** -0.5)
    lru_b_a = nrm(16, (DEPTH, 2, LRU_WIDTH), 0.02)
    lru_w_x = nrm(17, (DEPTH, 2, LRU_BLOCKS, LRU_BLOCK, LRU_BLOCK), LRU_BLOCK ** -0.5)
    lru_b_x = nrm(18, (DEPTH, 2, LRU_WIDTH), 0.02)
    u = jax.random.uniform(ks[19], (DEPTH, 2, LRU_WIDTH), f32, 0.9, 0.999)
    sig = u ** (1.0 / LRU_C)
    lru_lambda = jnp.log(sig) - jnp.log1p(-sig)
    dn_conv_w = nrm(20, (DEPTH, CONV_W, 2 * DN_K + DN_V), CONV_W ** -0.5)
    dn_a_log = jnp.log(jax.random.uniform(ks[21], (DEPTH, 2, DN_HEADS), f32, 1.0, 16.0))
    dt = jnp.exp(jax.random.uniform(ks[22], (DEPTH, 2, DN_HEADS), f32, math.log(1e-3), math.log(0.1)))
    dn_dt_bias = dt + jnp.log(-jnp.expm1(-dt))
    dn_norm_g = 1.0 + nrm(23, (DEPTH, DN_DV), 0.02)
    w_branch = nrm(24, (DEPTH, N_BRANCH, BRANCH_W, D_MODEL), BRANCH_W ** -0.5)
    w_out = nrm(25, (DEPTH, D_MODEL, D_MODEL), D_MODEL ** -0.5)
    return {"x": x, "c": c, "ctx": ctx, "c_ctx": c_ctx, "w_mod": w_mod, "b_mod": b_mod,
            "norm_g": norm_g, "ffn_w_gu": ffn_w_gu, "ffn_w_down": ffn_w_down, "w_in": w_in,
            "ml_gate_b": ml_gate_b, "ml_norm_g": ml_norm_g, "lru_conv_w": lru_conv_w,
            "lru_conv_b": lru_conv_b, "lru_w_a": lru_w_a, "lru_b_a": lru_b_a, "lru_w_x": lru_w_x,
            "lru_b_x": lru_b_x, "lru_lambda": lru_lambda, "dn_conv_w": dn_conv_w,
            "dn_a_log": dn_a_log, "dn_dt_bias": dn_dt_bias, "dn_norm_g": dn_norm_g,
            "w_branch": w_branch, "w_out": w_out}


def reference(x, c, ctx, c_ctx, w_mod, b_mod, norm_g, ffn_w_gu, ffn_w_down, w_in, ml_gate_b, ml_norm_g,
              lru_conv_w, lru_conv_b, lru_w_a, lru_b_a, lru_w_x, lru_b_x, lru_lambda, dn_conv_w,
              dn_a_log, dn_dt_bias, dn_norm_g, w_branch, w_out):
    B = x.shape[0]
    z = ctx
    for l in range(DEPTH):
        last = l == DEPTH - 1
        mod_l = (jax.nn.silu(c) @ w_mod[l] + b_mod[l]).reshape(B, N_MOD, 1, D_MODEL)
        mod_c = (jax.nn.silu(c_ctx) @ w_mod[l] + b_mod[l]).reshape(1, N_MOD, 1, D_MODEL)
        x = ffn_sub(x, mod_l, 0, norm_g[l, 0], norm_g[l, 1], ffn_w_gu[l, 0], ffn_w_down[l, 0])
        z = ffn_sub(z, mod_c, 0, norm_g[l, 0], norm_g[l, 1], ffn_w_gu[l, 0], ffn_w_down[l, 0])
        hl = modulate(x, norm_g[l, 2], mod_l[:, 3], mod_l[:, 4])
        hc = modulate(z, norm_g[l, 2], mod_c[:, 3], mod_c[:, 4])
        yc, yl = mixer(hc, hl, w_in[l], ml_gate_b[l], ml_norm_g[l], lru_conv_w[l], lru_conv_b[l],
                       lru_w_a[l], lru_b_a[l], lru_w_x[l], lru_b_x[l], lru_lambda[l], dn_conv_w[l],
                       dn_a_log[l], dn_dt_bias[l], dn_norm_g[l], w_branch[l], w_out[l], not last)
        x = x + mod_l[:, 5] * rmsnorm(yl, norm_g[l, 3])
        x = ffn_sub(x, mod_l, 2, norm_g[l, 4], norm_g[l, 5], ffn_w_gu[l, 1], ffn_w_down[l, 1])
        if not last:
            z = z + mod_c[:, 5] * rmsnorm(yc, norm_g[l, 3])
            z = ffn_sub(z, mod_c, 2, norm_g[l, 4], norm_g[l, 5], ffn_w_gu[l, 1], ffn_w_down[l, 1])
    return x
```

```python
import functools
import math

import jax
import jax.numpy as jnp
from jax import lax
from jax.experimental import pallas as pl
from jax.experimental.pallas import tpu as pltpu

F32 = jnp.float32
BF16 = jnp.bfloat16

EPS = 1e-6
N_MOD = 9
N_BRANCH = 3
CONV_W = 4
CONV_LEFT = 2
ML_HEADS = 4
LRU_C = 8.0
DN_HEADS = 4
CHUNK = 64
DN_GROUP = 4
N_GATES = 16

V7X_SUBLANES = 8
V7X_LANES = 128
V7X_VMEM_BYTES = 64 * 1024 * 1024
VMEM_LIMIT = V7X_VMEM_BYTES - 8 * 1024 * 1024

TM = 256
NEG = -1e30


def _sigmoid(x):
    return 1.0 / (1.0 + jnp.exp(-x))


def _silu(x):
    return x * _sigmoid(x)


def _softplus(x):
    return jnp.maximum(x, 0.0) + jnp.log(1.0 + jnp.exp(-jnp.abs(x)))


def _rms(x):
    return x * lax.rsqrt(jnp.mean(x * x, axis=-1, keepdims=True) + EPS)


def _rms_heads(x, n_heads):
    hd = x.shape[-1] // n_heads
    return jnp.concatenate([_rms(x[:, h * hd:(h + 1) * hd]) for h in range(n_heads)], axis=-1)


def _gelu_tanh(x):
    return 0.5 * x * (1.0 + jnp.tanh(math.sqrt(2.0 / math.pi) * (x + 0.044715 * (x * x * x))))


def _dot(a, b):
    return jnp.dot(a, b, preferred_element_type=F32)


def _dot_nt(a, b):
    return lax.dot_general(a, b, (((1,), (1,)), ((), ())), preferred_element_type=F32)


def _dot_tn(a, b):
    return lax.dot_general(a, b, (((0,), (0,)), ((), ())), preferred_element_type=F32)


def _params(*sem):
    return pltpu.CompilerParams(dimension_semantics=sem, vmem_limit_bytes=VMEM_LIMIT)


def _const_spec(block, index):
    return pl.BlockSpec(block, lambda *_: index, pipeline_mode=pl.Buffered(1))


def _stack_heads(x, n):
    w = x.shape[1] // n
    return jnp.concatenate([x[:, h * w:(h + 1) * w] for h in range(n)], axis=0)


def _unstack_heads(x, n):
    t = x.shape[0] // n
    return jnp.concatenate([x[h * t:(h + 1) * t] for h in range(n)], axis=1)


def _cumsum_groups(x, axis, period, rev):
    n = x.shape[axis]
    idx = lax.broadcasted_iota(jnp.int32, x.shape, axis) % period
    sh = 1
    while sh < period:
        if rev:
            x = x + jnp.where(idx < period - sh, pltpu.roll(x, n - sh, axis=axis), 0.0)
        else:
            x = x + jnp.where(idx >= sh, pltpu.roll(x, sh, axis=axis), 0.0)
        sh *= 2
    return x


def _shift_rows(x, prev8, nxt8, off):
    t = x.shape[0]
    row8 = lax.broadcasted_iota(jnp.int32, (V7X_SUBLANES, 1), 0)
    rolled = pltpu.roll(x, (-off) % t, axis=0)
    if off < 0:
        edge = jnp.where(row8 < -off, pltpu.roll(prev8, -off, axis=0), rolled[:V7X_SUBLANES])
        return jnp.concatenate([edge, rolled[V7X_SUBLANES:]], axis=0)
    edge = jnp.where(row8 >= V7X_SUBLANES - off, pltpu.roll(nxt8, V7X_SUBLANES - off, axis=0),
                     rolled[t - V7X_SUBLANES:])
    return jnp.concatenate([rolled[:t - V7X_SUBLANES], edge], axis=0)


def _conv4(x, prev8, nxt8, w):
    acc = x * w[CONV_LEFT:CONV_LEFT + 1]
    for j in range(CONV_W):
        if j != CONV_LEFT:
            acc = acc + _shift_rows(x, prev8, nxt8, j - CONV_LEFT) * w[j:j + 1]
    return acc


def _mod_kernel(c_ref, w_ref, b_ref, o_ref):
    s = _silu(c_ref[...]).astype(BF16)
    o_ref[...] = _dot(s, w_ref[...].astype(BF16)) + b_ref[...]


def _mod_table(cc, w_mod, b_mod):
    depth, d, nd = w_mod.shape
    r = cc.shape[0]
    tn = nd // 4
    return pl.pallas_call(
        _mod_kernel,
        grid=(depth, nd // tn),
        in_specs=[pl.BlockSpec((r, d), lambda l, j: (0, 0)),
                  pl.BlockSpec((None, d, tn), lambda l, j: (l, 0, j)),
                  pl.BlockSpec((None, 1, tn), lambda l, j: (l, 0, j))],
        out_specs=pl.BlockSpec((None, r, tn), lambda l, j: (l, 0, j)),
        out_shape=jax.ShapeDtypeStruct((depth, r, nd), F32),
        compiler_params=_params("parallel", "parallel"),
        name="mod_table",
    )(cc, w_mod, b_mod.reshape(depth, 1, nd))


def _tok_spec(width, col=0):
    return pl.BlockSpec((None, TM, width), lambda bi, t: (bi, t, col))


def _mod_spec(l, d, n_lat_tiles, ctx_row):
    return pl.BlockSpec((None, None, N_MOD, d),
                        lambda bi, t: (l, jnp.where(t >= n_lat_tiles, ctx_row, bi), 0, 0))


def _ffn_kernel(x_ref, mod_ref, g_ref, wgu_ref, wd_ref, o_ref, *, j, dff):
    x = x_ref[...]
    shift, scale, gate = (mod_ref[3 * j + i:3 * j + i + 1, :] for i in range(3))
    g_pre, g_post = g_ref[2 * j:2 * j + 1, :], g_ref[2 * j + 1:2 * j + 2, :]
    h = (_rms(x) * g_pre * (1.0 + scale) + shift).astype(BF16)
    gu = _dot(h, wgu_ref[...])
    a = (_silu(gu[:, :dff]) * gu[:, dff:]).astype(BF16)
    y = _dot(a, wd_ref[...])
    o_ref[...] = x + 0.5 * gate * (_rms(y) * g_post)


def _ffn(xz, mod, norm_g, wgu, wd, l, j, n_lat_tiles, ctx_row, n_out_tiles):
    b, _, d = xz.shape
    dff = wd.shape[2]
    return pl.pallas_call(
        functools.partial(_ffn_kernel, j=j, dff=dff),
        grid=(b, n_out_tiles),
        in_specs=[_tok_spec(d),
                  _mod_spec(l, d, n_lat_tiles, ctx_row),
                  _const_spec((None,) + norm_g.shape[1:], (l, 0, 0)),
                  _const_spec((None, None, d, 2 * dff), (l, j // 2, 0, 0)),
                  _const_spec((None, None, dff, d), (l, j // 2, 0, 0))],
        out_specs=_tok_spec(d),
        out_shape=jax.ShapeDtypeStruct((b, n_out_tiles * TM, d), F32),
        compiler_params=_params("parallel", "parallel"),
        name=f"ffn{j}",
    )(xz, mod, norm_g, wgu, wd)


def _inproj_kernel(x_ref, mod_ref, g_ref, w_ref, wg_ref, *o_refs):
    shift, scale = mod_ref[3:4, :], mod_ref[4:5, :]
    h = (_rms(x_ref[...]) * g_ref[2:3, :] * (1.0 + scale) + shift).astype(BF16)
    col = 0
    for o_ref in o_refs[:-1]:
        n = o_ref.shape[-1]
        o_ref[...] = _dot(h, w_ref[:, col:col + n]).astype(o_ref.dtype)
        col += n
    o_refs[-1][...] = _dot(h, wg_ref[...])


def _inproj(xz, mod, norm_g, w_main, w_gate, widths, l, n_lat_tiles, ctx_row):
    b, lt, d = xz.shape
    ng = w_gate.shape[-1]
    return pl.pallas_call(
        _inproj_kernel,
        grid=(b, lt // TM),
        in_specs=[_tok_spec(d),
                  _mod_spec(l, d, n_lat_tiles, ctx_row),
                  _const_spec((None,) + norm_g.shape[1:], (l, 0, 0)),
                  _const_spec((None, d, w_main.shape[-1]), (l, 0, 0)),
                  _const_spec((None, d, ng), (l, 0, 0))],
        out_specs=[_tok_spec(w) for w in widths] + [_tok_spec(ng)],
        out_shape=[jax.ShapeDtypeStruct((b, lt, w), BF16) for w in widths]
                  + [jax.ShapeDtypeStruct((b, lt, ng), F32)],
        compiler_params=_params("parallel", "parallel"),
        name="inproj",
    )(xz, mod, norm_g, w_main, w_gate)


def _merge_kernel(x_ref, mod_ref, g_ref, mlf_ref, mlb_ref, mlo_ref, lrf_ref, lrb_ref, lry_ref,
                  dlf_ref, dlb_ref, dcf_ref, dcb_ref, dnz_ref, mg_ref, mlg_ref, dng_ref,
                  wb_ref, wo_ref, o_ref, *, n_lat_tiles):
    d = x_ref.shape[-1]
    f32 = lambda r: r[...].astype(F32)
    is_ctx = pl.program_id(1) >= n_lat_tiles
    dn_h = jnp.where(is_ctx, f32(dcf_ref) + f32(dcb_ref), f32(dlf_ref) + f32(dlb_ref))
    y_ml = _rms_heads(f32(mlf_ref) + f32(mlb_ref), ML_HEADS) * mlg_ref[...] * _sigmoid(f32(mlo_ref))
    y_lr = (f32(lrf_ref) + f32(lrb_ref)) * _gelu_tanh(f32(lry_ref))
    y_dn = _rms_heads(dn_h, DN_HEADS) * dng_ref[...] * _silu(f32(dnz_ref))
    mix = None
    for n, y in enumerate((y_ml, y_lr, y_dn)):
        term = _sigmoid(mg_ref[:, n * d:(n + 1) * d].astype(F32)) * _dot(y.astype(BF16), wb_ref[n])
        mix = term if mix is None else mix + term
    out = _dot(mix.astype(BF16), wo_ref[...])
    o_ref[...] = x_ref[...] + mod_ref[5:6, :] * (_rms(out) * g_ref[3:4, :])


def _merge(xz, mod, norm_g, ml_hf, ml_hb, ml, lr_hf, lr_hb, lru, dn_lf, dn_lb, dn_cf, dn_cb, dnz, mg,
           ml_g, dn_g, w_branch, w_out, l, n_lat_tiles, ctx_row):
    b, lt, d = xz.shape
    bw = w_branch.shape[2]
    n_ctx_tiles = lt // TM - n_lat_tiles
    lat_spec = pl.BlockSpec((None, TM, bw), lambda bi, t: (bi, jnp.minimum(t, n_lat_tiles - 1), 0))
    ctx_spec = pl.BlockSpec((None, TM, bw),
                            lambda bi, t: (bi, jnp.clip(t - n_lat_tiles, 0, n_ctx_tiles - 1), 0))
    return pl.pallas_call(
        functools.partial(_merge_kernel, n_lat_tiles=n_lat_tiles),
        grid=(b, lt // TM),
        in_specs=[_tok_spec(d),
                  _mod_spec(l, d, n_lat_tiles, ctx_row),
                  _const_spec((None,) + norm_g.shape[1:], (l, 0, 0)),
                  _tok_spec(bw), _tok_spec(bw), _tok_spec(bw, ml.shape[-1] // bw - 1),
                  _tok_spec(bw), _tok_spec(bw), _tok_spec(bw, lru.shape[-1] // bw - 1),
                  lat_spec, lat_spec, ctx_spec, ctx_spec, _tok_spec(bw),
                  _tok_spec(N_BRANCH * d),
                  _const_spec((None, 1, bw), (l, 0, 0)),
                  _const_spec((None, 1, bw), (l, 0, 0)),
                  _const_spec((None, N_BRANCH, bw, d), (l, 0, 0, 0)),
                  _const_spec((None, d, d), (l, 0, 0))],
        out_specs=_tok_spec(d),
        out_shape=jax.ShapeDtypeStruct(xz.shape, F32),
        compiler_params=_params("parallel", "parallel"),
        name="merge",
    )(xz, mod, norm_g, ml_hf, ml_hb, ml, lr_hf, lr_hb, lru, dn_lf, dn_lb, dn_cf, dn_cb, dnz, mg,
      ml_g, dn_g, w_branch, w_out)


def _log_scan(a, b, idx, period, rev):
    n = a.shape[0]
    sh = 1
    while sh < period:
        if rev:
            ok = idx < period - sh
            ar, br = pltpu.roll(a, n - sh, axis=0), pltpu.roll(b, n - sh, axis=0)
        else:
            ok = idx >= sh
            ar, br = pltpu.roll(a, sh, axis=0), pltpu.roll(b, sh, axis=0)
        b = b + a * jnp.where(ok, br, 0.0)
        a = a * jnp.where(ok, ar, 1.0)
        sh *= 2
    return a, b


def _tile_scan(a, b, h0, rev, sa_ref, sb_ref):
    t = a.shape[0]
    g = t // V7X_SUBLANES
    row = lax.broadcasted_iota(jnp.int32, (t, 1), 0)
    a, b = _log_scan(a, b, row % V7X_SUBLANES, V7X_SUBLANES, rev)
    edge = 0 if rev else V7X_SUBLANES - 1

    def group_edges(ref, val):
        parts = []
        for j in range(ref.shape[0]):
            ref[j] = val[:, j * V7X_LANES:(j + 1) * V7X_LANES]
            parts.append(ref[j, pl.ds(edge, g, stride=V7X_SUBLANES), :])
        return jnp.concatenate(parts, axis=1)

    ga = group_edges(sa_ref, a)
    gb = group_edges(sb_ref, b)
    grow = lax.broadcasted_iota(jnp.int32, (g, 1), 0)
    ga, gb = _log_scan(ga, gb, grow, g, rev)
    h_end = gb + ga * h0
    if rev:
        h_in = jnp.where(grow == g - 1, h0, pltpu.roll(h_end, g - 1, axis=0))
        carry = h_end[0:1]
    else:
        h_in = jnp.where(grow == 0, h0, pltpu.roll(h_end, 1, axis=0))
        carry = h_end[g - 1:g]
    out = jnp.concatenate(
        [b[V7X_SUBLANES * i:V7X_SUBLANES * (i + 1)] + a[V7X_SUBLANES * i:V7X_SUBLANES * (i + 1)] * h_in[i:i + 1]
         for i in range(g)], axis=0)
    return out, carry


def _lru_kernel(x_ref, cw_ref, cb_ref, w_ref, bias_ref, lam_ref, of_ref, ob_ref, *scratch, n_lat, n_ctx):
    t, c = TM, x_ref.shape[-1]
    n_tiles = n_lat + n_ctx
    halo = 2 * V7X_SUBLANES

    def load_conv(tile):
        r0 = pl.multiple_of(tile * t, t)
        first = jnp.logical_or(tile == 0, tile == n_lat)
        last = jnp.logical_or(tile == n_lat - 1, tile == n_tiles - 1)
        x = x_ref[pl.ds(r0, t), :].astype(F32)
        p0 = pl.multiple_of(jnp.maximum(r0 - halo, 0), halo)
        n0 = pl.multiple_of(jnp.minimum(r0 + t, n_tiles * t - halo), halo)
        prev8 = jnp.where(first, 0.0, x_ref[pl.ds(p0, halo), :].astype(F32)[V7X_SUBLANES:])
        nxt8 = jnp.where(last, 0.0, x_ref[pl.ds(n0, halo), :].astype(F32)[:V7X_SUBLANES])
        return r0, _conv4(x, prev8, nxt8, cw_ref[...]) + cb_ref[...]

    def direction(d, tile, h0, o_ref):
        r0, xc = load_conv(tile)
        z = _dot(xc.astype(BF16), w_ref[d]) + bias_ref[d]
        r, i = _sigmoid(z[:, :c]), _sigmoid(z[:, c:])
        la = (-LRU_C * _softplus(-lam_ref[d])) * r
        a = jnp.exp(la)
        bx = jnp.sqrt(jnp.tanh(-la) * (1.0 + a * a)) * (i * xc)
        h, carry = _tile_scan(a, bx, h0, d == 1, scratch[2 * d], scratch[2 * d + 1])
        o_ref[pl.ds(r0, t), :] = h.astype(o_ref.dtype)
        return carry

    def step(s, carry):
        hf, hb = carry
        hf = direction(0, jnp.where(s < n_ctx, n_lat + s, s - n_ctx), hf, of_ref)
        hb = direction(1, n_tiles - 1 - s, hb, ob_ref)
        return hf, hb

    zero = jnp.zeros((1, c), F32)
    lax.fori_loop(0, n_tiles, step, (zero, zero))


def _lru(lru, conv_w, conv_b, w_gates, b_gates, lam, l, n_lat, n_ctx):
    b, lt, _ = lru.shape
    c = conv_w.shape[-1]
    seq = pl.BlockSpec((None, lt, c), lambda bi: (bi, 0, 0))
    return pl.pallas_call(
        functools.partial(_lru_kernel, n_lat=n_lat, n_ctx=n_ctx),
        grid=(b,),
        in_specs=[seq,
                  _const_spec((None, CONV_W, c), (l, 0, 0)),
                  _const_spec((None, 1, c), (l, 0, 0)),
                  _const_spec((None, 2, c, 2 * c), (l, 0, 0, 0)),
                  _const_spec((None, 2, 1, 2 * c), (l, 0, 0, 0)),
                  _const_spec((None, 2, 1, c), (l, 0, 0, 0))],
        out_specs=[seq, seq],
        out_shape=[jax.ShapeDtypeStruct((b, lt, c), BF16)] * 2,
        scratch_shapes=[pltpu.VMEM((c // V7X_LANES, TM, V7X_LANES), F32)] * 4,
        compiler_params=_params("parallel"),
        name="lru",
    )(lru, conv_w, conv_b, w_gates, b_gates, lam)


def _mlstm_kernel(qf_ref, kf_ref, vf_ref, qb_ref, kb_ref, vb_ref, gcf_ref, gcb_ref, grf_ref, grb_ref,
                  bc_ref, br_ref, hf_ref, hb_ref, c_ref, m_ref):
    @pl.when(pl.program_id(1) == 0)
    def _():
        c_ref[...] = jnp.zeros(c_ref.shape, F32)
        m_ref[...] = jnp.zeros(m_ref.shape, F32)

    tc, nqk = qf_ref.shape
    dk = nqk // ML_HEADS
    dv = vf_ref.shape[1] // ML_HEADS
    ri = lax.broadcasted_iota(jnp.int32, (tc, tc), 0)
    ci = lax.broadcasted_iota(jnp.int32, (tc, tc), 1)
    lane_head = lax.broadcasted_iota(jnp.int32, (1, nqk), 1) // dk
    ones = jnp.ones((tc, dv), BF16)
    dirs = ((qf_ref, kf_ref, vf_ref, gcf_ref, grf_ref, hf_ref),
            (qb_ref, kb_ref, vb_ref, gcb_ref, grb_ref, hb_ref))
    for d, (q_ref, k_ref, v_ref, gc_ref, gr_ref, o_ref) in enumerate(dirs):
        rev = d == 1
        causal = (ci >= ri) if rev else (ci <= ri)
        gc = gc_ref[...] + bc_ref[...]
        gr = gr_ref[...] + br_ref[...]
        i_rows = gr[ML_HEADS * d:ML_HEADS * (d + 1)]
        fo = 2 * ML_HEADS + ML_HEADS * d
        b_cols = _cumsum_groups(-_softplus(-gc[:, fo:fo + ML_HEADS]), 0, tc, rev)
        b_rows = _cumsum_groups(-_softplus(-gr[fo:fo + ML_HEADS]), 1, tc, rev)
        q_all = q_ref[...] * (dk ** -0.5)
        k_all = k_ref[...]
        v_all = v_ref[...]
        kt_all = k_all.astype(F32).T
        c_all = c_ref[d]
        cb_all = c_all.astype(BF16)
        outs = []
        for h in range(ML_HEADS):
            r = ML_HEADS * d + h
            qh = jnp.where(lane_head == h, q_all, jnp.zeros_like(q_all))
            vp = jnp.concatenate([v_all[:, h * dv:(h + 1) * dv], ones], axis=1)
            b_c, b_r, i_r = b_cols[:, h:h + 1], b_rows[h:h + 1], i_rows[h:h + 1]
            m_prev = m_ref[r:r + 1, 0:1]
            logd = jnp.where(causal, b_c - b_r + i_r, NEG)
            inter = b_c + m_prev
            m_t = jnp.maximum(inter, jnp.max(logd, axis=1, keepdims=True))
            p = _dot_nt(qh, k_all) * jnp.exp(logd - m_t)
            num = _dot(p.astype(BF16), vp) + jnp.exp(inter - m_t) * _dot(qh, cb_all)
            outs.append(num[:, :dv] / jnp.maximum(jnp.abs(num[:, dv:]), jnp.exp(-m_t)))
            b_end = b_c[0:1] if rev else b_c[tc - 1:tc]
            lws = b_end - b_r + i_r
            m_new = jnp.maximum(b_end + m_prev, jnp.max(lws, axis=1, keepdims=True))
            kw = (kt_all[h * dk:(h + 1) * dk] * jnp.exp(lws - m_new)).astype(BF16)
            c_ref[d, h * dk:(h + 1) * dk, :] = (jnp.exp(b_end + m_prev - m_new) * c_all[h * dk:(h + 1) * dk]
                                                + _dot(kw, vp))
            m_ref[r:r + 1, :] = jnp.broadcast_to(m_new, (1, m_ref.shape[1]))
        o_ref[...] = jnp.concatenate(outs, axis=1).astype(o_ref.dtype)


def _mlstm(ml, gates, gates_t, bias_c, bias_r, l, n_lat, n_ctx):
    b, lt, _ = ml.shape
    ng = gates.shape[-1]
    n_tiles = n_lat + n_ctx
    dqk = ml.shape[-1] // 6
    fwd = lambda s: jnp.where(s < n_ctx, n_lat + s, s - n_ctx)
    bwd = lambda s: n_tiles - 1 - s

    def specs(tile):
        return [pl.BlockSpec((None, TM, dqk), lambda bi, s: (bi, tile(s), 0)),
                pl.BlockSpec((None, TM, dqk), lambda bi, s: (bi, tile(s), 1)),
                pl.BlockSpec((None, TM, 2 * dqk), lambda bi, s: (bi, tile(s), 1))]

    col = lambda tile: pl.BlockSpec((None, TM, ng), lambda bi, s: (bi, tile(s), 0))
    row = lambda tile: pl.BlockSpec((None, ng, TM), lambda bi, s: (bi, 0, tile(s)))
    out = lambda tile: pl.BlockSpec((None, TM, 2 * dqk), lambda bi, s: (bi, tile(s), 0))
    return pl.pallas_call(
        _mlstm_kernel,
        grid=(b, n_tiles),
        in_specs=specs(fwd) + specs(bwd) + [col(fwd), col(bwd), row(fwd), row(bwd),
                                            _const_spec((None, 1, ng), (l, 0, 0)),
                                            _const_spec((None, ng, 1), (l, 0, 0))],
        out_specs=[out(fwd), out(bwd)],
        out_shape=[jax.ShapeDtypeStruct((b, lt, 2 * dqk), BF16)] * 2,
        scratch_shapes=[pltpu.VMEM((2, dqk, 2 * (2 * dqk // ML_HEADS)), F32),
                        pltpu.VMEM((2 * ML_HEADS, V7X_LANES), F32)],
        compiler_params=_params("parallel", "arbitrary"),
        name="mlstm",
    )(ml, ml, ml, ml, ml, ml, gates, gates, gates_t, gates_t, bias_c, bias_r)


def _split2(x):
    hi = x.astype(BF16)
    return hi, (x - hi.astype(F32)).astype(BF16)


def _dot_x3(a, b):
    a_hi, a_lo = _split2(a)
    b_hi, b_lo = _split2(b)
    return _dot(a_hi, b_hi) + (_dot(a_hi, b_lo) + _dot(a_lo, b_hi))


INV_BASE = 8


def _tri_inv(a, ri, ci):
    blk = lambda s: (ri // s) == (ci // s)
    a0 = jnp.where(blk(INV_BASE), a, 0.0)
    x = jnp.where(ri == ci, 1.0, 0.0) - a0
    p = a0
    for _ in range(INV_BASE.bit_length() - 2):
        p = _dot_x3(p, p)
        x = x + _dot_x3(x, p)
    s = INV_BASE
    while s < CHUNK:
        off = jnp.where(jnp.logical_and(blk(2 * s), jnp.logical_not(blk(s))), a, 0.0)
        x = x - _dot_x3(_dot_x3(x, off), x)
        s *= 2
    return x


def _dn_prep_kernel(xl_ref, xp_ref, xn_ref, xc_ref, gcol_ref, grow_ref, cw_ref, acol_ref, arow_ref,
                    u_ref, w_ref, qe_ref, kd_ref, qk_ref, eg_ref, *, n_lat_groups):
    g = pl.program_id(1)
    is_ctx = g >= n_lat_groups
    width = xp_ref.shape[-1]
    dh = width // (3 * DN_HEADS)
    halo = 2 * V7X_SUBLANES
    xl = xl_ref[...]
    x_lat = jnp.concatenate([xl[:, i * width:(i + 1) * width] for i in range(DN_GROUP)], axis=0)
    x = jnp.where(is_ctx, xc_ref[...], x_lat).astype(F32)
    no_prev = jnp.logical_or(is_ctx, g == 0)
    no_next = jnp.logical_or(is_ctx, g == n_lat_groups - 1)
    prev8 = jnp.where(no_prev, 0.0, xp_ref[CHUNK - halo:, :].astype(F32)[V7X_SUBLANES:])
    nxt8 = jnp.where(no_next, 0.0, xn_ref[:halo, :].astype(F32)[:V7X_SUBLANES])
    xc = _silu(_conv4(x, prev8, nxt8, cw_ref[...]))

    n = DN_HEADS * CHUNK
    ri = lax.broadcasted_iota(jnp.int32, (n, n), 0)
    ci = lax.broadcasted_iota(jnp.int32, (n, n), 1)
    same = (ri // CHUNK) == (ci // CHUNK)
    eye = ri == ci
    for i in range(DN_GROUP):
        xi = xc[i * CHUNK:(i + 1) * CHUNK]
        q, k, v = (_stack_heads(xi[:, j * DN_HEADS * dh:(j + 1) * DN_HEADS * dh], DN_HEADS) for j in range(3))
        qn = q * lax.rsqrt(jnp.sum(q * q, axis=-1, keepdims=True) + EPS) * (dh ** -0.5)
        kn = k * lax.rsqrt(jnp.sum(k * k, axis=-1, keepdims=True) + EPS)
        kb = kn.astype(BF16)
        kk = _dot_nt(kb, kb)
        qk = _dot_nt(qn.astype(BF16), kb)
        gcol = gcol_ref[i]
        grow = grow_ref[i]
        for d in range(2):
            rev = d == 1
            beta = _sigmoid(gcol[:, d:d + 1])
            g_c = -jnp.exp(acol_ref[d, :, 0:1]) * _softplus(gcol[:, 2 + d:3 + d] + acol_ref[d, :, 1:2])
            g_r = -jnp.exp(arow_ref[d, 0:1, :]) * _softplus(grow[2 + d:3 + d] + arow_ref[d, 1:2, :])
            cs_c = _cumsum_groups(g_c, 0, CHUNK, rev)
            cs_r = _cumsum_groups(g_r, 1, CHUNK, rev)
            tot_c = cs_c + _cumsum_groups(g_c, 0, CHUNK, not rev) - g_c
            tot_r = cs_r + _cumsum_groups(g_r, 1, CHUNK, not rev) - g_r
            incl = jnp.logical_and(same, (ci >= ri) if rev else (ci <= ri))
            strict = jnp.logical_and(incl, jnp.logical_not(eye))
            gam = jnp.where(incl, jnp.exp(jnp.where(incl, cs_c - cs_r, 0.0)), 0.0)
            tinv = _tri_inv(jnp.where(strict, beta * kk * gam, 0.0), ri, ci)
            eg = jnp.exp(cs_c)
            sol = _dot_x3(tinv, jnp.concatenate([beta * v, beta * kn * eg], axis=1))
            u_ref[d, i] = sol[:, :dh].astype(u_ref.dtype)
            w_ref[d, i] = sol[:, dh:].astype(w_ref.dtype)
            qe_ref[d, i] = (qn * eg).astype(qe_ref.dtype)
            kd_ref[d, i] = (kn * jnp.exp(tot_c - cs_c)).astype(kd_ref.dtype)
            qk_ref[d, i] = (qk * gam).astype(qk_ref.dtype)
            eg_ref[d, i] = jnp.exp(tot_r)


def _dn_prep(dnqkv, gcol, grow, conv_w, acol, arow, l, n_lat):
    b, lt, width = dnqkv.shape
    nc = lt // CHUNK
    n_lat_groups = n_lat // (CHUNK * DN_GROUP)
    n_groups = nc // DN_GROUP
    n = DN_HEADS * CHUNK
    dh = width // (3 * DN_HEADS)
    view = dnqkv.reshape(b, nc, CHUNK * width)
    last_col = n_lat // CHUNK - 1
    lat_g = lambda g: jnp.minimum(g, n_lat_groups - 1)
    out = lambda w, dt: (jax.ShapeDtypeStruct((b, 2, nc, n, w), dt),
                         pl.BlockSpec((None, 2, DN_GROUP, n, w), lambda bi, g: (bi, 0, g, 0, 0)))
    outs = [out(dh, BF16)] * 4 + [out(n, BF16)]
    outs.append((jax.ShapeDtypeStruct((b, 2, nc, 1, n), F32),
                 pl.BlockSpec((None, 2, DN_GROUP, 1, n), lambda bi, g: (bi, 0, g, 0, 0))))
    return pl.pallas_call(
        functools.partial(_dn_prep_kernel, n_lat_groups=n_lat_groups),
        grid=(b, n_groups),
        in_specs=[pl.BlockSpec((None, CHUNK, DN_GROUP * width), lambda bi, g: (bi, 0, lat_g(g))),
                  pl.BlockSpec((None, CHUNK, width),
                               lambda bi, g: (bi, 0, jnp.maximum(lat_g(g) * DN_GROUP - 1, 0))),
                  pl.BlockSpec((None, CHUNK, width),
                               lambda bi, g: (bi, 0, jnp.minimum(lat_g(g) * DN_GROUP + DN_GROUP, last_col))),
                  pl.BlockSpec((None, DN_GROUP * CHUNK, width), lambda bi, g: (bi, n_lat_groups, 0)),
                  pl.BlockSpec((None, DN_GROUP, n, 4), lambda bi, g: (bi, g, 0, 0)),
                  pl.BlockSpec((None, DN_GROUP, 4, n), lambda bi, g: (bi, g, 0, 0)),
                  _const_spec((None, CONV_W, width), (l, 0, 0)),
                  _const_spec((None, 2, n, 2), (l, 0, 0, 0)),
                  _const_spec((None, 2, 2, n), (l, 0, 0, 0))],
        out_specs=[o[1] for o in outs],
        out_shape=[o[0] for o in outs],
        compiler_params=_params("parallel", "parallel"),
        name="dn_prep",
    )(view, view, view, dnqkv, gcol, grow, conv_w, acol, arow)


def _dn_scan_kernel(*refs, n_ctx):
    ins, (olf_ref, olb_ref, ocf_ref, ocb_ref, s_ref) = refs[:12], refs[12:]
    s = pl.program_id(1)

    @pl.when(s == 0)
    def _():
        s_ref[...] = jnp.zeros(s_ref.shape, F32)

    for d, (ol_ref, oc_ref) in enumerate(((olf_ref, ocf_ref), (olb_ref, ocb_ref))):
        u_ref, w_ref, qe_ref, kd_ref, qk_ref, eg_ref = ins[6 * d:6 * d + 6]
        w, qe, kd = w_ref[...], qe_ref[...], kd_ref[...]
        ws, qs = [], []
        for h in range(DN_HEADS):
            rows = slice(h * CHUNK, (h + 1) * CHUNK)
            sb = s_ref[d, h].astype(BF16)
            ws.append(_dot(w[rows], sb))
            qs.append(_dot(qe[rows], sb))
        vnew = (u_ref[...].astype(F32) - jnp.concatenate(ws, axis=0)).astype(BF16)
        o = jnp.concatenate(qs, axis=0) + _dot(qk_ref[...], vnew)
        for h in range(DN_HEADS):
            rows = slice(h * CHUNK, (h + 1) * CHUNK)
            s_ref[d, h] = eg_ref[:, h * CHUNK:h * CHUNK + 1] * s_ref[d, h] + _dot_tn(kd[rows], vnew[rows])
        o = _unstack_heads(o, DN_HEADS)

        @pl.when(s < n_ctx)
        def _():
            oc_ref[...] = o.astype(oc_ref.dtype)

        @pl.when(s >= n_ctx)
        def _():
            ol_ref[...] = o.astype(ol_ref.dtype)


def _dn_scan(prep, n_lat_chunks, n_ctx_chunks):
    u = prep[0]
    b, _, nc, n, dh = u.shape
    ow = DN_HEADS * dh
    fwd = lambda s: jnp.where(s < n_ctx_chunks, n_lat_chunks + s, s - n_ctx_chunks)
    bwd = lambda s: nc - 1 - s

    def spec(a, d, chunk):
        return pl.BlockSpec((None, None, None) + a.shape[3:], lambda bi, s: (bi, d, chunk(s), 0, 0))

    lat_idx = (lambda s: jnp.maximum(s - n_ctx_chunks, 0), lambda s: jnp.minimum(nc - 1 - s, n_lat_chunks - 1))
    ctx_idx = (lambda s: jnp.minimum(s, n_ctx_chunks - 1), lambda s: jnp.maximum(n_ctx_chunks - 1 - s, 0))
    lat = lambda d: pl.BlockSpec((None, CHUNK, ow), lambda bi, s: (bi, 0, lat_idx[d](s)))
    ctx = lambda d: pl.BlockSpec((None, None, CHUNK, ow), lambda bi, s: (bi, ctx_idx[d](s), 0, 0))
    lat_shape = jax.ShapeDtypeStruct((b, n_lat_chunks, n_lat_chunks * ow), BF16)
    ctx_shape = jax.ShapeDtypeStruct((b, n_ctx_chunks, CHUNK, ow), BF16)
    olf, olb, ocf, ocb = pl.pallas_call(
        functools.partial(_dn_scan_kernel, n_ctx=n_ctx_chunks),
        grid=(b, nc),
        in_specs=[spec(a, 0, fwd) for a in prep] + [spec(a, 1, bwd) for a in prep],
        out_specs=[lat(0), lat(1), ctx(0), ctx(1)],
        out_shape=[lat_shape, lat_shape, ctx_shape, ctx_shape],
        scratch_shapes=[pltpu.VMEM((2, DN_HEADS, dh, dh), F32)],
        compiler_params=_params("parallel", "arbitrary"),
        name="dn_scan",
    )(*prep, *prep)
    to_tokens = lambda a: a.reshape(b, -1, ow)
    return to_tokens(olf), to_tokens(olb), to_tokens(ocf), to_tokens(ocb)


def _block_diag(w):
    n, i, j = w.shape
    return jnp.einsum('nij,nm->nimj', w, jnp.eye(n, dtype=w.dtype)).reshape(n * i, n * j)


def _dn_gate_layout(raw, n_lat):
    b = raw.shape[0]
    rows = n_lat // CHUNK
    lat = raw[:, :n_lat].reshape(b, rows, CHUNK, 4, DN_HEADS)
    ctx = raw[:, n_lat:].reshape(b, -1, CHUNK, 4, DN_HEADS)
    col = jnp.concatenate([lat.transpose(0, 2, 4, 1, 3), ctx.transpose(0, 1, 4, 2, 3)], axis=1)
    row = jnp.concatenate([lat.transpose(0, 2, 3, 4, 1), ctx.transpose(0, 1, 3, 4, 2)], axis=1)
    nc = col.shape[1]
    return col.reshape(b, nc, DN_HEADS * CHUNK, 4), row.reshape(b, nc, 4, DN_HEADS * CHUNK)


def kernel(x, c, ctx, c_ctx, w_mod, b_mod, norm_g, ffn_w_gu, ffn_w_down, w_in, ml_gate_b, ml_norm_g,
           lru_conv_w, lru_conv_b, lru_w_a, lru_b_a, lru_w_x, lru_b_x, lru_lambda, dn_conv_w,
           dn_a_log, dn_dt_bias, dn_norm_g, w_branch, w_out):
    b, n_lat, d = x.shape
    n_ctx = ctx.shape[1]
    depth = w_mod.shape[0]
    bw = w_branch.shape[2]
    assert n_lat == CHUNK * CHUNK and n_lat % TM == 0 and n_ctx % TM == 0 and n_ctx % (CHUNK * DN_GROUP) == 0
    n_lat_tiles, n_ctx_tiles = n_lat // TM, n_ctx // TM
    n_tiles = n_lat_tiles + n_ctx_tiles

    ctx_row = b
    n_rows = -(-(b + 1) // V7X_SUBLANES) * V7X_SUBLANES
    cc = jnp.zeros((n_rows, d), F32).at[:b].set(c).at[b].set(c_ctx)
    mod = _mod_table(cc, w_mod, b_mod).reshape(depth, n_rows, N_MOD, d)

    dqk = bw // 2
    edges = [0]
    for wdt in (dqk, dqk, bw, bw, N_GATES, bw, bw, bw, bw, bw, bw, N_GATES, N_BRANCH * d):
        edges.append(edges[-1] + wdt)
    piece = lambda i, j: w_in[:, :, edges[i]:edges[j]]
    widths = (3 * bw, 2 * bw, 3 * bw, bw, N_BRANCH * d)
    w_main = jnp.concatenate([piece(0, 4), piece(5, 7), piece(7, 11), piece(12, 13)], axis=-1).astype(BF16)
    w_gate = jnp.concatenate([piece(4, 5), piece(11, 12)], axis=-1).astype(BF16)
    wgu = ffn_w_gu.astype(BF16)
    wdn = ffn_w_down.astype(BF16)
    wbr = w_branch.astype(BF16)
    wout = w_out.astype(BF16)
    ml_bias = jnp.concatenate([ml_gate_b.reshape(depth, N_GATES), jnp.zeros((depth, N_GATES), F32)], axis=-1)
    ml_bias_c, ml_bias_r = ml_bias[:, None, :], ml_bias[:, :, None]
    ml_g = ml_norm_g[:, None, :]
    dn_g = jnp.tile(dn_norm_g, (1, DN_HEADS))[:, None, :]
    lru_w = jnp.stack([jnp.concatenate([jax.vmap(_block_diag)(lru_w_a[:, dd]), jax.vmap(_block_diag)(lru_w_x[:, dd])],
                                       axis=-1) for dd in range(2)], axis=1).astype(BF16)
    lru_b = jnp.concatenate([lru_b_a, lru_b_x], axis=-1)[:, :, None, :]
    lru_lam = lru_lambda[:, :, None, :]
    lru_cb = lru_conv_b[:, None, :]
    dn_par = jnp.repeat(jnp.stack([dn_a_log, dn_dt_bias], axis=-1), CHUNK, axis=2)
    dn_acol, dn_arow = dn_par, dn_par.transpose(0, 1, 3, 2)

    xz = jnp.concatenate([x, ctx], axis=1)
    for l in range(depth):
        xz = _ffn(xz, mod, norm_g, wgu, wdn, l, 0, n_lat_tiles, ctx_row, n_tiles)
        ml, lru, dnqkv, dnz, mg, gates = _inproj(xz, mod, norm_g, w_main, w_gate, widths, l, n_lat_tiles, ctx_row)
        ml_hf, ml_hb = _mlstm(ml, gates, gates.transpose(0, 2, 1), ml_bias_c, ml_bias_r, l,
                              n_lat_tiles, n_ctx_tiles)
        lr_hf, lr_hb = _lru(lru, lru_conv_w, lru_cb, lru_w, lru_b, lru_lam, l, n_lat_tiles, n_ctx_tiles)
        gcol, grow = _dn_gate_layout(gates[:, :, N_GATES:], n_lat)
        prep = _dn_prep(dnqkv, gcol, grow, dn_conv_w, dn_acol, dn_arow, l, n_lat)
        dn_lf, dn_lb, dn_cf, dn_cb = _dn_scan(prep, n_lat // CHUNK, n_ctx // CHUNK)
        xz = _merge(xz, mod, norm_g, ml_hf, ml_hb, ml, lr_hf, lr_hb, lru, dn_lf, dn_lb, dn_cf, dn_cb, dnz, mg,
                    ml_g, dn_g, wbr, wout, l, n_lat_tiles, ctx_row)
        xz = _ffn(xz, mod, norm_g, wgu, wdn, l, 2, n_lat_tiles, ctx_row,
                  n_lat_tiles if l == depth - 1 else n_tiles)
    return xz
```

```python
import functools
import math

import jax
import jax.numpy as jnp
from jax import lax
from jax.experimental import pallas as pl
from jax.experimental.pallas import tpu as pltpu

F32 = jnp.float32
BF16 = jnp.bfloat16

EPS = 1e-6
N_MOD = 9
N_BRANCH = 3
CONV_W = 4
CONV_LEFT = 2
ML_HEADS = 4
LRU_C = 8.0
DN_HEADS = 4
CHUNK = 64
DN_GROUP = 4
N_GATES = 16

V7X_SUBLANES = 8
V7X_LANES = 128
V7X_VMEM_BYTES = 64 * 1024 * 1024
VMEM_LIMIT = V7X_VMEM_BYTES - 8 * 1024 * 1024

TM = 256
NEG = -1e30


def _sigmoid(x):
    return 1.0 / (1.0 + jnp.exp(-x))


def _silu(x):
    return x * _sigmoid(x)


def _softplus(x):
    return jnp.maximum(x, 0.0) + jnp.log(1.0 + jnp.exp(-jnp.abs(x)))


def _rms(x):
    return x * lax.rsqrt(jnp.mean(x * x, axis=-1, keepdims=True) + EPS)


def _rms_heads(x, n_heads):
    hd = x.shape[-1] // n_heads
    return jnp.concatenate([_rms(x[:, h * hd:(h + 1) * hd]) for h in range(n_heads)], axis=-1)


def _gelu_tanh(x):
    return 0.5 * x * (1.0 + jnp.tanh(math.sqrt(2.0 / math.pi) * (x + 0.044715 * (x * x * x))))


def _dot(a, b):
    return jnp.dot(a, b, preferred_element_type=F32)


def _dot_nt(a, b):
    return lax.dot_general(a, b, (((1,), (1,)), ((), ())), preferred_element_type=F32)


def _dot_tn(a, b):
    return lax.dot_general(a, b, (((0,), (0,)), ((), ())), preferred_element_type=F32)


def _params(*sem):
    return pltpu.CompilerParams(dimension_semantics=sem, vmem_limit_bytes=VMEM_LIMIT)


def _const_spec(block, index):
    return pl.BlockSpec(block, lambda *_: index, pipeline_mode=pl.Buffered(1))


def _stack_heads(x, n):
    w = x.shape[1] // n
    return jnp.concatenate([x[:, h * w:(h + 1) * w] for h in range(n)], axis=0)


def _unstack_heads(x, n):
    t = x.shape[0] // n
    return jnp.concatenate([x[h * t:(h + 1) * t] for h in range(n)], axis=1)


def _cumsum_groups(x, axis, period, rev):
    n = x.shape[axis]
    idx = lax.broadcasted_iota(jnp.int32, x.shape, axis) % period
    sh = 1
    while sh < period:
        if rev:
            x = x + jnp.where(idx < period - sh, pltpu.roll(x, n - sh, axis=axis), 0.0)
        else:
            x = x + jnp.where(idx >= sh, pltpu.roll(x, sh, axis=axis), 0.0)
        sh *= 2
    return x


def _shift_rows(x, prev8, nxt8, off):
    t = x.shape[0]
    row8 = lax.broadcasted_iota(jnp.int32, (V7X_SUBLANES, 1), 0)
    rolled = pltpu.roll(x, (-off) % t, axis=0)
    if off < 0:
        edge = jnp.where(row8 < -off, pltpu.roll(prev8, -off, axis=0), rolled[:V7X_SUBLANES])
        return jnp.concatenate([edge, rolled[V7X_SUBLANES:]], axis=0)
    edge = jnp.where(row8 >= V7X_SUBLANES - off, pltpu.roll(nxt8, V7X_SUBLANES - off, axis=0),
                     rolled[t - V7X_SUBLANES:])
    return jnp.concatenate([rolled[:t - V7X_SUBLANES], edge], axis=0)


def _conv4(x, prev8, nxt8, w):
    acc = x * w[CONV_LEFT:CONV_LEFT + 1]
    for j in range(CONV_W):
        if j != CONV_LEFT:
            acc = acc + _shift_rows(x, prev8, nxt8, j - CONV_LEFT) * w[j:j + 1]
    return acc


def _mod_kernel(c_ref, w_ref, b_ref, o_ref):
    s = _silu(c_ref[...]).astype(BF16)
    o_ref[...] = _dot(s, w_ref[...].astype(BF16)) + b_ref[...]


def _mod_table(cc, w_mod, b_mod):
    depth, d, nd = w_mod.shape
    r = cc.shape[0]
    tn = nd // 4
    return pl.pallas_call(
        _mod_kernel,
        grid=(depth, nd // tn),
        in_specs=[pl.BlockSpec((r, d), lambda l, j: (0, 0)),
                  pl.BlockSpec((None, d, tn), lambda l, j: (l, 0, j)),
                  pl.BlockSpec((None, 1, tn), lambda l, j: (l, 0, j))],
        out_specs=pl.BlockSpec((None, r, tn), lambda l, j: (l, 0, j)),
        out_shape=jax.ShapeDtypeStruct((depth, r, nd), F32),
        compiler_params=_params("parallel", "parallel"),
        name="mod_table",
    )(cc, w_mod, b_mod.reshape(depth, 1, nd))


def _tok_spec(width, col=0):
    return pl.BlockSpec((None, TM, width), lambda bi, t: (bi, t, col))


def _mod_spec(l, d, n_lat_tiles, ctx_row):
    return pl.BlockSpec((None, None, N_MOD, d),
                        lambda bi, t: (l, jnp.where(t >= n_lat_tiles, ctx_row, bi), 0, 0))


def _ffn_kernel(x_ref, mod_ref, g_ref, wgu_ref, wd_ref, o_ref, *, j, dff):
    x = x_ref[...]
    shift, scale, gate = (mod_ref[3 * j + i:3 * j + i + 1, :] for i in range(3))
    g_pre, g_post = g_ref[2 * j:2 * j + 1, :], g_ref[2 * j + 1:2 * j + 2, :]
    h = (_rms(x) * g_pre * (1.0 + scale) + shift).astype(BF16)
    gu = _dot(h, wgu_ref[...])
    a = (_silu(gu[:, :dff]) * gu[:, dff:]).astype(BF16)
    y = _dot(a, wd_ref[...])
    o_ref[...] = x + 0.5 * gate * (_rms(y) * g_post)


def _ffn(xz, mod, norm_g, wgu, wd, l, j, n_lat_tiles, ctx_row, n_out_tiles):
    b, _, d = xz.shape
    dff = wd.shape[2]
    return pl.pallas_call(
        functools.partial(_ffn_kernel, j=j, dff=dff),
        grid=(b, n_out_tiles),
        in_specs=[_tok_spec(d),
                  _mod_spec(l, d, n_lat_tiles, ctx_row),
                  _const_spec((None,) + norm_g.shape[1:], (l, 0, 0)),
                  _const_spec((None, None, d, 2 * dff), (l, j // 2, 0, 0)),
                  _const_spec((None, None, dff, d), (l, j // 2, 0, 0))],
        out_specs=_tok_spec(d),
        out_shape=jax.ShapeDtypeStruct((b, n_out_tiles * TM, d), F32),
        compiler_params=_params("parallel", "parallel"),
        name=f"ffn{j}",
    )(xz, mod, norm_g, wgu, wd)


def _inproj_kernel(x_ref, mod_ref, g_ref, w_ref, wg_ref, *o_refs):
    shift, scale = mod_ref[3:4, :], mod_ref[4:5, :]
    h = (_rms(x_ref[...]) * g_ref[2:3, :] * (1.0 + scale) + shift).astype(BF16)
    col = 0
    for o_ref in o_refs[:-1]:
        n = o_ref.shape[-1]
        o_ref[...] = _dot(h, w_ref[:, col:col + n]).astype(o_ref.dtype)
        col += n
    o_refs[-1][...] = _dot(h, wg_ref[...])


def _inproj(xz, mod, norm_g, w_main, w_gate, widths, l, n_lat_tiles, ctx_row):
    b, lt, d = xz.shape
    ng = w_gate.shape[-1]
    return pl.pallas_call(
        _inproj_kernel,
        grid=(b, lt // TM),
        in_specs=[_tok_spec(d),
                  _mod_spec(l, d, n_lat_tiles, ctx_row),
                  _const_spec((None,) + norm_g.shape[1:], (l, 0, 0)),
                  _const_spec((None, d, w_main.shape[-1]), (l, 0, 0)),
                  _const_spec((None, d, ng), (l, 0, 0))],
        out_specs=[_tok_spec(w) for w in widths] + [_tok_spec(ng)],
        out_shape=[jax.ShapeDtypeStruct((b, lt, w), BF16) for w in widths]
                  + [jax.ShapeDtypeStruct((b, lt, ng), F32)],
        compiler_params=_params("parallel", "parallel"),
        name="inproj",
    )(xz, mod, norm_g, w_main, w_gate)


def _merge_kernel(x_ref, mod_ref, g_ref, mlf_ref, mlb_ref, mlo_ref, lrf_ref, lrb_ref, lry_ref,
                  dlf_ref, dlb_ref, dcf_ref, dcb_ref, dnz_ref, mg_ref, mlg_ref, dng_ref,
                  wb_ref, wo_ref, o_ref, *, n_lat_tiles):
    d = x_ref.shape[-1]
    f32 = lambda r: r[...].astype(F32)
    is_ctx = pl.program_id(1) >= n_lat_tiles
    dn_h = jnp.where(is_ctx, f32(dcf_ref) + f32(dcb_ref), f32(dlf_ref) + f32(dlb_ref))
    y_ml = _rms_heads(f32(mlf_ref) + f32(mlb_ref), ML_HEADS) * mlg_ref[...] * _sigmoid(f32(mlo_ref))
    y_lr = (f32(lrf_ref) + f32(lrb_ref)) * _gelu_tanh(f32(lry_ref))
    y_dn = _rms_heads(dn_h, DN_HEADS) * dng_ref[...] * _silu(f32(dnz_ref))
    mix = None
    for n, y in enumerate((y_ml, y_lr, y_dn)):
        term = _sigmoid(mg_ref[:, n * d:(n + 1) * d].astype(F32)) * _dot(y.astype(BF16), wb_ref[n])
        mix = term if mix is None else mix + term
    out = _dot(mix.astype(BF16), wo_ref[...])
    o_ref[...] = x_ref[...] + mod_ref[5:6, :] * (_rms(out) * g_ref[3:4, :])


def _merge(xz, mod, norm_g, ml_hf, ml_hb, ml, lr_hf, lr_hb, lru, dn_lf, dn_lb, dn_cf, dn_cb, dnz, mg,
           ml_g, dn_g, w_branch, w_out, l, n_lat_tiles, ctx_row):
    b, lt, d = xz.shape
    bw = w_branch.shape[2]
    n_ctx_tiles = lt // TM - n_lat_tiles
    lat_spec = pl.BlockSpec((None, TM, bw), lambda bi, t: (bi, jnp.minimum(t, n_lat_tiles - 1), 0))
    ctx_spec = pl.BlockSpec((None, TM, bw),
                            lambda bi, t: (bi, jnp.clip(t - n_lat_tiles, 0, n_ctx_tiles - 1), 0))
    return pl.pallas_call(
        functools.partial(_merge_kernel, n_lat_tiles=n_lat_tiles),
        grid=(b, lt // TM),
        in_specs=[_tok_spec(d),
                  _mod_spec(l, d, n_lat_tiles, ctx_row),
                  _const_spec((None,) + norm_g.shape[1:], (l, 0, 0)),
                  _tok_spec(bw), _tok_spec(bw), _tok_spec(bw, ml.shape[-1] // bw - 1),
                  _tok_spec(bw), _tok_spec(bw), _tok_spec(bw, lru.shape[-1] // bw - 1),
                  lat_spec, lat_spec, ctx_spec, ctx_spec, _tok_spec(bw),
                  _tok_spec(N_BRANCH * d),
                  _const_spec((None, 1, bw), (l, 0, 0)),
                  _const_spec((None, 1, bw), (l, 0, 0)),
                  _const_spec((None, N_BRANCH, bw, d), (l, 0, 0, 0)),
                  _const_spec((None, d, d), (l, 0, 0))],
        out_specs=_tok_spec(d),
        out_shape=jax.ShapeDtypeStruct(xz.shape, F32),
        compiler_params=_params("parallel", "parallel"),
        name="merge",
    )(xz, mod, norm_g, ml_hf, ml_hb, ml, lr_hf, lr_hb, lru, dn_lf, dn_lb, dn_cf, dn_cb, dnz, mg,
      ml_g, dn_g, w_branch, w_out)


def _log_scan(a, b, idx, period, rev):
    n = a.shape[0]
    sh = 1
    while sh < period:
        if rev:
            ok = idx < period - sh
            ar, br = pltpu.roll(a, n - sh, axis=0), pltpu.roll(b, n - sh, axis=0)
        else:
            ok = idx >= sh
            ar, br = pltpu.roll(a, sh, axis=0), pltpu.roll(b, sh, axis=0)
        b = b + a * jnp.where(ok, br, 0.0)
        a = a * jnp.where(ok, ar, 1.0)
        sh *= 2
    return a, b


def _tile_scan(a, b, h0, rev, sa_ref, sb_ref):
    t = a.shape[0]
    g = t // V7X_SUBLANES
    row = lax.broadcasted_iota(jnp.int32, (t, 1), 0)
    a, b = _log_scan(a, b, row % V7X_SUBLANES, V7X_SUBLANES, rev)
    edge = 0 if rev else V7X_SUBLANES - 1

    def group_edges(ref, val):
        parts = []
        for j in range(ref.shape[0]):
            ref[j] = val[:, j * V7X_LANES:(j + 1) * V7X_LANES]
            parts.append(ref[j, pl.ds(edge, g, stride=V7X_SUBLANES), :])
        return jnp.concatenate(parts, axis=1)

    ga = group_edges(sa_ref, a)
    gb = group_edges(sb_ref, b)
    grow = lax.broadcasted_iota(jnp.int32, (g, 1), 0)
    ga, gb = _log_scan(ga, gb, grow, g, rev)
    h_end = gb + ga * h0
    if rev:
        h_in = jnp.where(grow == g - 1, h0, pltpu.roll(h_end, g - 1, axis=0))
        carry = h_end[0:1]
    else:
        h_in = jnp.where(grow == 0, h0, pltpu.roll(h_end, 1, axis=0))
        carry = h_end[g - 1:g]
    out = jnp.concatenate(
        [b[V7X_SUBLANES * i:V7X_SUBLANES * (i + 1)] + a[V7X_SUBLANES * i:V7X_SUBLANES * (i + 1)] * h_in[i:i + 1]
         for i in range(g)], axis=0)
    return out, carry


def _lru_kernel(x_ref, cw_ref, cb_ref, w_ref, bias_ref, lam_ref, of_ref, ob_ref, *scratch, n_lat, n_ctx):
    t, c = TM, x_ref.shape[-1]
    n_tiles = n_lat + n_ctx
    halo = 2 * V7X_SUBLANES

    def load_conv(tile):
        r0 = pl.multiple_of(tile * t, t)
        first = jnp.logical_or(tile == 0, tile == n_lat)
        last = jnp.logical_or(tile == n_lat - 1, tile == n_tiles - 1)
        x = x_ref[pl.ds(r0, t), :].astype(F32)
        p0 = pl.multiple_of(jnp.maximum(r0 - halo, 0), halo)
        n0 = pl.multiple_of(jnp.minimum(r0 + t, n_tiles * t - halo), halo)
        prev8 = jnp.where(first, 0.0, x_ref[pl.ds(p0, halo), :].astype(F32)[V7X_SUBLANES:])
        nxt8 = jnp.where(last, 0.0, x_ref[pl.ds(n0, halo), :].astype(F32)[:V7X_SUBLANES])
        return r0, _conv4(x, prev8, nxt8, cw_ref[...]) + cb_ref[...]

    def direction(d, tile, h0, o_ref):
        r0, xc = load_conv(tile)
        z = _dot(xc.astype(BF16), w_ref[d]) + bias_ref[d]
        r, i = _sigmoid(z[:, :c]), _sigmoid(z[:, c:])
        la = (-LRU_C * _softplus(-lam_ref[d])) * r
        a = jnp.exp(la)
        bx = jnp.sqrt(jnp.tanh(-la) * (1.0 + a * a)) * (i * xc)
        h, carry = _tile_scan(a, bx, h0, d == 1, scratch[2 * d], scratch[2 * d + 1])
        o_ref[pl.ds(r0, t), :] = h.astype(o_ref.dtype)
        return carry

    def step(s, carry):
        hf, hb = carry
        hf = direction(0, jnp.where(s < n_ctx, n_lat + s, s - n_ctx), hf, of_ref)
        hb = direction(1, n_tiles - 1 - s, hb, ob_ref)
        return hf, hb

    zero = jnp.zeros((1, c), F32)
    lax.fori_loop(0, n_tiles, step, (zero, zero))


def _lru(lru, conv_w, conv_b, w_gates, b_gates, lam, l, n_lat, n_ctx):
    b, lt, _ = lru.shape
    c = conv_w.shape[-1]
    seq = pl.BlockSpec((None, lt, c), lambda bi: (bi, 0, 0))
    return pl.pallas_call(
        functools.partial(_lru_kernel, n_lat=n_lat, n_ctx=n_ctx),
        grid=(b,),
        in_specs=[seq,
                  _const_spec((None, CONV_W, c), (l, 0, 0)),
                  _const_spec((None, 1, c), (l, 0, 0)),
                  _const_spec((None, 2, c, 2 * c), (l, 0, 0, 0)),
                  _const_spec((None, 2, 1, 2 * c), (l, 0, 0, 0)),
                  _const_spec((None, 2, 1, c), (l, 0, 0, 0))],
        out_specs=[seq, seq],
        out_shape=[jax.ShapeDtypeStruct((b, lt, c), BF16)] * 2,
        scratch_shapes=[pltpu.VMEM((c // V7X_LANES, TM, V7X_LANES), F32)] * 4,
        compiler_params=_params("parallel"),
        name="lru",
    )(lru, conv_w, conv_b, w_gates, b_gates, lam)


def _mlstm_kernel(qf_ref, kf_ref, vf_ref, qb_ref, kb_ref, vb_ref, gcf_ref, gcb_ref, grf_ref, grb_ref,
                  bc_ref, br_ref, hf_ref, hb_ref, c_ref, m_ref):
    @pl.when(pl.program_id(1) == 0)
    def _():
        c_ref[...] = jnp.zeros(c_ref.shape, F32)
        m_ref[...] = jnp.zeros(m_ref.shape, F32)

    tc, nqk = qf_ref.shape
    dk = nqk // ML_HEADS
    dv = vf_ref.shape[1] // ML_HEADS
    ri = lax.broadcasted_iota(jnp.int32, (tc, tc), 0)
    ci = lax.broadcasted_iota(jnp.int32, (tc, tc), 1)
    lane_head = lax.broadcasted_iota(jnp.int32, (1, nqk), 1) // dk
    ones = jnp.ones((tc, dv), BF16)
    dirs = ((qf_ref, kf_ref, vf_ref, gcf_ref, grf_ref, hf_ref),
            (qb_ref, kb_ref, vb_ref, gcb_ref, grb_ref, hb_ref))
    for d, (q_ref, k_ref, v_ref, gc_ref, gr_ref, o_ref) in enumerate(dirs):
        rev = d == 1
        causal = (ci >= ri) if rev else (ci <= ri)
        gc = gc_ref[...] + bc_ref[...]
        gr = gr_ref[...] + br_ref[...]
        i_rows = gr[ML_HEADS * d:ML_HEADS * (d + 1)]
        fo = 2 * ML_HEADS + ML_HEADS * d
        b_cols = _cumsum_groups(-_softplus(-gc[:, fo:fo + ML_HEADS]), 0, tc, rev)
        b_rows = _cumsum_groups(-_softplus(-gr[fo:fo + ML_HEADS]), 1, tc, rev)
        q_all = q_ref[...] * (dk ** -0.5)
        k_all = k_ref[...]
        v_all = v_ref[...]
        kt_all = k_all.astype(F32).T
        c_all = c_ref[d]
        cb_all = c_all.astype(BF16)
        outs = []
        for h in range(ML_HEADS):
            r = ML_HEADS * d + h
            qh = jnp.where(lane_head == h, q_all, jnp.zeros_like(q_all))
            vp = jnp.concatenate([v_all[:, h * dv:(h + 1) * dv], ones], axis=1)
            b_c, b_r, i_r = b_cols[:, h:h + 1], b_rows[h:h + 1], i_rows[h:h + 1]
            m_prev = m_ref[r:r + 1, 0:1]
            logd = jnp.where(causal, b_c - b_r + i_r, NEG)
            inter = b_c + m_prev
            m_t = jnp.maximum(inter, jnp.max(logd, axis=1, keepdims=True))
            p = _dot_nt(qh, k_all) * jnp.exp(logd - m_t)
            num = _dot(p.astype(BF16), vp) + jnp.exp(inter - m_t) * _dot(qh, cb_all)
            outs.append(num[:, :dv] / jnp.maximum(jnp.abs(num[:, dv:]), jnp.exp(-m_t)))
            b_end = b_c[0:1] if rev else b_c[tc - 1:tc]
            lws = b_end - b_r + i_r
            m_new = jnp.maximum(b_end + m_prev, jnp.max(lws, axis=1, keepdims=True))
            kw = (kt_all[h * dk:(h + 1) * dk] * jnp.exp(lws - m_new)).astype(BF16)
            c_ref[d, h * dk:(h + 1) * dk, :] = (jnp.exp(b_end + m_prev - m_new) * c_all[h * dk:(h + 1) * dk]
                                                + _dot(kw, vp))
            m_ref[r:r + 1, :] = jnp.broadcast_to(m_new, (1, m_ref.shape[1]))
        o_ref[...] = jnp.concatenate(outs, axis=1).astype(o_ref.dtype)


def _mlstm(ml, gates, gates_t, bias_c, bias_r, l, n_lat, n_ctx):
    b, lt, _ = ml.shape
    ng = gates.shape[-1]
    n_tiles = n_lat + n_ctx
    dqk = ml.shape[-1] // 6
    fwd = lambda s: jnp.where(s < n_ctx, n_lat + s, s - n_ctx)
    bwd = lambda s: n_tiles - 1 - s

    def specs(tile):
        return [pl.BlockSpec((None, TM, dqk), lambda bi, s: (bi, tile(s), 0)),
                pl.BlockSpec((None, TM, dqk), lambda bi, s: (bi, tile(s), 1)),
                pl.BlockSpec((None, TM, 2 * dqk), lambda bi, s: (bi, tile(s), 1))]

    col = lambda tile: pl.BlockSpec((None, TM, ng), lambda bi, s: (bi, tile(s), 0))
    row = lambda tile: pl.BlockSpec((None, ng, TM), lambda bi, s: (bi, 0, tile(s)))
    out = lambda tile: pl.BlockSpec((None, TM, 2 * dqk), lambda bi, s: (bi, tile(s), 0))
    return pl.pallas_call(
        _mlstm_kernel,
        grid=(b, n_tiles),
        in_specs=specs(fwd) + specs(bwd) + [col(fwd), col(bwd), row(fwd), row(bwd),
                                            _const_spec((None, 1, ng), (l, 0, 0)),
                                            _const_spec((None, ng, 1), (l, 0, 0))],
        out_specs=[out(fwd), out(bwd)],
        out_shape=[jax.ShapeDtypeStruct((b, lt, 2 * dqk), BF16)] * 2,
        scratch_shapes=[pltpu.VMEM((2, dqk, 2 * (2 * dqk // ML_HEADS)), F32),
                        pltpu.VMEM((2 * ML_HEADS, V7X_LANES), F32)],
        compiler_params=_params("parallel", "arbitrary"),
        name="mlstm",
    )(ml, ml, ml, ml, ml, ml, gates, gates, gates_t, gates_t, bias_c, bias_r)


INV_BASE = 8


def _tri_inv(a, ri, ci):
    blk = lambda s: (ri // s) == (ci // s)
    a0 = jnp.where(blk(INV_BASE), a, 0.0)
    x = jnp.where(ri == ci, 1.0, 0.0) - a0
    p = a0.astype(BF16)
    for _ in range(INV_BASE.bit_length() - 2):
        p = _dot(p, p).astype(BF16)
        x = x + _dot(x.astype(BF16), p)
    s = INV_BASE
    while s < CHUNK:
        off = jnp.where(jnp.logical_and(blk(2 * s), jnp.logical_not(blk(s))), a, 0.0).astype(BF16)
        xb = x.astype(BF16)
        x = x - _dot(_dot(xb, off).astype(BF16), xb)
        s *= 2
    return x


def _dn_prep_kernel(xl_ref, xp_ref, xn_ref, xc_ref, gcol_ref, grow_ref, cw_ref, acol_ref, arow_ref,
                    u_ref, w_ref, qe_ref, kd_ref, qk_ref, eg_ref, *, n_lat_groups):
    g = pl.program_id(1)
    is_ctx = g >= n_lat_groups
    width = xp_ref.shape[-1]
    dh = width // (3 * DN_HEADS)
    halo = 2 * V7X_SUBLANES
    xl = xl_ref[...]
    x_lat = jnp.concatenate([xl[:, i * width:(i + 1) * width] for i in range(DN_GROUP)], axis=0)
    x = jnp.where(is_ctx, xc_ref[...], x_lat).astype(F32)
    no_prev = jnp.logical_or(is_ctx, g == 0)
    no_next = jnp.logical_or(is_ctx, g == n_lat_groups - 1)
    prev8 = jnp.where(no_prev, 0.0, xp_ref[CHUNK - halo:, :].astype(F32)[V7X_SUBLANES:])
    nxt8 = jnp.where(no_next, 0.0, xn_ref[:halo, :].astype(F32)[:V7X_SUBLANES])
    xc = _silu(_conv4(x, prev8, nxt8, cw_ref[...]))

    n = DN_HEADS * CHUNK
    ri = lax.broadcasted_iota(jnp.int32, (n, n), 0)
    ci = lax.broadcasted_iota(jnp.int32, (n, n), 1)
    same = (ri // CHUNK) == (ci // CHUNK)
    eye = ri == ci
    for i in range(DN_GROUP):
        xi = xc[i * CHUNK:(i + 1) * CHUNK]
        q, k, v = (_stack_heads(xi[:, j * DN_HEADS * dh:(j + 1) * DN_HEADS * dh], DN_HEADS) for j in range(3))
        qn = q * lax.rsqrt(jnp.sum(q * q, axis=-1, keepdims=True) + EPS) * (dh ** -0.5)
        kn = k * lax.rsqrt(jnp.sum(k * k, axis=-1, keepdims=True) + EPS)
        kb = kn.astype(BF16)
        kk = _dot_nt(kb, kb)
        qk = _dot_nt(qn.astype(BF16), kb)
        gcol = gcol_ref[i]
        grow = grow_ref[i]
        for d in range(2):
            rev = d == 1
            beta = _sigmoid(gcol[:, d:d + 1])
            g_c = -jnp.exp(acol_ref[d, :, 0:1]) * _softplus(gcol[:, 2 + d:3 + d] + acol_ref[d, :, 1:2])
            g_r = -jnp.exp(arow_ref[d, 0:1, :]) * _softplus(grow[2 + d:3 + d] + arow_ref[d, 1:2, :])
            cs_c = _cumsum_groups(g_c, 0, CHUNK, rev)
            cs_r = _cumsum_groups(g_r, 1, CHUNK, rev)
            tot_c = cs_c + _cumsum_groups(g_c, 0, CHUNK, not rev) - g_c
            tot_r = cs_r + _cumsum_groups(g_r, 1, CHUNK, not rev) - g_r
            incl = jnp.logical_and(same, (ci >= ri) if rev else (ci <= ri))
            strict = jnp.logical_and(incl, jnp.logical_not(eye))
            gam = jnp.where(incl, jnp.exp(jnp.where(incl, cs_c - cs_r, 0.0)), 0.0)
            tinv = _tri_inv(jnp.where(strict, beta * kk * gam, 0.0), ri, ci)
            eg = jnp.exp(cs_c)
            rhs = jnp.concatenate([beta * v, beta * kn * eg], axis=1).astype(BF16)
            sol = _dot(tinv.astype(BF16), rhs)
            u_ref[d, i] = sol[:, :dh].astype(u_ref.dtype)
            w_ref[d, i] = sol[:, dh:].astype(w_ref.dtype)
            qe_ref[d, i] = (qn * eg).astype(qe_ref.dtype)
            kd_ref[d, i] = (kn * jnp.exp(tot_c - cs_c)).astype(kd_ref.dtype)
            qk_ref[d, i] = (qk * gam).astype(qk_ref.dtype)
            eg_ref[d, i] = jnp.exp(tot_r)


def _dn_prep(dnqkv, gcol, grow, conv_w, acol, arow, l, n_lat):
    b, lt, width = dnqkv.shape
    nc = lt // CHUNK
    n_lat_groups = n_lat // (CHUNK * DN_GROUP)
    n_groups = nc // DN_GROUP
    n = DN_HEADS * CHUNK
    dh = width // (3 * DN_HEADS)
    view = dnqkv.reshape(b, nc, CHUNK * width)
    last_col = n_lat // CHUNK - 1
    lat_g = lambda g: jnp.minimum(g, n_lat_groups - 1)
    out = lambda w, dt: (jax.ShapeDtypeStruct((b, 2, nc, n, w), dt),
                         pl.BlockSpec((None, 2, DN_GROUP, n, w), lambda bi, g: (bi, 0, g, 0, 0)))
    outs = [out(dh, BF16)] * 4 + [out(n, BF16)]
    outs.append((jax.ShapeDtypeStruct((b, 2, nc, 1, n), F32),
                 pl.BlockSpec((None, 2, DN_GROUP, 1, n), lambda bi, g: (bi, 0, g, 0, 0))))
    return pl.pallas_call(
        functools.partial(_dn_prep_kernel, n_lat_groups=n_lat_groups),
        grid=(b, n_groups),
        in_specs=[pl.BlockSpec((None, CHUNK, DN_GROUP * width), lambda bi, g: (bi, 0, lat_g(g))),
                  pl.BlockSpec((None, CHUNK, width),
                               lambda bi, g: (bi, 0, jnp.maximum(lat_g(g) * DN_GROUP - 1, 0))),
                  pl.BlockSpec((None, CHUNK, width),
                               lambda bi, g: (bi, 0, jnp.minimum(lat_g(g) * DN_GROUP + DN_GROUP, last_col))),
                  pl.BlockSpec((None, DN_GROUP * CHUNK, width), lambda bi, g: (bi, n_lat_groups, 0)),
                  pl.BlockSpec((None, DN_GROUP, n, 4), lambda bi, g: (bi, g, 0, 0)),
                  pl.BlockSpec((None, DN_GROUP, 4, n), lambda bi, g: (bi, g, 0, 0)),
                  _const_spec((None, CONV_W, width), (l, 0, 0)),
                  _const_spec((None, 2, n, 2), (l, 0, 0, 0)),
                  _const_spec((None, 2, 2, n), (l, 0, 0, 0))],
        out_specs=[o[1] for o in outs],
        out_shape=[o[0] for o in outs],
        compiler_params=_params("parallel", "parallel"),
        name="dn_prep",
    )(view, view, view, dnqkv, gcol, grow, conv_w, acol, arow)


def _dn_scan_kernel(*refs, n_ctx):
    ins, (olf_ref, olb_ref, ocf_ref, ocb_ref, s_ref) = refs[:12], refs[12:]
    s = pl.program_id(1)

    @pl.when(s == 0)
    def _():
        s_ref[...] = jnp.zeros(s_ref.shape, F32)

    for d, (ol_ref, oc_ref) in enumerate(((olf_ref, ocf_ref), (olb_ref, ocb_ref))):
        u_ref, w_ref, qe_ref, kd_ref, qk_ref, eg_ref = ins[6 * d:6 * d + 6]
        w, qe, kd = w_ref[...], qe_ref[...], kd_ref[...]
        ws, qs = [], []
        for h in range(DN_HEADS):
            rows = slice(h * CHUNK, (h + 1) * CHUNK)
            sb = s_ref[d, h].astype(BF16)
            ws.append(_dot(w[rows], sb))
            qs.append(_dot(qe[rows], sb))
        vnew = (u_ref[...].astype(F32) - jnp.concatenate(ws, axis=0)).astype(BF16)
        o = jnp.concatenate(qs, axis=0) + _dot(qk_ref[...], vnew)
        for h in range(DN_HEADS):
            rows = slice(h * CHUNK, (h + 1) * CHUNK)
            s_ref[d, h] = eg_ref[:, h * CHUNK:h * CHUNK + 1] * s_ref[d, h] + _dot_tn(kd[rows], vnew[rows])
        o = _unstack_heads(o, DN_HEADS)

        @pl.when(s < n_ctx)
        def _():
            oc_ref[...] = o.astype(oc_ref.dtype)

        @pl.when(s >= n_ctx)
        def _():
            ol_ref[...] = o.astype(ol_ref.dtype)


def _dn_scan(prep, n_lat_chunks, n_ctx_chunks):
    u = prep[0]
    b, _, nc, n, dh = u.shape
    ow = DN_HEADS * dh
    fwd = lambda s: jnp.where(s < n_ctx_chunks, n_lat_chunks + s, s - n_ctx_chunks)
    bwd = lambda s: nc - 1 - s

    def spec(a, d, chunk):
        return pl.BlockSpec((None, None, None) + a.shape[3:], lambda bi, s: (bi, d, chunk(s), 0, 0))

    lat_idx = (lambda s: jnp.maximum(s - n_ctx_chunks, 0), lambda s: jnp.minimum(nc - 1 - s, n_lat_chunks - 1))
    ctx_idx = (lambda s: jnp.minimum(s, n_ctx_chunks - 1), lambda s: jnp.maximum(n_ctx_chunks - 1 - s, 0))
    lat = lambda d: pl.BlockSpec((None, CHUNK, ow), lambda bi, s: (bi, 0, lat_idx[d](s)))
    ctx = lambda d: pl.BlockSpec((None, None, CHUNK, ow), lambda bi, s: (bi, ctx_idx[d](s), 0, 0))
    lat_shape = jax.ShapeDtypeStruct((b, n_lat_chunks, n_lat_chunks * ow), BF16)
    ctx_shape = jax.ShapeDtypeStruct((b, n_ctx_chunks, CHUNK, ow), BF16)
    olf, olb, ocf, ocb = pl.pallas_call(
        functools.partial(_dn_scan_kernel, n_ctx=n_ctx_chunks),
        grid=(b, nc),
        in_specs=[spec(a, 0, fwd) for a in prep] + [spec(a, 1, bwd) for a in prep],
        out_specs=[lat(0), lat(1), ctx(0), ctx(1)],
        out_shape=[lat_shape, lat_shape, ctx_shape, ctx_shape],
        scratch_shapes=[pltpu.VMEM((2, DN_HEADS, dh, dh), F32)],
        compiler_params=_params("parallel", "arbitrary"),
        name="dn_scan",
    )(*prep, *prep)
    to_tokens = lambda a: a.reshape(b, -1, ow)
    return to_tokens(olf), to_tokens(olb), to_tokens(ocf), to_tokens(ocb)


def _block_diag(w):
    n, i, j = w.shape
    return jnp.einsum('nij,nm->nimj', w, jnp.eye(n, dtype=w.dtype)).reshape(n * i, n * j)


def _dn_gate_layout(raw, n_lat):
    b = raw.shape[0]
    rows = n_lat // CHUNK
    lat = raw[:, :n_lat].reshape(b, rows, CHUNK, 4, DN_HEADS)
    ctx = raw[:, n_lat:].reshape(b, -1, CHUNK, 4, DN_HEADS)
    col = jnp.concatenate([lat.transpose(0, 2, 4, 1, 3), ctx.transpose(0, 1, 4, 2, 3)], axis=1)
    row = jnp.concatenate([lat.transpose(0, 2, 3, 4, 1), ctx.transpose(0, 1, 3, 4, 2)], axis=1)
    nc = col.shape[1]
    return col.reshape(b, nc, DN_HEADS * CHUNK, 4), row.reshape(b, nc, 4, DN_HEADS * CHUNK)


def kernel(x, c, ctx, c_ctx, w_mod, b_mod, norm_g, ffn_w_gu, ffn_w_down, w_in, ml_gate_b, ml_norm_g,
           lru_conv_w, lru_conv_b, lru_w_a, lru_b_a, lru_w_x, lru_b_x, lru_lambda, dn_conv_w,
           dn_a_log, dn_dt_bias, dn_norm_g, w_branch, w_out):
    b, n_lat, d = x.shape
    n_ctx = ctx.shape[1]
    depth = w_mod.shape[0]
    bw = w_branch.shape[2]
    assert n_lat == CHUNK * CHUNK and n_lat % TM == 0 and n_ctx % TM == 0 and n_ctx % (CHUNK * DN_GROUP) == 0
    n_lat_tiles, n_ctx_tiles = n_lat // TM, n_ctx // TM
    n_tiles = n_lat_tiles + n_ctx_tiles

    ctx_row = b
    n_rows = -(-(b + 1) // V7X_SUBLANES) * V7X_SUBLANES
    cc = jnp.zeros((n_rows, d), F32).at[:b].set(c).at[b].set(c_ctx)
    mod = _mod_table(cc, w_mod, b_mod).reshape(depth, n_rows, N_MOD, d)

    dqk = bw // 2
    edges = [0]
    for wdt in (dqk, dqk, bw, bw, N_GATES, bw, bw, bw, bw, bw, bw, N_GATES, N_BRANCH * d):
        edges.append(edges[-1] + wdt)
    piece = lambda i, j: w_in[:, :, edges[i]:edges[j]]
    widths = (3 * bw, 2 * bw, 3 * bw, bw, N_BRANCH * d)
    w_main = jnp.concatenate([piece(0, 4), piece(5, 7), piece(7, 11), piece(12, 13)], axis=-1).astype(BF16)
    w_gate = jnp.concatenate([piece(4, 5), piece(11, 12)], axis=-1).astype(BF16)
    wgu = ffn_w_gu.astype(BF16)
    wdn = ffn_w_down.astype(BF16)
    wbr = w_branch.astype(BF16)
    wout = w_out.astype(BF16)
    ml_bias = jnp.concatenate([ml_gate_b.reshape(depth, N_GATES), jnp.zeros((depth, N_GATES), F32)], axis=-1)
    ml_bias_c, ml_bias_r = ml_bias[:, None, :], ml_bias[:, :, None]
    ml_g = ml_norm_g[:, None, :]
    dn_g = jnp.tile(dn_norm_g, (1, DN_HEADS))[:, None, :]
    lru_w = jnp.stack([jnp.concatenate([jax.vmap(_block_diag)(lru_w_a[:, dd]), jax.vmap(_block_diag)(lru_w_x[:, dd])],
                                       axis=-1) for dd in range(2)], axis=1).astype(BF16)
    lru_b = jnp.concatenate([lru_b_a, lru_b_x], axis=-1)[:, :, None, :]
    lru_lam = lru_lambda[:, :, None, :]
    lru_cb = lru_conv_b[:, None, :]
    dn_par = jnp.repeat(jnp.stack([dn_a_log, dn_dt_bias], axis=-1), CHUNK, axis=2)
    dn_acol, dn_arow = dn_par, dn_par.transpose(0, 1, 3, 2)

    xz = jnp.concatenate([x, ctx], axis=1)
    for l in range(depth):
        xz = _ffn(xz, mod, norm_g, wgu, wdn, l, 0, n_lat_tiles, ctx_row, n_tiles)
        ml, lru, dnqkv, dnz, mg, gates = _inproj(xz, mod, norm_g, w_main, w_gate, widths, l, n_lat_tiles, ctx_row)
        ml_hf, ml_hb = _mlstm(ml, gates, gates.transpose(0, 2, 1), ml_bias_c, ml_bias_r, l,
                              n_lat_tiles, n_ctx_tiles)
        lr_hf, lr_hb = _lru(lru, lru_conv_w, lru_cb, lru_w, lru_b, lru_lam, l, n_lat_tiles, n_ctx_tiles)
        gcol, grow = _dn_gate_layout(gates[:, :, N_GATES:], n_lat)
        prep = _dn_prep(dnqkv, gcol, grow, dn_conv_w, dn_acol, dn_arow, l, n_lat)
        dn_lf, dn_lb, dn_cf, dn_cb = _dn_scan(prep, n_lat // CHUNK, n_ctx // CHUNK)
        xz = _merge(xz, mod, norm_g, ml_hf, ml_hb, ml, lr_hf, lr_hb, lru, dn_lf, dn_lb, dn_cf, dn_cb, dnz, mg,
                    ml_g, dn_g, wbr, wout, l, n_lat_tiles, ctx_row)
        xz = _ffn(xz, mod, norm_g, wgu, wdn, l, 2, n_lat_tiles, ctx_row,
                  n_lat_tiles if l == depth - 1 else n_tiles)
    return xz
```

```python
import functools
import math

import jax
import jax.numpy as jnp
from jax import lax
from jax.experimental import pallas as pl
from jax.experimental.pallas import tpu as pltpu

F32 = jnp.float32
BF16 = jnp.bfloat16

EPS = 1e-6
N_MOD = 9
N_BRANCH = 3
CONV_W = 4
CONV_LEFT = 2
ML_HEADS = 4
LRU_C = 8.0
DN_HEADS = 4
CHUNK = 64
DN_GROUP = 4
N_GATES = 16

V7X_SUBLANES = 8
V7X_LANES = 128
V7X_VMEM_BYTES = 64 * 1024 * 1024
VMEM_LIMIT = V7X_VMEM_BYTES - 8 * 1024 * 1024

TM = 256
NEG = -1e30


def _sigmoid(x):
    return 1.0 / (1.0 + jnp.exp(-x))


def _silu(x):
    return x * _sigmoid(x)


def _softplus(x):
    return jnp.maximum(x, 0.0) + jnp.log(1.0 + jnp.exp(-jnp.abs(x)))


def _rms(x):
    return x * lax.rsqrt(jnp.mean(x * x, axis=-1, keepdims=True) + EPS)


def _rms_heads(x, n_heads):
    hd = x.shape[-1] // n_heads
    return jnp.concatenate([_rms(x[:, h * hd:(h + 1) * hd]) for h in range(n_heads)], axis=-1)


def _gelu_tanh(x):
    return 0.5 * x * (1.0 + jnp.tanh(math.sqrt(2.0 / math.pi) * (x + 0.044715 * (x * x * x))))


def _dot(a, b):
    return jnp.dot(a, b, preferred_element_type=F32)


def _dot_nt(a, b):
    return lax.dot_general(a, b, (((1,), (1,)), ((), ())), preferred_element_type=F32)


def _dot_tn(a, b):
    return lax.dot_general(a, b, (((0,), (0,)), ((), ())), preferred_element_type=F32)


def _params(*sem):
    return pltpu.CompilerParams(dimension_semantics=sem, vmem_limit_bytes=VMEM_LIMIT)


def _const_spec(block, index):
    return pl.BlockSpec(block, lambda *_: index, pipeline_mode=pl.Buffered(1))


def _stack_heads(x, n):
    w = x.shape[1] // n
    return jnp.concatenate([x[:, h * w:(h + 1) * w] for h in range(n)], axis=0)


def _unstack_heads(x, n):
    t = x.shape[0] // n
    return jnp.concatenate([x[h * t:(h + 1) * t] for h in range(n)], axis=1)


def _cumsum_groups(x, axis, period, rev):
    n = x.shape[axis]
    idx = lax.broadcasted_iota(jnp.int32, x.shape, axis) % period
    sh = 1
    while sh < period:
        if rev:
            x = x + jnp.where(idx < period - sh, pltpu.roll(x, n - sh, axis=axis), 0.0)
        else:
            x = x + jnp.where(idx >= sh, pltpu.roll(x, sh, axis=axis), 0.0)
        sh *= 2
    return x


def _shift_rows(x, prev8, nxt8, off):
    t = x.shape[0]
    row8 = lax.broadcasted_iota(jnp.int32, (V7X_SUBLANES, 1), 0)
    rolled = pltpu.roll(x, (-off) % t, axis=0)
    if off < 0:
        edge = jnp.where(row8 < -off, pltpu.roll(prev8, -off, axis=0), rolled[:V7X_SUBLANES])
        return jnp.concatenate([edge, rolled[V7X_SUBLANES:]], axis=0)
    edge = jnp.where(row8 >= V7X_SUBLANES - off, pltpu.roll(nxt8, V7X_SUBLANES - off, axis=0),
                     rolled[t - V7X_SUBLANES:])
    return jnp.concatenate([rolled[:t - V7X_SUBLANES], edge], axis=0)


def _conv4(x, prev8, nxt8, w):
    acc = x * w[CONV_LEFT:CONV_LEFT + 1]
    for j in range(CONV_W):
        if j != CONV_LEFT:
            acc = acc + _shift_rows(x, prev8, nxt8, j - CONV_LEFT) * w[j:j + 1]
    return acc


def _mod_kernel(c_ref, w_ref, b_ref, o_ref):
    s = _silu(c_ref[...]).astype(BF16)
    o_ref[...] = _dot(s, w_ref[...].astype(BF16)) + b_ref[...]


def _mod_table(cc, w_mod, b_mod):
    depth, d, nd = w_mod.shape
    r = cc.shape[0]
    tn = nd // 4
    return pl.pallas_call(
        _mod_kernel,
        grid=(depth, nd // tn),
        in_specs=[pl.BlockSpec((r, d), lambda l, j: (0, 0)),
                  pl.BlockSpec((None, d, tn), lambda l, j: (l, 0, j)),
                  pl.BlockSpec((None, 1, tn), lambda l, j: (l, 0, j))],
        out_specs=pl.BlockSpec((None, r, tn), lambda l, j: (l, 0, j)),
        out_shape=jax.ShapeDtypeStruct((depth, r, nd), F32),
        compiler_params=_params("parallel", "parallel"),
        name="mod_table",
    )(cc, w_mod, b_mod.reshape(depth, 1, nd))


def _tok_spec(width, col=0):
    return pl.BlockSpec((None, TM, width), lambda bi, t: (bi, t, col))


def _mod_spec(l, d, n_lat_tiles, ctx_row):
    return pl.BlockSpec((None, None, N_MOD, d),
                        lambda bi, t: (l, jnp.where(t >= n_lat_tiles, ctx_row, bi), 0, 0))


def _ffn_kernel(x_ref, mod_ref, g_ref, wgu_ref, wd_ref, o_ref, *, j, dff):
    x = x_ref[...]
    shift, scale, gate = (mod_ref[3 * j + i:3 * j + i + 1, :] for i in range(3))
    g_pre, g_post = g_ref[2 * j:2 * j + 1, :], g_ref[2 * j + 1:2 * j + 2, :]
    h = (_rms(x) * g_pre * (1.0 + scale) + shift).astype(BF16)
    gu = _dot(h, wgu_ref[...])
    a = (_silu(gu[:, :dff]) * gu[:, dff:]).astype(BF16)
    y = _dot(a, wd_ref[...])
    o_ref[...] = x + 0.5 * gate * (_rms(y) * g_post)


def _ffn(xz, mod, norm_g, wgu, wd, l, j, n_lat_tiles, ctx_row, n_out_tiles):
    b, _, d = xz.shape
    dff = wd.shape[2]
    return pl.pallas_call(
        functools.partial(_ffn_kernel, j=j, dff=dff),
        grid=(b, n_out_tiles),
        in_specs=[_tok_spec(d),
                  _mod_spec(l, d, n_lat_tiles, ctx_row),
                  _const_spec((None,) + norm_g.shape[1:], (l, 0, 0)),
                  _const_spec((None, None, d, 2 * dff), (l, j // 2, 0, 0)),
                  _const_spec((None, None, dff, d), (l, j // 2, 0, 0))],
        out_specs=_tok_spec(d),
        out_shape=jax.ShapeDtypeStruct((b, n_out_tiles * TM, d), F32),
        compiler_params=_params("parallel", "parallel"),
        name=f"ffn{j}",
    )(xz, mod, norm_g, wgu, wd)


def _inproj_kernel(x_ref, mod_ref, g_ref, w_ref, wg_ref, *o_refs):
    shift, scale = mod_ref[3:4, :], mod_ref[4:5, :]
    h = (_rms(x_ref[...]) * g_ref[2:3, :] * (1.0 + scale) + shift).astype(BF16)
    col = 0
    for o_ref in o_refs[:-1]:
        n = o_ref.shape[-1]
        o_ref[...] = _dot(h, w_ref[:, col:col + n]).astype(o_ref.dtype)
        col += n
    o_refs[-1][...] = _dot(h, wg_ref[...])


def _inproj(xz, mod, norm_g, w_main, w_gate, widths, l, n_lat_tiles, ctx_row):
    b, lt, d = xz.shape
    ng = w_gate.shape[-1]
    return pl.pallas_call(
        _inproj_kernel,
        grid=(b, lt // TM),
        in_specs=[_tok_spec(d),
                  _mod_spec(l, d, n_lat_tiles, ctx_row),
                  _const_spec((None,) + norm_g.shape[1:], (l, 0, 0)),
                  _const_spec((None, d, w_main.shape[-1]), (l, 0, 0)),
                  _const_spec((None, d, ng), (l, 0, 0))],
        out_specs=[_tok_spec(w) for w in widths] + [_tok_spec(ng)],
        out_shape=[jax.ShapeDtypeStruct((b, lt, w), BF16) for w in widths]
                  + [jax.ShapeDtypeStruct((b, lt, ng), F32)],
        compiler_params=_params("parallel", "parallel"),
        name="inproj",
    )(xz, mod, norm_g, w_main, w_gate)


def _merge_kernel(x_ref, mod_ref, g_ref, mlf_ref, mlb_ref, mlo_ref, lrf_ref, lrb_ref, lry_ref,
                  dnf_ref, dnb_ref, dnz_ref, mg_ref, mlg_ref, dng_ref, wb_ref, wo_ref, o_ref):
    d = x_ref.shape[-1]
    f32 = lambda r: r[...].astype(F32)
    y_ml = _rms_heads(f32(mlf_ref) + f32(mlb_ref), ML_HEADS) * mlg_ref[...] * _sigmoid(f32(mlo_ref))
    y_lr = (f32(lrf_ref) + f32(lrb_ref)) * _gelu_tanh(f32(lry_ref))
    y_dn = _rms_heads(f32(dnf_ref) + f32(dnb_ref), DN_HEADS) * dng_ref[...] * _silu(f32(dnz_ref))
    mix = None
    for n, y in enumerate((y_ml, y_lr, y_dn)):
        term = _sigmoid(mg_ref[:, n * d:(n + 1) * d].astype(F32)) * _dot(y.astype(BF16), wb_ref[n])
        mix = term if mix is None else mix + term
    out = _dot(mix.astype(BF16), wo_ref[...])
    o_ref[...] = x_ref[...] + mod_ref[5:6, :] * (_rms(out) * g_ref[3:4, :])


def _merge(xz, mod, norm_g, ml_hf, ml_hb, ml, lr_hf, lr_hb, lru, dn_f, dn_b, dnz, mg,
           ml_g, dn_g, w_branch, w_out, l, n_lat_tiles, ctx_row):
    b, lt, d = xz.shape
    bw = w_branch.shape[2]
    return pl.pallas_call(
        _merge_kernel,
        grid=(b, lt // TM),
        in_specs=[_tok_spec(d),
                  _mod_spec(l, d, n_lat_tiles, ctx_row),
                  _const_spec((None,) + norm_g.shape[1:], (l, 0, 0)),
                  _tok_spec(bw), _tok_spec(bw), _tok_spec(bw, ml.shape[-1] // bw - 1),
                  _tok_spec(bw), _tok_spec(bw), _tok_spec(bw, lru.shape[-1] // bw - 1),
                  _tok_spec(bw), _tok_spec(bw), _tok_spec(bw),
                  _tok_spec(N_BRANCH * d),
                  _const_spec((None, 1, bw), (l, 0, 0)),
                  _const_spec((None, 1, bw), (l, 0, 0)),
                  _const_spec((None, N_BRANCH, bw, d), (l, 0, 0, 0)),
                  _const_spec((None, d, d), (l, 0, 0))],
        out_specs=_tok_spec(d),
        out_shape=jax.ShapeDtypeStruct(xz.shape, F32),
        compiler_params=_params("parallel", "parallel"),
        name="merge",
    )(xz, mod, norm_g, ml_hf, ml_hb, ml, lr_hf, lr_hb, lru, dn_f, dn_b, dnz, mg,
      ml_g, dn_g, w_branch, w_out)


def _log_scan(a, b, idx, period, rev):
    n = a.shape[0]
    sh = 1
    while sh < period:
        if rev:
            ok = idx < period - sh
            ar, br = pltpu.roll(a, n - sh, axis=0), pltpu.roll(b, n - sh, axis=0)
        else:
            ok = idx >= sh
            ar, br = pltpu.roll(a, sh, axis=0), pltpu.roll(b, sh, axis=0)
        b = b + a * jnp.where(ok, br, 0.0)
        a = a * jnp.where(ok, ar, 1.0)
        sh *= 2
    return a, b


def _tile_scan(a, b, h0, rev, sa_ref, sb_ref):
    t = a.shape[0]
    g = t // V7X_SUBLANES
    row = lax.broadcasted_iota(jnp.int32, (t, 1), 0)
    a, b = _log_scan(a, b, row % V7X_SUBLANES, V7X_SUBLANES, rev)
    edge = 0 if rev else V7X_SUBLANES - 1

    def group_edges(ref, val):
        parts = []
        for j in range(ref.shape[0]):
            ref[j] = val[:, j * V7X_LANES:(j + 1) * V7X_LANES]
            parts.append(ref[j, pl.ds(edge, g, stride=V7X_SUBLANES), :])
        return jnp.concatenate(parts, axis=1)

    ga = group_edges(sa_ref, a)
    gb = group_edges(sb_ref, b)
    grow = lax.broadcasted_iota(jnp.int32, (g, 1), 0)
    ga, gb = _log_scan(ga, gb, grow, g, rev)
    h_end = gb + ga * h0
    if rev:
        h_in = jnp.where(grow == g - 1, h0, pltpu.roll(h_end, g - 1, axis=0))
        carry = h_end[0:1]
    else:
        h_in = jnp.where(grow == 0, h0, pltpu.roll(h_end, 1, axis=0))
        carry = h_end[g - 1:g]
    out = jnp.concatenate(
        [b[V7X_SUBLANES * i:V7X_SUBLANES * (i + 1)] + a[V7X_SUBLANES * i:V7X_SUBLANES * (i + 1)] * h_in[i:i + 1]
         for i in range(g)], axis=0)
    return out, carry


def _lru_kernel(x_ref, cw_ref, cb_ref, w_ref, bias_ref, lam_ref, of_ref, ob_ref, *scratch, n_lat, n_ctx):
    t, c = TM, x_ref.shape[-1]
    n_tiles = n_lat + n_ctx
    halo = 2 * V7X_SUBLANES

    def load_conv(tile):
        r0 = pl.multiple_of(tile * t, t)
        first = jnp.logical_or(tile == 0, tile == n_lat)
        last = jnp.logical_or(tile == n_lat - 1, tile == n_tiles - 1)
        x = x_ref[pl.ds(r0, t), :].astype(F32)
        p0 = pl.multiple_of(jnp.maximum(r0 - halo, 0), halo)
        n0 = pl.multiple_of(jnp.minimum(r0 + t, n_tiles * t - halo), halo)
        prev8 = jnp.where(first, 0.0, x_ref[pl.ds(p0, halo), :].astype(F32)[V7X_SUBLANES:])
        nxt8 = jnp.where(last, 0.0, x_ref[pl.ds(n0, halo), :].astype(F32)[:V7X_SUBLANES])
        return r0, _conv4(x, prev8, nxt8, cw_ref[...]) + cb_ref[...]

    def direction(d, tile, h0, o_ref):
        r0, xc = load_conv(tile)
        z = _dot(xc.astype(BF16), w_ref[d]) + bias_ref[d]
        r, i = _sigmoid(z[:, :c]), _sigmoid(z[:, c:])
        la = (-LRU_C * _softplus(-lam_ref[d])) * r
        a = jnp.exp(la)
        bx = jnp.sqrt(jnp.tanh(-la) * (1.0 + a * a)) * (i * xc)
        h, carry = _tile_scan(a, bx, h0, d == 1, scratch[2 * d], scratch[2 * d + 1])
        o_ref[pl.ds(r0, t), :] = h.astype(o_ref.dtype)
        return carry

    def step(s, carry):
        hf, hb = carry
        hf = direction(0, jnp.where(s < n_ctx, n_lat + s, s - n_ctx), hf, of_ref)
        hb = direction(1, n_tiles - 1 - s, hb, ob_ref)
        return hf, hb

    zero = jnp.zeros((1, c), F32)
    lax.fori_loop(0, n_tiles, step, (zero, zero))


def _lru(lru, conv_w, conv_b, w_gates, b_gates, lam, l, n_lat, n_ctx):
    b, lt, _ = lru.shape
    c = conv_w.shape[-1]
    seq = pl.BlockSpec((None, lt, c), lambda bi: (bi, 0, 0))
    return pl.pallas_call(
        functools.partial(_lru_kernel, n_lat=n_lat, n_ctx=n_ctx),
        grid=(b,),
        in_specs=[seq,
                  _const_spec((None, CONV_W, c), (l, 0, 0)),
                  _const_spec((None, 1, c), (l, 0, 0)),
                  _const_spec((None, 2, c, 2 * c), (l, 0, 0, 0)),
                  _const_spec((None, 2, 1, 2 * c), (l, 0, 0, 0)),
                  _const_spec((None, 2, 1, c), (l, 0, 0, 0))],
        out_specs=[seq, seq],
        out_shape=[jax.ShapeDtypeStruct((b, lt, c), BF16)] * 2,
        scratch_shapes=[pltpu.VMEM((c // V7X_LANES, TM, V7X_LANES), F32)] * 4,
        compiler_params=_params("parallel"),
        name="lru",
    )(lru, conv_w, conv_b, w_gates, b_gates, lam)


def _mlstm_kernel(qf_ref, kf_ref, vf_ref, qb_ref, kb_ref, vb_ref, gcf_ref, gcb_ref, grf_ref, grb_ref,
                  bc_ref, br_ref, hf_ref, hb_ref, c_ref, m_ref):
    @pl.when(pl.program_id(1) == 0)
    def _():
        c_ref[...] = jnp.zeros(c_ref.shape, F32)
        m_ref[...] = jnp.zeros(m_ref.shape, F32)

    tc, nqk = qf_ref.shape
    dk = nqk // ML_HEADS
    dv = vf_ref.shape[1] // ML_HEADS
    ri = lax.broadcasted_iota(jnp.int32, (tc, tc), 0)
    ci = lax.broadcasted_iota(jnp.int32, (tc, tc), 1)
    lane_head = lax.broadcasted_iota(jnp.int32, (1, nqk), 1) // dk
    ones = jnp.ones((tc, dv), BF16)
    dirs = ((qf_ref, kf_ref, vf_ref, gcf_ref, grf_ref, hf_ref),
            (qb_ref, kb_ref, vb_ref, gcb_ref, grb_ref, hb_ref))
    for d, (q_ref, k_ref, v_ref, gc_ref, gr_ref, o_ref) in enumerate(dirs):
        rev = d == 1
        causal = (ci >= ri) if rev else (ci <= ri)
        gc = gc_ref[...] + bc_ref[...]
        gr = gr_ref[...] + br_ref[...]
        i_rows = gr[ML_HEADS * d:ML_HEADS * (d + 1)]
        fo = 2 * ML_HEADS + ML_HEADS * d
        b_cols = _cumsum_groups(-_softplus(-gc[:, fo:fo + ML_HEADS]), 0, tc, rev)
        b_rows = _cumsum_groups(-_softplus(-gr[fo:fo + ML_HEADS]), 1, tc, rev)
        q_all = q_ref[...] * (dk ** -0.5)
        k_all = k_ref[...]
        v_all = v_ref[...]
        kt_all = k_all.astype(F32).T
        c_all = c_ref[d]
        cb_all = c_all.astype(BF16)
        outs = []
        for h in range(ML_HEADS):
            r = ML_HEADS * d + h
            qh = jnp.where(lane_head == h, q_all, jnp.zeros_like(q_all))
            vp = jnp.concatenate([v_all[:, h * dv:(h + 1) * dv], ones], axis=1)
            b_c, b_r, i_r = b_cols[:, h:h + 1], b_rows[h:h + 1], i_rows[h:h + 1]
            m_prev = m_ref[r:r + 1, 0:1]
            logd = jnp.where(causal, b_c - b_r + i_r, NEG)
            inter = b_c + m_prev
            m_t = jnp.maximum(inter, jnp.max(logd, axis=1, keepdims=True))
            p = _dot_nt(qh, k_all) * jnp.exp(logd - m_t)
            num = _dot(p.astype(BF16), vp) + jnp.exp(inter - m_t) * _dot(qh, cb_all)
            outs.append(num[:, :dv] / jnp.maximum(jnp.abs(num[:, dv:]), jnp.exp(-m_t)))
            b_end = b_c[0:1] if rev else b_c[tc - 1:tc]
            lws = b_end - b_r + i_r
            m_new = jnp.maximum(b_end + m_prev, jnp.max(lws, axis=1, keepdims=True))
            kw = (kt_all[h * dk:(h + 1) * dk] * jnp.exp(lws - m_new)).astype(BF16)
            c_ref[d, h * dk:(h + 1) * dk, :] = (jnp.exp(b_end + m_prev - m_new) * c_all[h * dk:(h + 1) * dk]
                                                + _dot(kw, vp))
            m_ref[r:r + 1, :] = jnp.broadcast_to(m_new, (1, m_ref.shape[1]))
        o_ref[...] = jnp.concatenate(outs, axis=1).astype(o_ref.dtype)


def _mlstm(ml, gates, gates_t, bias_c, bias_r, l, n_lat, n_ctx):
    b, lt, _ = ml.shape
    ng = gates.shape[-1]
    n_tiles = n_lat + n_ctx
    dqk = ml.shape[-1] // 6
    fwd = lambda s: jnp.where(s < n_ctx, n_lat + s, s - n_ctx)
    bwd = lambda s: n_tiles - 1 - s

    def specs(tile):
        return [pl.BlockSpec((None, TM, dqk), lambda bi, s: (bi, tile(s), 0)),
                pl.BlockSpec((None, TM, dqk), lambda bi, s: (bi, tile(s), 1)),
                pl.BlockSpec((None, TM, 2 * dqk), lambda bi, s: (bi, tile(s), 1))]

    col = lambda tile: pl.BlockSpec((None, TM, ng), lambda bi, s: (bi, tile(s), 0))
    row = lambda tile: pl.BlockSpec((None, ng, TM), lambda bi, s: (bi, 0, tile(s)))
    out = lambda tile: pl.BlockSpec((None, TM, 2 * dqk), lambda bi, s: (bi, tile(s), 0))
    return pl.pallas_call(
        _mlstm_kernel,
        grid=(b, n_tiles),
        in_specs=specs(fwd) + specs(bwd) + [col(fwd), col(bwd), row(fwd), row(bwd),
                                            _const_spec((None, 1, ng), (l, 0, 0)),
                                            _const_spec((None, ng, 1), (l, 0, 0))],
        out_specs=[out(fwd), out(bwd)],
        out_shape=[jax.ShapeDtypeStruct((b, lt, 2 * dqk), BF16)] * 2,
        scratch_shapes=[pltpu.VMEM((2, dqk, 2 * (2 * dqk // ML_HEADS)), F32),
                        pltpu.VMEM((2 * ML_HEADS, V7X_LANES), F32)],
        compiler_params=_params("parallel", "arbitrary"),
        name="mlstm",
    )(ml, ml, ml, ml, ml, ml, gates, gates, gates_t, gates_t, bias_c, bias_r)


INV_BASE = 8


def _inv_masks(ri, ci):
    blk = lambda s: (ri // s) == (ci // s)
    as_bf = lambda m: jnp.where(m, 1.0, 0.0).astype(BF16)
    masks, s = [as_bf(blk(INV_BASE))], INV_BASE
    while s < CHUNK:
        masks.append(as_bf(jnp.logical_and(blk(2 * s), jnp.logical_not(blk(s)))))
        s *= 2
    return masks


def _tri_inv_many(a_list, eye, masks):
    ps = [a * masks[0] for a in a_list]
    xs = [eye - p.astype(F32) for p in ps]
    for _ in range(INV_BASE.bit_length() - 2):
        ps = [_dot(p, p).astype(BF16) for p in ps]
        xs = [x + _dot(x.astype(BF16), p) for x, p in zip(xs, ps)]
    for m in masks[1:]:
        xbs = [x.astype(BF16) for x in xs]
        ys = [_dot(xb, a * m).astype(BF16) for xb, a in zip(xbs, a_list)]
        xs = [x - _dot(y, xb) for x, y, xb in zip(xs, ys, xbs)]
    return xs


def _dn_prep_kernel(x_ref, xp_ref, xn_ref, grow_ref, cw_ref, arow_ref,
                    u_ref, w_ref, qe_ref, kd_ref, qk_ref, eg_ref, *, first_groups, last_groups):
    g = pl.program_id(1)
    width = x_ref.shape[-1]
    dh = width // (3 * DN_HEADS)
    n = DN_HEADS * CHUNK
    halo = 2 * V7X_SUBLANES
    any_of = lambda groups: functools.reduce(jnp.logical_or, [g == v for v in groups])
    x = x_ref[...].reshape(DN_GROUP * CHUNK, width).astype(F32)
    prev8 = jnp.where(any_of(first_groups), 0.0, xp_ref[CHUNK - halo:, :].astype(F32)[V7X_SUBLANES:])
    nxt8 = jnp.where(any_of(last_groups), 0.0, xn_ref[:halo, :].astype(F32)[:V7X_SUBLANES])
    xc = _silu(_conv4(x, prev8, nxt8, cw_ref[...]))

    ri = lax.broadcasted_iota(jnp.int32, (n, n), 0)
    ci = lax.broadcasted_iota(jnp.int32, (n, n), 1)
    same = (ri // CHUNK) == (ci // CHUNK)
    eye = jnp.where(ri == ci, 1.0, 0.0)
    incl = (jnp.logical_and(same, ci <= ri), jnp.logical_and(same, ci >= ri))
    strict = (jnp.logical_and(same, ci < ri), jnp.logical_and(same, ci > ri))
    masks = _inv_masks(ri, ci)

    qn, kn, v, kk, qk = [], [], [], [], []
    for i in range(DN_GROUP):
        xi = xc[i * CHUNK:(i + 1) * CHUNK]
        q, k, vi = (_stack_heads(xi[:, j * DN_HEADS * dh:(j + 1) * DN_HEADS * dh], DN_HEADS) for j in range(3))
        qn.append(q * lax.rsqrt(jnp.sum(q * q, axis=-1, keepdims=True) + EPS) * (dh ** -0.5))
        kn.append(k * lax.rsqrt(jnp.sum(k * k, axis=-1, keepdims=True) + EPS))
        v.append(vi)
        kb = kn[i].astype(BF16)
        kk.append(_dot_nt(kb, kb))
        qk.append(_dot_nt(qn[i].astype(BF16), kb))

    systems = [(i, d) for i in range(DN_GROUP) for d in range(2)]
    rows, cs_rows = [], []
    for i, d in systems:
        raw = grow_ref[i]
        beta = _sigmoid(raw[d:d + 1])
        g_r = -jnp.exp(arow_ref[d, 0:1, :]) * _softplus(raw[2 + d:3 + d] + arow_ref[d, 1:2, :])
        cs = _cumsum_groups(g_r, 1, CHUNK, d == 1)
        tot = cs + _cumsum_groups(g_r, 1, CHUNK, d == 0) - g_r
        eg = jnp.exp(cs)
        rows += [cs, beta, eg, jnp.exp(tot - cs), beta * eg]
        cs_rows.append(cs)
        eg_ref[d, i] = jnp.exp(tot)
    n_col = len(rows) // len(systems)
    cols = jnp.concatenate(rows, axis=0).T

    a_list, rhs = [], []
    for s, (i, d) in enumerate(systems):
        cs_c, beta_c, eg_c, ekd_c, beg_c = (cols[:, n_col * s + j:n_col * s + j + 1] for j in range(n_col))
        gam = jnp.exp(jnp.where(incl[d], cs_c - cs_rows[s], NEG))
        a_list.append((jnp.where(strict[d], kk[i], 0.0) * gam * beta_c).astype(BF16))
        rhs.append(jnp.concatenate([beta_c * v[i], beg_c * kn[i]], axis=1).astype(BF16))
        qk_ref[d, i] = (qk[i] * gam).astype(qk_ref.dtype)
        qe_ref[d, i] = (qn[i] * eg_c).astype(qe_ref.dtype)
        kd_ref[d, i] = (kn[i] * ekd_c).astype(kd_ref.dtype)
    for (i, d), tinv, r in zip(systems, _tri_inv_many(a_list, eye, masks), rhs):
        sol = _dot(tinv.astype(BF16), r)
        u_ref[d, i] = sol[:, :dh].astype(u_ref.dtype)
        w_ref[d, i] = sol[:, dh:].astype(w_ref.dtype)


def _dn_prep(xs, grow, conv_w, arow, l, n_lat_groups):
    b, nc, _, width = xs.shape
    n_groups = nc // DN_GROUP
    n = DN_HEADS * CHUNK
    dh = width // (3 * DN_HEADS)
    out = lambda w, dt: (jax.ShapeDtypeStruct((b, 2, nc, n, w), dt),
                         pl.BlockSpec((None, 2, DN_GROUP, n, w), lambda bi, g: (bi, 0, g, 0, 0)))
    outs = [out(dh, BF16)] * 4 + [out(n, BF16)]
    outs.append((jax.ShapeDtypeStruct((b, 2, nc, 1, n), F32),
                 pl.BlockSpec((None, 2, DN_GROUP, 1, n), lambda bi, g: (bi, 0, g, 0, 0))))
    chunk = lambda idx: pl.BlockSpec((None, None, CHUNK, width), lambda bi, g: (bi, idx(g), 0, 0))
    return pl.pallas_call(
        functools.partial(_dn_prep_kernel, first_groups=(0, n_lat_groups),
                          last_groups=(n_lat_groups - 1, n_groups - 1)),
        grid=(b, n_groups),
        in_specs=[pl.BlockSpec((None, DN_GROUP, CHUNK, width), lambda bi, g: (bi, g, 0, 0)),
                  chunk(lambda g: jnp.maximum(g * DN_GROUP - 1, 0)),
                  chunk(lambda g: jnp.minimum(g * DN_GROUP + DN_GROUP, nc - 1)),
                  pl.BlockSpec((None, DN_GROUP, 4, n), lambda bi, g: (bi, g, 0, 0)),
                  _const_spec((None, CONV_W, width), (l, 0, 0)),
                  _const_spec((None, 2, 2, n), (l, 0, 0, 0))],
        out_specs=[o[1] for o in outs],
        out_shape=[o[0] for o in outs],
        compiler_params=_params("parallel", "parallel"),
        name="dn_prep",
    )(xs, xs, xs, grow, conv_w, arow)


def _dn_scan_kernel(*refs):
    ins, (of_ref, ob_ref, s_ref) = refs[:12], refs[12:]

    @pl.when(pl.program_id(1) == 0)
    def _():
        s_ref[...] = jnp.zeros(s_ref.shape, F32)

    for d, o_ref in enumerate((of_ref, ob_ref)):
        u_ref, w_ref, qe_ref, kd_ref, qk_ref, eg_ref = ins[6 * d:6 * d + 6]
        w, qe, kd = w_ref[...], qe_ref[...], kd_ref[...]
        ws, qs = [], []
        for h in range(DN_HEADS):
            rows = slice(h * CHUNK, (h + 1) * CHUNK)
            sb = s_ref[d, h].astype(BF16)
            ws.append(_dot(w[rows], sb))
            qs.append(_dot(qe[rows], sb))
        vnew = (u_ref[...].astype(F32) - jnp.concatenate(ws, axis=0)).astype(BF16)
        o = jnp.concatenate(qs, axis=0) + _dot(qk_ref[...], vnew)
        for h in range(DN_HEADS):
            rows = slice(h * CHUNK, (h + 1) * CHUNK)
            s_ref[d, h] = eg_ref[:, h * CHUNK:h * CHUNK + 1] * s_ref[d, h] + _dot_tn(kd[rows], vnew[rows])
        o_ref[...] = _unstack_heads(o, DN_HEADS).astype(o_ref.dtype)


def _dn_scan(prep, n_lat_chunks):
    b, _, nc, _, dh = prep[0].shape
    ow = DN_HEADS * dh
    n_ctx_chunks = nc - n_lat_chunks
    fwd = lambda s: jnp.where(s < n_ctx_chunks, n_lat_chunks + s, s - n_ctx_chunks)
    bwd = lambda s: nc - 1 - s

    def spec(a, d, chunk):
        return pl.BlockSpec((None, None, None) + a.shape[3:], lambda bi, s: (bi, d, chunk(s), 0, 0))

    out = lambda chunk: pl.BlockSpec((None, None, CHUNK, ow), lambda bi, s: (bi, chunk(s), 0, 0))
    return pl.pallas_call(
        _dn_scan_kernel,
        grid=(b, nc),
        in_specs=[spec(a, 0, fwd) for a in prep] + [spec(a, 1, bwd) for a in prep],
        out_specs=[out(fwd), out(bwd)],
        out_shape=[jax.ShapeDtypeStruct((b, nc, CHUNK, ow), BF16)] * 2,
        scratch_shapes=[pltpu.VMEM((2, DN_HEADS, dh, dh), F32)],
        compiler_params=_params("parallel", "arbitrary"),
        name="dn_scan",
    )(*prep, *prep)


def _block_diag(w):
    n, i, j = w.shape
    return jnp.einsum('nij,nm->nimj', w, jnp.eye(n, dtype=w.dtype)).reshape(n * i, n * j)


def _to_chunks(a, n_lat):
    b, _, w = a.shape
    lat = a[:, :n_lat].reshape(b, n_lat // CHUNK, CHUNK, w).swapaxes(1, 2)
    return jnp.concatenate([lat, a[:, n_lat:].reshape(b, -1, CHUNK, w)], axis=1)


def _from_chunks(a, n_lat):
    b, _, _, w = a.shape
    lat = a[:, :n_lat // CHUNK].swapaxes(1, 2).reshape(b, n_lat, w)
    return jnp.concatenate([lat, a[:, n_lat // CHUNK:].reshape(b, -1, w)], axis=1)


def _dn_gate_rows(raw, n_lat):
    b = raw.shape[0]
    lat = raw[:, :n_lat].reshape(b, n_lat // CHUNK, CHUNK, 4, DN_HEADS)
    ctx = raw[:, n_lat:].reshape(b, -1, CHUNK, 4, DN_HEADS)
    row = jnp.concatenate([lat.transpose(0, 2, 3, 4, 1), ctx.transpose(0, 1, 3, 4, 2)], axis=1)
    return row.reshape(b, row.shape[1], 4, DN_HEADS * CHUNK)


def kernel(x, c, ctx, c_ctx, w_mod, b_mod, norm_g, ffn_w_gu, ffn_w_down, w_in, ml_gate_b, ml_norm_g,
           lru_conv_w, lru_conv_b, lru_w_a, lru_b_a, lru_w_x, lru_b_x, lru_lambda, dn_conv_w,
           dn_a_log, dn_dt_bias, dn_norm_g, w_branch, w_out):
    b, n_lat, d = x.shape
    n_ctx = ctx.shape[1]
    depth = w_mod.shape[0]
    bw = w_branch.shape[2]
    assert n_lat == CHUNK * CHUNK and n_lat % TM == 0 and n_ctx % TM == 0 and n_ctx % (CHUNK * DN_GROUP) == 0
    n_lat_tiles, n_ctx_tiles = n_lat // TM, n_ctx // TM
    n_tiles = n_lat_tiles + n_ctx_tiles

    ctx_row = b
    n_rows = -(-(b + 1) // V7X_SUBLANES) * V7X_SUBLANES
    cc = jnp.zeros((n_rows, d), F32).at[:b].set(c).at[b].set(c_ctx)
    mod = _mod_table(cc, w_mod, b_mod).reshape(depth, n_rows, N_MOD, d)

    dqk = bw // 2
    edges = [0]
    for wdt in (dqk, dqk, bw, bw, N_GATES, bw, bw, bw, bw, bw, bw, N_GATES, N_BRANCH * d):
        edges.append(edges[-1] + wdt)
    piece = lambda i, j: w_in[:, :, edges[i]:edges[j]]
    widths = (3 * bw, 2 * bw, 3 * bw, bw, N_BRANCH * d)
    w_main = jnp.concatenate([piece(0, 4), piece(5, 7), piece(7, 11), piece(12, 13)], axis=-1).astype(BF16)
    w_gate = jnp.concatenate([piece(4, 5), piece(11, 12)], axis=-1).astype(BF16)
    wgu = ffn_w_gu.astype(BF16)
    wdn = ffn_w_down.astype(BF16)
    wbr = w_branch.astype(BF16)
    wout = w_out.astype(BF16)
    ml_bias = jnp.concatenate([ml_gate_b.reshape(depth, N_GATES), jnp.zeros((depth, N_GATES), F32)], axis=-1)
    ml_bias_c, ml_bias_r = ml_bias[:, None, :], ml_bias[:, :, None]
    ml_g = ml_norm_g[:, None, :]
    dn_g = jnp.tile(dn_norm_g, (1, DN_HEADS))[:, None, :]
    lru_w = jnp.stack([jnp.concatenate([jax.vmap(_block_diag)(lru_w_a[:, dd]), jax.vmap(_block_diag)(lru_w_x[:, dd])],
                                       axis=-1) for dd in range(2)], axis=1).astype(BF16)
    lru_b = jnp.concatenate([lru_b_a, lru_b_x], axis=-1)[:, :, None, :]
    lru_lam = lru_lambda[:, :, None, :]
    lru_cb = lru_conv_b[:, None, :]
    dn_arow = jnp.repeat(jnp.stack([dn_a_log, dn_dt_bias], axis=2), CHUNK, axis=3)

    xz = jnp.concatenate([x, ctx], axis=1)
    for l in range(depth):
        xz = _ffn(xz, mod, norm_g, wgu, wdn, l, 0, n_lat_tiles, ctx_row, n_tiles)
        ml, lru, dnqkv, dnz, mg, gates = _inproj(xz, mod, norm_g, w_main, w_gate, widths, l, n_lat_tiles, ctx_row)
        ml_hf, ml_hb = _mlstm(ml, gates, gates.transpose(0, 2, 1), ml_bias_c, ml_bias_r, l,
                              n_lat_tiles, n_ctx_tiles)
        lr_hf, lr_hb = _lru(lru, lru_conv_w, lru_cb, lru_w, lru_b, lru_lam, l, n_lat_tiles, n_ctx_tiles)
        prep = _dn_prep(_to_chunks(dnqkv, n_lat), _dn_gate_rows(gates[:, :, N_GATES:], n_lat), dn_conv_w,
                        dn_arow, l, n_lat // (CHUNK * DN_GROUP))
        dn_f, dn_b = (_from_chunks(o, n_lat) for o in _dn_scan(prep, n_lat // CHUNK))
        xz = _merge(xz, mod, norm_g, ml_hf, ml_hb, ml, lr_hf, lr_hb, lru, dn_f, dn_b, dnz, mg,
                    ml_g, dn_g, wbr, wout, l, n_lat_tiles, ctx_row)
        xz = _ffn(xz, mod, norm_g, wgu, wdn, l, 2, n_lat_tiles, ctx_row,
                  n_lat_tiles if l == depth - 1 else n_tiles)
    return xz
```

```python
import functools
import math

import jax
import jax.numpy as jnp
from jax import lax
from jax.experimental import pallas as pl
from jax.experimental.pallas import tpu as pltpu

F32 = jnp.float32
BF16 = jnp.bfloat16

EPS = 1e-6
N_MOD = 9
N_BRANCH = 3
CONV_W = 4
CONV_LEFT = 2
ML_HEADS = 4
LRU_C = 8.0
DN_HEADS = 4
CHUNK = 64
DN_GROUP = 4
DN_SCAN_ROWS = 4
N_GATES = 16

V7X_SUBLANES = 8
V7X_LANES = 128
V7X_VMEM_BYTES = 64 * 1024 * 1024
VMEM_LIMIT = V7X_VMEM_BYTES - 8 * 1024 * 1024

TM = 256
NEG = -1e30


def _sigmoid(x):
    return 1.0 / (1.0 + jnp.exp(-x))


def _silu(x):
    return x * _sigmoid(x)


def _softplus(x):
    return jnp.maximum(x, 0.0) + jnp.log(1.0 + jnp.exp(-jnp.abs(x)))


def _rms(x):
    return x * lax.rsqrt(jnp.mean(x * x, axis=-1, keepdims=True) + EPS)


def _rms_heads(x, n_heads):
    hd = x.shape[-1] // n_heads
    return jnp.concatenate([_rms(x[:, h * hd:(h + 1) * hd]) for h in range(n_heads)], axis=-1)


def _gelu_tanh(x):
    return 0.5 * x * (1.0 + jnp.tanh(math.sqrt(2.0 / math.pi) * (x + 0.044715 * (x * x * x))))


def _dot(a, b):
    return jnp.dot(a, b, preferred_element_type=F32)


def _dot_nt(a, b):
    return lax.dot_general(a, b, (((1,), (1,)), ((), ())), preferred_element_type=F32)


def _dot_tn(a, b):
    return lax.dot_general(a, b, (((0,), (0,)), ((), ())), preferred_element_type=F32)


def _params(*sem):
    return pltpu.CompilerParams(dimension_semantics=sem, vmem_limit_bytes=VMEM_LIMIT)


def _const_spec(block, index):
    return pl.BlockSpec(block, lambda *_: index, pipeline_mode=pl.Buffered(1))


def _stack_heads(x, n):
    w = x.shape[1] // n
    return jnp.concatenate([x[:, h * w:(h + 1) * w] for h in range(n)], axis=0)


def _unstack_heads(x, n):
    t = x.shape[0] // n
    return jnp.concatenate([x[h * t:(h + 1) * t] for h in range(n)], axis=1)


def _cumsum_groups(x, axis, period, rev):
    n = x.shape[axis]
    idx = lax.broadcasted_iota(jnp.int32, x.shape, axis) % period
    sh = 1
    while sh < period:
        if rev:
            x = x + jnp.where(idx < period - sh, pltpu.roll(x, n - sh, axis=axis), 0.0)
        else:
            x = x + jnp.where(idx >= sh, pltpu.roll(x, sh, axis=axis), 0.0)
        sh *= 2
    return x


def _cummax_groups(x, axis, period, rev):
    n = x.shape[axis]
    idx = lax.broadcasted_iota(jnp.int32, x.shape, axis) % period
    sh = 1
    while sh < period:
        if rev:
            x = jnp.maximum(x, jnp.where(idx < period - sh, pltpu.roll(x, n - sh, axis=axis), NEG))
        else:
            x = jnp.maximum(x, jnp.where(idx >= sh, pltpu.roll(x, sh, axis=axis), NEG))
        sh *= 2
    return x


def _shift_rows(x, prev8, nxt8, off):
    t = x.shape[0]
    row8 = lax.broadcasted_iota(jnp.int32, (V7X_SUBLANES, 1), 0)
    rolled = pltpu.roll(x, (-off) % t, axis=0)
    if off < 0:
        edge = jnp.where(row8 < -off, pltpu.roll(prev8, -off, axis=0), rolled[:V7X_SUBLANES])
        return jnp.concatenate([edge, rolled[V7X_SUBLANES:]], axis=0)
    edge = jnp.where(row8 >= V7X_SUBLANES - off, pltpu.roll(nxt8, V7X_SUBLANES - off, axis=0),
                     rolled[t - V7X_SUBLANES:])
    return jnp.concatenate([rolled[:t - V7X_SUBLANES], edge], axis=0)


def _conv4(x, prev8, nxt8, w):
    acc = x * w[CONV_LEFT:CONV_LEFT + 1]
    for j in range(CONV_W):
        if j != CONV_LEFT:
            acc = acc + _shift_rows(x, prev8, nxt8, j - CONV_LEFT) * w[j:j + 1]
    return acc


def _mod_kernel(c_ref, w_ref, b_ref, o_ref):
    s = _silu(c_ref[...]).astype(BF16)
    o_ref[...] = _dot(s, w_ref[...].astype(BF16)) + b_ref[...]


def _mod_table(cc, w_mod, b_mod):
    depth, d, nd = w_mod.shape
    r = cc.shape[0]
    tn = nd // 4
    return pl.pallas_call(
        _mod_kernel,
        grid=(depth, nd // tn),
        in_specs=[pl.BlockSpec((r, d), lambda l, j: (0, 0)),
                  pl.BlockSpec((None, d, tn), lambda l, j: (l, 0, j)),
                  pl.BlockSpec((None, 1, tn), lambda l, j: (l, 0, j))],
        out_specs=pl.BlockSpec((None, r, tn), lambda l, j: (l, 0, j)),
        out_shape=jax.ShapeDtypeStruct((depth, r, nd), F32),
        compiler_params=_params("parallel", "parallel"),
        name="mod_table",
    )(cc, w_mod, b_mod.reshape(depth, 1, nd))


class _TilePlan:
    def __init__(self, b, n_tiles, n_lat_tiles, n_vis, ctx_row, layer):
        self.b, self.n_tiles, self.n_lat_tiles, self.n_vis = b, n_tiles, n_lat_tiles, n_vis
        self.ctx_row, self.layer = ctx_row, layer
        self.sub = 2 if (b * n_vis) % 2 == 0 else 1
        self.grid = (b * n_vis // self.sub,)

    def _tile(self, i, k):
        tid = i * self.sub + k
        return tid // self.n_vis, tid % self.n_vis

    def tok(self, k, width, col=0):
        def index(i):
            bi, t = self._tile(i, k)
            return bi * self.n_tiles + t, col
        return pl.BlockSpec((TM, width), index)

    def toks(self, width, col=0):
        return [self.tok(k, width, col) for k in range(self.sub)]

    def mods(self, d):
        def spec(k):
            def index(i):
                bi, t = self._tile(i, k)
                return self.layer, jnp.where(t >= self.n_lat_tiles, self.ctx_row, bi), 0, 0
            return pl.BlockSpec((None, None, N_MOD, d), index)
        return [spec(k) for k in range(self.sub)]

    def out(self, width):
        return pl.BlockSpec((self.sub * TM, width), lambda i: (i, 0))

    def out_shape(self, width, dtype):
        return jax.ShapeDtypeStruct((self.b * self.n_vis * TM, width), dtype)

    def rows(self, k):
        return slice(k * TM, (k + 1) * TM)


def _flat(a):
    return a.reshape(-1, a.shape[-1])


def _ffn_kernel(*refs, j, dff, sub):
    x_refs, mod_refs = refs[:sub], refs[sub:2 * sub]
    g_ref, wgu_ref, wd_ref, o_ref = refs[2 * sub:]
    g_pre, g_post = g_ref[2 * j:2 * j + 1, :], g_ref[2 * j + 1:2 * j + 2, :]
    mods = [[m[3 * j + i:3 * j + i + 1, :] for i in range(3)] for m in mod_refs]
    xs = [x[...] for x in x_refs]
    hs = [(_rms(x) * g_pre * (1.0 + m[1]) + m[0]).astype(BF16) for x, m in zip(xs, mods)]
    gus = [_dot(h, wgu_ref[...]) for h in hs]
    acts = [(_silu(gu[:, :dff]) * gu[:, dff:]).astype(BF16) for gu in gus]
    ys = [_dot(a, wd_ref[...]) for a in acts]
    for k, (x, m, y) in enumerate(zip(xs, mods, ys)):
        o_ref[k * TM:(k + 1) * TM, :] = x + 0.5 * m[2] * (_rms(y) * g_post)


def _ffn(plan, xz, mod, norm_g, wgu, wd, j):
    d = xz.shape[-1]
    dff = wd.shape[2]
    l = plan.layer
    return pl.pallas_call(
        functools.partial(_ffn_kernel, j=j, dff=dff, sub=plan.sub),
        grid=plan.grid,
        in_specs=plan.toks(d) + plan.mods(d)
                 + [_const_spec((None,) + norm_g.shape[1:], (l, 0, 0)),
                    _const_spec((None, None, d, 2 * dff), (l, j // 2, 0, 0)),
                    _const_spec((None, None, dff, d), (l, j // 2, 0, 0))],
        out_specs=plan.out(d),
        out_shape=plan.out_shape(d, F32),
        compiler_params=_params("parallel"),
        name=f"ffn{j}",
    )(*([xz] * plan.sub), *([mod] * plan.sub), norm_g, wgu, wd)


def _inproj_kernel(*refs, sub):
    x_refs, mod_refs = refs[:sub], refs[sub:2 * sub]
    g_ref, w_ref, wg_ref = refs[2 * sub:2 * sub + 3]
    o_refs = refs[2 * sub + 3:]
    hs = [(_rms(x[...]) * g_ref[2:3, :] * (1.0 + m[4:5, :]) + m[3:4, :]).astype(BF16)
          for x, m in zip(x_refs, mod_refs)]
    col = 0
    for o_ref in o_refs[:-1]:
        n = o_ref.shape[-1]
        for k, h in enumerate(hs):
            o_ref[k * TM:(k + 1) * TM, :] = _dot(h, w_ref[:, col:col + n]).astype(o_ref.dtype)
        col += n
    for k, h in enumerate(hs):
        o_refs[-1][k * TM:(k + 1) * TM, :] = _dot(h, wg_ref[...])


def _inproj(plan, xz, mod, norm_g, w_main, w_gate, widths):
    d = xz.shape[-1]
    ng = w_gate.shape[-1]
    l = plan.layer
    return pl.pallas_call(
        functools.partial(_inproj_kernel, sub=plan.sub),
        grid=plan.grid,
        in_specs=plan.toks(d) + plan.mods(d)
                 + [_const_spec((None,) + norm_g.shape[1:], (l, 0, 0)),
                    _const_spec((None, d, w_main.shape[-1]), (l, 0, 0)),
                    _const_spec((None, d, ng), (l, 0, 0))],
        out_specs=[plan.out(w) for w in widths] + [plan.out(ng)],
        out_shape=[plan.out_shape(w, BF16) for w in widths] + [plan.out_shape(ng, F32)],
        compiler_params=_params("parallel"),
        name="inproj",
    )(*([xz] * plan.sub), *([mod] * plan.sub), norm_g, w_main, w_gate)


N_MERGE_STREAMS = 10


def _merge_kernel(*refs, sub):
    x_refs, mod_refs = refs[:sub], refs[sub:2 * sub]
    tok = refs[2 * sub:(2 + N_MERGE_STREAMS) * sub]
    g_ref, mlg_ref, dng_ref, wb_ref, wo_ref, o_ref = refs[(2 + N_MERGE_STREAMS) * sub:]
    d = o_ref.shape[-1]
    f32 = lambda r: r[...].astype(F32)
    mixes = []
    for k in range(sub):
        mlf, mlb, mlo, lrf, lrb, lry, dnf, dnb, dnz, mg = (tok[s * sub + k] for s in range(N_MERGE_STREAMS))
        y_ml = _rms_heads(f32(mlf) + f32(mlb), ML_HEADS) * mlg_ref[...] * _sigmoid(f32(mlo))
        y_lr = (f32(lrf) + f32(lrb)) * _gelu_tanh(f32(lry))
        y_dn = _rms_heads(f32(dnf) + f32(dnb), DN_HEADS) * dng_ref[...] * _silu(f32(dnz))
        mix = None
        for n, y in enumerate((y_ml, y_lr, y_dn)):
            term = _sigmoid(mg[:, n * d:(n + 1) * d].astype(F32)) * _dot(y.astype(BF16), wb_ref[n])
            mix = term if mix is None else mix + term
        mixes.append(mix.astype(BF16))
    outs = [_dot(mix, wo_ref[...]) for mix in mixes]
    for k, out in enumerate(outs):
        o_ref[k * TM:(k + 1) * TM, :] = x_refs[k][...] + mod_refs[k][5:6, :] * (_rms(out) * g_ref[3:4, :])


def _merge(plan, xz, mod, norm_g, ml_hf, ml_hb, ml, lr_hf, lr_hb, lru, dn_f, dn_b, dnz, mg,
           ml_g, dn_g, w_branch, w_out):
    d = xz.shape[-1]
    bw = w_branch.shape[2]
    l = plan.layer
    streams = [(ml_hf, bw, 0), (ml_hb, bw, 0), (ml, bw, ml.shape[-1] // bw - 1),
               (lr_hf, bw, 0), (lr_hb, bw, 0), (lru, bw, lru.shape[-1] // bw - 1),
               (dn_f, bw, 0), (dn_b, bw, 0), (dnz, bw, 0), (mg, N_BRANCH * d, 0)]
    assert len(streams) == N_MERGE_STREAMS
    tok_specs = [sp for _, w, c in streams for sp in plan.toks(w, c)]
    tok_args = [a for a, _, _ in streams for _ in range(plan.sub)]
    return pl.pallas_call(
        functools.partial(_merge_kernel, sub=plan.sub),
        grid=plan.grid,
        in_specs=plan.toks(d) + plan.mods(d) + tok_specs
                 + [_const_spec((None,) + norm_g.shape[1:], (l, 0, 0)),
                    _const_spec((None, 1, bw), (l, 0, 0)),
                    _const_spec((None, 1, bw), (l, 0, 0)),
                    _const_spec((None, N_BRANCH, bw, d), (l, 0, 0, 0)),
                    _const_spec((None, d, d), (l, 0, 0))],
        out_specs=plan.out(d),
        out_shape=plan.out_shape(d, F32),
        compiler_params=_params("parallel"),
        name="merge",
    )(*([xz] * plan.sub), *([mod] * plan.sub), *tok_args, norm_g, ml_g, dn_g, w_branch, w_out)


def _log_scan(a, b, idx, period, rev):
    n = a.shape[0]
    sh = 1
    while sh < period:
        if rev:
            ok = idx < period - sh
            ar, br = pltpu.roll(a, n - sh, axis=0), pltpu.roll(b, n - sh, axis=0)
        else:
            ok = idx >= sh
            ar, br = pltpu.roll(a, sh, axis=0), pltpu.roll(b, sh, axis=0)
        b = b + a * jnp.where(ok, br, 0.0)
        a = a * jnp.where(ok, ar, 1.0)
        sh *= 2
    return a, b


def _tile_scan(a, b, h0, rev, sa_ref, sb_ref):
    t = a.shape[0]
    g = t // V7X_SUBLANES
    row = lax.broadcasted_iota(jnp.int32, (t, 1), 0)
    a, b = _log_scan(a, b, row % V7X_SUBLANES, V7X_SUBLANES, rev)
    edge = 0 if rev else V7X_SUBLANES - 1

    def group_edges(ref, val):
        parts = []
        for j in range(ref.shape[0]):
            ref[j] = val[:, j * V7X_LANES:(j + 1) * V7X_LANES]
            parts.append(ref[j, pl.ds(edge, g, stride=V7X_SUBLANES), :])
        return jnp.concatenate(parts, axis=1)

    ga = group_edges(sa_ref, a)
    gb = group_edges(sb_ref, b)
    grow = lax.broadcasted_iota(jnp.int32, (g, 1), 0)
    ga, gb = _log_scan(ga, gb, grow, g, rev)
    h_end = gb + ga * h0
    if rev:
        h_in = jnp.where(grow == g - 1, h0, pltpu.roll(h_end, g - 1, axis=0))
        carry = h_end[0:1]
    else:
        h_in = jnp.where(grow == 0, h0, pltpu.roll(h_end, 1, axis=0))
        carry = h_end[g - 1:g]
    out = jnp.concatenate(
        [b[V7X_SUBLANES * i:V7X_SUBLANES * (i + 1)] + a[V7X_SUBLANES * i:V7X_SUBLANES * (i + 1)] * h_in[i:i + 1]
         for i in range(g)], axis=0)
    return out, carry


def _lru_kernel(x_ref, cw_ref, cb_ref, w_ref, bias_ref, lam_ref, of_ref, ob_ref, *scratch, n_lat, n_ctx):
    t, c = TM, x_ref.shape[-1]
    n_tiles = n_lat + n_ctx
    halo = 2 * V7X_SUBLANES

    def load_conv(tile):
        r0 = pl.multiple_of(tile * t, t)
        first = jnp.logical_or(tile == 0, tile == n_lat)
        last = jnp.logical_or(tile == n_lat - 1, tile == n_tiles - 1)
        x = x_ref[pl.ds(r0, t), :].astype(F32)
        p0 = pl.multiple_of(jnp.maximum(r0 - halo, 0), halo)
        n0 = pl.multiple_of(jnp.minimum(r0 + t, n_tiles * t - halo), halo)
        prev8 = jnp.where(first, 0.0, x_ref[pl.ds(p0, halo), :].astype(F32)[V7X_SUBLANES:])
        nxt8 = jnp.where(last, 0.0, x_ref[pl.ds(n0, halo), :].astype(F32)[:V7X_SUBLANES])
        return r0, _conv4(x, prev8, nxt8, cw_ref[...]) + cb_ref[...]

    def direction(d, tile, h0, o_ref):
        r0, xc = load_conv(tile)
        z = _dot(xc.astype(BF16), w_ref[d]) + bias_ref[d]
        r, i = _sigmoid(z[:, :c]), _sigmoid(z[:, c:])
        la = (-LRU_C * _softplus(-lam_ref[d])) * r
        a = jnp.exp(la)
        bx = jnp.sqrt(jnp.tanh(-la) * (1.0 + a * a)) * (i * xc)
        h, carry = _tile_scan(a, bx, h0, d == 1, scratch[2 * d], scratch[2 * d + 1])
        o_ref[pl.ds(r0, t), :] = h.astype(o_ref.dtype)
        return carry

    def step(s, carry):
        hf, hb = carry
        hf = direction(0, jnp.where(s < n_ctx, n_lat + s, s - n_ctx), hf, of_ref)
        hb = direction(1, n_tiles - 1 - s, hb, ob_ref)
        return hf, hb

    zero = jnp.zeros((1, c), F32)
    lax.fori_loop(0, n_tiles, step, (zero, zero))


def _lru(lru, conv_w, conv_b, w_gates, b_gates, lam, l, n_lat, n_ctx):
    b, lt, _ = lru.shape
    c = conv_w.shape[-1]
    seq = pl.BlockSpec((None, lt, c), lambda bi: (bi, 0, 0))
    return pl.pallas_call(
        functools.partial(_lru_kernel, n_lat=n_lat, n_ctx=n_ctx),
        grid=(b,),
        in_specs=[seq,
                  _const_spec((None, CONV_W, c), (l, 0, 0)),
                  _const_spec((None, 1, c), (l, 0, 0)),
                  _const_spec((None, 2, c, 2 * c), (l, 0, 0, 0)),
                  _const_spec((None, 2, 1, 2 * c), (l, 0, 0, 0)),
                  _const_spec((None, 2, 1, c), (l, 0, 0, 0))],
        out_specs=[seq, seq],
        out_shape=[jax.ShapeDtypeStruct((b, lt, c), BF16)] * 2,
        scratch_shapes=[pltpu.VMEM((c // V7X_LANES, TM, V7X_LANES), F32)] * 4,
        compiler_params=_params("parallel"),
        name="lru",
    )(lru, conv_w, conv_b, w_gates, b_gates, lam)


def _mlstm_kernel(qf_ref, kf_ref, vf_ref, qb_ref, kb_ref, vb_ref, grf_ref, grb_ref, br_ref,
                  hf_ref, hb_ref, c_ref, m_ref):
    @pl.when(pl.program_id(1) == 0)
    def _():
        c_ref[...] = jnp.zeros(c_ref.shape, F32)
        m_ref[...] = jnp.zeros(m_ref.shape, F32)

    tc, nqk = qf_ref.shape
    dk = nqk // ML_HEADS
    dv = vf_ref.shape[1] // ML_HEADS
    nh = ML_HEADS
    ri = lax.broadcasted_iota(jnp.int32, (tc, tc), 0)
    ci = lax.broadcasted_iota(jnp.int32, (tc, tc), 1)
    causal = (ci <= ri, ci >= ri)
    lane_head = lax.broadcasted_iota(jnp.int32, (1, nqk), 1) // dk
    ones = jnp.ones((tc, dv), BF16)
    dirs = ((qf_ref, kf_ref, vf_ref, grf_ref), (qb_ref, kb_ref, vb_ref, grb_ref))

    rows, v_rows, ws_rows, decs = [], [], [], []
    for d, (_, _, _, gr_ref) in enumerate(dirs):
        rev = d == 1
        gr = gr_ref[...] + br_ref[...]
        i_r = gr[nh * d:nh * (d + 1)]
        b_r = _cumsum_groups(-_softplus(-gr[nh * (2 + d):nh * (3 + d)]), 1, tc, rev)
        m_prev = m_ref[nh * d:nh * (d + 1), 0:1]
        m_t = b_r + jnp.maximum(m_prev, _cummax_groups(i_r - b_r, 1, tc, rev))
        b_end = b_r[:, 0:1] if rev else b_r[:, tc - 1:tc]
        lws = b_end - b_r + i_r
        m_new = jnp.maximum(b_end + m_prev, jnp.max(lws, axis=1, keepdims=True))
        rows += [b_r - m_t, jnp.exp(b_r + m_prev - m_t), jnp.exp(-m_t)]
        v_rows.append(b_r - i_r)
        ws_rows.append(jnp.exp(lws - m_new))
        decs.append(jnp.exp(b_end + m_prev - m_new))
        m_ref[nh * d:nh * (d + 1), :] = jnp.broadcast_to(m_new, (nh, m_ref.shape[1]))
    cols = jnp.concatenate(rows, axis=0).T

    chains = [(d, h) for d in range(2) for h in range(nh)]
    q_all = [dirs[d][0][...] * (dk ** -0.5) for d in range(2)]
    k_all = [dirs[d][1][...] for d in range(2)]
    v_all = [dirs[d][2][...] for d in range(2)]
    kt_all = [k.astype(F32).T for k in k_all]
    c_all = [c_ref[d] for d in range(2)]
    cb_all = [c.astype(BF16) for c in c_all]
    col = lambda d, j, h: cols[:, (3 * d + j) * nh + h:(3 * d + j) * nh + h + 1]
    qh = [jnp.where(lane_head == h, q_all[d], jnp.zeros_like(q_all[d])) for d, h in chains]
    vp = [jnp.concatenate([v_all[d][:, h * dv:(h + 1) * dv], ones], axis=1) for d, h in chains]
    s_raw = [_dot_nt(q, k_all[d]) for q, (d, h) in zip(qh, chains)]
    p = [(s * jnp.exp(jnp.where(causal[d], col(d, 0, h) - v_rows[d][h:h + 1], NEG))).astype(BF16)
         for s, (d, h) in zip(s_raw, chains)]
    num = [_dot(pc, vc) + col(d, 1, h) * _dot(q, cb_all[d]) for pc, vc, q, (d, h) in zip(p, vp, qh, chains)]
    outs = [nm[:, :dv] / jnp.maximum(jnp.abs(nm[:, dv:]), col(d, 2, h)) for nm, (d, h) in zip(num, chains)]
    for vc, (d, h) in zip(vp, chains):
        kw = (kt_all[d][h * dk:(h + 1) * dk] * ws_rows[d][h:h + 1]).astype(BF16)
        c_ref[d, h * dk:(h + 1) * dk, :] = decs[d][h:h + 1] * c_all[d][h * dk:(h + 1) * dk] + _dot(kw, vc)
    hf_ref[...] = jnp.concatenate(outs[:nh], axis=1).astype(hf_ref.dtype)
    hb_ref[...] = jnp.concatenate(outs[nh:], axis=1).astype(hb_ref.dtype)


def _mlstm(ml, gates_t, bias_r, l, n_lat, n_ctx):
    b, lt, _ = ml.shape
    ng = gates_t.shape[1]
    n_tiles = n_lat + n_ctx
    dqk = ml.shape[-1] // 6
    fwd = lambda s: jnp.where(s < n_ctx, n_lat + s, s - n_ctx)
    bwd = lambda s: n_tiles - 1 - s

    def specs(tile):
        return [pl.BlockSpec((None, TM, dqk), lambda bi, s: (bi, tile(s), 0)),
                pl.BlockSpec((None, TM, dqk), lambda bi, s: (bi, tile(s), 1)),
                pl.BlockSpec((None, TM, 2 * dqk), lambda bi, s: (bi, tile(s), 1))]

    row = lambda tile: pl.BlockSpec((None, ng, TM), lambda bi, s: (bi, 0, tile(s)))
    out = lambda tile: pl.BlockSpec((None, TM, 2 * dqk), lambda bi, s: (bi, tile(s), 0))
    return pl.pallas_call(
        _mlstm_kernel,
        grid=(b, n_tiles),
        in_specs=specs(fwd) + specs(bwd) + [row(fwd), row(bwd), _const_spec((None, ng, 1), (l, 0, 0))],
        out_specs=[out(fwd), out(bwd)],
        out_shape=[jax.ShapeDtypeStruct((b, lt, 2 * dqk), BF16)] * 2,
        scratch_shapes=[pltpu.VMEM((2, dqk, 2 * (2 * dqk // ML_HEADS)), F32),
                        pltpu.VMEM((2 * ML_HEADS, V7X_LANES), F32)],
        compiler_params=_params("parallel", "arbitrary"),
        name="mlstm",
    )(ml, ml, ml, ml, ml, ml, gates_t, gates_t, bias_r)


INV_BASE = 8


def _inv_masks(ri, ci):
    blk = lambda s: (ri // s) == (ci // s)
    as_bf = lambda m: jnp.where(m, 1.0, 0.0).astype(BF16)
    masks, s = [as_bf(blk(INV_BASE))], INV_BASE
    while s < CHUNK:
        masks.append(as_bf(jnp.logical_and(blk(2 * s), jnp.logical_not(blk(s)))))
        s *= 2
    return masks


def _tri_inv_many(a_list, eye, masks):
    ps = [a * masks[0] for a in a_list]
    xs = [eye - p.astype(F32) for p in ps]
    for _ in range(INV_BASE.bit_length() - 2):
        ps = [_dot(p, p).astype(BF16) for p in ps]
        xs = [x + _dot(x.astype(BF16), p) for x, p in zip(xs, ps)]
    for m in masks[1:]:
        xbs = [x.astype(BF16) for x in xs]
        ys = [_dot(xb, a * m).astype(BF16) for xb, a in zip(xbs, a_list)]
        xs = [x - _dot(y, xb) for x, y, xb in zip(xs, ys, xbs)]
    return xs


def _dn_prep_kernel(x_ref, xp_ref, xn_ref, grow_ref, cw_ref, arow_ref,
                    u_ref, w_ref, qe_ref, kd_ref, qk_ref, eg_ref, *, first_groups, last_groups):
    g = pl.program_id(1)
    width = x_ref.shape[-1]
    dh = width // (3 * DN_HEADS)
    n = DN_HEADS * CHUNK
    halo = 2 * V7X_SUBLANES
    any_of = lambda groups: functools.reduce(jnp.logical_or, [g == v for v in groups])
    x = x_ref[...].reshape(DN_GROUP * CHUNK, width).astype(F32)
    prev8 = jnp.where(any_of(first_groups), 0.0, xp_ref[CHUNK - halo:, :].astype(F32)[V7X_SUBLANES:])
    nxt8 = jnp.where(any_of(last_groups), 0.0, xn_ref[:halo, :].astype(F32)[:V7X_SUBLANES])
    xc = _silu(_conv4(x, prev8, nxt8, cw_ref[...]))

    ri = lax.broadcasted_iota(jnp.int32, (n, n), 0)
    ci = lax.broadcasted_iota(jnp.int32, (n, n), 1)
    same = (ri // CHUNK) == (ci // CHUNK)
    eye = jnp.where(ri == ci, 1.0, 0.0)
    incl = (jnp.logical_and(same, ci <= ri), jnp.logical_and(same, ci >= ri))
    strict = (jnp.logical_and(same, ci < ri), jnp.logical_and(same, ci > ri))
    masks = _inv_masks(ri, ci)

    qn, kn, v, kk, qk = [], [], [], [], []
    for i in range(DN_GROUP):
        xi = xc[i * CHUNK:(i + 1) * CHUNK]
        q, k, vi = (_stack_heads(xi[:, j * DN_HEADS * dh:(j + 1) * DN_HEADS * dh], DN_HEADS) for j in range(3))
        qn.append(q * lax.rsqrt(jnp.sum(q * q, axis=-1, keepdims=True) + EPS) * (dh ** -0.5))
        kn.append(k * lax.rsqrt(jnp.sum(k * k, axis=-1, keepdims=True) + EPS))
        v.append(vi)
        kb = kn[i].astype(BF16)
        kk.append(_dot_nt(kb, kb))
        qk.append(_dot_nt(qn[i].astype(BF16), kb))

    systems = [(i, d) for i in range(DN_GROUP) for d in range(2)]
    rows, cs_rows = [], []
    for i, d in systems:
        raw = grow_ref[i]
        beta = _sigmoid(raw[d:d + 1])
        g_r = -jnp.exp(arow_ref[d, 0:1, :]) * _softplus(raw[2 + d:3 + d] + arow_ref[d, 1:2, :])
        cs = _cumsum_groups(g_r, 1, CHUNK, d == 1)
        tot = cs + _cumsum_groups(g_r, 1, CHUNK, d == 0) - g_r
        eg = jnp.exp(cs)
        rows += [cs, beta, eg, jnp.exp(tot - cs), beta * eg]
        cs_rows.append(cs)
        eg_ref[d, i] = jnp.exp(tot)
    n_col = len(rows) // len(systems)
    cols = jnp.concatenate(rows, axis=0).T

    a_list, rhs = [], []
    for s, (i, d) in enumerate(systems):
        cs_c, beta_c, eg_c, ekd_c, beg_c = (cols[:, n_col * s + j:n_col * s + j + 1] for j in range(n_col))
        gam = jnp.exp(jnp.where(incl[d], cs_c - cs_rows[s], NEG))
        a_list.append((jnp.where(strict[d], kk[i], 0.0) * gam * beta_c).astype(BF16))
        rhs.append(jnp.concatenate([beta_c * v[i], beg_c * kn[i]], axis=1).astype(BF16))
        qk_ref[d, i] = (qk[i] * gam).astype(qk_ref.dtype)
        qe_ref[d, i] = (qn[i] * eg_c).astype(qe_ref.dtype)
        kd_ref[d, i] = (kn[i] * ekd_c).astype(kd_ref.dtype)
    for (i, d), tinv, r in zip(systems, _tri_inv_many(a_list, eye, masks), rhs):
        sol = _dot(tinv.astype(BF16), r)
        u_ref[d, i] = sol[:, :dh].astype(u_ref.dtype)
        w_ref[d, i] = sol[:, dh:].astype(w_ref.dtype)


def _dn_prep(xs, grow, conv_w, arow, l, n_lat_groups):
    b, nc, _, width = xs.shape
    n_groups = nc // DN_GROUP
    n = DN_HEADS * CHUNK
    dh = width // (3 * DN_HEADS)
    out = lambda w, dt: (jax.ShapeDtypeStruct((b, 2, nc, n, w), dt),
                         pl.BlockSpec((None, 2, DN_GROUP, n, w), lambda bi, g: (bi, 0, g, 0, 0)))
    outs = [out(dh, BF16)] * 4 + [out(n, BF16)]
    outs.append((jax.ShapeDtypeStruct((b, 2, nc, 1, n), F32),
                 pl.BlockSpec((None, 2, DN_GROUP, 1, n), lambda bi, g: (bi, 0, g, 0, 0))))
    chunk = lambda idx: pl.BlockSpec((None, None, CHUNK, width), lambda bi, g: (bi, idx(g), 0, 0))
    return pl.pallas_call(
        functools.partial(_dn_prep_kernel, first_groups=(0, n_lat_groups),
                          last_groups=(n_lat_groups - 1, n_groups - 1)),
        grid=(b, n_groups),
        in_specs=[pl.BlockSpec((None, DN_GROUP, CHUNK, width), lambda bi, g: (bi, g, 0, 0)),
                  chunk(lambda g: jnp.maximum(g * DN_GROUP - 1, 0)),
                  chunk(lambda g: jnp.minimum(g * DN_GROUP + DN_GROUP, nc - 1)),
                  pl.BlockSpec((None, DN_GROUP, 4, n), lambda bi, g: (bi, g, 0, 0)),
                  _const_spec((None, CONV_W, width), (l, 0, 0)),
                  _const_spec((None, 2, 2, n), (l, 0, 0, 0))],
        out_specs=[o[1] for o in outs],
        out_shape=[o[0] for o in outs],
        compiler_params=_params("parallel", "parallel"),
        name="dn_prep",
    )(xs, xs, xs, grow, conv_w, arow)


def _dn_scan_kernel(*refs):
    ins, (of_ref, ob_ref, s_ref) = refs[:12], refs[12:]

    @pl.when(pl.program_id(1) == 0)
    def _():
        s_ref[...] = jnp.zeros(s_ref.shape, F32)

    rows = [slice(h * CHUNK, (h + 1) * CHUNK) for h in range(DN_HEADS)]
    seqs = [(r, d) for r in range(of_ref.shape[0]) for d in range(2)]
    get = lambda j, r, d: ins[6 * d + j][r]
    sb = [[s_ref[r, d, h].astype(BF16) for h in range(DN_HEADS)] for r, d in seqs]
    ws = [jnp.concatenate([_dot(get(1, r, d)[rows[h]], sb[i][h]) for h in range(DN_HEADS)], axis=0)
          for i, (r, d) in enumerate(seqs)]
    qs = [jnp.concatenate([_dot(get(2, r, d)[rows[h]], sb[i][h]) for h in range(DN_HEADS)], axis=0)
          for i, (r, d) in enumerate(seqs)]
    vnew = [(get(0, r, d).astype(F32) - w).astype(BF16) for w, (r, d) in zip(ws, seqs)]
    o = [q + _dot(get(4, r, d), vn) for q, vn, (r, d) in zip(qs, vnew, seqs)]
    for vn, (r, d) in zip(vnew, seqs):
        kd, eg = get(3, r, d), get(5, r, d)
        for h in range(DN_HEADS):
            s_ref[r, d, h] = (eg[:, h * CHUNK:h * CHUNK + 1] * s_ref[r, d, h]
                              + _dot_tn(kd[rows[h]], vn[rows[h]]))
    for oi, (r, d) in zip(o, seqs):
        o_ref = (of_ref, ob_ref)[d]
        o_ref[r] = _unstack_heads(oi, DN_HEADS).astype(o_ref.dtype)


def _dn_scan(prep, n_lat_chunks):
    b, _, nc, _, dh = prep[0].shape
    ow = DN_HEADS * dh
    n_ctx_chunks = nc - n_lat_chunks
    fwd = lambda s: jnp.where(s < n_ctx_chunks, n_lat_chunks + s, s - n_ctx_chunks)
    bwd = lambda s: nc - 1 - s

    rpb = DN_SCAN_ROWS if b % DN_SCAN_ROWS == 0 else 1

    def spec(a, d, chunk):
        return pl.BlockSpec((rpb, None, None) + a.shape[3:], lambda bi, s: (bi, d, chunk(s), 0, 0))

    out = lambda chunk: pl.BlockSpec((rpb, None, CHUNK, ow), lambda bi, s: (bi, chunk(s), 0, 0))
    return pl.pallas_call(
        _dn_scan_kernel,
        grid=(b // rpb, nc),
        in_specs=[spec(a, 0, fwd) for a in prep] + [spec(a, 1, bwd) for a in prep],
        out_specs=[out(fwd), out(bwd)],
        out_shape=[jax.ShapeDtypeStruct((b, nc, CHUNK, ow), BF16)] * 2,
        scratch_shapes=[pltpu.VMEM((rpb, 2, DN_HEADS, dh, dh), F32)],
        compiler_params=_params("parallel", "arbitrary"),
        name="dn_scan",
    )(*prep, *prep)


def _block_diag(w):
    n, i, j = w.shape
    return jnp.einsum('nij,nm->nimj', w, jnp.eye(n, dtype=w.dtype)).reshape(n * i, n * j)


def _to_chunks(a, n_lat):
    b, _, w = a.shape
    lat = a[:, :n_lat].reshape(b, n_lat // CHUNK, CHUNK, w).swapaxes(1, 2)
    return jnp.concatenate([lat, a[:, n_lat:].reshape(b, -1, CHUNK, w)], axis=1)


def _from_chunks(a, n_lat):
    b, _, _, w = a.shape
    lat = a[:, :n_lat // CHUNK].swapaxes(1, 2).reshape(b, n_lat, w)
    return jnp.concatenate([lat, a[:, n_lat // CHUNK:].reshape(b, -1, w)], axis=1)


def _dn_gate_rows(raw, n_lat):
    b = raw.shape[0]
    lat = raw[:, :n_lat].reshape(b, n_lat // CHUNK, CHUNK, 4, DN_HEADS)
    ctx = raw[:, n_lat:].reshape(b, -1, CHUNK, 4, DN_HEADS)
    row = jnp.concatenate([lat.transpose(0, 2, 3, 4, 1), ctx.transpose(0, 1, 3, 4, 2)], axis=1)
    return row.reshape(b, row.shape[1], 4, DN_HEADS * CHUNK)


def kernel(x, c, ctx, c_ctx, w_mod, b_mod, norm_g, ffn_w_gu, ffn_w_down, w_in, ml_gate_b, ml_norm_g,
           lru_conv_w, lru_conv_b, lru_w_a, lru_b_a, lru_w_x, lru_b_x, lru_lambda, dn_conv_w,
           dn_a_log, dn_dt_bias, dn_norm_g, w_branch, w_out):
    b, n_lat, d = x.shape
    n_ctx = ctx.shape[1]
    depth = w_mod.shape[0]
    bw = w_branch.shape[2]
    assert n_lat == CHUNK * CHUNK and n_lat % TM == 0 and n_ctx % TM == 0 and n_ctx % (CHUNK * DN_GROUP) == 0
    n_lat_tiles, n_ctx_tiles = n_lat // TM, n_ctx // TM
    n_tiles = n_lat_tiles + n_ctx_tiles

    ctx_row = b
    n_rows = -(-(b + 1) // V7X_SUBLANES) * V7X_SUBLANES
    cc = jnp.zeros((n_rows, d), F32).at[:b].set(c).at[b].set(c_ctx)
    mod = _mod_table(cc, w_mod, b_mod).reshape(depth, n_rows, N_MOD, d)

    dqk = bw // 2
    edges = [0]
    for wdt in (dqk, dqk, bw, bw, N_GATES, bw, bw, bw, bw, bw, bw, N_GATES, N_BRANCH * d):
        edges.append(edges[-1] + wdt)
    piece = lambda i, j: w_in[:, :, edges[i]:edges[j]]
    widths = (3 * bw, 2 * bw, 3 * bw, bw, N_BRANCH * d)
    w_main = jnp.concatenate([piece(0, 4), piece(5, 7), piece(7, 11), piece(12, 13)], axis=-1).astype(BF16)
    w_gate = jnp.concatenate([piece(4, 5), piece(11, 12)], axis=-1).astype(BF16)
    wgu = ffn_w_gu.astype(BF16)
    wdn = ffn_w_down.astype(BF16)
    wbr = w_branch.astype(BF16)
    wout = w_out.astype(BF16)
    ml_bias = jnp.concatenate([ml_gate_b.reshape(depth, N_GATES), jnp.zeros((depth, N_GATES), F32)], axis=-1)
    ml_bias_r = ml_bias[:, :, None]
    ml_g = ml_norm_g[:, None, :]
    dn_g = jnp.tile(dn_norm_g, (1, DN_HEADS))[:, None, :]
    lru_w = jnp.stack([jnp.concatenate([jax.vmap(_block_diag)(lru_w_a[:, dd]), jax.vmap(_block_diag)(lru_w_x[:, dd])],
                                       axis=-1) for dd in range(2)], axis=1).astype(BF16)
    lru_b = jnp.concatenate([lru_b_a, lru_b_x], axis=-1)[:, :, None, :]
    lru_lam = lru_lambda[:, :, None, :]
    lru_cb = lru_conv_b[:, None, :]
    dn_arow = jnp.repeat(jnp.stack([dn_a_log, dn_dt_bias], axis=2), CHUNK, axis=3)

    xz = _flat(jnp.concatenate([x, ctx], axis=1))
    rows3 = lambda a: a.reshape(b, n_lat + n_ctx, a.shape[-1])
    for l in range(depth):
        full = _TilePlan(b, n_tiles, n_lat_tiles, n_tiles, ctx_row, l)
        xz = _ffn(full, xz, mod, norm_g, wgu, wdn, 0)
        ml, lru, dnqkv, dnz, mg, gates = _inproj(full, xz, mod, norm_g, w_main, w_gate, widths)
        gates = rows3(gates)
        ml_hf, ml_hb = _mlstm(rows3(ml), gates.transpose(0, 2, 1), ml_bias_r, l, n_lat_tiles, n_ctx_tiles)
        lr_hf, lr_hb = _lru(rows3(lru), lru_conv_w, lru_cb, lru_w, lru_b, lru_lam, l, n_lat_tiles, n_ctx_tiles)
        prep = _dn_prep(_to_chunks(rows3(dnqkv), n_lat), _dn_gate_rows(gates[:, :, N_GATES:], n_lat),
                        dn_conv_w, dn_arow, l, n_lat // (CHUNK * DN_GROUP))
        dn_f, dn_b = (_flat(_from_chunks(o, n_lat)) for o in _dn_scan(prep, n_lat // CHUNK))
        xz = _merge(full, xz, mod, norm_g, _flat(ml_hf), _flat(ml_hb), ml, _flat(lr_hf), _flat(lr_hb), lru,
                    dn_f, dn_b, dnz, mg, ml_g, dn_g, wbr, wout)
        visit = n_lat_tiles if l == depth - 1 else n_tiles
        xz = _ffn(_TilePlan(b, n_tiles, n_lat_tiles, visit, ctx_row, l), xz, mod, norm_g, wgu, wdn, 2)
    return xz.reshape(b, n_lat, d)
```

```python
import functools
import math

import jax
import jax.numpy as jnp
from jax import lax
from jax.experimental import pallas as pl
from jax.experimental.pallas import tpu as pltpu

F32 = jnp.float32
BF16 = jnp.bfloat16

EPS = 1e-6
N_MOD = 9
N_BRANCH = 3
CONV_W = 4
CONV_LEFT = 2
ML_HEADS = 4
LRU_C = 8.0
DN_HEADS = 4
CHUNK = 64
DN_GROUP = 4
DN_SCAN_ROWS = 4
N_GATES = 16

V7X_SUBLANES = 8
V7X_LANES = 128
V7X_VMEM_BYTES = 64 * 1024 * 1024
VMEM_LIMIT = V7X_VMEM_BYTES - 8 * 1024 * 1024

TM = 256
NEG = -1e30


def _sigmoid(x):
    return 0.5 * jnp.tanh(0.5 * x) + 0.5


def _silu(x):
    return x * _sigmoid(x)


def _softplus(x):
    return jnp.maximum(x, 0.0) + jnp.log(1.0 + jnp.exp(-jnp.abs(x)))


def _rms(x):
    return x * lax.rsqrt(jnp.mean(x * x, axis=-1, keepdims=True) + EPS)


def _rms_heads(x, n_heads):
    hd = x.shape[-1] // n_heads
    return jnp.concatenate([_rms(x[:, h * hd:(h + 1) * hd]) for h in range(n_heads)], axis=-1)


def _gelu_tanh(x):
    return 0.5 * x * (1.0 + jnp.tanh(math.sqrt(2.0 / math.pi) * (x + 0.044715 * (x * x * x))))


def _dot(a, b):
    return jnp.dot(a, b, preferred_element_type=F32)


def _dot_nt(a, b):
    return lax.dot_general(a, b, (((1,), (1,)), ((), ())), preferred_element_type=F32)


def _dot_tn(a, b):
    return lax.dot_general(a, b, (((0,), (0,)), ((), ())), preferred_element_type=F32)


def _params(*sem):
    return pltpu.CompilerParams(dimension_semantics=sem, vmem_limit_bytes=VMEM_LIMIT)


def _const_spec(block, index):
    return pl.BlockSpec(block, lambda *_: index, pipeline_mode=pl.Buffered(1))


def _stack_heads(x, n):
    w = x.shape[1] // n
    return jnp.concatenate([x[:, h * w:(h + 1) * w] for h in range(n)], axis=0)


def _unstack_heads(x, n):
    t = x.shape[0] // n
    return jnp.concatenate([x[h * t:(h + 1) * t] for h in range(n)], axis=1)


def _cumsum_groups(x, axis, period, rev):
    n = x.shape[axis]
    idx = lax.broadcasted_iota(jnp.int32, x.shape, axis) % period
    sh = 1
    while sh < period:
        if rev:
            x = x + jnp.where(idx < period - sh, pltpu.roll(x, n - sh, axis=axis), 0.0)
        else:
            x = x + jnp.where(idx >= sh, pltpu.roll(x, sh, axis=axis), 0.0)
        sh *= 2
    return x


def _cummax_groups(x, axis, period, rev):
    n = x.shape[axis]
    idx = lax.broadcasted_iota(jnp.int32, x.shape, axis) % period
    sh = 1
    while sh < period:
        if rev:
            x = jnp.maximum(x, jnp.where(idx < period - sh, pltpu.roll(x, n - sh, axis=axis), NEG))
        else:
            x = jnp.maximum(x, jnp.where(idx >= sh, pltpu.roll(x, sh, axis=axis), NEG))
        sh *= 2
    return x


def _shift_rows(x, prev8, nxt8, off):
    t = x.shape[0]
    row8 = lax.broadcasted_iota(jnp.int32, (V7X_SUBLANES, 1), 0)
    rolled = pltpu.roll(x, (-off) % t, axis=0)
    if off < 0:
        edge = jnp.where(row8 < -off, pltpu.roll(prev8, -off, axis=0), rolled[:V7X_SUBLANES])
        return jnp.concatenate([edge, rolled[V7X_SUBLANES:]], axis=0)
    edge = jnp.where(row8 >= V7X_SUBLANES - off, pltpu.roll(nxt8, V7X_SUBLANES - off, axis=0),
                     rolled[t - V7X_SUBLANES:])
    return jnp.concatenate([rolled[:t - V7X_SUBLANES], edge], axis=0)


def _conv4(x, prev8, nxt8, w):
    acc = x * w[CONV_LEFT:CONV_LEFT + 1]
    for j in range(CONV_W):
        if j != CONV_LEFT:
            acc = acc + _shift_rows(x, prev8, nxt8, j - CONV_LEFT) * w[j:j + 1]
    return acc


def _mod_kernel(c_ref, w_ref, b_ref, o_ref):
    s = _silu(c_ref[...]).astype(BF16)
    o_ref[...] = _dot(s, w_ref[...].astype(BF16)) + b_ref[...]


def _mod_table(cc, w_mod, b_mod):
    depth, d, nd = w_mod.shape
    r = cc.shape[0]
    tn = nd // 4
    return pl.pallas_call(
        _mod_kernel,
        grid=(depth, nd // tn),
        in_specs=[pl.BlockSpec((r, d), lambda l, j: (0, 0)),
                  pl.BlockSpec((None, d, tn), lambda l, j: (l, 0, j)),
                  pl.BlockSpec((None, 1, tn), lambda l, j: (l, 0, j))],
        out_specs=pl.BlockSpec((None, r, tn), lambda l, j: (l, 0, j)),
        out_shape=jax.ShapeDtypeStruct((depth, r, nd), F32),
        compiler_params=_params("parallel", "parallel"),
        name="mod_table",
    )(cc, w_mod, b_mod.reshape(depth, 1, nd))


class _TilePlan:
    def __init__(self, b, n_tiles, n_lat_tiles, n_vis, ctx_row, layer):
        self.b, self.n_tiles, self.n_lat_tiles, self.n_vis = b, n_tiles, n_lat_tiles, n_vis
        self.ctx_row, self.layer = ctx_row, layer
        self.sub = 2 if (b * n_vis) % 2 == 0 else 1
        self.grid = (b * n_vis // self.sub,)

    def _tile(self, i, k):
        tid = i * self.sub + k
        return tid // self.n_vis, tid % self.n_vis

    def tok(self, k, width, col=0):
        def index(i):
            bi, t = self._tile(i, k)
            return bi * self.n_tiles + t, col
        return pl.BlockSpec((TM, width), index)

    def toks(self, width, col=0):
        return [self.tok(k, width, col) for k in range(self.sub)]

    def mods(self, d):
        def spec(k):
            def index(i):
                bi, t = self._tile(i, k)
                return self.layer, jnp.where(t >= self.n_lat_tiles, self.ctx_row, bi), 0, 0
            return pl.BlockSpec((None, None, N_MOD, d), index)
        return [spec(k) for k in range(self.sub)]

    def out(self, width):
        return pl.BlockSpec((self.sub * TM, width), lambda i: (i, 0))

    def out_shape(self, width, dtype):
        return jax.ShapeDtypeStruct((self.b * self.n_vis * TM, width), dtype)

    def rows(self, k):
        return slice(k * TM, (k + 1) * TM)


def _flat(a):
    return a.reshape(-1, a.shape[-1])


def _ffn_kernel(*refs, j, dff, sub):
    x_refs, mod_refs = refs[:sub], refs[sub:2 * sub]
    g_ref, wgu_ref, wd_ref, o_ref = refs[2 * sub:]
    g_pre, g_post = g_ref[2 * j:2 * j + 1, :], g_ref[2 * j + 1:2 * j + 2, :]
    mods = [[m[3 * j + i:3 * j + i + 1, :] for i in range(3)] for m in mod_refs]
    xs = [x[...] for x in x_refs]
    hs = [(_rms(x) * g_pre * (1.0 + m[1]) + m[0]).astype(BF16) for x, m in zip(xs, mods)]
    gus = [_dot(h, wgu_ref[...]) for h in hs]
    acts = [(_silu(gu[:, :dff]) * gu[:, dff:]).astype(BF16) for gu in gus]
    ys = [_dot(a, wd_ref[...]) for a in acts]
    for k, (x, m, y) in enumerate(zip(xs, mods, ys)):
        o_ref[k * TM:(k + 1) * TM, :] = x + 0.5 * m[2] * (_rms(y) * g_post)


def _ffn(plan, xz, mod, norm_g, wgu, wd, j):
    d = xz.shape[-1]
    dff = wd.shape[2]
    l = plan.layer
    return pl.pallas_call(
        functools.partial(_ffn_kernel, j=j, dff=dff, sub=plan.sub),
        grid=plan.grid,
        in_specs=plan.toks(d) + plan.mods(d)
                 + [_const_spec((None,) + norm_g.shape[1:], (l, 0, 0)),
                    _const_spec((None, None, d, 2 * dff), (l, j // 2, 0, 0)),
                    _const_spec((None, None, dff, d), (l, j // 2, 0, 0))],
        out_specs=plan.out(d),
        out_shape=plan.out_shape(d, F32),
        compiler_params=_params("parallel"),
        name=f"ffn{j}",
    )(*([xz] * plan.sub), *([mod] * plan.sub), norm_g, wgu, wd)


def _inproj_kernel(*refs, sub, perm_out, perm_width):
    x_refs, mod_refs = refs[:sub], refs[sub:2 * sub]
    g_ref, w_ref, wg_ref = refs[2 * sub:2 * sub + 3]
    o_refs = refs[2 * sub + 3:]
    hs = [(_rms(x[...]) * g_ref[2:3, :] * (1.0 + m[4:5, :]) + m[3:4, :]).astype(BF16)
          for x, m in zip(x_refs, mod_refs)]
    perm = _time_perm(False)
    col = 0
    for oi, o_ref in enumerate(o_refs[:-1]):
        n = o_ref.shape[-1]
        for k, h in enumerate(hs):
            res = _dot(h, w_ref[:, col:col + n]).astype(o_ref.dtype)
            if oi == perm_out:
                res = jnp.concatenate([_dot(perm, res[:, :perm_width]).astype(o_ref.dtype),
                                       res[:, perm_width:]], axis=1)
            o_ref[k * TM:(k + 1) * TM, :] = res
        col += n
    for k, h in enumerate(hs):
        o_refs[-1][k * TM:(k + 1) * TM, :] = _dot(h, wg_ref[...])


def _inproj(plan, xz, mod, norm_g, w_main, w_gate, widths, perm_out, perm_width):
    d = xz.shape[-1]
    ng = w_gate.shape[-1]
    l = plan.layer
    return pl.pallas_call(
        functools.partial(_inproj_kernel, sub=plan.sub, perm_out=perm_out, perm_width=perm_width),
        grid=plan.grid,
        in_specs=plan.toks(d) + plan.mods(d)
                 + [_const_spec((None,) + norm_g.shape[1:], (l, 0, 0)),
                    _const_spec((None, d, w_main.shape[-1]), (l, 0, 0)),
                    _const_spec((None, d, ng), (l, 0, 0))],
        out_specs=[plan.out(w) for w in widths] + [plan.out(ng)],
        out_shape=[plan.out_shape(w, BF16) for w in widths] + [plan.out_shape(ng, F32)],
        compiler_params=_params("parallel"),
        name="inproj",
    )(*([xz] * plan.sub), *([mod] * plan.sub), norm_g, w_main, w_gate)


N_MERGE_STREAMS = 10


def _merge_kernel(*refs, sub):
    x_refs, mod_refs = refs[:sub], refs[sub:2 * sub]
    tok = refs[2 * sub:(2 + N_MERGE_STREAMS) * sub]
    g_ref, mlg_ref, dng_ref, wb_ref, wo_ref, o_ref = refs[(2 + N_MERGE_STREAMS) * sub:]
    d = o_ref.shape[-1]
    f32 = lambda r: r[...].astype(F32)
    unperm = _time_perm(True)
    mixes = []
    for k in range(sub):
        mlf, mlb, mlo, lrf, lrb, lry, dnf, dnb, dnz, mg = (tok[s * sub + k] for s in range(N_MERGE_STREAMS))
        y_ml = _rms_heads(f32(mlf) + f32(mlb), ML_HEADS) * mlg_ref[...] * _sigmoid(f32(mlo))
        y_lr = (_dot(unperm, lrf[...]) + _dot(unperm, lrb[...])) * _gelu_tanh(f32(lry))
        y_dn = _rms_heads(f32(dnf) + f32(dnb), DN_HEADS) * dng_ref[...] * _silu(f32(dnz))
        mix = None
        for n, y in enumerate((y_ml, y_lr, y_dn)):
            term = _sigmoid(mg[:, n * d:(n + 1) * d].astype(F32)) * _dot(y.astype(BF16), wb_ref[n])
            mix = term if mix is None else mix + term
        mixes.append(mix.astype(BF16))
    outs = [_dot(mix, wo_ref[...]) for mix in mixes]
    for k, out in enumerate(outs):
        o_ref[k * TM:(k + 1) * TM, :] = x_refs[k][...] + mod_refs[k][5:6, :] * (_rms(out) * g_ref[3:4, :])


def _merge(plan, xz, mod, norm_g, ml_hf, ml_hb, ml, lr_hf, lr_hb, lru, dn_f, dn_b, dnz, mg,
           ml_g, dn_g, w_branch, w_out):
    d = xz.shape[-1]
    bw = w_branch.shape[2]
    l = plan.layer
    streams = [(ml_hf, bw, 0), (ml_hb, bw, 0), (ml, bw, ml.shape[-1] // bw - 1),
               (lr_hf, bw, 0), (lr_hb, bw, 0), (lru, bw, lru.shape[-1] // bw - 1),
               (dn_f, bw, 0), (dn_b, bw, 0), (dnz, bw, 0), (mg, N_BRANCH * d, 0)]
    assert len(streams) == N_MERGE_STREAMS
    tok_specs = [sp for _, w, c in streams for sp in plan.toks(w, c)]
    tok_args = [a for a, _, _ in streams for _ in range(plan.sub)]
    return pl.pallas_call(
        functools.partial(_merge_kernel, sub=plan.sub),
        grid=plan.grid,
        in_specs=plan.toks(d) + plan.mods(d) + tok_specs
                 + [_const_spec((None,) + norm_g.shape[1:], (l, 0, 0)),
                    _const_spec((None, 1, bw), (l, 0, 0)),
                    _const_spec((None, 1, bw), (l, 0, 0)),
                    _const_spec((None, N_BRANCH, bw, d), (l, 0, 0, 0)),
                    _const_spec((None, d, d), (l, 0, 0))],
        out_specs=plan.out(d),
        out_shape=plan.out_shape(d, F32),
        compiler_params=_params("parallel"),
        name="merge",
    )(*([xz] * plan.sub), *([mod] * plan.sub), *tok_args, norm_g, ml_g, dn_g, w_branch, w_out)


SEG = TM // V7X_SUBLANES


def _time_perm(inverse):
    ri = lax.broadcasted_iota(jnp.int32, (TM, TM), 0)
    ci = lax.broadcasted_iota(jnp.int32, (TM, TM), 1)
    r, t = (ci, ri) if inverse else (ri, ci)
    return jnp.where(t == (r % V7X_SUBLANES) * SEG + r // V7X_SUBLANES, 1.0, 0.0).astype(BF16)


def _conv4_perm(x, before1, before2, after1, w):
    sub = lax.broadcasted_iota(jnp.int32, (V7X_SUBLANES, 1), 0)
    vrow = lambda i: x[V7X_SUBLANES * i:V7X_SUBLANES * (i + 1)]
    m1_edge = jnp.where(sub == 0, before1, pltpu.roll(vrow(SEG - 1), 1, axis=0))
    m2_edge = jnp.where(sub == 0, before2, pltpu.roll(vrow(SEG - 2), 1, axis=0))
    p1_edge = jnp.where(sub == V7X_SUBLANES - 1, after1, pltpu.roll(vrow(0), V7X_SUBLANES - 1, axis=0))
    x_m1 = jnp.concatenate([m1_edge, x[:-V7X_SUBLANES]], axis=0)
    x_m2 = jnp.concatenate([m2_edge, m1_edge, x[:-2 * V7X_SUBLANES]], axis=0)
    x_p1 = jnp.concatenate([x[V7X_SUBLANES:], p1_edge], axis=0)
    taps = {-2: x_m2, -1: x_m1, 0: x, 1: x_p1}
    acc = None
    for j in range(CONV_W):
        term = taps[j - CONV_LEFT] * w[j:j + 1]
        acc = term if acc is None else acc + term
    return acc


def _tile_scan(a, b, h0, rev):
    vrow = lambda x, i: x[V7X_SUBLANES * i:V7X_SUBLANES * (i + 1)]
    h = jnp.zeros_like(vrow(a, 0))
    p = jnp.ones_like(h)
    hs, ps = [None] * SEG, [None] * SEG
    for i in (range(SEG - 1, -1, -1) if rev else range(SEG)):
        ai = vrow(a, i)
        h = ai * h + vrow(b, i)
        p = ai * p
        hs[i], ps[i] = h, p
    carry = h0
    enter = [None] * V7X_SUBLANES
    for s in (range(V7X_SUBLANES - 1, -1, -1) if rev else range(V7X_SUBLANES)):
        enter[s] = carry
        carry = p[s:s + 1] * carry + h[s:s + 1]
    enter = jnp.concatenate(enter, axis=0)
    return jnp.concatenate([hi + pi * enter for hi, pi in zip(hs, ps)], axis=0), carry


def _lru_kernel(x_ref, cw_ref, cb_ref, w_ref, bias_ref, lam_ref, of_ref, ob_ref, *, n_lat, n_ctx):
    t, c = TM, x_ref.shape[-1]
    n_tiles = n_lat + n_ctx
    halo = 2 * V7X_SUBLANES

    def load_conv(tile):
        r0 = pl.multiple_of(tile * t, t)
        first = jnp.logical_or(tile == 0, tile == n_lat)
        last = jnp.logical_or(tile == n_lat - 1, tile == n_tiles - 1)
        x = x_ref[pl.ds(r0, t), :].astype(F32)
        p0 = pl.multiple_of(jnp.maximum(r0 - halo, 0), halo)
        n0 = pl.multiple_of(jnp.minimum(r0 + t, n_tiles * t - halo), halo)
        prev = jnp.where(first, 0.0, x_ref[pl.ds(p0, halo), :].astype(F32))
        nxt = jnp.where(last, 0.0, x_ref[pl.ds(n0, halo), :].astype(F32))
        before1, before2 = prev[halo - 1:halo], prev[V7X_SUBLANES - 1:V7X_SUBLANES]
        return r0, _conv4_perm(x, before1, before2, nxt[0:1], cw_ref[...]) + cb_ref[...]

    def direction(d, tile, h0, o_ref):
        r0, xc = load_conv(tile)
        z = _dot(xc.astype(BF16), w_ref[d]) + bias_ref[d]
        r, i = _sigmoid(z[:, :c]), _sigmoid(z[:, c:])
        la = (-LRU_C * _softplus(-lam_ref[d])) * r
        a = jnp.exp(la)
        bx = jnp.sqrt(jnp.tanh(-la) * (1.0 + a * a)) * (i * xc)
        h, carry = _tile_scan(a, bx, h0, d == 1)
        o_ref[pl.ds(r0, t), :] = h.astype(o_ref.dtype)
        return carry

    def step(s, carry):
        hf, hb = carry
        hf = direction(0, jnp.where(s < n_ctx, n_lat + s, s - n_ctx), hf, of_ref)
        hb = direction(1, n_tiles - 1 - s, hb, ob_ref)
        return hf, hb

    zero = jnp.zeros((1, c), F32)
    lax.fori_loop(0, n_tiles, step, (zero, zero))


def _lru(lru, conv_w, conv_b, w_gates, b_gates, lam, l, n_lat, n_ctx):
    b, lt, _ = lru.shape
    c = conv_w.shape[-1]
    seq = pl.BlockSpec((None, lt, c), lambda bi: (bi, 0, 0))
    return pl.pallas_call(
        functools.partial(_lru_kernel, n_lat=n_lat, n_ctx=n_ctx),
        grid=(b,),
        in_specs=[seq,
                  _const_spec((None, CONV_W, c), (l, 0, 0)),
                  _const_spec((None, 1, c), (l, 0, 0)),
                  _const_spec((None, 2, c, 2 * c), (l, 0, 0, 0)),
                  _const_spec((None, 2, 1, 2 * c), (l, 0, 0, 0)),
                  _const_spec((None, 2, 1, c), (l, 0, 0, 0))],
        out_specs=[seq, seq],
        out_shape=[jax.ShapeDtypeStruct((b, lt, c), BF16)] * 2,
        compiler_params=_params("parallel"),
        name="lru",
    )(lru, conv_w, conv_b, w_gates, b_gates, lam)


def _mlstm_kernel(qf_ref, kf_ref, vf_ref, qb_ref, kb_ref, vb_ref, grf_ref, grb_ref, br_ref,
                  hf_ref, hb_ref, c_ref, m_ref):
    @pl.when(pl.program_id(1) == 0)
    def _():
        c_ref[...] = jnp.zeros(c_ref.shape, F32)
        m_ref[...] = jnp.zeros(m_ref.shape, F32)

    tc, nqk = qf_ref.shape
    dk = nqk // ML_HEADS
    dv = vf_ref.shape[1] // ML_HEADS
    nh = ML_HEADS
    ri = lax.broadcasted_iota(jnp.int32, (tc, tc), 0)
    ci = lax.broadcasted_iota(jnp.int32, (tc, tc), 1)
    causal = (ci <= ri, ci >= ri)
    lane_head = lax.broadcasted_iota(jnp.int32, (1, nqk), 1) // dk
    ones = jnp.ones((tc, dv), BF16)
    dirs = ((qf_ref, kf_ref, vf_ref, grf_ref), (qb_ref, kb_ref, vb_ref, grb_ref))

    rows, v_rows, ws_rows, decs = [], [], [], []
    for d, (_, _, _, gr_ref) in enumerate(dirs):
        rev = d == 1
        gr = gr_ref[...] + br_ref[...]
        i_r = gr[nh * d:nh * (d + 1)]
        b_r = _cumsum_groups(-_softplus(-gr[nh * (2 + d):nh * (3 + d)]), 1, tc, rev)
        m_prev = m_ref[nh * d:nh * (d + 1), 0:1]
        m_t = b_r + jnp.maximum(m_prev, _cummax_groups(i_r - b_r, 1, tc, rev))
        b_end = b_r[:, 0:1] if rev else b_r[:, tc - 1:tc]
        lws = b_end - b_r + i_r
        m_new = jnp.maximum(b_end + m_prev, jnp.max(lws, axis=1, keepdims=True))
        rows += [b_r - m_t, jnp.exp(b_r + m_prev - m_t), jnp.exp(-m_t)]
        v_rows.append(b_r - i_r)
        ws_rows.append(jnp.exp(lws - m_new))
        decs.append(jnp.exp(b_end + m_prev - m_new))
        m_ref[nh * d:nh * (d + 1), :] = jnp.broadcast_to(m_new, (nh, m_ref.shape[1]))
    cols = jnp.concatenate(rows, axis=0).T

    chains = [(d, h) for d in range(2) for h in range(nh)]
    q_all = [dirs[d][0][...] * (dk ** -0.5) for d in range(2)]
    k_all = [dirs[d][1][...] for d in range(2)]
    v_all = [dirs[d][2][...] for d in range(2)]
    kt_all = [k.astype(F32).T for k in k_all]
    c_all = [c_ref[d] for d in range(2)]
    cb_all = [c.astype(BF16) for c in c_all]
    col = lambda d, j, h: cols[:, (3 * d + j) * nh + h:(3 * d + j) * nh + h + 1]
    qh = [jnp.where(lane_head == h, q_all[d], jnp.zeros_like(q_all[d])) for d, h in chains]
    vp = [jnp.concatenate([v_all[d][:, h * dv:(h + 1) * dv], ones], axis=1) for d, h in chains]
    s_raw = [_dot_nt(q, k_all[d]) for q, (d, h) in zip(qh, chains)]
    p = [(s * jnp.exp(jnp.where(causal[d], col(d, 0, h) - v_rows[d][h:h + 1], NEG))).astype(BF16)
         for s, (d, h) in zip(s_raw, chains)]
    num = [_dot(pc, vc) + col(d, 1, h) * _dot(q, cb_all[d]) for pc, vc, q, (d, h) in zip(p, vp, qh, chains)]
    outs = [nm[:, :dv] / jnp.maximum(jnp.abs(nm[:, dv:]), col(d, 2, h)) for nm, (d, h) in zip(num, chains)]
    for vc, (d, h) in zip(vp, chains):
        kw = (kt_all[d][h * dk:(h + 1) * dk] * ws_rows[d][h:h + 1]).astype(BF16)
        c_ref[d, h * dk:(h + 1) * dk, :] = decs[d][h:h + 1] * c_all[d][h * dk:(h + 1) * dk] + _dot(kw, vc)
    hf_ref[...] = jnp.concatenate(outs[:nh], axis=1).astype(hf_ref.dtype)
    hb_ref[...] = jnp.concatenate(outs[nh:], axis=1).astype(hb_ref.dtype)


def _mlstm(ml, gates_t, bias_r, l, n_lat, n_ctx):
    b, lt, _ = ml.shape
    ng = gates_t.shape[1]
    n_tiles = n_lat + n_ctx
    dqk = ml.shape[-1] // 6
    fwd = lambda s: jnp.where(s < n_ctx, n_lat + s, s - n_ctx)
    bwd = lambda s: n_tiles - 1 - s

    def specs(tile):
        return [pl.BlockSpec((None, TM, dqk), lambda bi, s: (bi, tile(s), 0)),
                pl.BlockSpec((None, TM, dqk), lambda bi, s: (bi, tile(s), 1)),
                pl.BlockSpec((None, TM, 2 * dqk), lambda bi, s: (bi, tile(s), 1))]

    row = lambda tile: pl.BlockSpec((None, ng, TM), lambda bi, s: (bi, 0, tile(s)))
    out = lambda tile: pl.BlockSpec((None, TM, 2 * dqk), lambda bi, s: (bi, tile(s), 0))
    return pl.pallas_call(
        _mlstm_kernel,
        grid=(b, n_tiles),
        in_specs=specs(fwd) + specs(bwd) + [row(fwd), row(bwd), _const_spec((None, ng, 1), (l, 0, 0))],
        out_specs=[out(fwd), out(bwd)],
        out_shape=[jax.ShapeDtypeStruct((b, lt, 2 * dqk), BF16)] * 2,
        scratch_shapes=[pltpu.VMEM((2, dqk, 2 * (2 * dqk // ML_HEADS)), F32),
                        pltpu.VMEM((2 * ML_HEADS, V7X_LANES), F32)],
        compiler_params=_params("parallel", "arbitrary"),
        name="mlstm",
    )(ml, ml, ml, ml, ml, ml, gates_t, gates_t, bias_r)


INV_BASE = 8


def _inv_masks(ri, ci):
    blk = lambda s: (ri // s) == (ci // s)
    as_bf = lambda m: jnp.where(m, 1.0, 0.0).astype(BF16)
    masks, s = [as_bf(blk(INV_BASE))], INV_BASE
    while s < CHUNK:
        masks.append(as_bf(jnp.logical_and(blk(2 * s), jnp.logical_not(blk(s)))))
        s *= 2
    return masks


def _tri_inv_many(a_list, eye, masks):
    ps = [a * masks[0] for a in a_list]
    xs = [eye - p.astype(F32) for p in ps]
    for _ in range(INV_BASE.bit_length() - 2):
        ps = [_dot(p, p).astype(BF16) for p in ps]
        xs = [x + _dot(x.astype(BF16), p) for x, p in zip(xs, ps)]
    xbs = [x.astype(BF16) for x in xs]
    for m in masks[1:]:
        ys = [_dot(xb, a * m).astype(BF16) for xb, a in zip(xbs, a_list)]
        xbs = [xb - _dot(y, xb).astype(BF16) for y, xb in zip(ys, xbs)]
    return xbs


def _dn_prep_kernel(x_ref, xp_ref, xn_ref, grow_ref, cw_ref, arow_ref,
                    u_ref, w_ref, qe_ref, kd_ref, qk_ref, eg_ref, *, first_groups, last_groups):
    g = pl.program_id(1)
    width = x_ref.shape[-1]
    dh = width // (3 * DN_HEADS)
    n = DN_HEADS * CHUNK
    halo = 2 * V7X_SUBLANES
    any_of = lambda groups: functools.reduce(jnp.logical_or, [g == v for v in groups])
    x = x_ref[...].reshape(DN_GROUP * CHUNK, width).astype(F32)
    prev8 = jnp.where(any_of(first_groups), 0.0, xp_ref[CHUNK - halo:, :].astype(F32)[V7X_SUBLANES:])
    nxt8 = jnp.where(any_of(last_groups), 0.0, xn_ref[:halo, :].astype(F32)[:V7X_SUBLANES])
    xc = _silu(_conv4(x, prev8, nxt8, cw_ref[...]))

    ri = lax.broadcasted_iota(jnp.int32, (n, n), 0)
    ci = lax.broadcasted_iota(jnp.int32, (n, n), 1)
    same = (ri // CHUNK) == (ci // CHUNK)
    eye = jnp.where(ri == ci, 1.0, 0.0)
    incl = (jnp.logical_and(same, ci <= ri), jnp.logical_and(same, ci >= ri))
    strict = (jnp.logical_and(same, ci < ri), jnp.logical_and(same, ci > ri))
    masks = _inv_masks(ri, ci)

    qn, kn, v, kk, qk = [], [], [], [], []
    for i in range(DN_GROUP):
        xi = xc[i * CHUNK:(i + 1) * CHUNK]
        q, k, vi = (_stack_heads(xi[:, j * DN_HEADS * dh:(j + 1) * DN_HEADS * dh], DN_HEADS) for j in range(3))
        qn.append(q * lax.rsqrt(jnp.sum(q * q, axis=-1, keepdims=True) + EPS) * (dh ** -0.5))
        kn.append(k * lax.rsqrt(jnp.sum(k * k, axis=-1, keepdims=True) + EPS))
        v.append(vi)
        kb = kn[i].astype(BF16)
        kk.append(_dot_nt(kb, kb))
        qk.append(_dot_nt(qn[i].astype(BF16), kb))

    systems = [(i, d) for i in range(DN_GROUP) for d in range(2)]
    rows, cs_rows = [], []
    for i, d in systems:
        raw = grow_ref[i]
        beta = _sigmoid(raw[d:d + 1])
        g_r = -jnp.exp(arow_ref[d, 0:1, :]) * _softplus(raw[2 + d:3 + d] + arow_ref[d, 1:2, :])
        cs = _cumsum_groups(g_r, 1, CHUNK, d == 1)
        tot = cs + _cumsum_groups(g_r, 1, CHUNK, d == 0) - g_r
        eg = jnp.exp(cs)
        rows += [cs, beta, eg, jnp.exp(tot - cs), beta * eg]
        cs_rows.append(cs)
        eg_ref[d, i] = jnp.exp(tot)
    n_col = len(rows) // len(systems)
    cols = jnp.concatenate(rows, axis=0).T

    a_list, rhs = [], []
    for s, (i, d) in enumerate(systems):
        cs_c, beta_c, eg_c, ekd_c, beg_c = (cols[:, n_col * s + j:n_col * s + j + 1] for j in range(n_col))
        gam = jnp.exp(jnp.where(incl[d], cs_c - cs_rows[s], NEG))
        a_list.append((jnp.where(strict[d], kk[i], 0.0) * gam * beta_c).astype(BF16))
        rhs.append(jnp.concatenate([beta_c * v[i], beg_c * kn[i]], axis=1).astype(BF16))
        qk_ref[d, i] = (qk[i] * gam).astype(qk_ref.dtype)
        qe_ref[d, i] = (qn[i] * eg_c).astype(qe_ref.dtype)
        kd_ref[d, i] = (kn[i] * ekd_c).astype(kd_ref.dtype)
    for (i, d), tinv, r in zip(systems, _tri_inv_many(a_list, eye, masks), rhs):
        sol = _dot(tinv, r)
        u_ref[d, i] = sol[:, :dh].astype(u_ref.dtype)
        w_ref[d, i] = sol[:, dh:].astype(w_ref.dtype)


def _dn_prep(xs, grow, conv_w, arow, l, n_lat_groups):
    b, nc, _, width = xs.shape
    n_groups = nc // DN_GROUP
    n = DN_HEADS * CHUNK
    dh = width // (3 * DN_HEADS)
    out = lambda w, dt: (jax.ShapeDtypeStruct((b, 2, nc, n, w), dt),
                         pl.BlockSpec((None, 2, DN_GROUP, n, w), lambda bi, g: (bi, 0, g, 0, 0)))
    outs = [out(dh, BF16)] * 4 + [out(n, BF16)]
    outs.append((jax.ShapeDtypeStruct((b, 2, nc, 1, n), F32),
                 pl.BlockSpec((None, 2, DN_GROUP, 1, n), lambda bi, g: (bi, 0, g, 0, 0))))
    chunk = lambda idx: pl.BlockSpec((None, None, CHUNK, width), lambda bi, g: (bi, idx(g), 0, 0))
    return pl.pallas_call(
        functools.partial(_dn_prep_kernel, first_groups=(0, n_lat_groups),
                          last_groups=(n_lat_groups - 1, n_groups - 1)),
        grid=(b, n_groups),
        in_specs=[pl.BlockSpec((None, DN_GROUP, CHUNK, width), lambda bi, g: (bi, g, 0, 0)),
                  chunk(lambda g: jnp.maximum(g * DN_GROUP - 1, 0)),
                  chunk(lambda g: jnp.minimum(g * DN_GROUP + DN_GROUP, nc - 1)),
                  pl.BlockSpec((None, DN_GROUP, 4, n), lambda bi, g: (bi, g, 0, 0)),
                  _const_spec((None, CONV_W, width), (l, 0, 0)),
                  _const_spec((None, 2, 2, n), (l, 0, 0, 0))],
        out_specs=[o[1] for o in outs],
        out_shape=[o[0] for o in outs],
        compiler_params=_params("parallel", "parallel"),
        name="dn_prep",
    )(xs, xs, xs, grow, conv_w, arow)


def _dn_scan_kernel(*refs):
    ins, (of_ref, ob_ref, s_ref) = refs[:12], refs[12:]

    @pl.when(pl.program_id(1) == 0)
    def _():
        s_ref[...] = jnp.zeros(s_ref.shape, F32)

    rows = [slice(h * CHUNK, (h + 1) * CHUNK) for h in range(DN_HEADS)]
    seqs = [(r, d) for r in range(of_ref.shape[0]) for d in range(2)]
    get = lambda j, r, d: ins[6 * d + j][r]
    sb = [[s_ref[r, d, h].astype(BF16) for h in range(DN_HEADS)] for r, d in seqs]
    ws = [jnp.concatenate([_dot(get(1, r, d)[rows[h]], sb[i][h]) for h in range(DN_HEADS)], axis=0)
          for i, (r, d) in enumerate(seqs)]
    qs = [jnp.concatenate([_dot(get(2, r, d)[rows[h]], sb[i][h]) for h in range(DN_HEADS)], axis=0)
          for i, (r, d) in enumerate(seqs)]
    vnew = [(get(0, r, d).astype(F32) - w).astype(BF16) for w, (r, d) in zip(ws, seqs)]
    o = [q + _dot(get(4, r, d), vn) for q, vn, (r, d) in zip(qs, vnew, seqs)]
    for vn, (r, d) in zip(vnew, seqs):
        kd, eg = get(3, r, d), get(5, r, d)
        for h in range(DN_HEADS):
            s_ref[r, d, h] = (eg[:, h * CHUNK:h * CHUNK + 1] * s_ref[r, d, h]
                              + _dot_tn(kd[rows[h]], vn[rows[h]]))
    for oi, (r, d) in zip(o, seqs):
        o_ref = (of_ref, ob_ref)[d]
        o_ref[r] = _unstack_heads(oi, DN_HEADS).astype(o_ref.dtype)


def _dn_scan(prep, n_lat_chunks):
    b, _, nc, _, dh = prep[0].shape
    ow = DN_HEADS * dh
    n_ctx_chunks = nc - n_lat_chunks
    fwd = lambda s: jnp.where(s < n_ctx_chunks, n_lat_chunks + s, s - n_ctx_chunks)
    bwd = lambda s: nc - 1 - s

    rpb = DN_SCAN_ROWS if b % DN_SCAN_ROWS == 0 else 1

    def spec(a, d, chunk):
        return pl.BlockSpec((rpb, None, None) + a.shape[3:], lambda bi, s: (bi, d, chunk(s), 0, 0))

    out = lambda chunk: pl.BlockSpec((rpb, None, CHUNK, ow), lambda bi, s: (bi, chunk(s), 0, 0))
    return pl.pallas_call(
        _dn_scan_kernel,
        grid=(b // rpb, nc),
        in_specs=[spec(a, 0, fwd) for a in prep] + [spec(a, 1, bwd) for a in prep],
        out_specs=[out(fwd), out(bwd)],
        out_shape=[jax.ShapeDtypeStruct((b, nc, CHUNK, ow), BF16)] * 2,
        scratch_shapes=[pltpu.VMEM((rpb, 2, DN_HEADS, dh, dh), F32)],
        compiler_params=_params("parallel", "arbitrary"),
        name="dn_scan",
    )(*prep, *prep)


def _block_diag(w):
    n, i, j = w.shape
    return jnp.einsum('nij,nm->nimj', w, jnp.eye(n, dtype=w.dtype)).reshape(n * i, n * j)


def _to_chunks(a, n_lat):
    b, _, w = a.shape
    lat = a[:, :n_lat].reshape(b, n_lat // CHUNK, CHUNK, w).swapaxes(1, 2)
    return jnp.concatenate([lat, a[:, n_lat:].reshape(b, -1, CHUNK, w)], axis=1)


def _from_chunks(a, n_lat):
    b, _, _, w = a.shape
    lat = a[:, :n_lat // CHUNK].swapaxes(1, 2).reshape(b, n_lat, w)
    return jnp.concatenate([lat, a[:, n_lat // CHUNK:].reshape(b, -1, w)], axis=1)


def _dn_gate_rows(raw, n_lat):
    b = raw.shape[0]
    lat = raw[:, :n_lat].reshape(b, n_lat // CHUNK, CHUNK, 4, DN_HEADS)
    ctx = raw[:, n_lat:].reshape(b, -1, CHUNK, 4, DN_HEADS)
    row = jnp.concatenate([lat.transpose(0, 2, 3, 4, 1), ctx.transpose(0, 1, 3, 4, 2)], axis=1)
    return row.reshape(b, row.shape[1], 4, DN_HEADS * CHUNK)


def kernel(x, c, ctx, c_ctx, w_mod, b_mod, norm_g, ffn_w_gu, ffn_w_down, w_in, ml_gate_b, ml_norm_g,
           lru_conv_w, lru_conv_b, lru_w_a, lru_b_a, lru_w_x, lru_b_x, lru_lambda, dn_conv_w,
           dn_a_log, dn_dt_bias, dn_norm_g, w_branch, w_out):
    b, n_lat, d = x.shape
    n_ctx = ctx.shape[1]
    depth = w_mod.shape[0]
    bw = w_branch.shape[2]
    assert n_lat == CHUNK * CHUNK and n_lat % TM == 0 and n_ctx % TM == 0 and n_ctx % (CHUNK * DN_GROUP) == 0
    n_lat_tiles, n_ctx_tiles = n_lat // TM, n_ctx // TM
    n_tiles = n_lat_tiles + n_ctx_tiles

    ctx_row = b
    n_rows = -(-(b + 1) // V7X_SUBLANES) * V7X_SUBLANES
    cc = jnp.zeros((n_rows, d), F32).at[:b].set(c).at[b].set(c_ctx)
    mod = _mod_table(cc, w_mod, b_mod).reshape(depth, n_rows, N_MOD, d)

    dqk = bw // 2
    edges = [0]
    for wdt in (dqk, dqk, bw, bw, N_GATES, bw, bw, bw, bw, bw, bw, N_GATES, N_BRANCH * d):
        edges.append(edges[-1] + wdt)
    piece = lambda i, j: w_in[:, :, edges[i]:edges[j]]
    widths = (3 * bw, 2 * bw, 3 * bw, bw, N_BRANCH * d)
    w_main = jnp.concatenate([piece(0, 4), piece(5, 7), piece(7, 11), piece(12, 13)], axis=-1).astype(BF16)
    w_gate = jnp.concatenate([piece(4, 5), piece(11, 12)], axis=-1).astype(BF16)
    wgu = ffn_w_gu.astype(BF16)
    wdn = ffn_w_down.astype(BF16)
    wbr = w_branch.astype(BF16)
    wout = w_out.astype(BF16)
    ml_bias = jnp.concatenate([ml_gate_b.reshape(depth, N_GATES), jnp.zeros((depth, N_GATES), F32)], axis=-1)
    ml_bias_r = ml_bias[:, :, None]
    ml_g = ml_norm_g[:, None, :]
    dn_g = jnp.tile(dn_norm_g, (1, DN_HEADS))[:, None, :]
    lru_w = jnp.stack([jnp.concatenate([jax.vmap(_block_diag)(lru_w_a[:, dd]), jax.vmap(_block_diag)(lru_w_x[:, dd])],
                                       axis=-1) for dd in range(2)], axis=1).astype(BF16)
    lru_b = jnp.concatenate([lru_b_a, lru_b_x], axis=-1)[:, :, None, :]
    lru_lam = lru_lambda[:, :, None, :]
    lru_cb = lru_conv_b[:, None, :]
    dn_arow = jnp.repeat(jnp.stack([dn_a_log, dn_dt_bias], axis=2), CHUNK, axis=3)

    xz = _flat(jnp.concatenate([x, ctx], axis=1))
    rows3 = lambda a: a.reshape(b, n_lat + n_ctx, a.shape[-1])
    for l in range(depth):
        full = _TilePlan(b, n_tiles, n_lat_tiles, n_tiles, ctx_row, l)
        xz = _ffn(full, xz, mod, norm_g, wgu, wdn, 0)
        ml, lru, dnqkv, dnz, mg, gates = _inproj(full, xz, mod, norm_g, w_main, w_gate, widths, 1, bw)
        gates = rows3(gates)
        ml_hf, ml_hb = _mlstm(rows3(ml), gates.transpose(0, 2, 1), ml_bias_r, l, n_lat_tiles, n_ctx_tiles)
        lr_hf, lr_hb = _lru(rows3(lru), lru_conv_w, lru_cb, lru_w, lru_b, lru_lam, l, n_lat_tiles, n_ctx_tiles)
        prep = _dn_prep(_to_chunks(rows3(dnqkv), n_lat), _dn_gate_rows(gates[:, :, N_GATES:], n_lat),
                        dn_conv_w, dn_arow, l, n_lat // (CHUNK * DN_GROUP))
        dn_f, dn_b = (_flat(_from_chunks(o, n_lat)) for o in _dn_scan(prep, n_lat // CHUNK))
        xz = _merge(full, xz, mod, norm_g, _flat(ml_hf), _flat(ml_hb), ml, _flat(lr_hf), _flat(lr_hb), lru,
                    dn_f, dn_b, dnz, mg, ml_g, dn_g, wbr, wout)
        visit = n_lat_tiles if l == depth - 1 else n_tiles
        xz = _ffn(_TilePlan(b, n_tiles, n_lat_tiles, visit, ctx_row, l), xz, mod, norm_g, wgu, wdn, 2)
    return xz.reshape(b, n_lat, d)
```

```python
import functools
import math

import jax
import jax.numpy as jnp
from jax import lax
from jax.experimental import pallas as pl
from jax.experimental.pallas import tpu as pltpu

F32 = jnp.float32
BF16 = jnp.bfloat16

EPS = 1e-6
N_MOD = 9
N_BRANCH = 3
CONV_W = 4
CONV_LEFT = 2
ML_HEADS = 4
LRU_C = 8.0
DN_HEADS = 4
CHUNK = 64
DN_GROUP = 4
DN_SCAN_ROWS = 4
N_GATES = 16

V7X_SUBLANES = 8
V7X_LANES = 128
V7X_VMEM_BYTES = 64 * 1024 * 1024
VMEM_LIMIT = V7X_VMEM_BYTES - 8 * 1024 * 1024

TM = 256
NEG = -1e30


def _sigmoid(x):
    return 0.5 * jnp.tanh(0.5 * x) + 0.5


def _silu(x):
    return x * _sigmoid(x)


def _softplus(x):
    return jnp.maximum(x, 0.0) + jnp.log(1.0 + jnp.exp(-jnp.abs(x)))


def _rms(x):
    return x * lax.rsqrt(jnp.mean(x * x, axis=-1, keepdims=True) + EPS)


def _rms_heads(x, n_heads):
    hd = x.shape[-1] // n_heads
    return jnp.concatenate([_rms(x[:, h * hd:(h + 1) * hd]) for h in range(n_heads)], axis=-1)


def _gelu_tanh(x):
    return 0.5 * x * (1.0 + jnp.tanh(math.sqrt(2.0 / math.pi) * (x + 0.044715 * (x * x * x))))


def _dot(a, b):
    return jnp.dot(a, b, preferred_element_type=F32)


def _dot_nt(a, b):
    return lax.dot_general(a, b, (((1,), (1,)), ((), ())), preferred_element_type=F32)


def _dot_tn(a, b):
    return lax.dot_general(a, b, (((0,), (0,)), ((), ())), preferred_element_type=F32)


def _params(*sem):
    return pltpu.CompilerParams(dimension_semantics=sem, vmem_limit_bytes=VMEM_LIMIT)


def _const_spec(block, index):
    return pl.BlockSpec(block, lambda *_: index, pipeline_mode=pl.Buffered(1))


def _stack_heads(x, n):
    w = x.shape[1] // n
    return jnp.concatenate([x[:, h * w:(h + 1) * w] for h in range(n)], axis=0)


def _unstack_heads(x, n):
    t = x.shape[0] // n
    return jnp.concatenate([x[h * t:(h + 1) * t] for h in range(n)], axis=1)


def _cumsum_groups(x, axis, period, rev):
    n = x.shape[axis]
    idx = lax.broadcasted_iota(jnp.int32, x.shape, axis) % period
    sh = 1
    while sh < period:
        if rev:
            x = x + jnp.where(idx < period - sh, pltpu.roll(x, n - sh, axis=axis), 0.0)
        else:
            x = x + jnp.where(idx >= sh, pltpu.roll(x, sh, axis=axis), 0.0)
        sh *= 2
    return x


def _cummax_groups(x, axis, period, rev):
    n = x.shape[axis]
    idx = lax.broadcasted_iota(jnp.int32, x.shape, axis) % period
    sh = 1
    while sh < period:
        if rev:
            x = jnp.maximum(x, jnp.where(idx < period - sh, pltpu.roll(x, n - sh, axis=axis), NEG))
        else:
            x = jnp.maximum(x, jnp.where(idx >= sh, pltpu.roll(x, sh, axis=axis), NEG))
        sh *= 2
    return x


def _shift_rows(xb, prev8, nxt8, off):
    t = xb.shape[0]
    ri = lax.broadcasted_iota(jnp.int32, (t, t), 0)
    ci = lax.broadcasted_iota(jnp.int32, (t, t), 1)
    shifted = _dot(jnp.where(ci == ri + off, 1.0, 0.0).astype(BF16), xb)
    row8 = lax.broadcasted_iota(jnp.int32, (V7X_SUBLANES, 1), 0)
    if off < 0:
        edge = jnp.where(row8 < -off, pltpu.roll(prev8, -off, axis=0), shifted[:V7X_SUBLANES])
        return jnp.concatenate([edge, shifted[V7X_SUBLANES:]], axis=0)
    edge = jnp.where(row8 >= V7X_SUBLANES - off, pltpu.roll(nxt8, V7X_SUBLANES - off, axis=0),
                     shifted[t - V7X_SUBLANES:])
    return jnp.concatenate([shifted[:t - V7X_SUBLANES], edge], axis=0)


def _conv4(xb, prev8, nxt8, w):
    acc = xb.astype(F32) * w[CONV_LEFT:CONV_LEFT + 1]
    for j in range(CONV_W):
        if j != CONV_LEFT:
            acc = acc + _shift_rows(xb, prev8, nxt8, j - CONV_LEFT) * w[j:j + 1]
    return acc


def _mod_kernel(c_ref, w_ref, b_ref, o_ref):
    s = _silu(c_ref[...]).astype(BF16)
    o_ref[...] = _dot(s, w_ref[...].astype(BF16)) + b_ref[...]


def _mod_table(cc, w_mod, b_mod):
    depth, d, nd = w_mod.shape
    r = cc.shape[0]
    tn = nd // 4
    return pl.pallas_call(
        _mod_kernel,
        grid=(depth, nd // tn),
        in_specs=[pl.BlockSpec((r, d), lambda l, j: (0, 0)),
                  pl.BlockSpec((None, d, tn), lambda l, j: (l, 0, j)),
                  pl.BlockSpec((None, 1, tn), lambda l, j: (l, 0, j))],
        out_specs=pl.BlockSpec((None, r, tn), lambda l, j: (l, 0, j)),
        out_shape=jax.ShapeDtypeStruct((depth, r, nd), F32),
        compiler_params=_params("parallel", "parallel"),
        name="mod_table",
    )(cc, w_mod, b_mod.reshape(depth, 1, nd))


class _TilePlan:
    def __init__(self, b, n_tiles, n_lat_tiles, n_vis, ctx_row, layer):
        self.b, self.n_tiles, self.n_lat_tiles, self.n_vis = b, n_tiles, n_lat_tiles, n_vis
        self.ctx_row, self.layer = ctx_row, layer
        self.sub = 2 if (b * n_vis) % 2 == 0 else 1
        self.grid = (b * n_vis // self.sub,)

    def _tile(self, i, k):
        tid = i * self.sub + k
        return tid // self.n_vis, tid % self.n_vis

    def tok(self, k, width, col=0):
        def index(i):
            bi, t = self._tile(i, k)
            return bi * self.n_tiles + t, col
        return pl.BlockSpec((TM, width), index)

    def toks(self, width, col=0):
        return [self.tok(k, width, col) for k in range(self.sub)]

    def is_ctx(self, i, k):
        return self._tile(i, k)[1] >= self.n_lat_tiles

    def tok_split(self, k, width, ctx):
        n_ctx_tiles = self.n_tiles - self.n_lat_tiles

        def index(i):
            bi, t = self._tile(i, k)
            if ctx:
                return bi * n_ctx_tiles + jnp.clip(t - self.n_lat_tiles, 0, n_ctx_tiles - 1), 0
            return bi * self.n_lat_tiles + jnp.minimum(t, self.n_lat_tiles - 1), 0
        return pl.BlockSpec((TM, width), index)

    def mods(self, d):
        def spec(k):
            def index(i):
                bi, t = self._tile(i, k)
                return self.layer, jnp.where(t >= self.n_lat_tiles, self.ctx_row, bi), 0, 0
            return pl.BlockSpec((None, None, N_MOD, d), index)
        return [spec(k) for k in range(self.sub)]

    def out(self, width):
        return pl.BlockSpec((self.sub * TM, width), lambda i: (i, 0))

    def out_shape(self, width, dtype):
        return jax.ShapeDtypeStruct((self.b * self.n_vis * TM, width), dtype)

    def rows(self, k):
        return slice(k * TM, (k + 1) * TM)


def _flat(a):
    return a.reshape(-1, a.shape[-1])


def _ffn_kernel(*refs, j, dff, sub, ctx_tiles):
    if ctx_tiles is not None:
        lat_refs, ctx_refs, refs = refs[:sub], refs[sub:2 * sub], refs[sub:]
        xs = [jnp.where(ctx_tiles(k), c[...], x[...]) for k, (x, c) in enumerate(zip(lat_refs, ctx_refs))]
    else:
        xs = [x[...] for x in refs[:sub]]
    mod_refs = refs[sub:2 * sub]
    g_ref, wgu_ref, wd_ref, o_ref = refs[2 * sub:]
    g_pre, g_post = g_ref[2 * j:2 * j + 1, :], g_ref[2 * j + 1:2 * j + 2, :]
    mods = [[m[3 * j + i:3 * j + i + 1, :] for i in range(3)] for m in mod_refs]
    hs = [(_rms(x) * g_pre * (1.0 + m[1]) + m[0]).astype(BF16) for x, m in zip(xs, mods)]
    gus = [_dot(h, wgu_ref[...]) for h in hs]
    acts = [(_silu(gu[:, :dff]) * gu[:, dff:]).astype(BF16) for gu in gus]
    ys = [_dot(a, wd_ref[...]) for a in acts]
    for k, (x, m, y) in enumerate(zip(xs, mods, ys)):
        o_ref[k * TM:(k + 1) * TM, :] = x + 0.5 * m[2] * (_rms(y) * g_post)


def _ffn(plan, xz, mod, norm_g, wgu, wd, j, ctx=None):
    d = xz.shape[-1]
    dff = wd.shape[2]
    l = plan.layer
    if ctx is None:
        tok_specs, tok_args, ctx_tiles = plan.toks(d), [xz] * plan.sub, None
    else:
        tok_specs = [plan.tok_split(k, d, False) for k in range(plan.sub)] \
                    + [plan.tok_split(k, d, True) for k in range(plan.sub)]
        tok_args = [xz] * plan.sub + [ctx] * plan.sub
        ctx_tiles = lambda k: plan.is_ctx(pl.program_id(0), k)
    return pl.pallas_call(
        functools.partial(_ffn_kernel, j=j, dff=dff, sub=plan.sub, ctx_tiles=ctx_tiles),
        grid=plan.grid,
        in_specs=tok_specs + plan.mods(d)
                 + [_const_spec((None,) + norm_g.shape[1:], (l, 0, 0)),
                    _const_spec((None, None, d, 2 * dff), (l, j // 2, 0, 0)),
                    _const_spec((None, None, dff, d), (l, j // 2, 0, 0))],
        out_specs=plan.out(d),
        out_shape=plan.out_shape(d, F32),
        compiler_params=_params("parallel"),
        name=f"ffn{j}",
    )(*tok_args, *([mod] * plan.sub), norm_g, wgu, wd)


def _inproj_kernel(*refs, sub, perm_out, perm_width):
    x_refs, mod_refs = refs[:sub], refs[sub:2 * sub]
    g_ref, w_ref, wg_ref = refs[2 * sub:2 * sub + 3]
    o_refs = refs[2 * sub + 3:]
    hs = [(_rms(x[...]) * g_ref[2:3, :] * (1.0 + m[4:5, :]) + m[3:4, :]).astype(BF16)
          for x, m in zip(x_refs, mod_refs)]
    perm = _time_perm(False)
    col = 0
    for oi, o_ref in enumerate(o_refs[:-1]):
        n = o_ref.shape[-1]
        for k, h in enumerate(hs):
            res = _dot(h, w_ref[:, col:col + n]).astype(o_ref.dtype)
            if oi == perm_out:
                res = jnp.concatenate([_dot(perm, res[:, :perm_width]).astype(o_ref.dtype),
                                       res[:, perm_width:]], axis=1)
            o_ref[k * TM:(k + 1) * TM, :] = res
        col += n
    for k, h in enumerate(hs):
        o_refs[-1][k * TM:(k + 1) * TM, :] = _dot(h, wg_ref[...])


def _inproj(plan, xz, mod, norm_g, w_main, w_gate, widths, perm_out, perm_width):
    d = xz.shape[-1]
    ng = w_gate.shape[-1]
    l = plan.layer
    return pl.pallas_call(
        functools.partial(_inproj_kernel, sub=plan.sub, perm_out=perm_out, perm_width=perm_width),
        grid=plan.grid,
        in_specs=plan.toks(d) + plan.mods(d)
                 + [_const_spec((None,) + norm_g.shape[1:], (l, 0, 0)),
                    _const_spec((None, d, w_main.shape[-1]), (l, 0, 0)),
                    _const_spec((None, d, ng), (l, 0, 0))],
        out_specs=[plan.out(w) for w in widths] + [plan.out(ng)],
        out_shape=[plan.out_shape(w, BF16) for w in widths] + [plan.out_shape(ng, F32)],
        compiler_params=_params("parallel"),
        name="inproj",
    )(*([xz] * plan.sub), *([mod] * plan.sub), norm_g, w_main, w_gate)


N_MERGE_STREAMS = 10


def _merge_kernel(*refs, sub):
    x_refs, mod_refs = refs[:sub], refs[sub:2 * sub]
    tok = refs[2 * sub:(2 + N_MERGE_STREAMS) * sub]
    g_ref, mlg_ref, dng_ref, wb_ref, wo_ref, o_ref = refs[(2 + N_MERGE_STREAMS) * sub:]
    d = o_ref.shape[-1]
    f32 = lambda r: r[...].astype(F32)
    unperm = _time_perm(True)
    mixes = []
    for k in range(sub):
        mlf, mlb, mlo, lrf, lrb, lry, dnf, dnb, dnz, mg = (tok[s * sub + k] for s in range(N_MERGE_STREAMS))
        y_ml = _rms_heads(f32(mlf) + f32(mlb), ML_HEADS) * mlg_ref[...] * _sigmoid(f32(mlo))
        y_lr = (_dot(unperm, lrf[...]) + _dot(unperm, lrb[...])) * _gelu_tanh(f32(lry))
        y_dn = _rms_heads(f32(dnf) + f32(dnb), DN_HEADS) * dng_ref[...] * _silu(f32(dnz))
        mix = None
        for n, y in enumerate((y_ml, y_lr, y_dn)):
            term = _sigmoid(mg[:, n * d:(n + 1) * d].astype(F32)) * _dot(y.astype(BF16), wb_ref[n])
            mix = term if mix is None else mix + term
        mixes.append(mix.astype(BF16))
    outs = [_dot(mix, wo_ref[...]) for mix in mixes]
    for k, out in enumerate(outs):
        o_ref[k * TM:(k + 1) * TM, :] = x_refs[k][...] + mod_refs[k][5:6, :] * (_rms(out) * g_ref[3:4, :])


def _merge(plan, xz, mod, norm_g, ml_hf, ml_hb, ml, lr_hf, lr_hb, lru, dn_f, dn_b, dnz, mg,
           ml_g, dn_g, w_branch, w_out):
    d = xz.shape[-1]
    bw = w_branch.shape[2]
    l = plan.layer
    streams = [(ml_hf, bw, 0), (ml_hb, bw, 0), (ml, bw, ml.shape[-1] // bw - 1),
               (lr_hf, bw, 0), (lr_hb, bw, 0), (lru, bw, lru.shape[-1] // bw - 1),
               (dn_f, bw, 0), (dn_b, bw, 0), (dnz, bw, 0), (mg, N_BRANCH * d, 0)]
    assert len(streams) == N_MERGE_STREAMS
    tok_specs = [sp for _, w, c in streams for sp in plan.toks(w, c)]
    tok_args = [a for a, _, _ in streams for _ in range(plan.sub)]
    return pl.pallas_call(
        functools.partial(_merge_kernel, sub=plan.sub),
        grid=plan.grid,
        in_specs=plan.toks(d) + plan.mods(d) + tok_specs
                 + [_const_spec((None,) + norm_g.shape[1:], (l, 0, 0)),
                    _const_spec((None, 1, bw), (l, 0, 0)),
                    _const_spec((None, 1, bw), (l, 0, 0)),
                    _const_spec((None, N_BRANCH, bw, d), (l, 0, 0, 0)),
                    _const_spec((None, d, d), (l, 0, 0))],
        out_specs=plan.out(d),
        out_shape=plan.out_shape(d, F32),
        compiler_params=_params("parallel"),
        name="merge",
    )(*([xz] * plan.sub), *([mod] * plan.sub), *tok_args, norm_g, ml_g, dn_g, w_branch, w_out)


SEG = TM // V7X_SUBLANES


def _time_perm(inverse):
    ri = lax.broadcasted_iota(jnp.int32, (TM, TM), 0)
    ci = lax.broadcasted_iota(jnp.int32, (TM, TM), 1)
    r, t = (ci, ri) if inverse else (ri, ci)
    return jnp.where(t == (r % V7X_SUBLANES) * SEG + r // V7X_SUBLANES, 1.0, 0.0).astype(BF16)


def _conv4_perm(x, before1, before2, after1, w):
    sub = lax.broadcasted_iota(jnp.int32, (V7X_SUBLANES, 1), 0)
    vrow = lambda i: x[V7X_SUBLANES * i:V7X_SUBLANES * (i + 1)]
    m1_edge = jnp.where(sub == 0, before1, pltpu.roll(vrow(SEG - 1), 1, axis=0))
    m2_edge = jnp.where(sub == 0, before2, pltpu.roll(vrow(SEG - 2), 1, axis=0))
    p1_edge = jnp.where(sub == V7X_SUBLANES - 1, after1, pltpu.roll(vrow(0), V7X_SUBLANES - 1, axis=0))
    x_m1 = jnp.concatenate([m1_edge, x[:-V7X_SUBLANES]], axis=0)
    x_m2 = jnp.concatenate([m2_edge, m1_edge, x[:-2 * V7X_SUBLANES]], axis=0)
    x_p1 = jnp.concatenate([x[V7X_SUBLANES:], p1_edge], axis=0)
    taps = {-2: x_m2, -1: x_m1, 0: x, 1: x_p1}
    acc = None
    for j in range(CONV_W):
        term = taps[j - CONV_LEFT] * w[j:j + 1]
        acc = term if acc is None else acc + term
    return acc


def _tile_scan(a, b, h0, rev):
    vrow = lambda x, i: x[V7X_SUBLANES * i:V7X_SUBLANES * (i + 1)]
    h = jnp.zeros_like(vrow(a, 0))
    p = jnp.ones_like(h)
    hs, ps = [None] * SEG, [None] * SEG
    for i in (range(SEG - 1, -1, -1) if rev else range(SEG)):
        ai = vrow(a, i)
        h = ai * h + vrow(b, i)
        p = ai * p
        hs[i], ps[i] = h, p
    carry = h0
    enter = [None] * V7X_SUBLANES
    for s in (range(V7X_SUBLANES - 1, -1, -1) if rev else range(V7X_SUBLANES)):
        enter[s] = carry
        carry = p[s:s + 1] * carry + h[s:s + 1]
    enter = jnp.concatenate(enter, axis=0)
    return jnp.concatenate([hi + pi * enter for hi, pi in zip(hs, ps)], axis=0), carry


def _lru_kernel(x_ref, cw_ref, cb_ref, w_ref, bias_ref, lam_ref, of_ref, ob_ref, *, n_lat, n_ctx):
    t, c = TM, x_ref.shape[-1]
    n_tiles = n_lat + n_ctx
    halo = 2 * V7X_SUBLANES

    def load_conv(tile):
        r0 = pl.multiple_of(tile * t, t)
        first = jnp.logical_or(tile == 0, tile == n_lat)
        last = jnp.logical_or(tile == n_lat - 1, tile == n_tiles - 1)
        x = x_ref[pl.ds(r0, t), :].astype(F32)
        p0 = pl.multiple_of(jnp.maximum(r0 - halo, 0), halo)
        n0 = pl.multiple_of(jnp.minimum(r0 + t, n_tiles * t - halo), halo)
        prev = jnp.where(first, 0.0, x_ref[pl.ds(p0, halo), :].astype(F32))
        nxt = jnp.where(last, 0.0, x_ref[pl.ds(n0, halo), :].astype(F32))
        before1, before2 = prev[halo - 1:halo], prev[V7X_SUBLANES - 1:V7X_SUBLANES]
        return r0, _conv4_perm(x, before1, before2, nxt[0:1], cw_ref[...]) + cb_ref[...]

    def direction(d, tile, h0, o_ref):
        r0, xc = load_conv(tile)
        z = _dot(xc.astype(BF16), w_ref[d]) + bias_ref[d]
        r, i = _sigmoid(z[:, :c]), _sigmoid(z[:, c:])
        la = (-LRU_C * _softplus(-lam_ref[d])) * r
        a = jnp.exp(la)
        bx = jnp.sqrt(jnp.tanh(-la) * (1.0 + a * a)) * (i * xc)
        h, carry = _tile_scan(a, bx, h0, d == 1)
        o_ref[pl.ds(r0, t), :] = h.astype(o_ref.dtype)
        return carry

    def step(s, carry):
        hf, hb = carry
        hf = direction(0, jnp.where(s < n_ctx, n_lat + s, s - n_ctx), hf, of_ref)
        hb = direction(1, n_tiles - 1 - s, hb, ob_ref)
        return hf, hb

    zero = jnp.zeros((1, c), F32)
    lax.fori_loop(0, n_tiles, step, (zero, zero))


def _lru(lru, conv_w, conv_b, w_gates, b_gates, lam, l, n_lat, n_ctx):
    b, lt, _ = lru.shape
    c = conv_w.shape[-1]
    seq = pl.BlockSpec((None, lt, c), lambda bi: (bi, 0, 0))
    return pl.pallas_call(
        functools.partial(_lru_kernel, n_lat=n_lat, n_ctx=n_ctx),
        grid=(b,),
        in_specs=[seq,
                  _const_spec((None, CONV_W, c), (l, 0, 0)),
                  _const_spec((None, 1, c), (l, 0, 0)),
                  _const_spec((None, 2, c, 2 * c), (l, 0, 0, 0)),
                  _const_spec((None, 2, 1, 2 * c), (l, 0, 0, 0)),
                  _const_spec((None, 2, 1, c), (l, 0, 0, 0))],
        out_specs=[seq, seq],
        out_shape=[jax.ShapeDtypeStruct((b, lt, c), BF16)] * 2,
        compiler_params=_params("parallel"),
        name="lru",
    )(lru, conv_w, conv_b, w_gates, b_gates, lam)


def _mlstm_kernel(qf_ref, kf_ref, vf_ref, qb_ref, kb_ref, vb_ref, grf_ref, grb_ref, br_ref,
                  hf_ref, hb_ref, c_ref, m_ref):
    @pl.when(pl.program_id(1) == 0)
    def _():
        c_ref[...] = jnp.zeros(c_ref.shape, F32)
        m_ref[...] = jnp.zeros(m_ref.shape, F32)

    tc, nqk = qf_ref.shape
    dk = nqk // ML_HEADS
    dv = vf_ref.shape[1] // ML_HEADS
    nh = ML_HEADS
    ri = lax.broadcasted_iota(jnp.int32, (tc, tc), 0)
    ci = lax.broadcasted_iota(jnp.int32, (tc, tc), 1)
    causal = (ci <= ri, ci >= ri)
    lane_head = lax.broadcasted_iota(jnp.int32, (1, nqk), 1) // dk
    ones = jnp.ones((tc, dv), BF16)
    dirs = ((qf_ref, kf_ref, vf_ref, grf_ref), (qb_ref, kb_ref, vb_ref, grb_ref))

    rows, v_rows, ws_rows, decs = [], [], [], []
    for d, (_, _, _, gr_ref) in enumerate(dirs):
        rev = d == 1
        gr = gr_ref[...].T + br_ref[...]
        i_r = gr[nh * d:nh * (d + 1)]
        b_r = _cumsum_groups(-_softplus(-gr[nh * (2 + d):nh * (3 + d)]), 1, tc, rev)
        m_prev = m_ref[nh * d:nh * (d + 1), 0:1]
        m_t = b_r + jnp.maximum(m_prev, _cummax_groups(i_r - b_r, 1, tc, rev))
        b_end = b_r[:, 0:1] if rev else b_r[:, tc - 1:tc]
        lws = b_end - b_r + i_r
        m_new = jnp.maximum(b_end + m_prev, jnp.max(lws, axis=1, keepdims=True))
        rows += [b_r - m_t, jnp.exp(b_r + m_prev - m_t), jnp.exp(-m_t)]
        v_rows.append(b_r - i_r)
        ws_rows.append(jnp.exp(lws - m_new))
        decs.append(jnp.exp(b_end + m_prev - m_new))
        m_ref[nh * d:nh * (d + 1), :] = jnp.broadcast_to(m_new, (nh, m_ref.shape[1]))
    cols = jnp.concatenate(rows, axis=0).T

    chains = [(d, h) for d in range(2) for h in range(nh)]
    q_all = [dirs[d][0][...] * (dk ** -0.5) for d in range(2)]
    k_all = [dirs[d][1][...] for d in range(2)]
    v_all = [dirs[d][2][...] for d in range(2)]
    kt_all = [k.astype(F32).T for k in k_all]
    c_all = [c_ref[d] for d in range(2)]
    cb_all = [c.astype(BF16) for c in c_all]
    col = lambda d, j, h: cols[:, (3 * d + j) * nh + h:(3 * d + j) * nh + h + 1]
    qh = [jnp.where(lane_head == h, q_all[d], jnp.zeros_like(q_all[d])) for d, h in chains]
    vp = [jnp.concatenate([v_all[d][:, h * dv:(h + 1) * dv], ones], axis=1) for d, h in chains]
    s_raw = [_dot_nt(q, k_all[d]) for q, (d, h) in zip(qh, chains)]
    p = [(s * jnp.exp(jnp.where(causal[d], col(d, 0, h) - v_rows[d][h:h + 1], NEG))).astype(BF16)
         for s, (d, h) in zip(s_raw, chains)]
    num = [_dot(pc, vc) + col(d, 1, h) * _dot(q, cb_all[d]) for pc, vc, q, (d, h) in zip(p, vp, qh, chains)]
    outs = [nm[:, :dv] / jnp.maximum(jnp.abs(nm[:, dv:]), col(d, 2, h)) for nm, (d, h) in zip(num, chains)]
    for vc, (d, h) in zip(vp, chains):
        kw = (kt_all[d][h * dk:(h + 1) * dk] * ws_rows[d][h:h + 1]).astype(BF16)
        c_ref[d, h * dk:(h + 1) * dk, :] = decs[d][h:h + 1] * c_all[d][h * dk:(h + 1) * dk] + _dot(kw, vc)
    hf_ref[...] = jnp.concatenate(outs[:nh], axis=1).astype(hf_ref.dtype)
    hb_ref[...] = jnp.concatenate(outs[nh:], axis=1).astype(hb_ref.dtype)


def _mlstm(ml, gates, bias_r, l, n_lat, n_ctx):
    b, lt, _ = ml.shape
    ng = gates.shape[-1]
    n_tiles = n_lat + n_ctx
    dqk = ml.shape[-1] // 6
    fwd = lambda s: jnp.where(s < n_ctx, n_lat + s, s - n_ctx)
    bwd = lambda s: n_tiles - 1 - s

    def specs(tile):
        return [pl.BlockSpec((None, TM, dqk), lambda bi, s: (bi, tile(s), 0)),
                pl.BlockSpec((None, TM, dqk), lambda bi, s: (bi, tile(s), 1)),
                pl.BlockSpec((None, TM, 2 * dqk), lambda bi, s: (bi, tile(s), 1))]

    row = lambda tile: pl.BlockSpec((None, TM, ng), lambda bi, s: (bi, tile(s), 0))
    out = lambda tile: pl.BlockSpec((None, TM, 2 * dqk), lambda bi, s: (bi, tile(s), 0))
    return pl.pallas_call(
        _mlstm_kernel,
        grid=(b, n_tiles),
        in_specs=specs(fwd) + specs(bwd) + [row(fwd), row(bwd), _const_spec((None, ng, 1), (l, 0, 0))],
        out_specs=[out(fwd), out(bwd)],
        out_shape=[jax.ShapeDtypeStruct((b, lt, 2 * dqk), BF16)] * 2,
        scratch_shapes=[pltpu.VMEM((2, dqk, 2 * (2 * dqk // ML_HEADS)), F32),
                        pltpu.VMEM((2 * ML_HEADS, V7X_LANES), F32)],
        compiler_params=_params("parallel", "arbitrary"),
        name="mlstm",
    )(ml, ml, ml, ml, ml, ml, gates, gates, bias_r)


INV_BASE = 8


def _inv_masks(ri, ci):
    blk = lambda s: (ri // s) == (ci // s)
    as_bf = lambda m: jnp.where(m, 1.0, 0.0).astype(BF16)
    masks, s = [as_bf(blk(INV_BASE))], INV_BASE
    while s < CHUNK:
        masks.append(as_bf(jnp.logical_and(blk(2 * s), jnp.logical_not(blk(s)))))
        s *= 2
    return masks


def _tri_inv_many(a_list, eye, masks):
    ps = [a * masks[0] for a in a_list]
    xbs = [eye - p for p in ps]
    for _ in range(INV_BASE.bit_length() - 2):
        ps = [_dot(p, p).astype(BF16) for p in ps]
        xbs = [_dot(xb, eye + p).astype(BF16) for xb, p in zip(xbs, ps)]
    for m in masks[1:]:
        ys = [_dot(xb, a * m).astype(BF16) for xb, a in zip(xbs, a_list)]
        xbs = [xb - _dot(y, xb).astype(BF16) for y, xb in zip(ys, xbs)]
    return xbs


def _dn_prep_kernel(x_ref, xp_ref, xn_ref, grow_ref, cw_ref, arow_ref,
                    u_ref, w_ref, qe_ref, kd_ref, qk_ref, eg_ref, *, first_groups, last_groups):
    g = pl.program_id(1)
    width = x_ref.shape[-1]
    dh = width // (3 * DN_HEADS)
    n = DN_HEADS * CHUNK
    halo = 2 * V7X_SUBLANES
    any_of = lambda groups: functools.reduce(jnp.logical_or, [g == v for v in groups])
    x = x_ref[...].reshape(DN_GROUP * CHUNK, width)
    prev8 = jnp.where(any_of(first_groups), 0.0, xp_ref[CHUNK - halo:, :].astype(F32)[V7X_SUBLANES:])
    nxt8 = jnp.where(any_of(last_groups), 0.0, xn_ref[:halo, :].astype(F32)[:V7X_SUBLANES])
    xc = _silu(_conv4(x, prev8, nxt8, cw_ref[...]))

    ri = lax.broadcasted_iota(jnp.int32, (n, n), 0)
    ci = lax.broadcasted_iota(jnp.int32, (n, n), 1)
    same = (ri // CHUNK) == (ci // CHUNK)
    eye = jnp.where(ri == ci, 1.0, 0.0).astype(BF16)
    incl =(jnp.logical_and(same, ci <= ri), jnp.logical_and(same, ci >= ri))
    strict = (jnp.logical_and(same, ci < ri), jnp.logical_and(same, ci > ri))
    masks = _inv_masks(ri, ci)

    qn, kn, v, kk, qk = [], [], [], [], []
    for i in range(DN_GROUP):
        xi = xc[i * CHUNK:(i + 1) * CHUNK]
        q, k, vi = (_stack_heads(xi[:, j * DN_HEADS * dh:(j + 1) * DN_HEADS * dh], DN_HEADS) for j in range(3))
        qn.append(q * lax.rsqrt(jnp.sum(q * q, axis=-1, keepdims=True) + EPS) * (dh ** -0.5))
        kn.append(k * lax.rsqrt(jnp.sum(k * k, axis=-1, keepdims=True) + EPS))
        v.append(vi)
        kb = kn[i].astype(BF16)
        kk.append(_dot_nt(kb, kb))
        qk.append(_dot_nt(qn[i].astype(BF16), kb))

    systems = [(i, d) for i in range(DN_GROUP) for d in range(2)]
    rows, cs_rows = [], []
    for i, d in systems:
        raw = grow_ref[i]
        beta = _sigmoid(raw[d:d + 1])
        g_r = -jnp.exp(arow_ref[d, 0:1, :]) * _softplus(raw[2 + d:3 + d] + arow_ref[d, 1:2, :])
        cs = _cumsum_groups(g_r, 1, CHUNK, d == 1)
        tot = cs + _cumsum_groups(g_r, 1, CHUNK, d == 0) - g_r
        eg = jnp.exp(cs)
        rows += [cs, beta, eg, jnp.exp(tot - cs), beta * eg]
        cs_rows.append(cs)
        eg_ref[d, i] = jnp.exp(tot)
    n_col = len(rows) // len(systems)
    cols = jnp.concatenate(rows, axis=0).T

    a_list, rhs = [], []
    for s, (i, d) in enumerate(systems):
        cs_c, beta_c, eg_c, ekd_c, beg_c = (cols[:, n_col * s + j:n_col * s + j + 1] for j in range(n_col))
        gam = jnp.exp(jnp.where(incl[d], cs_c - cs_rows[s], NEG))
        a_list.append((jnp.where(strict[d], kk[i], 0.0) * gam * beta_c).astype(BF16))
        rhs.append(jnp.concatenate([beta_c * v[i], beg_c * kn[i]], axis=1).astype(BF16))
        qk_ref[d, i] = (qk[i] * gam).astype(qk_ref.dtype)
        qe_ref[d, i] = (qn[i] * eg_c).astype(qe_ref.dtype)
        kd_ref[d, i] = (kn[i] * ekd_c).astype(kd_ref.dtype)
    for (i, d), tinv, r in zip(systems, _tri_inv_many(a_list, eye, masks), rhs):
        sol = _dot(tinv, r)
        u_ref[d, i] = sol[:, :dh].astype(u_ref.dtype)
        w_ref[d, i] = sol[:, dh:].astype(w_ref.dtype)


def _dn_prep(xs, grow, conv_w, arow, l, n_lat_groups):
    b, nc, _, width = xs.shape
    n_groups = nc // DN_GROUP
    n = DN_HEADS * CHUNK
    dh = width // (3 * DN_HEADS)
    out = lambda w, dt: (jax.ShapeDtypeStruct((b, 2, nc, n, w), dt),
                         pl.BlockSpec((None, 2, DN_GROUP, n, w), lambda bi, g: (bi, 0, g, 0, 0)))
    outs = [out(dh, BF16)] * 4 + [out(n, BF16)]
    outs.append((jax.ShapeDtypeStruct((b, 2, nc, 1, n), F32),
                 pl.BlockSpec((None, 2, DN_GROUP, 1, n), lambda bi, g: (bi, 0, g, 0, 0))))
    chunk = lambda idx: pl.BlockSpec((None, None, CHUNK, width), lambda bi, g: (bi, idx(g), 0, 0))
    return pl.pallas_call(
        functools.partial(_dn_prep_kernel, first_groups=(0, n_lat_groups),
                          last_groups=(n_lat_groups - 1, n_groups - 1)),
        grid=(b, n_groups),
        in_specs=[pl.BlockSpec((None, DN_GROUP, CHUNK, width), lambda bi, g: (bi, g, 0, 0)),
                  chunk(lambda g: jnp.maximum(g * DN_GROUP - 1, 0)),
                  chunk(lambda g: jnp.minimum(g * DN_GROUP + DN_GROUP, nc - 1)),
                  pl.BlockSpec((None, DN_GROUP, 4, n), lambda bi, g: (bi, g, 0, 0)),
                  _const_spec((None, CONV_W, width), (l, 0, 0)),
                  _const_spec((None, 2, 2, n), (l, 0, 0, 0))],
        out_specs=[o[1] for o in outs],
        out_shape=[o[0] for o in outs],
        compiler_params=_params("parallel", "parallel"),
        name="dn_prep",
    )(xs, xs, xs, grow, conv_w, arow)


def _dn_scan_kernel(*refs):
    ins, (of_ref, ob_ref, s_ref) = refs[:12], refs[12:]

    @pl.when(pl.program_id(1) == 0)
    def _():
        s_ref[...] = jnp.zeros(s_ref.shape, F32)

    rows = [slice(h * CHUNK, (h + 1) * CHUNK) for h in range(DN_HEADS)]
    seqs = [(r, d) for r in range(of_ref.shape[0]) for d in range(2)]
    get = lambda j, r, d: ins[6 * d + j][r]
    sb = [[s_ref[r, d, h].astype(BF16) for h in range(DN_HEADS)] for r, d in seqs]
    ws = [jnp.concatenate([_dot(get(1, r, d)[rows[h]], sb[i][h]) for h in range(DN_HEADS)], axis=0)
          for i, (r, d) in enumerate(seqs)]
    qs = [jnp.concatenate([_dot(get(2, r, d)[rows[h]], sb[i][h]) for h in range(DN_HEADS)], axis=0)
          for i, (r, d) in enumerate(seqs)]
    vnew = [(get(0, r, d).astype(F32) - w).astype(BF16) for w, (r, d) in zip(ws, seqs)]
    o = [q + _dot(get(4, r, d), vn) for q, vn, (r, d) in zip(qs, vnew, seqs)]
    for vn, (r, d) in zip(vnew, seqs):
        kd, eg = get(3, r, d), get(5, r, d)
        for h in range(DN_HEADS):
            s_ref[r, d, h] = (eg[:, h * CHUNK:h * CHUNK + 1] * s_ref[r, d, h]
                              + _dot_tn(kd[rows[h]], vn[rows[h]]))
    for oi, (r, d) in zip(o, seqs):
        o_ref = (of_ref, ob_ref)[d]
        o_ref[r] = _unstack_heads(oi, DN_HEADS).astype(o_ref.dtype)


def _dn_scan(prep, n_lat_chunks):
    b, _, nc, _, dh = prep[0].shape
    ow = DN_HEADS * dh
    n_ctx_chunks = nc - n_lat_chunks
    fwd = lambda s: jnp.where(s < n_ctx_chunks, n_lat_chunks + s, s - n_ctx_chunks)
    bwd = lambda s: nc - 1 - s

    rpb = DN_SCAN_ROWS if b % DN_SCAN_ROWS == 0 else 1

    def spec(a, d, chunk):
        return pl.BlockSpec((rpb, None, None) + a.shape[3:], lambda bi, s: (bi, d, chunk(s), 0, 0))

    out = lambda chunk: pl.BlockSpec((rpb, None, CHUNK, ow), lambda bi, s: (bi, chunk(s), 0, 0))
    return pl.pallas_call(
        _dn_scan_kernel,
        grid=(b // rpb, nc),
        in_specs=[spec(a, 0, fwd) for a in prep] + [spec(a, 1, bwd) for a in prep],
        out_specs=[out(fwd), out(bwd)],
        out_shape=[jax.ShapeDtypeStruct((b, nc, CHUNK, ow), BF16)] * 2,
        scratch_shapes=[pltpu.VMEM((rpb, 2, DN_HEADS, dh, dh), F32)],
        compiler_params=_params("parallel", "arbitrary"),
        name="dn_scan",
    )(*prep, *prep)


def _block_diag(w):
    n, i, j = w.shape
    return jnp.einsum('nij,nm->nimj', w, jnp.eye(n, dtype=w.dtype)).reshape(n * i, n * j)


def _to_chunks(a, n_lat):
    b, _, w = a.shape
    lat = a[:, :n_lat].reshape(b, n_lat // CHUNK, CHUNK, w).swapaxes(1, 2)
    return jnp.concatenate([lat, a[:, n_lat:].reshape(b, -1, CHUNK, w)], axis=1)


def _from_chunks(a, n_lat):
    b, _, _, w = a.shape
    lat = a[:, :n_lat // CHUNK].swapaxes(1, 2).reshape(b, n_lat, w)
    return jnp.concatenate([lat, a[:, n_lat // CHUNK:].reshape(b, -1, w)], axis=1)


def _dn_gate_rows(raw, n_lat):
    b = raw.shape[0]
    lat = raw[:, :n_lat].reshape(b, n_lat // CHUNK, CHUNK, 4, DN_HEADS)
    ctx = raw[:, n_lat:].reshape(b, -1, CHUNK, 4, DN_HEADS)
    row = jnp.concatenate([lat.transpose(0, 2, 3, 4, 1), ctx.transpose(0, 1, 3, 4, 2)], axis=1)
    return row.reshape(b, row.shape[1], 4, DN_HEADS * CHUNK)


def kernel(x, c, ctx, c_ctx, w_mod, b_mod, norm_g, ffn_w_gu, ffn_w_down, w_in, ml_gate_b, ml_norm_g,
           lru_conv_w, lru_conv_b, lru_w_a, lru_b_a, lru_w_x, lru_b_x, lru_lambda, dn_conv_w,
           dn_a_log, dn_dt_bias, dn_norm_g, w_branch, w_out):
    b, n_lat, d = x.shape
    n_ctx = ctx.shape[1]
    depth = w_mod.shape[0]
    bw = w_branch.shape[2]
    assert n_lat == CHUNK * CHUNK and n_lat % TM == 0 and n_ctx % TM == 0 and n_ctx % (CHUNK * DN_GROUP) == 0
    n_lat_tiles, n_ctx_tiles = n_lat // TM, n_ctx // TM
    n_tiles = n_lat_tiles + n_ctx_tiles

    ctx_row = b
    n_rows = -(-(b + 1) // V7X_SUBLANES) * V7X_SUBLANES
    cc = jnp.zeros((n_rows, d), F32).at[:b].set(c).at[b].set(c_ctx)
    mod = _mod_table(cc, w_mod, b_mod).reshape(depth, n_rows, N_MOD, d)

    dqk = bw // 2
    edges = [0]
    for wdt in (dqk, dqk, bw, bw, N_GATES, bw, bw, bw, bw, bw, bw, N_GATES, N_BRANCH * d):
        edges.append(edges[-1] + wdt)
    piece = lambda i, j: w_in[:, :, edges[i]:edges[j]]
    widths = (3 * bw, 2 * bw, 3 * bw, bw, N_BRANCH * d)
    w_main = jnp.concatenate([piece(0, 4), piece(5, 7), piece(7, 11), piece(12, 13)], axis=-1).astype(BF16)
    w_gate = jnp.concatenate([piece(4, 5), piece(11, 12)], axis=-1).astype(BF16)
    wgu = ffn_w_gu.astype(BF16)
    wdn = ffn_w_down.astype(BF16)
    wbr = w_branch.astype(BF16)
    wout = w_out.astype(BF16)
    ml_bias = jnp.concatenate([ml_gate_b.reshape(depth, N_GATES), jnp.zeros((depth, N_GATES), F32)], axis=-1)
    ml_bias_r = ml_bias[:, :, None]
    ml_g = ml_norm_g[:, None, :]
    dn_g = jnp.tile(dn_norm_g, (1, DN_HEADS))[:, None, :]
    lru_w = jnp.stack([jnp.concatenate([jax.vmap(_block_diag)(lru_w_a[:, dd]), jax.vmap(_block_diag)(lru_w_x[:, dd])],
                                       axis=-1) for dd in range(2)], axis=1).astype(BF16)
    lru_b = jnp.concatenate([lru_b_a, lru_b_x], axis=-1)[:, :, None, :]
    lru_lam = lru_lambda[:, :, None, :]
    lru_cb = lru_conv_b[:, None, :]
    dn_arow = jnp.repeat(jnp.stack([dn_a_log, dn_dt_bias], axis=2), CHUNK, axis=3)

    rows3 = lambda a: a.reshape(b, n_lat + n_ctx, a.shape[-1])
    for l in range(depth):
        full = _TilePlan(b, n_tiles, n_lat_tiles, n_tiles, ctx_row, l)
        if l == 0:
            xz = _ffn(full, _flat(x), mod, norm_g, wgu, wdn, 0, ctx=_flat(ctx))
        else:
            xz = _ffn(full, xz, mod, norm_g, wgu, wdn, 0)
        ml, lru, dnqkv, dnz, mg, gates = _inproj(full, xz, mod, norm_g, w_main, w_gate, widths, 1, bw)
        gates = rows3(gates)
        ml_hf, ml_hb = _mlstm(rows3(ml), gates, ml_bias_r, l, n_lat_tiles, n_ctx_tiles)
        lr_hf, lr_hb = _lru(rows3(lru), lru_conv_w, lru_cb, lru_w, lru_b, lru_lam, l, n_lat_tiles, n_ctx_tiles)
        prep = _dn_prep(_to_chunks(rows3(dnqkv), n_lat), _dn_gate_rows(gates[:, :, N_GATES:], n_lat),
                        dn_conv_w, dn_arow, l, n_lat // (CHUNK * DN_GROUP))
        dn_f, dn_b = (_flat(_from_chunks(o, n_lat)) for o in _dn_scan(prep, n_lat // CHUNK))
        xz = _merge(full, xz, mod, norm_g, _flat(ml_hf), _flat(ml_hb), ml, _flat(lr_hf), _flat(lr_hb), lru,
                    dn_f, dn_b, dnz, mg, ml_g, dn_g, wbr, wout)
        visit = n_lat_tiles if l == depth - 1 else n_tiles
        xz = _ffn(_TilePlan(b, n_tiles, n_lat_tiles, visit, ctx_row, l), xz, mod, norm_g, wgu, wdn, 2)
    return xz.reshape(b, n_lat, d)
```

```python
import functools
import math

import jax
import jax.numpy as jnp
from jax import lax
from jax.experimental import pallas as pl
from jax.experimental.pallas import tpu as pltpu

F32 = jnp.float32
BF16 = jnp.bfloat16

EPS = 1e-6
N_MOD = 9
N_BRANCH = 3
CONV_W = 4
CONV_LEFT = 2
ML_HEADS = 4
LRU_C = 8.0
DN_HEADS = 4
CHUNK = 64
DN_GROUP = 4
DN_SCAN_ROWS = 4
ML_ROWS = 2
N_GATES = 16

V7X_SUBLANES = 8
V7X_LANES = 128
V7X_VMEM_BYTES = 64 * 1024 * 1024
VMEM_LIMIT = V7X_VMEM_BYTES - 8 * 1024 * 1024

TM = 256
NEG = -1e30


def _sigmoid(x):
    return 0.5 * jnp.tanh(0.5 * x) + 0.5


def _silu(x):
    return x * _sigmoid(x)


def _softplus(x):
    return jnp.maximum(x, 0.0) + jnp.log(1.0 + jnp.exp(-jnp.abs(x)))


def _rms(x):
    return x * lax.rsqrt(jnp.mean(x * x, axis=-1, keepdims=True) + EPS)


def _rms_heads(x, n_heads):
    hd = x.shape[-1] // n_heads
    return jnp.concatenate([_rms(x[:, h * hd:(h + 1) * hd]) for h in range(n_heads)], axis=-1)


def _gelu_tanh(x):
    return 0.5 * x * (1.0 + jnp.tanh(math.sqrt(2.0 / math.pi) * (x + 0.044715 * (x * x * x))))


def _dot(a, b):
    return jnp.dot(a, b, preferred_element_type=F32)


def _dot_nt(a, b):
    return lax.dot_general(a, b, (((1,), (1,)), ((), ())), preferred_element_type=F32)


def _dot_tn(a, b):
    return lax.dot_general(a, b, (((0,), (0,)), ((), ())), preferred_element_type=F32)


def _params(*sem):
    return pltpu.CompilerParams(dimension_semantics=sem, vmem_limit_bytes=VMEM_LIMIT)


def _const_spec(block, index):
    return pl.BlockSpec(block, lambda *_: index, pipeline_mode=pl.Buffered(1))


def _stack_heads(x, n):
    w = x.shape[1] // n
    return jnp.concatenate([x[:, h * w:(h + 1) * w] for h in range(n)], axis=0)


def _unstack_heads(x, n):
    t = x.shape[0] // n
    return jnp.concatenate([x[h * t:(h + 1) * t] for h in range(n)], axis=1)


def _cumsum_groups(x, axis, period, rev):
    n = x.shape[axis]
    idx = lax.broadcasted_iota(jnp.int32, x.shape, axis) % period
    sh = 1
    while sh < period:
        if rev:
            x = x + jnp.where(idx < period - sh, pltpu.roll(x, n - sh, axis=axis), 0.0)
        else:
            x = x + jnp.where(idx >= sh, pltpu.roll(x, sh, axis=axis), 0.0)
        sh *= 2
    return x


def _cummax_groups(x, axis, period, rev):
    n = x.shape[axis]
    idx = lax.broadcasted_iota(jnp.int32, x.shape, axis) % period
    sh = 1
    while sh < period:
        if rev:
            x = jnp.maximum(x, jnp.where(idx < period - sh, pltpu.roll(x, n - sh, axis=axis), NEG))
        else:
            x = jnp.maximum(x, jnp.where(idx >= sh, pltpu.roll(x, sh, axis=axis), NEG))
        sh *= 2
    return x


def _shift_rows(xb, prev8, nxt8, off):
    t = xb.shape[0]
    ri = lax.broadcasted_iota(jnp.int32, (t, t), 0)
    ci = lax.broadcasted_iota(jnp.int32, (t, t), 1)
    shifted = _dot(jnp.where(ci == ri + off, 1.0, 0.0).astype(BF16), xb)
    row8 = lax.broadcasted_iota(jnp.int32, (V7X_SUBLANES, 1), 0)
    if off < 0:
        edge = jnp.where(row8 < -off, pltpu.roll(prev8, -off, axis=0), shifted[:V7X_SUBLANES])
        return jnp.concatenate([edge, shifted[V7X_SUBLANES:]], axis=0)
    edge = jnp.where(row8 >= V7X_SUBLANES - off, pltpu.roll(nxt8, V7X_SUBLANES - off, axis=0),
                     shifted[t - V7X_SUBLANES:])
    return jnp.concatenate([shifted[:t - V7X_SUBLANES], edge], axis=0)


def _conv4(xb, prev8, nxt8, w):
    acc = xb.astype(F32) * w[CONV_LEFT:CONV_LEFT + 1]
    for j in range(CONV_W):
        if j != CONV_LEFT:
            acc = acc + _shift_rows(xb, prev8, nxt8, j - CONV_LEFT) * w[j:j + 1]
    return acc


def _mod_kernel(c_ref, w_ref, b_ref, o_ref):
    s = _silu(c_ref[...]).astype(BF16)
    o_ref[...] = _dot(s, w_ref[...].astype(BF16)) + b_ref[...]


def _mod_table(cc, w_mod, b_mod):
    depth, d, nd = w_mod.shape
    r = cc.shape[0]
    tn = nd // 4
    return pl.pallas_call(
        _mod_kernel,
        grid=(depth, nd // tn),
        in_specs=[pl.BlockSpec((r, d), lambda l, j: (0, 0)),
                  pl.BlockSpec((None, d, tn), lambda l, j: (l, 0, j)),
                  pl.BlockSpec((None, 1, tn), lambda l, j: (l, 0, j))],
        out_specs=pl.BlockSpec((None, r, tn), lambda l, j: (l, 0, j)),
        out_shape=jax.ShapeDtypeStruct((depth, r, nd), F32),
        compiler_params=_params("parallel", "parallel"),
        name="mod_table",
    )(cc, w_mod, b_mod.reshape(depth, 1, nd))


class _TilePlan:
    def __init__(self, b, n_tiles, n_lat_tiles, n_vis, ctx_row, layer):
        self.b, self.n_tiles, self.n_lat_tiles, self.n_vis = b, n_tiles, n_lat_tiles, n_vis
        self.ctx_row, self.layer = ctx_row, layer
        self.sub = 2 if (b * n_vis) % 2 == 0 else 1
        self.grid = (b * n_vis // self.sub,)

    def _tile(self, i, k):
        tid = i * self.sub + k
        return tid // self.n_vis, tid % self.n_vis

    def tok(self, k, width, col=0):
        def index(i):
            bi, t = self._tile(i, k)
            return bi * self.n_tiles + t, col
        return pl.BlockSpec((TM, width), index)

    def toks(self, width, col=0):
        return [self.tok(k, width, col) for k in range(self.sub)]

    def is_ctx(self, i, k):
        return self._tile(i, k)[1] >= self.n_lat_tiles

    def tok_ctx_rest(self, k, width):
        def index(i):
            bi, t = self._tile(i, k)
            return bi * self.n_tiles + jnp.maximum(t, self.n_lat_tiles), 0
        return pl.BlockSpec((TM, width), index)

    def tok_split(self, k, width, ctx):
        n_ctx_tiles = self.n_tiles - self.n_lat_tiles

        def index(i):
            bi, t = self._tile(i, k)
            if ctx:
                return bi * n_ctx_tiles + jnp.clip(t - self.n_lat_tiles, 0, n_ctx_tiles - 1), 0
            return bi * self.n_lat_tiles + jnp.minimum(t, self.n_lat_tiles - 1), 0
        return pl.BlockSpec((TM, width), index)

    def mods(self, d):
        def spec(k):
            def index(i):
                bi, t = self._tile(i, k)
                return self.layer, jnp.where(t >= self.n_lat_tiles, self.ctx_row, bi), 0, 0
            return pl.BlockSpec((None, None, N_MOD, d), index)
        return [spec(k) for k in range(self.sub)]

    def out(self, width):
        return pl.BlockSpec((self.sub * TM, width), lambda i: (i, 0))

    def out_shape(self, width, dtype):
        return jax.ShapeDtypeStruct((self.b * self.n_vis * TM, width), dtype)

    def rows(self, k):
        return slice(k * TM, (k + 1) * TM)


def _flat(a):
    return a.reshape(-1, a.shape[-1])


def _ffn_kernel(*refs, j, dff, sub, ctx_tiles):
    if ctx_tiles is not None:
        lat_refs, ctx_refs, refs = refs[:sub], refs[sub:2 * sub], refs[sub:]
        xs = [jnp.where(ctx_tiles(k), c[...], x[...]) for k, (x, c) in enumerate(zip(lat_refs, ctx_refs))]
    else:
        xs = [x[...] for x in refs[:sub]]
    mod_refs = refs[sub:2 * sub]
    g_ref, wgu_ref, wd_ref, o_ref = refs[2 * sub:]
    g_pre, g_post = g_ref[2 * j:2 * j + 1, :], g_ref[2 * j + 1:2 * j + 2, :]
    mods = [[m[3 * j + i:3 * j + i + 1, :] for i in range(3)] for m in mod_refs]
    hs = [(_rms(x) * g_pre * (1.0 + m[1]) + m[0]).astype(BF16) for x, m in zip(xs, mods)]
    gus = [_dot(h, wgu_ref[...]) for h in hs]
    acts = [(_silu(gu[:, :dff]) * gu[:, dff:]).astype(BF16) for gu in gus]
    ys = [_dot(a, wd_ref[...]) for a in acts]
    for k, (x, m, y) in enumerate(zip(xs, mods, ys)):
        o_ref[k * TM:(k + 1) * TM, :] = x + 0.5 * m[2] * (_rms(y) * g_post)


def _ffn(plan, xz, mod, norm_g, wgu, wd, j, ctx=None):
    d = xz.shape[-1]
    dff = wd.shape[2]
    l = plan.layer
    if ctx is None:
        tok_specs, tok_args, ctx_tiles = plan.toks(d), [xz] * plan.sub, None
    else:
        tok_specs = [plan.tok_split(k, d, False) for k in range(plan.sub)] \
                    + [plan.tok_split(k, d, True) for k in range(plan.sub)]
        tok_args = [xz] * plan.sub + [ctx] * plan.sub
        ctx_tiles = lambda k: plan.is_ctx(pl.program_id(0), k)
    return pl.pallas_call(
        functools.partial(_ffn_kernel, j=j, dff=dff, sub=plan.sub, ctx_tiles=ctx_tiles),
        grid=plan.grid,
        in_specs=tok_specs + plan.mods(d)
                 + [_const_spec((None,) + norm_g.shape[1:], (l, 0, 0)),
                    _const_spec((None, None, d, 2 * dff), (l, j // 2, 0, 0)),
                    _const_spec((None, None, dff, d), (l, j // 2, 0, 0))],
        out_specs=plan.out(d),
        out_shape=plan.out_shape(d, F32),
        compiler_params=_params("parallel"),
        name=f"ffn{j}",
    )(*tok_args, *([mod] * plan.sub), norm_g, wgu, wd)


def _inproj_kernel(*refs, sub, perm_out, perm_width):
    x_refs, mod_refs = refs[:sub], refs[sub:2 * sub]
    g_ref, w_ref, wg_ref = refs[2 * sub:2 * sub + 3]
    o_refs = refs[2 * sub + 3:]
    hs = [(_rms(x[...]) * g_ref[2:3, :] * (1.0 + m[4:5, :]) + m[3:4, :]).astype(BF16)
          for x, m in zip(x_refs, mod_refs)]
    perm = _time_perm(False)
    col = 0
    for oi, o_ref in enumerate(o_refs[:-1]):
        n = o_ref.shape[-1]
        for k, h in enumerate(hs):
            res = _dot(h, w_ref[:, col:col + n]).astype(o_ref.dtype)
            if oi == perm_out:
                res = jnp.concatenate([_dot(perm, res[:, :perm_width]).astype(o_ref.dtype),
                                       res[:, perm_width:]], axis=1)
            o_ref[k * TM:(k + 1) * TM, :] = res
        col += n
    for k, h in enumerate(hs):
        o_refs[-1][k * TM:(k + 1) * TM, :] = _dot(h, wg_ref[...])


def _inproj(plan, xz, mod, norm_g, w_main, w_gate, widths, perm_out, perm_width):
    d = xz.shape[-1]
    ng = w_gate.shape[-1]
    l = plan.layer
    return pl.pallas_call(
        functools.partial(_inproj_kernel, sub=plan.sub, perm_out=perm_out, perm_width=perm_width),
        grid=plan.grid,
        in_specs=plan.toks(d) + plan.mods(d)
                 + [_const_spec((None,) + norm_g.shape[1:], (l, 0, 0)),
                    _const_spec((None, d, w_main.shape[-1]), (l, 0, 0)),
                    _const_spec((None, d, ng), (l, 0, 0))],
        out_specs=[plan.out(w) for w in widths] + [plan.out(ng)],
        out_shape=[plan.out_shape(w, BF16) for w in widths] + [plan.out_shape(ng, F32)],
        compiler_params=_params("parallel"),
        name="inproj",
    )(*([xz] * plan.sub), *([mod] * plan.sub), norm_g, w_main, w_gate)


N_MERGE_STREAMS = 12


def _merge_kernel(*refs, sub, ctx_tiles):
    x_refs, mod_refs = refs[:sub], refs[sub:2 * sub]
    tok = refs[2 * sub:(2 + N_MERGE_STREAMS) * sub]
    g_ref, mlg_ref, dng_ref, wb_ref, wo_ref, o_ref = refs[(2 + N_MERGE_STREAMS) * sub:]
    d = o_ref.shape[-1]
    f32 = lambda r: r[...].astype(F32)
    unperm = _time_perm(True)
    mixes = []
    for k in range(sub):
        mlf, mlb, mlo, lrf, lrb, lry, dlf, dlb, dcf, dcb, dnz, mg = (
            tok[s * sub + k] for s in range(N_MERGE_STREAMS))
        y_ml = _rms_heads(f32(mlf) + f32(mlb), ML_HEADS) * mlg_ref[...] * _sigmoid(f32(mlo))
        y_lr = (_dot(unperm, lrf[...]) + _dot(unperm, lrb[...])) * _gelu_tanh(f32(lry))
        dn_h = jnp.where(ctx_tiles(k), f32(dcf) + f32(dcb), f32(dlf) + f32(dlb))
        y_dn = _rms_heads(dn_h, DN_HEADS) * dng_ref[...] * _silu(f32(dnz))
        mix = None
        for n, y in enumerate((y_ml, y_lr, y_dn)):
            term = _sigmoid(mg[:, n * d:(n + 1) * d].astype(F32)) * _dot(y.astype(BF16), wb_ref[n])
            mix = term if mix is None else mix + term
        mixes.append(mix.astype(BF16))
    outs = [_dot(mix, wo_ref[...]) for mix in mixes]
    for k, out in enumerate(outs):
        o_ref[k * TM:(k + 1) * TM, :] = x_refs[k][...] + mod_refs[k][5:6, :] * (_rms(out) * g_ref[3:4, :])


def _merge(plan, xz, mod, norm_g, ml_hf, ml_hb, ml, lr_hf, lr_hb, lru, dn_lat, dn_all, dnz, mg,
           ml_g, dn_g, w_branch, w_out):
    d = xz.shape[-1]
    bw = w_branch.shape[2]
    l = plan.layer
    sub = range(plan.sub)
    whole = lambda a, w, c=0: (a, plan.toks(w, c))
    streams = [whole(ml_hf, bw), whole(ml_hb, bw), whole(ml, bw, ml.shape[-1] // bw - 1),
               whole(lr_hf, bw), whole(lr_hb, bw), whole(lru, bw, lru.shape[-1] // bw - 1),
               (dn_lat[0], [plan.tok_split(k, bw, False) for k in sub]),
               (dn_lat[1], [plan.tok_split(k, bw, False) for k in sub]),
               (dn_all[0], [plan.tok_ctx_rest(k, bw) for k in sub]),
               (dn_all[1], [plan.tok_ctx_rest(k, bw) for k in sub]),
               whole(dnz, bw), whole(mg, N_BRANCH * d)]
    assert len(streams) == N_MERGE_STREAMS
    tok_specs = [sp for _, specs in streams for sp in specs]
    tok_args = [a for a, _ in streams for _ in sub]
    return pl.pallas_call(
        functools.partial(_merge_kernel, sub=plan.sub, ctx_tiles=lambda k: plan.is_ctx(pl.program_id(0), k)),
        grid=plan.grid,
        in_specs=plan.toks(d) + plan.mods(d) + tok_specs
                 + [_const_spec((None,) + norm_g.shape[1:], (l, 0, 0)),
                    _const_spec((None, 1, bw), (l, 0, 0)),
                    _const_spec((None, 1, bw), (l, 0, 0)),
                    _const_spec((None, N_BRANCH, bw, d), (l, 0, 0, 0)),
                    _const_spec((None, d, d), (l, 0, 0))],
        out_specs=plan.out(d),
        out_shape=plan.out_shape(d, F32),
        compiler_params=_params("parallel"),
        name="merge",
    )(*([xz] * plan.sub), *([mod] * plan.sub), *tok_args, norm_g, ml_g, dn_g, w_branch, w_out)


SEG = TM // V7X_SUBLANES


def _time_perm(inverse):
    ri = lax.broadcasted_iota(jnp.int32, (TM, TM), 0)
    ci = lax.broadcasted_iota(jnp.int32, (TM, TM), 1)
    r, t = (ci, ri) if inverse else (ri, ci)
    return jnp.where(t == (r % V7X_SUBLANES) * SEG + r // V7X_SUBLANES, 1.0, 0.0).astype(BF16)


def _conv4_perm(x, before1, before2, after1, w):
    sub = lax.broadcasted_iota(jnp.int32, (V7X_SUBLANES, 1), 0)
    vrow = lambda i: x[V7X_SUBLANES * i:V7X_SUBLANES * (i + 1)]
    m1_edge = jnp.where(sub == 0, before1, pltpu.roll(vrow(SEG - 1), 1, axis=0))
    m2_edge = jnp.where(sub == 0, before2, pltpu.roll(vrow(SEG - 2), 1, axis=0))
    p1_edge = jnp.where(sub == V7X_SUBLANES - 1, after1, pltpu.roll(vrow(0), V7X_SUBLANES - 1, axis=0))
    x_m1 = jnp.concatenate([m1_edge, x[:-V7X_SUBLANES]], axis=0)
    x_m2 = jnp.concatenate([m2_edge, m1_edge, x[:-2 * V7X_SUBLANES]], axis=0)
    x_p1 = jnp.concatenate([x[V7X_SUBLANES:], p1_edge], axis=0)
    taps = {-2: x_m2, -1: x_m1, 0: x, 1: x_p1}
    acc = None
    for j in range(CONV_W):
        term = taps[j - CONV_LEFT] * w[j:j + 1]
        acc = term if acc is None else acc + term
    return acc


def _tile_scan(a, b, h0, rev):
    vrow = lambda x, i: x[V7X_SUBLANES * i:V7X_SUBLANES * (i + 1)]
    h = jnp.zeros_like(vrow(a, 0))
    p = jnp.ones_like(h)
    hs, ps = [None] * SEG, [None] * SEG
    for i in (range(SEG - 1, -1, -1) if rev else range(SEG)):
        ai = vrow(a, i)
        h = ai * h + vrow(b, i)
        p = ai * p
        hs[i], ps[i] = h, p
    carry = h0
    enter = [None] * V7X_SUBLANES
    for s in (range(V7X_SUBLANES - 1, -1, -1) if rev else range(V7X_SUBLANES)):
        enter[s] = carry
        carry = p[s:s + 1] * carry + h[s:s + 1]
    enter = jnp.concatenate(enter, axis=0)
    return jnp.concatenate([hi + pi * enter for hi, pi in zip(hs, ps)], axis=0), carry


def _lru_kernel(x_ref, cw_ref, cb_ref, w_ref, bias_ref, lam_ref, of_ref, ob_ref, *, n_lat, n_ctx):
    t, c = TM, x_ref.shape[-1]
    n_tiles = n_lat + n_ctx
    halo = 2 * V7X_SUBLANES

    def load_conv(tile):
        r0 = pl.multiple_of(tile * t, t)
        first = jnp.logical_or(tile == 0, tile == n_lat)
        last = jnp.logical_or(tile == n_lat - 1, tile == n_tiles - 1)
        x = x_ref[pl.ds(r0, t), :].astype(F32)
        p0 = pl.multiple_of(jnp.maximum(r0 - halo, 0), halo)
        n0 = pl.multiple_of(jnp.minimum(r0 + t, n_tiles * t - halo), halo)
        prev = jnp.where(first, 0.0, x_ref[pl.ds(p0, halo), :].astype(F32))
        nxt = jnp.where(last, 0.0, x_ref[pl.ds(n0, halo), :].astype(F32))
        before1, before2 = prev[halo - 1:halo], prev[V7X_SUBLANES - 1:V7X_SUBLANES]
        return r0, _conv4_perm(x, before1, before2, nxt[0:1], cw_ref[...]) + cb_ref[...]

    def direction(d, tile, h0, o_ref):
        r0, xc = load_conv(tile)
        z = _dot(xc.astype(BF16), w_ref[d]) + bias_ref[d]
        r, i = _sigmoid(z[:, :c]), _sigmoid(z[:, c:])
        la = (-LRU_C * _softplus(-lam_ref[d])) * r
        a = jnp.exp(la)
        bx = jnp.sqrt(jnp.tanh(-la) * (1.0 + a * a)) * (i * xc)
        h, carry = _tile_scan(a, bx, h0, d == 1)
        o_ref[pl.ds(r0, t), :] = h.astype(o_ref.dtype)
        return carry

    def step(s, carry):
        hf, hb = carry
        hf = direction(0, jnp.where(s < n_ctx, n_lat + s, s - n_ctx), hf, of_ref)
        hb = direction(1, n_tiles - 1 - s, hb, ob_ref)
        return hf, hb

    zero = jnp.zeros((1, c), F32)
    lax.fori_loop(0, n_tiles, step, (zero, zero))


def _lru(lru, conv_w, conv_b, w_gates, b_gates, lam, l, n_lat, n_ctx):
    b, lt, _ = lru.shape
    c = conv_w.shape[-1]
    seq = pl.BlockSpec((None, lt, c), lambda bi: (bi, 0, 0))
    return pl.pallas_call(
        functools.partial(_lru_kernel, n_lat=n_lat, n_ctx=n_ctx),
        grid=(b,),
        in_specs=[seq,
                  _const_spec((None, CONV_W, c), (l, 0, 0)),
                  _const_spec((None, 1, c), (l, 0, 0)),
                  _const_spec((None, 2, c, 2 * c), (l, 0, 0, 0)),
                  _const_spec((None, 2, 1, 2 * c), (l, 0, 0, 0)),
                  _const_spec((None, 2, 1, c), (l, 0, 0, 0))],
        out_specs=[seq, seq],
        out_shape=[jax.ShapeDtypeStruct((b, lt, c), BF16)] * 2,
        compiler_params=_params("parallel"),
        name="lru",
    )(lru, conv_w, conv_b, w_gates, b_gates, lam)


def _mlstm_kernel(qf_ref, kf_ref, vf_ref, qb_ref, kb_ref, vb_ref, grf_ref, grb_ref, br_ref,
                  hf_ref, hb_ref, c_ref, m_ref):
    @pl.when(pl.program_id(1) == 0)
    def _():
        c_ref[...] = jnp.zeros(c_ref.shape, F32)
        m_ref[...] = jnp.zeros(m_ref.shape, F32)

    nrow, tc, nqk = qf_ref.shape
    dk = nqk // ML_HEADS
    dv = vf_ref.shape[-1] // ML_HEADS
    nh = ML_HEADS
    ri = lax.broadcasted_iota(jnp.int32, (tc, tc), 0)
    ci = lax.broadcasted_iota(jnp.int32, (tc, tc), 1)
    causal = (ci <= ri, ci >= ri)
    lane_head = lax.broadcasted_iota(jnp.int32, (1, nqk), 1) // dk
    ones = jnp.ones((tc, dv), BF16)
    dirs = ((qf_ref, kf_ref, vf_ref, grf_ref), (qb_ref, kb_ref, vb_ref, grb_ref))
    seqs = [(r, d) for r in range(nrow) for d in range(2)]

    rows, v_rows, ws_rows, decs = [], [], [], []
    for r, d in seqs:
        rev = d == 1
        gr = dirs[d][3][r].T + br_ref[...]
        i_r = gr[nh * d:nh * (d + 1)]
        b_r = _cumsum_groups(-_softplus(-gr[nh * (2 + d):nh * (3 + d)]), 1, tc, rev)
        m_prev = m_ref[r, nh * d:nh * (d + 1), 0:1]
        m_t = b_r + jnp.maximum(m_prev, _cummax_groups(i_r - b_r, 1, tc, rev))
        b_end = b_r[:, 0:1] if rev else b_r[:, tc - 1:tc]
        lws = b_end - b_r + i_r
        m_new = jnp.maximum(b_end + m_prev, jnp.max(lws, axis=1, keepdims=True))
        rows += [b_r - m_t, jnp.exp(b_r + m_prev - m_t), jnp.exp(-m_t)]
        v_rows.append(b_r - i_r)
        ws_rows.append(jnp.exp(lws - m_new))
        decs.append(jnp.exp(b_end + m_prev - m_new))
        m_ref[r, nh * d:nh * (d + 1), :] = jnp.broadcast_to(m_new, (nh, m_ref.shape[-1]))
    cols = jnp.concatenate(rows, axis=0).T

    chains = [(s, h) for s in range(len(seqs)) for h in range(nh)]
    q_all = [dirs[d][0][r] * (dk ** -0.5) for r, d in seqs]
    k_all = [dirs[d][1][r] for r, d in seqs]
    v_all = [dirs[d][2][r] for r, d in seqs]
    kt_all = [k.astype(F32).T for k in k_all]
    c_all = [c_ref[r, d] for r, d in seqs]
    cb_all = [c.astype(BF16) for c in c_all]
    col = lambda s, j, h: cols[:, (3 * s + j) * nh + h:(3 * s + j) * nh + h + 1]
    qh = [jnp.where(lane_head == h, q_all[s], jnp.zeros_like(q_all[s])) for s, h in chains]
    vp = [jnp.concatenate([v_all[s][:, h * dv:(h + 1) * dv], ones], axis=1) for s, h in chains]
    s_raw = [_dot_nt(q, k_all[s]) for q, (s, h) in zip(qh, chains)]
    p = [(sr * jnp.exp(jnp.where(causal[seqs[s][1]], col(s, 0, h) - v_rows[s][h:h + 1], NEG))).astype(BF16)
         for sr, (s, h) in zip(s_raw, chains)]
    num = [_dot(pc, vc) + col(s, 1, h) * _dot(q, cb_all[s]) for pc, vc, q, (s, h) in zip(p, vp, qh, chains)]
    outs = [nm[:, :dv] / jnp.maximum(jnp.abs(nm[:, dv:]), col(s, 2, h)) for nm, (s, h) in zip(num, chains)]
    for vc, (s, h) in zip(vp, chains):
        r, d = seqs[s]
        kw = (kt_all[s][h * dk:(h + 1) * dk] * ws_rows[s][h:h + 1]).astype(BF16)
        c_ref[r, d, h * dk:(h + 1) * dk, :] = decs[s][h:h + 1] * c_all[s][h * dk:(h + 1) * dk] + _dot(kw, vc)
    for s, (r, d) in enumerate(seqs):
        o_ref = (hf_ref, hb_ref)[d]
        o_ref[r] = jnp.concatenate(outs[s * nh:(s + 1) * nh], axis=1).astype(o_ref.dtype)


def _mlstm(ml, gates, bias_r, l, n_lat, n_ctx):
    b, lt, _ = ml.shape
    ng = gates.shape[-1]
    n_tiles = n_lat + n_ctx
    dqk = ml.shape[-1] // 6
    fwd = lambda s: jnp.where(s < n_ctx, n_lat + s, s - n_ctx)
    bwd = lambda s: n_tiles - 1 - s
    rpb = ML_ROWS if b % ML_ROWS == 0 else 1

    def specs(tile):
        return [pl.BlockSpec((rpb, TM, dqk), lambda bi, s: (bi, tile(s), 0)),
                pl.BlockSpec((rpb, TM, dqk), lambda bi, s: (bi, tile(s), 1)),
                pl.BlockSpec((rpb, TM, 2 * dqk), lambda bi, s: (bi, tile(s), 1))]

    row = lambda tile: pl.BlockSpec((rpb, TM, ng), lambda bi, s: (bi, tile(s), 0))
    out = lambda tile: pl.BlockSpec((rpb, TM, 2 * dqk), lambda bi, s: (bi, tile(s), 0))
    return pl.pallas_call(
        _mlstm_kernel,
        grid=(b // rpb, n_tiles),
        in_specs=specs(fwd) + specs(bwd) + [row(fwd), row(bwd), _const_spec((None, ng, 1), (l, 0, 0))],
        out_specs=[out(fwd), out(bwd)],
        out_shape=[jax.ShapeDtypeStruct((b, lt, 2 * dqk), BF16)] * 2,
        scratch_shapes=[pltpu.VMEM((rpb, 2, dqk, 2 * (2 * dqk // ML_HEADS)), F32),
                        pltpu.VMEM((rpb, 2 * ML_HEADS, V7X_LANES), F32)],
        compiler_params=_params("parallel", "arbitrary"),
        name="mlstm",
    )(ml, ml, ml, ml, ml, ml, gates, gates, bias_r)


INV_BASE = 8


def _inv_masks(ri, ci):
    blk = lambda s: (ri // s) == (ci // s)
    as_bf = lambda m: jnp.where(m, 1.0, 0.0).astype(BF16)
    masks, s = [as_bf(blk(INV_BASE))], INV_BASE
    while s < CHUNK:
        masks.append(as_bf(jnp.logical_and(blk(2 * s), jnp.logical_not(blk(s)))))
        s *= 2
    return masks


def _tri_inv_many(a_list, eye, masks):
    ps = [a * masks[0] for a in a_list]
    xbs = [eye - p for p in ps]
    for _ in range(INV_BASE.bit_length() - 2):
        ps = [_dot(p, p).astype(BF16) for p in ps]
        xbs = [_dot(xb, eye + p).astype(BF16) for xb, p in zip(xbs, ps)]
    for m in masks[1:]:
        ys = [_dot(xb, a * m).astype(BF16) for xb, a in zip(xbs, a_list)]
        xbs = [xb - _dot(y, xb).astype(BF16) for y, xb in zip(ys, xbs)]
    return xbs


def _dn_prep_kernel(xl_ref, xp_ref, xn_ref, xc_ref, grow_ref, cw_ref, arow_ref,
                    u_ref, w_ref, qe_ref, kd_ref, qk_ref, eg_ref, *, n_lat_groups):
    g = pl.program_id(1)
    is_ctx = g >= n_lat_groups
    width = xl_ref.shape[-1]
    dh = width // (3 * DN_HEADS)
    n = DN_HEADS * CHUNK
    halo = 2 * V7X_SUBLANES
    x = jnp.where(is_ctx, xc_ref[...], xl_ref[...]).reshape(DN_GROUP * CHUNK, width)
    no_prev = jnp.logical_or(is_ctx, g == 0)
    no_next = jnp.logical_or(is_ctx, g == n_lat_groups - 1)
    prev8 = jnp.where(no_prev, 0.0, xp_ref[CHUNK - halo:, :].astype(F32)[V7X_SUBLANES:])
    nxt8 = jnp.where(no_next, 0.0, xn_ref[:halo, :].astype(F32)[:V7X_SUBLANES])
    xc = _silu(_conv4(x, prev8, nxt8, cw_ref[...]))

    ri = lax.broadcasted_iota(jnp.int32, (n, n), 0)
    ci = lax.broadcasted_iota(jnp.int32, (n, n), 1)
    same = (ri // CHUNK) == (ci // CHUNK)
    eye = jnp.where(ri == ci, 1.0, 0.0).astype(BF16)
    incl =(jnp.logical_and(same, ci <= ri), jnp.logical_and(same, ci >= ri))
    strict = (jnp.logical_and(same, ci < ri), jnp.logical_and(same, ci > ri))
    masks = _inv_masks(ri, ci)

    qn, kn, v, kk, qk = [], [], [], [], []
    for i in range(DN_GROUP):
        xi = xc[i * CHUNK:(i + 1) * CHUNK]
        q, k, vi = (_stack_heads(xi[:, j * DN_HEADS * dh:(j + 1) * DN_HEADS * dh], DN_HEADS) for j in range(3))
        qn.append(q * lax.rsqrt(jnp.sum(q * q, axis=-1, keepdims=True) + EPS) * (dh ** -0.5))
        kn.append(k * lax.rsqrt(jnp.sum(k * k, axis=-1, keepdims=True) + EPS))
        v.append(vi)
        kb = kn[i].astype(BF16)
        kk.append(_dot_nt(kb, kb))
        qk.append(_dot_nt(qn[i].astype(BF16), kb))

    systems = [(i, d) for i in range(DN_GROUP) for d in range(2)]
    rows, cs_rows = [], []
    for i, d in systems:
        raw = grow_ref[i]
        beta = _sigmoid(raw[d:d + 1])
        g_r = -jnp.exp(arow_ref[d, 0:1, :]) * _softplus(raw[2 + d:3 + d] + arow_ref[d, 1:2, :])
        cs = _cumsum_groups(g_r, 1, CHUNK, d == 1)
        tot = cs + _cumsum_groups(g_r, 1, CHUNK, d == 0) - g_r
        eg = jnp.exp(cs)
        rows += [cs, beta, eg, jnp.exp(tot - cs), beta * eg]
        cs_rows.append(cs)
        eg_ref[d, i] = jnp.exp(tot)
    n_col = len(rows) // len(systems)
    cols = jnp.concatenate(rows, axis=0).T

    a_list, rhs = [], []
    for s, (i, d) in enumerate(systems):
        cs_c, beta_c, eg_c, ekd_c, beg_c = (cols[:, n_col * s + j:n_col * s + j + 1] for j in range(n_col))
        gam = jnp.exp(jnp.where(incl[d], cs_c - cs_rows[s], NEG))
        a_list.append((jnp.where(strict[d], kk[i], 0.0) * gam * beta_c).astype(BF16))
        rhs.append(jnp.concatenate([beta_c * v[i], beg_c * kn[i]], axis=1).astype(BF16))
        qk_ref[d, i] = (qk[i] * gam).astype(qk_ref.dtype)
        qe_ref[d, i] = (qn[i] * eg_c).astype(qe_ref.dtype)
        kd_ref[d, i] = (kn[i] * ekd_c).astype(kd_ref.dtype)
    for (i, d), tinv, r in zip(systems, _tri_inv_many(a_list, eye, masks), rhs):
        sol = _dot(tinv, r)
        u_ref[d, i] = sol[:, :dh].astype(u_ref.dtype)
        w_ref[d, i] = sol[:, dh:].astype(w_ref.dtype)


def _dn_prep(x_cols, x_rows, grow, conv_w, arow, l):
    b, nc, _, width = x_rows.shape
    n_cols = x_cols.shape[1]
    n_groups = nc // DN_GROUP
    n_lat_groups = n_cols // DN_GROUP
    assert n_groups == n_lat_groups + 1
    n = DN_HEADS * CHUNK
    dh = width // (3 * DN_HEADS)
    lat_g = lambda g: jnp.minimum(g, n_lat_groups - 1)
    out = lambda w, dt: (jax.ShapeDtypeStruct((b, 2, nc, n, w), dt),
                         pl.BlockSpec((None, 2, DN_GROUP, n, w), lambda bi, g: (bi, 0, g, 0, 0)))
    outs = [out(dh, BF16)] * 4 + [out(n, BF16)]
    outs.append((jax.ShapeDtypeStruct((b, 2, nc, 1, n), F32),
                 pl.BlockSpec((None, 2, DN_GROUP, 1, n), lambda bi, g: (bi, 0, g, 0, 0))))
    chunk = lambda idx: pl.BlockSpec((None, None, CHUNK, width), lambda bi, g: (bi, idx(g), 0, 0))
    return pl.pallas_call(
        functools.partial(_dn_prep_kernel, n_lat_groups=n_lat_groups),
        grid=(b, n_groups),
        in_specs=[pl.BlockSpec((None, DN_GROUP, CHUNK, width), lambda bi, g: (bi, lat_g(g), 0, 0)),
                  chunk(lambda g: jnp.maximum(lat_g(g) * DN_GROUP - 1, 0)),
                  chunk(lambda g: jnp.minimum(lat_g(g) * DN_GROUP + DN_GROUP, n_cols - 1)),
                  pl.BlockSpec((None, DN_GROUP, CHUNK, width), lambda bi, g: (bi, n_groups - 1, 0, 0)),
                  pl.BlockSpec((None, DN_GROUP, 4, n), lambda bi, g: (bi, g, 0, 0)),
                  _const_spec((None, CONV_W, width), (l, 0, 0)),
                  _const_spec((None, 2, 2, n), (l, 0, 0, 0))],
        out_specs=[o[1] for o in outs],
        out_shape=[o[0] for o in outs],
        compiler_params=_params("parallel", "parallel"),
        name="dn_prep",
    )(x_cols, x_cols, x_cols, x_rows, grow, conv_w, arow)


def _dn_scan_kernel(*refs):
    ins, (of_ref, ob_ref, s_ref) = refs[:12], refs[12:]

    @pl.when(pl.program_id(1) == 0)
    def _():
        s_ref[...] = jnp.zeros(s_ref.shape, F32)

    rows = [slice(h * CHUNK, (h + 1) * CHUNK) for h in range(DN_HEADS)]
    seqs = [(r, d) for r in range(of_ref.shape[0]) for d in range(2)]
    get = lambda j, r, d: ins[6 * d + j][r]
    sb = [[s_ref[r, d, h].astype(BF16) for h in range(DN_HEADS)] for r, d in seqs]
    ws = [jnp.concatenate([_dot(get(1, r, d)[rows[h]], sb[i][h]) for h in range(DN_HEADS)], axis=0)
          for i, (r, d) in enumerate(seqs)]
    qs = [jnp.concatenate([_dot(get(2, r, d)[rows[h]], sb[i][h]) for h in range(DN_HEADS)], axis=0)
          for i, (r, d) in enumerate(seqs)]
    vnew = [(get(0, r, d).astype(F32) - w).astype(BF16) for w, (r, d) in zip(ws, seqs)]
    o = [q + _dot(get(4, r, d), vn) for q, vn, (r, d) in zip(qs, vnew, seqs)]
    for vn, (r, d) in zip(vnew, seqs):
        kd, eg = get(3, r, d), get(5, r, d)
        for h in range(DN_HEADS):
            s_ref[r, d, h] = (eg[:, h * CHUNK:h * CHUNK + 1] * s_ref[r, d, h]
                              + _dot_tn(kd[rows[h]], vn[rows[h]]))
    for oi, (r, d) in zip(o, seqs):
        o_ref = (of_ref, ob_ref)[d]
        o_ref[r] = _unstack_heads(oi, DN_HEADS).astype(o_ref.dtype)


def _dn_scan(prep, n_lat_chunks):
    b, _, nc, _, dh = prep[0].shape
    ow = DN_HEADS * dh
    n_ctx_chunks = nc - n_lat_chunks
    fwd = lambda s: jnp.where(s < n_ctx_chunks, n_lat_chunks + s, s - n_ctx_chunks)
    bwd = lambda s: nc - 1 - s

    rpb = DN_SCAN_ROWS if b % DN_SCAN_ROWS == 0 else 1

    def spec(a, d, chunk):
        return pl.BlockSpec((rpb, None, None) + a.shape[3:], lambda bi, s: (bi, d, chunk(s), 0, 0))

    out = lambda chunk: pl.BlockSpec((rpb, None, CHUNK, ow), lambda bi, s: (bi, chunk(s), 0, 0))
    return pl.pallas_call(
        _dn_scan_kernel,
        grid=(b // rpb, nc),
        in_specs=[spec(a, 0, fwd) for a in prep] + [spec(a, 1, bwd) for a in prep],
        out_specs=[out(fwd), out(bwd)],
        out_shape=[jax.ShapeDtypeStruct((b, nc, CHUNK, ow), BF16)] * 2,
        scratch_shapes=[pltpu.VMEM((rpb, 2, DN_HEADS, dh, dh), F32)],
        compiler_params=_params("parallel", "arbitrary"),
        name="dn_scan",
    )(*prep, *prep)


def _block_diag(w):
    n, i, j = w.shape
    return jnp.einsum('nij,nm->nimj', w, jnp.eye(n, dtype=w.dtype)).reshape(n * i, n * j)


def _dn_gate_rows(raw, n_lat):
    b = raw.shape[0]
    lat = raw[:, :n_lat].reshape(b, n_lat // CHUNK, CHUNK, 4, DN_HEADS)
    ctx = raw[:, n_lat:].reshape(b, -1, CHUNK, 4, DN_HEADS)
    row = jnp.concatenate([lat.transpose(0, 2, 3, 4, 1), ctx.transpose(0, 1, 3, 4, 2)], axis=1)
    return row.reshape(b, row.shape[1], 4, DN_HEADS * CHUNK)


def kernel(x, c, ctx, c_ctx, w_mod, b_mod, norm_g, ffn_w_gu, ffn_w_down, w_in, ml_gate_b, ml_norm_g,
           lru_conv_w, lru_conv_b, lru_w_a, lru_b_a, lru_w_x, lru_b_x, lru_lambda, dn_conv_w,
           dn_a_log, dn_dt_bias, dn_norm_g, w_branch, w_out):
    b, n_lat, d = x.shape
    n_ctx = ctx.shape[1]
    depth = w_mod.shape[0]
    bw = w_branch.shape[2]
    assert n_lat == CHUNK * CHUNK and n_lat % TM == 0 and n_ctx % TM == 0 and n_ctx == CHUNK * DN_GROUP
    n_lat_tiles, n_ctx_tiles = n_lat // TM, n_ctx // TM
    n_tiles = n_lat_tiles + n_ctx_tiles

    ctx_row = b
    n_rows = -(-(b + 1) // V7X_SUBLANES) * V7X_SUBLANES
    cc = jnp.zeros((n_rows, d), F32).at[:b].set(c).at[b].set(c_ctx)
    mod = _mod_table(cc, w_mod, b_mod).reshape(depth, n_rows, N_MOD, d)

    dqk = bw // 2
    edges = [0]
    for wdt in (dqk, dqk, bw, bw, N_GATES, bw, bw, bw, bw, bw, bw, N_GATES, N_BRANCH * d):
        edges.append(edges[-1] + wdt)
    piece = lambda i, j: w_in[:, :, edges[i]:edges[j]]
    widths = (3 * bw, 2 * bw, 3 * bw, bw, N_BRANCH * d)
    w_main = jnp.concatenate([piece(0, 4), piece(5, 7), piece(7, 11), piece(12, 13)], axis=-1).astype(BF16)
    w_gate = jnp.concatenate([piece(4, 5), piece(11, 12)], axis=-1).astype(BF16)
    wgu = ffn_w_gu.astype(BF16)
    wdn = ffn_w_down.astype(BF16)
    wbr = w_branch.astype(BF16)
    wout = w_out.astype(BF16)
    ml_bias = jnp.concatenate([ml_gate_b.reshape(depth, N_GATES), jnp.zeros((depth, N_GATES), F32)], axis=-1)
    ml_bias_r = ml_bias[:, :, None]
    ml_g = ml_norm_g[:, None, :]
    dn_g = jnp.tile(dn_norm_g, (1, DN_HEADS))[:, None, :]
    lru_w = jnp.stack([jnp.concatenate([jax.vmap(_block_diag)(lru_w_a[:, dd]), jax.vmap(_block_diag)(lru_w_x[:, dd])],
                                       axis=-1) for dd in range(2)], axis=1).astype(BF16)
    lru_b = jnp.concatenate([lru_b_a, lru_b_x], axis=-1)[:, :, None, :]
    lru_lam = lru_lambda[:, :, None, :]
    lru_cb = lru_conv_b[:, None, :]
    dn_arow = jnp.repeat(jnp.stack([dn_a_log, dn_dt_bias], axis=2), CHUNK, axis=3)

    rows3 = lambda a: a.reshape(b, n_lat + n_ctx, a.shape[-1])
    for l in range(depth):
        full = _TilePlan(b, n_tiles, n_lat_tiles, n_tiles, ctx_row, l)
        if l == 0:
            xz = _ffn(full, _flat(x), mod, norm_g, wgu, wdn, 0, ctx=_flat(ctx))
        else:
            xz = _ffn(full, xz, mod, norm_g, wgu, wdn, 0)
        ml, lru, dnqkv, dnz, mg, gates = _inproj(full, xz, mod, norm_g, w_main, w_gate, widths, 1, bw)
        gates = rows3(gates)
        ml_hf, ml_hb = _mlstm(rows3(ml), gates, ml_bias_r, l, n_lat_tiles, n_ctx_tiles)
        lr_hf, lr_hb = _lru(rows3(lru), lru_conv_w, lru_cb, lru_w, lru_b, lru_lam, l, n_lat_tiles, n_ctx_tiles)
        n_cols = n_lat // CHUNK
        dq = rows3(dnqkv)
        x_cols = dq[:, :n_lat].reshape(b, n_cols, CHUNK, dq.shape[-1]).swapaxes(1, 2)
        x_rows = dq.reshape(b, -1, CHUNK, dq.shape[-1])
        prep = _dn_prep(x_cols, x_rows, _dn_gate_rows(gates[:, :, N_GATES:], n_lat), dn_conv_w, dn_arow, l)
        dn_all = _dn_scan(prep, n_cols)
        dn_lat = [_flat(o[:, :n_cols].swapaxes(1, 2)) for o in dn_all]
        xz = _merge(full, xz, mod, norm_g, _flat(ml_hf), _flat(ml_hb), ml, _flat(lr_hf), _flat(lr_hb), lru,
                    dn_lat, [_flat(o) for o in dn_all], dnz, mg, ml_g, dn_g, wbr, wout)
        visit = n_lat_tiles if l == depth - 1 else n_tiles
        xz = _ffn(_TilePlan(b, n_tiles, n_lat_tiles, visit, ctx_row, l), xz, mod, norm_g, wgu, wdn, 2)
    return xz.reshape(b, n_lat, d)
```

```python
import functools
import math

import jax
import jax.numpy as jnp
from jax import lax
from jax.experimental import pallas as pl
from jax.experimental.pallas import tpu as pltpu

F32 = jnp.float32
BF16 = jnp.bfloat16

EPS = 1e-6
N_MOD = 9
N_BRANCH = 3
CONV_W = 4
CONV_LEFT = 2
ML_HEADS = 4
LRU_C = 8.0
DN_HEADS = 4
CHUNK = 64
DN_GROUP = 4
DN_SCAN_ROWS = 4
ML_ROWS = 2
N_GATES = 16

V7X_SUBLANES = 8
V7X_LANES = 128
V7X_VMEM_BYTES = 64 * 1024 * 1024
VMEM_LIMIT = V7X_VMEM_BYTES - 8 * 1024 * 1024

TM = 256
NEG = -1e30


def _sigmoid(x):
    return 0.5 * jnp.tanh(0.5 * x) + 0.5


def _silu(x):
    return x * _sigmoid(x)


def _softplus(x):
    return jnp.maximum(x, 0.0) + jnp.log(1.0 + jnp.exp(-jnp.abs(x)))


def _rms(x):
    return x * lax.rsqrt(jnp.mean(x * x, axis=-1, keepdims=True) + EPS)


def _rms_heads(x, n_heads):
    hd = x.shape[-1] // n_heads
    return jnp.concatenate([_rms(x[:, h * hd:(h + 1) * hd]) for h in range(n_heads)], axis=-1)


def _gelu_tanh(x):
    return 0.5 * x * (1.0 + jnp.tanh(math.sqrt(2.0 / math.pi) * (x + 0.044715 * (x * x * x))))


def _dot(a, b):
    return jnp.dot(a, b, preferred_element_type=F32)


def _dot_nt(a, b):
    return lax.dot_general(a, b, (((1,), (1,)), ((), ())), preferred_element_type=F32)


def _dot_tn(a, b):
    return lax.dot_general(a, b, (((0,), (0,)), ((), ())), preferred_element_type=F32)


def _params(*sem):
    return pltpu.CompilerParams(dimension_semantics=sem, vmem_limit_bytes=VMEM_LIMIT)


def _const_spec(block, index):
    return pl.BlockSpec(block, lambda *_: index, pipeline_mode=pl.Buffered(1))


def _stack_heads(x, n):
    w = x.shape[1] // n
    return jnp.concatenate([x[:, h * w:(h + 1) * w] for h in range(n)], axis=0)


def _unstack_heads(x, n):
    t = x.shape[0] // n
    return jnp.concatenate([x[h * t:(h + 1) * t] for h in range(n)], axis=1)


def _cumsum_groups(x, axis, period, rev):
    n = x.shape[axis]
    idx = lax.broadcasted_iota(jnp.int32, x.shape, axis) % period
    sh = 1
    while sh < period:
        if rev:
            x = x + jnp.where(idx < period - sh, pltpu.roll(x, n - sh, axis=axis), 0.0)
        else:
            x = x + jnp.where(idx >= sh, pltpu.roll(x, sh, axis=axis), 0.0)
        sh *= 2
    return x


def _cummax_groups(x, axis, period, rev):
    n = x.shape[axis]
    idx = lax.broadcasted_iota(jnp.int32, x.shape, axis) % period
    sh = 1
    while sh < period:
        if rev:
            x = jnp.maximum(x, jnp.where(idx < period - sh, pltpu.roll(x, n - sh, axis=axis), NEG))
        else:
            x = jnp.maximum(x, jnp.where(idx >= sh, pltpu.roll(x, sh, axis=axis), NEG))
        sh *= 2
    return x


def _shift_rows(xb, prev8, nxt8, off):
    t = xb.shape[0]
    ri = lax.broadcasted_iota(jnp.int32, (t, t), 0)
    ci = lax.broadcasted_iota(jnp.int32, (t, t), 1)
    shifted = _dot(jnp.where(ci == ri + off, 1.0, 0.0).astype(BF16), xb)
    row8 = lax.broadcasted_iota(jnp.int32, (V7X_SUBLANES, 1), 0)
    if off < 0:
        edge = jnp.where(row8 < -off, pltpu.roll(prev8, -off, axis=0), shifted[:V7X_SUBLANES])
        return jnp.concatenate([edge, shifted[V7X_SUBLANES:]], axis=0)
    edge = jnp.where(row8 >= V7X_SUBLANES - off, pltpu.roll(nxt8, V7X_SUBLANES - off, axis=0),
                     shifted[t - V7X_SUBLANES:])
    return jnp.concatenate([shifted[:t - V7X_SUBLANES], edge], axis=0)


def _conv4(xb, prev8, nxt8, w):
    acc = xb.astype(F32) * w[CONV_LEFT:CONV_LEFT + 1]
    for j in range(CONV_W):
        if j != CONV_LEFT:
            acc = acc + _shift_rows(xb, prev8, nxt8, j - CONV_LEFT) * w[j:j + 1]
    return acc


def _mod_kernel(c_ref, w_ref, b_ref, o_ref):
    s = _silu(c_ref[...]).astype(BF16)
    o_ref[...] = _dot(s, w_ref[...].astype(BF16)) + b_ref[...]


def _mod_table(cc, w_mod, b_mod):
    depth, d, nd = w_mod.shape
    r = cc.shape[0]
    tn = nd // 4
    return pl.pallas_call(
        _mod_kernel,
        grid=(depth, nd // tn),
        in_specs=[pl.BlockSpec((r, d), lambda l, j: (0, 0)),
                  pl.BlockSpec((None, d, tn), lambda l, j: (l, 0, j)),
                  pl.BlockSpec((None, 1, tn), lambda l, j: (l, 0, j))],
        out_specs=pl.BlockSpec((None, r, tn), lambda l, j: (l, 0, j)),
        out_shape=jax.ShapeDtypeStruct((depth, r, nd), F32),
        compiler_params=_params("parallel", "parallel"),
        name="mod_table",
    )(cc, w_mod, b_mod.reshape(depth, 1, nd))


class _TilePlan:
    def __init__(self, b, n_tiles, n_lat_tiles, n_vis, ctx_row, layer):
        self.b, self.n_tiles, self.n_lat_tiles, self.n_vis = b, n_tiles, n_lat_tiles, n_vis
        self.ctx_row, self.layer = ctx_row, layer
        self.sub = 2 if (b * n_vis) % 2 == 0 else 1
        self.grid = (b * n_vis // self.sub,)

    def _tile(self, i, k):
        tid = i * self.sub + k
        return tid // self.n_vis, tid % self.n_vis

    def tok(self, k, width, col=0):
        def index(i):
            bi, t = self._tile(i, k)
            return bi * self.n_tiles + t, col
        return pl.BlockSpec((TM, width), index)

    def toks(self, width, col=0):
        return [self.tok(k, width, col) for k in range(self.sub)]

    def is_ctx(self, i, k):
        return self._tile(i, k)[1] >= self.n_lat_tiles

    def tok_ctx_rest(self, k, width):
        def index(i):
            bi, t = self._tile(i, k)
            return bi * self.n_tiles + jnp.maximum(t, self.n_lat_tiles), 0
        return pl.BlockSpec((TM, width), index)

    def tok_split(self, k, width, ctx):
        n_ctx_tiles = self.n_tiles - self.n_lat_tiles

        def index(i):
            bi, t = self._tile(i, k)
            if ctx:
                return bi * n_ctx_tiles + jnp.clip(t - self.n_lat_tiles, 0, n_ctx_tiles - 1), 0
            return bi * self.n_lat_tiles + jnp.minimum(t, self.n_lat_tiles - 1), 0
        return pl.BlockSpec((TM, width), index)

    def mods(self, d):
        def spec(k):
            def index(i):
                bi, t = self._tile(i, k)
                return self.layer, jnp.where(t >= self.n_lat_tiles, self.ctx_row, bi), 0, 0
            return pl.BlockSpec((None, None, N_MOD, d), index)
        return [spec(k) for k in range(self.sub)]

    def out(self, width):
        return pl.BlockSpec((self.sub * TM, width), lambda i: (i, 0))

    def out_shape(self, width, dtype):
        return jax.ShapeDtypeStruct((self.b * self.n_vis * TM, width), dtype)

    def rows(self, k):
        return slice(k * TM, (k + 1) * TM)


def _flat(a):
    return a.reshape(-1, a.shape[-1])


def _ffn_kernel(*refs, j, dff, sub, ctx_tiles):
    if ctx_tiles is not None:
        lat_refs, ctx_refs, refs = refs[:sub], refs[sub:2 * sub], refs[sub:]
        xs = [jnp.where(ctx_tiles(k), c[...], x[...]) for k, (x, c) in enumerate(zip(lat_refs, ctx_refs))]
    else:
        xs = [x[...] for x in refs[:sub]]
    mod_refs = refs[sub:2 * sub]
    g_ref, wgu_ref, wd_ref, o_ref = refs[2 * sub:]
    g_pre, g_post = g_ref[2 * j:2 * j + 1, :], g_ref[2 * j + 1:2 * j + 2, :]
    mods = [[m[3 * j + i:3 * j + i + 1, :] for i in range(3)] for m in mod_refs]
    hs = [(_rms(x) * g_pre * (1.0 + m[1]) + m[0]).astype(BF16) for x, m in zip(xs, mods)]
    gus = [_dot(h, wgu_ref[...]) for h in hs]
    acts = [(_silu(gu[:, :dff]) * gu[:, dff:]).astype(BF16) for gu in gus]
    ys = [_dot(a, wd_ref[...]) for a in acts]
    for k, (x, m, y) in enumerate(zip(xs, mods, ys)):
        o_ref[k * TM:(k + 1) * TM, :] = x + 0.5 * m[2] * (_rms(y) * g_post)


def _ffn(plan, xz, mod, norm_g, wgu, wd, j, ctx=None):
    d = xz.shape[-1]
    dff = wd.shape[2]
    l = plan.layer
    if ctx is None:
        tok_specs, tok_args, ctx_tiles = plan.toks(d), [xz] * plan.sub, None
    else:
        tok_specs = [plan.tok_split(k, d, False) for k in range(plan.sub)] \
                    + [plan.tok_split(k, d, True) for k in range(plan.sub)]
        tok_args = [xz] * plan.sub + [ctx] * plan.sub
        ctx_tiles = lambda k: plan.is_ctx(pl.program_id(0), k)
    return pl.pallas_call(
        functools.partial(_ffn_kernel, j=j, dff=dff, sub=plan.sub, ctx_tiles=ctx_tiles),
        grid=plan.grid,
        in_specs=tok_specs + plan.mods(d)
                 + [_const_spec((None,) + norm_g.shape[1:], (l, 0, 0)),
                    _const_spec((None, None, d, 2 * dff), (l, j // 2, 0, 0)),
                    _const_spec((None, None, dff, d), (l, j // 2, 0, 0))],
        out_specs=plan.out(d),
        out_shape=plan.out_shape(d, F32),
        compiler_params=_params("parallel"),
        name=f"ffn{j}",
    )(*tok_args, *([mod] * plan.sub), norm_g, wgu, wd)


def _inproj_kernel(*refs, sub, perm_out, perm_width):
    x_refs, mod_refs = refs[:sub], refs[sub:2 * sub]
    g_ref, w_ref, wg_ref = refs[2 * sub:2 * sub + 3]
    o_refs = refs[2 * sub + 3:]
    hs = [(_rms(x[...]) * g_ref[2:3, :] * (1.0 + m[4:5, :]) + m[3:4, :]).astype(BF16)
          for x, m in zip(x_refs, mod_refs)]
    perm = _time_perm(False)
    col = 0
    for oi, o_ref in enumerate(o_refs[:-1]):
        n = o_ref.shape[-1]
        for k, h in enumerate(hs):
            res = _dot(h, w_ref[:, col:col + n]).astype(o_ref.dtype)
            if oi == perm_out:
                res = jnp.concatenate([_dot(perm, res[:, :perm_width]).astype(o_ref.dtype),
                                       res[:, perm_width:]], axis=1)
            o_ref[k * TM:(k + 1) * TM, :] = res
        col += n
    for k, h in enumerate(hs):
        o_refs[-1][k * TM:(k + 1) * TM, :] = _dot(h, wg_ref[...])


def _inproj(plan, xz, mod, norm_g, w_main, w_gate, widths, perm_out, perm_width):
    d = xz.shape[-1]
    ng = w_gate.shape[-1]
    l = plan.layer
    return pl.pallas_call(
        functools.partial(_inproj_kernel, sub=plan.sub, perm_out=perm_out, perm_width=perm_width),
        grid=plan.grid,
        in_specs=plan.toks(d) + plan.mods(d)
                 + [_const_spec((None,) + norm_g.shape[1:], (l, 0, 0)),
                    _const_spec((None, d, w_main.shape[-1]), (l, 0, 0)),
                    _const_spec((None, d, ng), (l, 0, 0))],
        out_specs=[plan.out(w) for w in widths] + [plan.out(ng)],
        out_shape=[plan.out_shape(w, BF16) for w in widths] + [plan.out_shape(ng, F32)],
        compiler_params=_params("parallel"),
        name="inproj",
    )(*([xz] * plan.sub), *([mod] * plan.sub), norm_g, w_main, w_gate)


N_MERGE_STREAMS = 12


def _merge_kernel(*refs, sub, ctx_tiles):
    x_refs, mod_refs = refs[:sub], refs[sub:2 * sub]
    tok = refs[2 * sub:(2 + N_MERGE_STREAMS) * sub]
    g_ref, mlg_ref, dng_ref, wb_ref, wo_ref, o_ref = refs[(2 + N_MERGE_STREAMS) * sub:]
    d = o_ref.shape[-1]
    f32 = lambda r: r[...].astype(F32)
    unperm = _time_perm(True)
    mixes = []
    for k in range(sub):
        mlf, mlb, mlo, lrf, lrb, lry, dlf, dlb, dcf, dcb, dnz, mg = (
            tok[s * sub + k] for s in range(N_MERGE_STREAMS))
        y_ml = _rms_heads(f32(mlf) + f32(mlb), ML_HEADS) * mlg_ref[...] * _sigmoid(f32(mlo))
        y_lr = (_dot(unperm, lrf[...]) + _dot(unperm, lrb[...])) * _gelu_tanh(f32(lry))
        dn_h = jnp.where(ctx_tiles(k), f32(dcf) + f32(dcb), f32(dlf) + f32(dlb))
        y_dn = _rms_heads(dn_h, DN_HEADS) * dng_ref[...] * _silu(f32(dnz))
        mix = None
        for n, y in enumerate((y_ml, y_lr, y_dn)):
            term = _sigmoid(mg[:, n * d:(n + 1) * d].astype(F32)) * _dot(y.astype(BF16), wb_ref[n])
            mix = term if mix is None else mix + term
        mixes.append(mix.astype(BF16))
    outs = [_dot(mix, wo_ref[...]) for mix in mixes]
    for k, out in enumerate(outs):
        o_ref[k * TM:(k + 1) * TM, :] = x_refs[k][...] + mod_refs[k][5:6, :] * (_rms(out) * g_ref[3:4, :])


def _merge(plan, xz, mod, norm_g, ml_hf, ml_hb, ml, lr_hf, lr_hb, lru, dn_lat, dn_all, dnz, mg,
           ml_g, dn_g, w_branch, w_out):
    d = xz.shape[-1]
    bw = w_branch.shape[2]
    l = plan.layer
    sub = range(plan.sub)
    whole = lambda a, w, c=0: (a, plan.toks(w, c))
    streams = [whole(ml_hf, bw), whole(ml_hb, bw), whole(ml, bw, ml.shape[-1] // bw - 1),
               whole(lr_hf, bw), whole(lr_hb, bw), whole(lru, bw, lru.shape[-1] // bw - 1),
               (dn_lat[0], [plan.tok_split(k, bw, False) for k in sub]),
               (dn_lat[1], [plan.tok_split(k, bw, False) for k in sub]),
               (dn_all[0], [plan.tok_ctx_rest(k, bw) for k in sub]),
               (dn_all[1], [plan.tok_ctx_rest(k, bw) for k in sub]),
               whole(dnz, bw), whole(mg, N_BRANCH * d)]
    assert len(streams) == N_MERGE_STREAMS
    tok_specs = [sp for _, specs in streams for sp in specs]
    tok_args = [a for a, _ in streams for _ in sub]
    return pl.pallas_call(
        functools.partial(_merge_kernel, sub=plan.sub, ctx_tiles=lambda k: plan.is_ctx(pl.program_id(0), k)),
        grid=plan.grid,
        in_specs=plan.toks(d) + plan.mods(d) + tok_specs
                 + [_const_spec((None,) + norm_g.shape[1:], (l, 0, 0)),
                    _const_spec((None, 1, bw), (l, 0, 0)),
                    _const_spec((None, 1, bw), (l, 0, 0)),
                    _const_spec((None, N_BRANCH, bw, d), (l, 0, 0, 0)),
                    _const_spec((None, d, d), (l, 0, 0))],
        out_specs=plan.out(d),
        out_shape=plan.out_shape(d, F32),
        compiler_params=_params("parallel"),
        name="merge",
    )(*([xz] * plan.sub), *([mod] * plan.sub), *tok_args, norm_g, ml_g, dn_g, w_branch, w_out)


SEG = TM // V7X_SUBLANES


def _time_perm(inverse):
    ri = lax.broadcasted_iota(jnp.int32, (TM, TM), 0)
    ci = lax.broadcasted_iota(jnp.int32, (TM, TM), 1)
    r, t = (ci, ri) if inverse else (ri, ci)
    return jnp.where(t == (r % V7X_SUBLANES) * SEG + r // V7X_SUBLANES, 1.0, 0.0).astype(BF16)


def _conv4_perm(x, before1, before2, after1, w):
    sub = lax.broadcasted_iota(jnp.int32, (V7X_SUBLANES, 1), 0)
    vrow = lambda i: x[V7X_SUBLANES * i:V7X_SUBLANES * (i + 1)]
    m1_edge = jnp.where(sub == 0, before1, pltpu.roll(vrow(SEG - 1), 1, axis=0))
    m2_edge = jnp.where(sub == 0, before2, pltpu.roll(vrow(SEG - 2), 1, axis=0))
    p1_edge = jnp.where(sub == V7X_SUBLANES - 1, after1, pltpu.roll(vrow(0), V7X_SUBLANES - 1, axis=0))
    x_m1 = jnp.concatenate([m1_edge, x[:-V7X_SUBLANES]], axis=0)
    x_m2 = jnp.concatenate([m2_edge, m1_edge, x[:-2 * V7X_SUBLANES]], axis=0)
    x_p1 = jnp.concatenate([x[V7X_SUBLANES:], p1_edge], axis=0)
    taps = {-2: x_m2, -1: x_m1, 0: x, 1: x_p1}
    acc = None
    for j in range(CONV_W):
        term = taps[j - CONV_LEFT] * w[j:j + 1]
        acc = term if acc is None else acc + term
    return acc


def _tile_scan(a, b, h0, rev):
    vrow = lambda x, i: x[V7X_SUBLANES * i:V7X_SUBLANES * (i + 1)]
    h = jnp.zeros_like(vrow(a, 0))
    p = jnp.ones_like(h)
    hs, ps = [None] * SEG, [None] * SEG
    for i in (range(SEG - 1, -1, -1) if rev else range(SEG)):
        ai = vrow(a, i)
        h = ai * h + vrow(b, i)
        p = ai * p
        hs[i], ps[i] = h, p
    carry = h0
    enter = [None] * V7X_SUBLANES
    for s in (range(V7X_SUBLANES - 1, -1, -1) if rev else range(V7X_SUBLANES)):
        enter[s] = carry
        carry = p[s:s + 1] * carry + h[s:s + 1]
    enter = jnp.concatenate(enter, axis=0)
    return jnp.concatenate([hi + pi * enter for hi, pi in zip(hs, ps)], axis=0), carry


def _lru_kernel(x_ref, cw_ref, cb_ref, w_ref, bias_ref, lam_ref, of_ref, ob_ref, *, n_lat, n_ctx):
    t, c = TM, x_ref.shape[-1]
    n_tiles = n_lat + n_ctx
    halo = 2 * V7X_SUBLANES

    def load_conv(tile):
        r0 = pl.multiple_of(tile * t, t)
        first = jnp.logical_or(tile == 0, tile == n_lat)
        last = jnp.logical_or(tile == n_lat - 1, tile == n_tiles - 1)
        x = x_ref[pl.ds(r0, t), :].astype(F32)
        p0 = pl.multiple_of(jnp.maximum(r0 - halo, 0), halo)
        n0 = pl.multiple_of(jnp.minimum(r0 + t, n_tiles * t - halo), halo)
        prev = jnp.where(first, 0.0, x_ref[pl.ds(p0, halo), :].astype(F32))
        nxt = jnp.where(last, 0.0, x_ref[pl.ds(n0, halo), :].astype(F32))
        before1, before2 = prev[halo - 1:halo], prev[V7X_SUBLANES - 1:V7X_SUBLANES]
        return r0, _conv4_perm(x, before1, before2, nxt[0:1], cw_ref[...]) + cb_ref[...]

    def direction(d, tile, h0, o_ref):
        r0, xc = load_conv(tile)
        z = _dot(xc.astype(BF16), w_ref[d]) + bias_ref[d]
        r, i = _sigmoid(z[:, :c]), _sigmoid(z[:, c:])
        la = (-LRU_C * _softplus(-lam_ref[d])) * r
        a = jnp.exp(la)
        bx = jnp.sqrt(jnp.tanh(-la) * (1.0 + a * a)) * (i * xc)
        h, carry = _tile_scan(a, bx, h0, d == 1)
        o_ref[pl.ds(r0, t), :] = h.astype(o_ref.dtype)
        return carry

    def step(s, carry):
        hf, hb = carry
        hf = direction(0, jnp.where(s < n_ctx, n_lat + s, s - n_ctx), hf, of_ref)
        hb = direction(1, n_tiles - 1 - s, hb, ob_ref)
        return hf, hb

    zero = jnp.zeros((1, c), F32)
    lax.fori_loop(0, n_tiles, step, (zero, zero))


def _lru(lru, conv_w, conv_b, w_gates, b_gates, lam, l, n_lat, n_ctx):
    b, lt, _ = lru.shape
    c = conv_w.shape[-1]
    seq = pl.BlockSpec((None, lt, c), lambda bi: (bi, 0, 0))
    return pl.pallas_call(
        functools.partial(_lru_kernel, n_lat=n_lat, n_ctx=n_ctx),
        grid=(b,),
        in_specs=[seq,
                  _const_spec((None, CONV_W, c), (l, 0, 0)),
                  _const_spec((None, 1, c), (l, 0, 0)),
                  _const_spec((None, 2, c, 2 * c), (l, 0, 0, 0)),
                  _const_spec((None, 2, 1, 2 * c), (l, 0, 0, 0)),
                  _const_spec((None, 2, 1, c), (l, 0, 0, 0))],
        out_specs=[seq, seq],
        out_shape=[jax.ShapeDtypeStruct((b, lt, c), BF16)] * 2,
        compiler_params=_params("parallel"),
        name="lru",
    )(lru, conv_w, conv_b, w_gates, b_gates, lam)


def _mlstm_kernel(qf_ref, kf_ref, vf_ref, qb_ref, kb_ref, vb_ref, grf_ref, grb_ref, br_ref,
                  hf_ref, hb_ref, c_ref, m_ref):
    @pl.when(pl.program_id(1) == 0)
    def _():
        c_ref[...] = jnp.zeros(c_ref.shape, F32)
        m_ref[...] = jnp.zeros(m_ref.shape, F32)

    nrow, tc, nqk = qf_ref.shape
    dk = nqk // ML_HEADS
    dv = vf_ref.shape[-1] // ML_HEADS
    nh = ML_HEADS
    ri = lax.broadcasted_iota(jnp.int32, (tc, tc), 0)
    ci = lax.broadcasted_iota(jnp.int32, (tc, tc), 1)
    causal = (ci <= ri, ci >= ri)
    lane_head = lax.broadcasted_iota(jnp.int32, (1, nqk), 1) // dk
    ones = jnp.ones((tc, dv), BF16)
    dirs = ((qf_ref, kf_ref, vf_ref, grf_ref), (qb_ref, kb_ref, vb_ref, grb_ref))
    seqs = [(r, d) for r in range(nrow) for d in range(2)]

    rows, v_rows, ws_rows, decs = [], [], [], []
    for r, d in seqs:
        rev = d == 1
        gr = dirs[d][3][r].T + br_ref[...]
        i_r = gr[nh * d:nh * (d + 1)]
        b_r = _cumsum_groups(-_softplus(-gr[nh * (2 + d):nh * (3 + d)]), 1, tc, rev)
        m_prev = m_ref[r, nh * d:nh * (d + 1), 0:1]
        m_t = b_r + jnp.maximum(m_prev, _cummax_groups(i_r - b_r, 1, tc, rev))
        b_end = b_r[:, 0:1] if rev else b_r[:, tc - 1:tc]
        lws = b_end - b_r + i_r
        m_new = jnp.maximum(b_end + m_prev, jnp.max(lws, axis=1, keepdims=True))
        rows += [b_r - m_t, jnp.exp(b_r + m_prev - m_t), jnp.exp(-m_t)]
        v_rows.append(b_r - i_r)
        ws_rows.append(jnp.exp(lws - m_new))
        decs.append(jnp.exp(b_end + m_prev - m_new))
        m_ref[r, nh * d:nh * (d + 1), :] = jnp.broadcast_to(m_new, (nh, m_ref.shape[-1]))
    cols = jnp.concatenate(rows, axis=0).T

    chains = [(s, h) for s in range(len(seqs)) for h in range(nh)]
    q_all = [dirs[d][0][r] * (dk ** -0.5) for r, d in seqs]
    k_all = [dirs[d][1][r] for r, d in seqs]
    v_all = [dirs[d][2][r] for r, d in seqs]
    kt_all = [k.astype(F32).T for k in k_all]
    c_all = [c_ref[r, d] for r, d in seqs]
    cb_all = [c.astype(BF16) for c in c_all]
    col = lambda s, j, h: cols[:, (3 * s + j) * nh + h:(3 * s + j) * nh + h + 1]
    qh = [jnp.where(lane_head == h, q_all[s], jnp.zeros_like(q_all[s])) for s, h in chains]
    vp = [jnp.concatenate([v_all[s][:, h * dv:(h + 1) * dv], ones], axis=1) for s, h in chains]
    s_raw = [_dot_nt(q, k_all[s]) for q, (s, h) in zip(qh, chains)]
    p = [(sr * jnp.exp(jnp.where(causal[seqs[s][1]], col(s, 0, h) - v_rows[s][h:h + 1], NEG))).astype(BF16)
         for sr, (s, h) in zip(s_raw, chains)]
    num = [_dot(pc, vc) + col(s, 1, h) * _dot(q, cb_all[s]) for pc, vc, q, (s, h) in zip(p, vp, qh, chains)]
    outs = [nm[:, :dv] / jnp.maximum(jnp.abs(nm[:, dv:]), col(s, 2, h)) for nm, (s, h) in zip(num, chains)]
    for vc, (s, h) in zip(vp, chains):
        r, d = seqs[s]
        kw = (kt_all[s][h * dk:(h + 1) * dk] * ws_rows[s][h:h + 1]).astype(BF16)
        c_ref[r, d, h * dk:(h + 1) * dk, :] = decs[s][h:h + 1] * c_all[s][h * dk:(h + 1) * dk] + _dot(kw, vc)
    for s, (r, d) in enumerate(seqs):
        o_ref = (hf_ref, hb_ref)[d]
        o_ref[r] = jnp.concatenate(outs[s * nh:(s + 1) * nh], axis=1).astype(o_ref.dtype)


def _mlstm(ml, gates, bias_r, l, n_lat, n_ctx):
    b, lt, _ = ml.shape
    ng = gates.shape[-1]
    n_tiles = n_lat + n_ctx
    dqk = ml.shape[-1] // 6
    fwd = lambda s: jnp.where(s < n_ctx, n_lat + s, s - n_ctx)
    bwd = lambda s: n_tiles - 1 - s
    rpb = ML_ROWS if b % ML_ROWS == 0 else 1

    def specs(tile):
        return [pl.BlockSpec((rpb, TM, dqk), lambda bi, s: (bi, tile(s), 0)),
                pl.BlockSpec((rpb, TM, dqk), lambda bi, s: (bi, tile(s), 1)),
                pl.BlockSpec((rpb, TM, 2 * dqk), lambda bi, s: (bi, tile(s), 1))]

    row = lambda tile: pl.BlockSpec((rpb, TM, ng), lambda bi, s: (bi, tile(s), 0))
    out = lambda tile: pl.BlockSpec((rpb, TM, 2 * dqk), lambda bi, s: (bi, tile(s), 0))
    return pl.pallas_call(
        _mlstm_kernel,
        grid=(b // rpb, n_tiles),
        in_specs=specs(fwd) + specs(bwd) + [row(fwd), row(bwd), _const_spec((None, ng, 1), (l, 0, 0))],
        out_specs=[out(fwd), out(bwd)],
        out_shape=[jax.ShapeDtypeStruct((b, lt, 2 * dqk), BF16)] * 2,
        scratch_shapes=[pltpu.VMEM((rpb, 2, dqk, 2 * (2 * dqk // ML_HEADS)), F32),
                        pltpu.VMEM((rpb, 2 * ML_HEADS, V7X_LANES), F32)],
        compiler_params=_params("parallel", "arbitrary"),
        name="mlstm",
    )(ml, ml, ml, ml, ml, ml, gates, gates, bias_r)


INV_BASE = 8


def _inv_masks(ri, ci):
    blk = lambda s: (ri // s) == (ci // s)
    as_bf = lambda m: jnp.where(m, 1.0, 0.0).astype(BF16)
    masks, s = [as_bf(blk(INV_BASE))], INV_BASE
    while s < CHUNK:
        masks.append(as_bf(jnp.logical_and(blk(2 * s), jnp.logical_not(blk(s)))))
        s *= 2
    return masks


def _tri_inv_many(a_list, eye, same, masks):
    nblk = a_list[0].shape[0] // CHUNK
    compact = lambda m: functools.reduce(lambda u, v: u + v, [m[i * CHUNK:(i + 1) * CHUNK] for i in range(nblk)])
    spread = lambda c: jnp.concatenate([c] * nblk, axis=0) * same
    eye_c = compact(eye)
    ps = [a * masks[0] for a in a_list]
    pcs = [compact(p) for p in ps]
    xcs = [eye_c - pc for pc in pcs]
    for _ in range(INV_BASE.bit_length() - 2):
        pcs = [_dot(pc, p).astype(BF16) for pc, p in zip(pcs, ps)]
        ps = [spread(pc) for pc in pcs]
        xcs = [_dot(xc, eye + p).astype(BF16) for xc, p in zip(xcs, ps)]
    for m in masks[1:]:
        ys = [_dot(xc, a * m).astype(BF16) for xc, a in zip(xcs, a_list)]
        xcs = [xc - _dot(y, spread(xc)).astype(BF16) for y, xc in zip(ys, xcs)]
    return [spread(xc) for xc in xcs]


def _dn_prep_kernel(xl_ref, xp_ref, xn_ref, xc_ref, grow_ref, cw_ref, arow_ref,
                    u_ref, w_ref, qe_ref, kd_ref, qk_ref, eg_ref, *, n_lat_groups):
    g = pl.program_id(1)
    is_ctx = g >= n_lat_groups
    width = xl_ref.shape[-1]
    dh = width // (3 * DN_HEADS)
    n = DN_HEADS * CHUNK
    halo = 2 * V7X_SUBLANES
    x = jnp.where(is_ctx, xc_ref[...], xl_ref[...]).reshape(DN_GROUP * CHUNK, width)
    no_prev = jnp.logical_or(is_ctx, g == 0)
    no_next = jnp.logical_or(is_ctx, g == n_lat_groups - 1)
    prev8 = jnp.where(no_prev, 0.0, xp_ref[CHUNK - halo:, :].astype(F32)[V7X_SUBLANES:])
    nxt8 = jnp.where(no_next, 0.0, xn_ref[:halo, :].astype(F32)[:V7X_SUBLANES])
    xc = _silu(_conv4(x, prev8, nxt8, cw_ref[...]))

    ri = lax.broadcasted_iota(jnp.int32, (n, n), 0)
    ci = lax.broadcasted_iota(jnp.int32, (n, n), 1)
    same = (ri // CHUNK) == (ci // CHUNK)
    eye = jnp.where(ri == ci, 1.0, 0.0).astype(BF16)
    incl =(jnp.logical_and(same, ci <= ri), jnp.logical_and(same, ci >= ri))
    strict = (jnp.logical_and(same, ci < ri), jnp.logical_and(same, ci > ri))
    masks = _inv_masks(ri, ci)

    qn, kn, v, kk, qk = [], [], [], [], []
    for i in range(DN_GROUP):
        xi = xc[i * CHUNK:(i + 1) * CHUNK]
        q, k, vi = (_stack_heads(xi[:, j * DN_HEADS * dh:(j + 1) * DN_HEADS * dh], DN_HEADS) for j in range(3))
        qn.append(q * lax.rsqrt(jnp.sum(q * q, axis=-1, keepdims=True) + EPS) * (dh ** -0.5))
        kn.append(k * lax.rsqrt(jnp.sum(k * k, axis=-1, keepdims=True) + EPS))
        v.append(vi)
        kb = kn[i].astype(BF16)
        kk.append(_dot_nt(kb, kb))
        qk.append(_dot_nt(qn[i].astype(BF16), kb))

    systems = [(i, d) for i in range(DN_GROUP) for d in range(2)]
    rows, cs_rows = [], []
    for i, d in systems:
        raw = grow_ref[i]
        beta = _sigmoid(raw[d:d + 1])
        g_r = -jnp.exp(arow_ref[d, 0:1, :]) * _softplus(raw[2 + d:3 + d] + arow_ref[d, 1:2, :])
        cs = _cumsum_groups(g_r, 1, CHUNK, d == 1)
        tot = cs + _cumsum_groups(g_r, 1, CHUNK, d == 0) - g_r
        eg = jnp.exp(cs)
        rows += [cs, beta, eg, jnp.exp(tot - cs), beta * eg]
        cs_rows.append(cs)
        eg_ref[d, i] = jnp.exp(tot)
    n_col = len(rows) // len(systems)
    cols = jnp.concatenate(rows, axis=0).T

    a_list, rhs = [], []
    for s, (i, d) in enumerate(systems):
        cs_c, beta_c, eg_c, ekd_c, beg_c = (cols[:, n_col * s + j:n_col * s + j + 1] for j in range(n_col))
        gam = jnp.exp(jnp.where(incl[d], cs_c - cs_rows[s], NEG))
        a_list.append((jnp.where(strict[d], kk[i], 0.0) * gam * beta_c).astype(BF16))
        rhs.append(jnp.concatenate([beta_c * v[i], beg_c * kn[i]], axis=1).astype(BF16))
        qk_ref[d, i] = (qk[i] * gam).astype(qk_ref.dtype)
        qe_ref[d, i] = (qn[i] * eg_c).astype(qe_ref.dtype)
        kd_ref[d, i] = (kn[i] * ekd_c).astype(kd_ref.dtype)
    same_bf = jnp.where(same, 1.0, 0.0).astype(BF16)
    for (i, d), tinv, r in zip(systems, _tri_inv_many(a_list, eye, same_bf, masks), rhs):
        sol = _dot(tinv, r)
        u_ref[d, i] = sol[:, :dh].astype(u_ref.dtype)
        w_ref[d, i] = sol[:, dh:].astype(w_ref.dtype)


def _dn_prep(x_cols, x_rows, grow, conv_w, arow, l):
    b, nc, _, width = x_rows.shape
    n_cols = x_cols.shape[1]
    n_groups = nc // DN_GROUP
    n_lat_groups = n_cols // DN_GROUP
    assert n_groups == n_lat_groups + 1
    n = DN_HEADS * CHUNK
    dh = width // (3 * DN_HEADS)
    lat_g = lambda g: jnp.minimum(g, n_lat_groups - 1)
    out = lambda w, dt: (jax.ShapeDtypeStruct((b, 2, nc, n, w), dt),
                         pl.BlockSpec((None, 2, DN_GROUP, n, w), lambda bi, g: (bi, 0, g, 0, 0)))
    outs = [out(dh, BF16)] * 4 + [out(n, BF16)]
    outs.append((jax.ShapeDtypeStruct((b, 2, nc, 1, n), F32),
                 pl.BlockSpec((None, 2, DN_GROUP, 1, n), lambda bi, g: (bi, 0, g, 0, 0))))
    chunk = lambda idx: pl.BlockSpec((None, None, CHUNK, width), lambda bi, g: (bi, idx(g), 0, 0))
    return pl.pallas_call(
        functools.partial(_dn_prep_kernel, n_lat_groups=n_lat_groups),
        grid=(b, n_groups),
        in_specs=[pl.BlockSpec((None, DN_GROUP, CHUNK, width), lambda bi, g: (bi, lat_g(g), 0, 0)),
                  chunk(lambda g: jnp.maximum(lat_g(g) * DN_GROUP - 1, 0)),
                  chunk(lambda g: jnp.minimum(lat_g(g) * DN_GROUP + DN_GROUP, n_cols - 1)),
                  pl.BlockSpec((None, DN_GROUP, CHUNK, width), lambda bi, g: (bi, n_groups - 1, 0, 0)),
                  pl.BlockSpec((None, DN_GROUP, 4, n), lambda bi, g: (bi, g, 0, 0)),
                  _const_spec((None, CONV_W, width), (l, 0, 0)),
                  _const_spec((None, 2, 2, n), (l, 0, 0, 0))],
        out_specs=[o[1] for o in outs],
        out_shape=[o[0] for o in outs],
        compiler_params=_params("parallel", "parallel"),
        name="dn_prep",
    )(x_cols, x_cols, x_cols, x_rows, grow, conv_w, arow)


def _dn_scan_kernel(*refs):
    ins, (of_ref, ob_ref, s_ref) = refs[:12], refs[12:]

    @pl.when(pl.program_id(1) == 0)
    def _():
        s_ref[...] = jnp.zeros(s_ref.shape, F32)

    rows = [slice(h * CHUNK, (h + 1) * CHUNK) for h in range(DN_HEADS)]
    seqs = [(r, d) for r in range(of_ref.shape[0]) for d in range(2)]
    get = lambda j, r, d: ins[6 * d + j][r]
    sb = [[s_ref[r, d, h].astype(BF16) for h in range(DN_HEADS)] for r, d in seqs]
    ws = [jnp.concatenate([_dot(get(1, r, d)[rows[h]], sb[i][h]) for h in range(DN_HEADS)], axis=0)
          for i, (r, d) in enumerate(seqs)]
    qs = [jnp.concatenate([_dot(get(2, r, d)[rows[h]], sb[i][h]) for h in range(DN_HEADS)], axis=0)
          for i, (r, d) in enumerate(seqs)]
    vnew = [(get(0, r, d).astype(F32) - w).astype(BF16) for w, (r, d) in zip(ws, seqs)]
    o = [q + _dot(get(4, r, d), vn) for q, vn, (r, d) in zip(qs, vnew, seqs)]
    for vn, (r, d) in zip(vnew, seqs):
        kd, eg = get(3, r, d), get(5, r, d)
        for h in range(DN_HEADS):
            s_ref[r, d, h] = (eg[:, h * CHUNK:h * CHUNK + 1] * s_ref[r, d, h]
                              + _dot_tn(kd[rows[h]], vn[rows[h]]))
    for oi, (r, d) in zip(o, seqs):
        o_ref = (of_ref, ob_ref)[d]
        o_ref[r] = _unstack_heads(oi, DN_HEADS).astype(o_ref.dtype)


def _dn_scan(prep, n_lat_chunks):
    b, _, nc, _, dh = prep[0].shape
    ow = DN_HEADS * dh
    n_ctx_chunks = nc - n_lat_chunks
    fwd = lambda s: jnp.where(s < n_ctx_chunks, n_lat_chunks + s, s - n_ctx_chunks)
    bwd = lambda s: nc - 1 - s

    rpb = DN_SCAN_ROWS if b % DN_SCAN_ROWS == 0 else 1

    def spec(a, d, chunk):
        return pl.BlockSpec((rpb, None, None) + a.shape[3:], lambda bi, s: (bi, d, chunk(s), 0, 0))

    out = lambda chunk: pl.BlockSpec((rpb, None, CHUNK, ow), lambda bi, s: (bi, chunk(s), 0, 0))
    return pl.pallas_call(
        _dn_scan_kernel,
        grid=(b // rpb, nc),
        in_specs=[spec(a, 0, fwd) for a in prep] + [spec(a, 1, bwd) for a in prep],
        out_specs=[out(fwd), out(bwd)],
        out_shape=[jax.ShapeDtypeStruct((b, nc, CHUNK, ow), BF16)] * 2,
        scratch_shapes=[pltpu.VMEM((rpb, 2, DN_HEADS, dh, dh), F32)],
        compiler_params=_params("parallel", "arbitrary"),
        name="dn_scan",
    )(*prep, *prep)


def _block_diag(w):
    n, i, j = w.shape
    return jnp.einsum('nij,nm->nimj', w, jnp.eye(n, dtype=w.dtype)).reshape(n * i, n * j)


def _dn_gate_rows(raw, n_lat):
    b = raw.shape[0]
    lat = raw[:, :n_lat].reshape(b, n_lat // CHUNK, CHUNK, 4, DN_HEADS)
    ctx = raw[:, n_lat:].reshape(b, -1, CHUNK, 4, DN_HEADS)
    row = jnp.concatenate([lat.transpose(0, 2, 3, 4, 1), ctx.transpose(0, 1, 3, 4, 2)], axis=1)
    return row.reshape(b, row.shape[1], 4, DN_HEADS * CHUNK)


def kernel(x, c, ctx, c_ctx, w_mod, b_mod, norm_g, ffn_w_gu, ffn_w_down, w_in, ml_gate_b, ml_norm_g,
           lru_conv_w, lru_conv_b, lru_w_a, lru_b_a, lru_w_x, lru_b_x, lru_lambda, dn_conv_w,
           dn_a_log, dn_dt_bias, dn_norm_g, w_branch, w_out):
    b, n_lat, d = x.shape
    n_ctx = ctx.shape[1]
    depth = w_mod.shape[0]
    bw = w_branch.shape[2]
    assert n_lat == CHUNK * CHUNK and n_lat % TM == 0 and n_ctx % TM == 0 and n_ctx == CHUNK * DN_GROUP
    n_lat_tiles, n_ctx_tiles = n_lat // TM, n_ctx // TM
    n_tiles = n_lat_tiles + n_ctx_tiles

    ctx_row = b
    n_rows = -(-(b + 1) // V7X_SUBLANES) * V7X_SUBLANES
    cc = jnp.zeros((n_rows, d), F32).at[:b].set(c).at[b].set(c_ctx)
    mod = _mod_table(cc, w_mod, b_mod).reshape(depth, n_rows, N_MOD, d)

    dqk = bw // 2
    edges = [0]
    for wdt in (dqk, dqk, bw, bw, N_GATES, bw, bw, bw, bw, bw, bw, N_GATES, N_BRANCH * d):
        edges.append(edges[-1] + wdt)
    piece = lambda i, j: w_in[:, :, edges[i]:edges[j]]
    widths = (3 * bw, 2 * bw, 3 * bw, bw, N_BRANCH * d)
    w_main = jnp.concatenate([piece(0, 4), piece(5, 7), piece(7, 11), piece(12, 13)], axis=-1).astype(BF16)
    w_gate = jnp.concatenate([piece(4, 5), piece(11, 12)], axis=-1).astype(BF16)
    wgu = ffn_w_gu.astype(BF16)
    wdn = ffn_w_down.astype(BF16)
    wbr = w_branch.astype(BF16)
    wout = w_out.astype(BF16)
    ml_bias = jnp.concatenate([ml_gate_b.reshape(depth, N_GATES), jnp.zeros((depth, N_GATES), F32)], axis=-1)
    ml_bias_r = ml_bias[:, :, None]
    ml_g = ml_norm_g[:, None, :]
    dn_g = jnp.tile(dn_norm_g, (1, DN_HEADS))[:, None, :]
    lru_w = jnp.stack([jnp.concatenate([jax.vmap(_block_diag)(lru_w_a[:, dd]), jax.vmap(_block_diag)(lru_w_x[:, dd])],
                                       axis=-1) for dd in range(2)], axis=1).astype(BF16)
    lru_b = jnp.concatenate([lru_b_a, lru_b_x], axis=-1)[:, :, None, :]
    lru_lam = lru_lambda[:, :, None, :]
    lru_cb = lru_conv_b[:, None, :]
    dn_arow = jnp.repeat(jnp.stack([dn_a_log, dn_dt_bias], axis=2), CHUNK, axis=3)

    rows3 = lambda a: a.reshape(b, n_lat + n_ctx, a.shape[-1])
    for l in range(depth):
        full = _TilePlan(b, n_tiles, n_lat_tiles, n_tiles, ctx_row, l)
        if l == 0:
            xz = _ffn(full, _flat(x), mod, norm_g, wgu, wdn, 0, ctx=_flat(ctx))
        else:
            xz = _ffn(full, xz, mod, norm_g, wgu, wdn, 0)
        ml, lru, dnqkv, dnz, mg, gates = _inproj(full, xz, mod, norm_g, w_main, w_gate, widths, 1, bw)
        gates = rows3(gates)
        ml_hf, ml_hb = _mlstm(rows3(ml), gates, ml_bias_r, l, n_lat_tiles, n_ctx_tiles)
        lr_hf, lr_hb = _lru(rows3(lru), lru_conv_w, lru_cb, lru_w, lru_b, lru_lam, l, n_lat_tiles, n_ctx_tiles)
        n_cols = n_lat // CHUNK
        dq = rows3(dnqkv)
        x_cols = dq[:, :n_lat].reshape(b, n_cols, CHUNK, dq.shape[-1]).swapaxes(1, 2)
        x_rows = dq.reshape(b, -1, CHUNK, dq.shape[-1])
        prep = _dn_prep(x_cols, x_rows, _dn_gate_rows(gates[:, :, N_GATES:], n_lat), dn_conv_w, dn_arow, l)
        dn_all = _dn_scan(prep, n_cols)
        dn_lat = [_flat(o[:, :n_cols].swapaxes(1, 2)) for o in dn_all]
        xz = _merge(full, xz, mod, norm_g, _flat(ml_hf), _flat(ml_hb), ml, _flat(lr_hf), _flat(lr_hb), lru,
                    dn_lat, [_flat(o) for o in dn_all], dnz, mg, ml_g, dn_g, wbr, wout)
        visit = n_lat_tiles if l == depth - 1 else n_tiles
        xz = _ffn(_TilePlan(b, n_tiles, n_lat_tiles, visit, ctx_row, l), xz, mod, norm_g, wgu, wdn, 2)
    return xz.reshape(b, n_lat, d)
```

```python
import functools
import math

import jax
import jax.numpy as jnp
from jax import lax
from jax.experimental import pallas as pl
from jax.experimental.pallas import tpu as pltpu

F32 = jnp.float32
BF16 = jnp.bfloat16

EPS = 1e-6
N_MOD = 9
N_BRANCH = 3
CONV_W = 4
CONV_LEFT = 2
ML_HEADS = 4
LRU_C = 8.0
DN_HEADS = 4
CHUNK = 64
DN_GROUP = 4
DN_SCAN_ROWS = 4
ML_ROWS = 2
N_GATES = 16

V7X_SUBLANES = 8
V7X_LANES = 128
V7X_VMEM_BYTES = 64 * 1024 * 1024
VMEM_LIMIT = V7X_VMEM_BYTES - 8 * 1024 * 1024

TM = 256
NEG = -1e30


def _sigmoid(x):
    return 0.5 * jnp.tanh(0.5 * x) + 0.5


def _silu(x):
    return x * _sigmoid(x)


def _softplus(x):
    return jnp.maximum(x, 0.0) + jnp.log(1.0 + jnp.exp(-jnp.abs(x)))


def _rms(x):
    return x * lax.rsqrt(jnp.mean(x * x, axis=-1, keepdims=True) + EPS)


def _rms_heads(x, n_heads):
    hd = x.shape[-1] // n_heads
    return jnp.concatenate([_rms(x[:, h * hd:(h + 1) * hd]) for h in range(n_heads)], axis=-1)


def _gelu_tanh(x):
    return 0.5 * x * (1.0 + jnp.tanh(math.sqrt(2.0 / math.pi) * (x + 0.044715 * (x * x * x))))


def _dot(a, b):
    return jnp.dot(a, b, preferred_element_type=F32)


def _dot_nt(a, b):
    return lax.dot_general(a, b, (((1,), (1,)), ((), ())), preferred_element_type=F32)


def _dot_tn(a, b):
    return lax.dot_general(a, b, (((0,), (0,)), ((), ())), preferred_element_type=F32)


def _params(*sem):
    return pltpu.CompilerParams(dimension_semantics=sem, vmem_limit_bytes=VMEM_LIMIT)


def _const_spec(block, index):
    return pl.BlockSpec(block, lambda *_: index, pipeline_mode=pl.Buffered(1))


def _stack_heads(x, n):
    w = x.shape[1] // n
    return jnp.concatenate([x[:, h * w:(h + 1) * w] for h in range(n)], axis=0)


def _unstack_heads(x, n):
    t = x.shape[0] // n
    return jnp.concatenate([x[h * t:(h + 1) * t] for h in range(n)], axis=1)


def _cumsum_groups(x, axis, period, rev):
    n = x.shape[axis]
    idx = lax.broadcasted_iota(jnp.int32, x.shape, axis) % period
    sh = 1
    while sh < period:
        if rev:
            x = x + jnp.where(idx < period - sh, pltpu.roll(x, n - sh, axis=axis), 0.0)
        else:
            x = x + jnp.where(idx >= sh, pltpu.roll(x, sh, axis=axis), 0.0)
        sh *= 2
    return x


def _cummax_groups(x, axis, period, rev):
    n = x.shape[axis]
    idx = lax.broadcasted_iota(jnp.int32, x.shape, axis) % period
    sh = 1
    while sh < period:
        if rev:
            x = jnp.maximum(x, jnp.where(idx < period - sh, pltpu.roll(x, n - sh, axis=axis), NEG))
        else:
            x = jnp.maximum(x, jnp.where(idx >= sh, pltpu.roll(x, sh, axis=axis), NEG))
        sh *= 2
    return x


def _shift_rows(xb, prev8, nxt8, off):
    t = xb.shape[0]
    ri = lax.broadcasted_iota(jnp.int32, (t, t), 0)
    ci = lax.broadcasted_iota(jnp.int32, (t, t), 1)
    shifted = _dot(jnp.where(ci == ri + off, 1.0, 0.0).astype(BF16), xb)
    row8 = lax.broadcasted_iota(jnp.int32, (V7X_SUBLANES, 1), 0)
    if off < 0:
        edge = jnp.where(row8 < -off, pltpu.roll(prev8, -off, axis=0), shifted[:V7X_SUBLANES])
        return jnp.concatenate([edge, shifted[V7X_SUBLANES:]], axis=0)
    edge = jnp.where(row8 >= V7X_SUBLANES - off, pltpu.roll(nxt8, V7X_SUBLANES - off, axis=0),
                     shifted[t - V7X_SUBLANES:])
    return jnp.concatenate([shifted[:t - V7X_SUBLANES], edge], axis=0)


def _conv4(xb, prev8, nxt8, w):
    acc = xb.astype(F32) * w[CONV_LEFT:CONV_LEFT + 1]
    for j in range(CONV_W):
        if j != CONV_LEFT:
            acc = acc + _shift_rows(xb, prev8, nxt8, j - CONV_LEFT) * w[j:j + 1]
    return acc


def _mod_kernel(c_ref, w_ref, b_ref, o_ref):
    s = _silu(c_ref[...]).astype(BF16)
    o_ref[...] = _dot(s, w_ref[...].astype(BF16)) + b_ref[...]


def _mod_table(cc, w_mod, b_mod):
    depth, d, nd = w_mod.shape
    r = cc.shape[0]
    tn = nd // 4
    return pl.pallas_call(
        _mod_kernel,
        grid=(depth, nd // tn),
        in_specs=[pl.BlockSpec((r, d), lambda l, j: (0, 0)),
                  pl.BlockSpec((None, d, tn), lambda l, j: (l, 0, j)),
                  pl.BlockSpec((None, 1, tn), lambda l, j: (l, 0, j))],
        out_specs=pl.BlockSpec((None, r, tn), lambda l, j: (l, 0, j)),
        out_shape=jax.ShapeDtypeStruct((depth, r, nd), F32),
        compiler_params=_params("parallel", "parallel"),
        name="mod_table",
    )(cc, w_mod, b_mod.reshape(depth, 1, nd))


class _TilePlan:
    def __init__(self, b, n_tiles, n_lat_tiles, n_vis, ctx_row, layer):
        self.b, self.n_tiles, self.n_lat_tiles, self.n_vis = b, n_tiles, n_lat_tiles, n_vis
        self.ctx_row, self.layer = ctx_row, layer
        self.sub = 2 if (b * n_vis) % 2 == 0 else 1
        self.grid = (b * n_vis // self.sub,)

    def _tile(self, i, k):
        tid = i * self.sub + k
        return tid // self.n_vis, tid % self.n_vis

    def tok(self, k, width, col=0):
        def index(i):
            bi, t = self._tile(i, k)
            return bi * self.n_tiles + t, col
        return pl.BlockSpec((TM, width), index)

    def toks(self, width, col=0):
        return [self.tok(k, width, col) for k in range(self.sub)]

    def is_ctx(self, i, k):
        return self._tile(i, k)[1] >= self.n_lat_tiles

    def tok_ctx_rest(self, k, width):
        def index(i):
            bi, t = self._tile(i, k)
            return bi * self.n_tiles + jnp.maximum(t, self.n_lat_tiles), 0
        return pl.BlockSpec((TM, width), index)

    def tok_split(self, k, width, ctx):
        n_ctx_tiles = self.n_tiles - self.n_lat_tiles

        def index(i):
            bi, t = self._tile(i, k)
            if ctx:
                return bi * n_ctx_tiles + jnp.clip(t - self.n_lat_tiles, 0, n_ctx_tiles - 1), 0
            return bi * self.n_lat_tiles + jnp.minimum(t, self.n_lat_tiles - 1), 0
        return pl.BlockSpec((TM, width), index)

    def mods(self, d):
        def spec(k):
            def index(i):
                bi, t = self._tile(i, k)
                return self.layer, jnp.where(t >= self.n_lat_tiles, self.ctx_row, bi), 0, 0
            return pl.BlockSpec((None, None, N_MOD, d), index)
        return [spec(k) for k in range(self.sub)]

    def out(self, width):
        return pl.BlockSpec((self.sub * TM, width), lambda i: (i, 0))

    def out_shape(self, width, dtype):
        return jax.ShapeDtypeStruct((self.b * self.n_vis * TM, width), dtype)

    def rows(self, k):
        return slice(k * TM, (k + 1) * TM)


def _flat(a):
    return a.reshape(-1, a.shape[-1])


def _ffn_kernel(*refs, j, dff, sub, ctx_tiles):
    if ctx_tiles is not None:
        lat_refs, ctx_refs, refs = refs[:sub], refs[sub:2 * sub], refs[sub:]
        xs = [jnp.where(ctx_tiles(k), c[...], x[...]) for k, (x, c) in enumerate(zip(lat_refs, ctx_refs))]
    else:
        xs = [x[...] for x in refs[:sub]]
    mod_refs = refs[sub:2 * sub]
    g_ref, wgu_ref, wd_ref, o_ref = refs[2 * sub:]
    g_pre, g_post = g_ref[2 * j:2 * j + 1, :], g_ref[2 * j + 1:2 * j + 2, :]
    mods = [[m[3 * j + i:3 * j + i + 1, :] for i in range(3)] for m in mod_refs]
    hs = [(_rms(x) * g_pre * (1.0 + m[1]) + m[0]).astype(BF16) for x, m in zip(xs, mods)]
    gus = [_dot(h, wgu_ref[...]) for h in hs]
    acts = [(_silu(gu[:, :dff]) * gu[:, dff:]).astype(BF16) for gu in gus]
    ys = [_dot(a, wd_ref[...]) for a in acts]
    for k, (x, m, y) in enumerate(zip(xs, mods, ys)):
        o_ref[k * TM:(k + 1) * TM, :] = x + 0.5 * m[2] * (_rms(y) * g_post)


def _ffn(plan, xz, mod, norm_g, wgu, wd, j, ctx=None):
    d = xz.shape[-1]
    dff = wd.shape[2]
    l = plan.layer
    if ctx is None:
        tok_specs, tok_args, ctx_tiles = plan.toks(d), [xz] * plan.sub, None
    else:
        tok_specs = [plan.tok_split(k, d, False) for k in range(plan.sub)] \
                    + [plan.tok_split(k, d, True) for k in range(plan.sub)]
        tok_args = [xz] * plan.sub + [ctx] * plan.sub
        ctx_tiles = lambda k: plan.is_ctx(pl.program_id(0), k)
    return pl.pallas_call(
        functools.partial(_ffn_kernel, j=j, dff=dff, sub=plan.sub, ctx_tiles=ctx_tiles),
        grid=plan.grid,
        in_specs=tok_specs + plan.mods(d)
                 + [_const_spec((None,) + norm_g.shape[1:], (l, 0, 0)),
                    _const_spec((None, None, d, 2 * dff), (l, j // 2, 0, 0)),
                    _const_spec((None, None, dff, d), (l, j // 2, 0, 0))],
        out_specs=plan.out(d),
        out_shape=plan.out_shape(d, F32),
        compiler_params=_params("parallel"),
        name=f"ffn{j}",
    )(*tok_args, *([mod] * plan.sub), norm_g, wgu, wd)


def _inproj_kernel(*refs, sub, layout):
    x_refs, mod_refs = refs[:sub], refs[sub:2 * sub]
    g_ref, w_ref, wg_ref = refs[2 * sub:2 * sub + 3]
    o_refs = refs[2 * sub + 3:]
    hs = [(_rms(x[...]) * g_ref[2:3, :] * (1.0 + m[4:5, :]) + m[3:4, :]).astype(BF16)
          for x, m in zip(x_refs, mod_refs)]
    perm = _time_perm(False)
    col = 0
    for o_ref, segments in zip(o_refs[:-1], layout):
        n = o_ref.shape[-1]
        for k, h in enumerate(hs):
            res = _dot(h, w_ref[:, col:col + n])
            parts, c0 = [], 0
            for width, post in segments:
                seg = res[:, c0:c0 + width]
                if post == "perm":
                    seg = _dot(perm, seg.astype(BF16))
                elif post is not None:
                    seg = post(seg)
                parts.append(seg.astype(o_ref.dtype))
                c0 += width
            o_ref[k * TM:(k + 1) * TM, :] = jnp.concatenate(parts, axis=1)
        col += n
    for k, h in enumerate(hs):
        o_refs[-1][k * TM:(k + 1) * TM, :] = _dot(h, wg_ref[...])


def _inproj(plan, xz, mod, norm_g, w_main, w_gate, layout):
    d = xz.shape[-1]
    ng = w_gate.shape[-1]
    l = plan.layer
    widths = [sum(w for w, _ in segments) for segments in layout]
    return pl.pallas_call(
        functools.partial(_inproj_kernel, sub=plan.sub, layout=layout),
        grid=plan.grid,
        in_specs=plan.toks(d) + plan.mods(d)
                 + [_const_spec((None,) + norm_g.shape[1:], (l, 0, 0)),
                    _const_spec((None, d, w_main.shape[-1]), (l, 0, 0)),
                    _const_spec((None, d, ng), (l, 0, 0))],
        out_specs=[plan.out(w) for w in widths] + [plan.out(ng)],
        out_shape=[plan.out_shape(w, BF16) for w in widths] + [plan.out_shape(ng, F32)],
        compiler_params=_params("parallel"),
        name="inproj",
    )(*([xz] * plan.sub), *([mod] * plan.sub), norm_g, w_main, w_gate)


N_MERGE_STREAMS = 12


def _merge_kernel(*refs, sub, ctx_tiles):
    x_refs, mod_refs = refs[:sub], refs[sub:2 * sub]
    tok = refs[2 * sub:(2 + N_MERGE_STREAMS) * sub]
    g_ref, mlg_ref, dng_ref, wb_ref, wo_ref, o_ref = refs[(2 + N_MERGE_STREAMS) * sub:]
    d = o_ref.shape[-1]
    f32 = lambda r: r[...].astype(F32)
    unperm = _time_perm(True)
    mixes = []
    for k in range(sub):
        mlf, mlb, mlo, lrf, lrb, lry, dlf, dlb, dcf, dcb, dnz, mg = (
            tok[s * sub + k] for s in range(N_MERGE_STREAMS))
        y_ml = _rms_heads(f32(mlf) + f32(mlb), ML_HEADS) * mlg_ref[...] * f32(mlo)
        y_lr = (_dot(unperm, lrf[...]) + _dot(unperm, lrb[...])) * f32(lry)
        dn_h = jnp.where(ctx_tiles(k), f32(dcf) + f32(dcb), f32(dlf) + f32(dlb))
        y_dn = _rms_heads(dn_h, DN_HEADS) * dng_ref[...] * f32(dnz)
        mix = None
        for n, y in enumerate((y_ml, y_lr, y_dn)):
            term = mg[:, n * d:(n + 1) * d].astype(F32) * _dot(y.astype(BF16), wb_ref[n])
            mix = term if mix is None else mix + term
        mixes.append(mix.astype(BF16))
    outs = [_dot(mix, wo_ref[...]) for mix in mixes]
    for k, out in enumerate(outs):
        o_ref[k * TM:(k + 1) * TM, :] = x_refs[k][...] + mod_refs[k][5:6, :] * (_rms(out) * g_ref[3:4, :])


def _merge(plan, xz, mod, norm_g, ml_hf, ml_hb, ml, lr_hf, lr_hb, lru, dn_lat, dn_all, dnz, mg,
           ml_g, dn_g, w_branch, w_out):
    d = xz.shape[-1]
    bw = w_branch.shape[2]
    l = plan.layer
    sub = range(plan.sub)
    whole = lambda a, w, c=0: (a, plan.toks(w, c))
    streams = [whole(ml_hf, bw), whole(ml_hb, bw), whole(ml, bw, ml.shape[-1] // bw - 1),
               whole(lr_hf, bw), whole(lr_hb, bw), whole(lru, bw, lru.shape[-1] // bw - 1),
               (dn_lat[0], [plan.tok_split(k, bw, False) for k in sub]),
               (dn_lat[1], [plan.tok_split(k, bw, False) for k in sub]),
               (dn_all[0], [plan.tok_ctx_rest(k, bw) for k in sub]),
               (dn_all[1], [plan.tok_ctx_rest(k, bw) for k in sub]),
               whole(dnz, bw), whole(mg, N_BRANCH * d)]
    assert len(streams) == N_MERGE_STREAMS
    tok_specs = [sp for _, specs in streams for sp in specs]
    tok_args = [a for a, _ in streams for _ in sub]
    return pl.pallas_call(
        functools.partial(_merge_kernel, sub=plan.sub, ctx_tiles=lambda k: plan.is_ctx(pl.program_id(0), k)),
        grid=plan.grid,
        in_specs=plan.toks(d) + plan.mods(d) + tok_specs
                 + [_const_spec((None,) + norm_g.shape[1:], (l, 0, 0)),
                    _const_spec((None, 1, bw), (l, 0, 0)),
                    _const_spec((None, 1, bw), (l, 0, 0)),
                    _const_spec((None, N_BRANCH, bw, d), (l, 0, 0, 0)),
                    _const_spec((None, d, d), (l, 0, 0))],
        out_specs=plan.out(d),
        out_shape=plan.out_shape(d, F32),
        compiler_params=_params("parallel"),
        name="merge",
    )(*([xz] * plan.sub), *([mod] * plan.sub), *tok_args, norm_g, ml_g, dn_g, w_branch, w_out)


SEG = TM // V7X_SUBLANES


def _time_perm(inverse):
    ri = lax.broadcasted_iota(jnp.int32, (TM, TM), 0)
    ci = lax.broadcasted_iota(jnp.int32, (TM, TM), 1)
    r, t = (ci, ri) if inverse else (ri, ci)
    return jnp.where(t == (r % V7X_SUBLANES) * SEG + r // V7X_SUBLANES, 1.0, 0.0).astype(BF16)


def _conv4_perm(x, before1, before2, after1, w):
    sub = lax.broadcasted_iota(jnp.int32, (V7X_SUBLANES, 1), 0)
    vrow = lambda i: x[V7X_SUBLANES * i:V7X_SUBLANES * (i + 1)]
    m1_edge = jnp.where(sub == 0, before1, pltpu.roll(vrow(SEG - 1), 1, axis=0))
    m2_edge = jnp.where(sub == 0, before2, pltpu.roll(vrow(SEG - 2), 1, axis=0))
    p1_edge = jnp.where(sub == V7X_SUBLANES - 1, after1, pltpu.roll(vrow(0), V7X_SUBLANES - 1, axis=0))
    x_m1 = jnp.concatenate([m1_edge, x[:-V7X_SUBLANES]], axis=0)
    x_m2 = jnp.concatenate([m2_edge, m1_edge, x[:-2 * V7X_SUBLANES]], axis=0)
    x_p1 = jnp.concatenate([x[V7X_SUBLANES:], p1_edge], axis=0)
    taps = {-2: x_m2, -1: x_m1, 0: x, 1: x_p1}
    acc = None
    for j in range(CONV_W):
        term = taps[j - CONV_LEFT] * w[j:j + 1]
        acc = term if acc is None else acc + term
    return acc


def _tile_scan(a, b, h0, rev):
    vrow = lambda x, i: x[V7X_SUBLANES * i:V7X_SUBLANES * (i + 1)]
    h = jnp.zeros_like(vrow(a, 0))
    p = jnp.ones_like(h)
    hs, ps = [None] * SEG, [None] * SEG
    for i in (range(SEG - 1, -1, -1) if rev else range(SEG)):
        ai = vrow(a, i)
        h = ai * h + vrow(b, i)
        p = ai * p
        hs[i], ps[i] = h, p
    carry = h0
    enter = [None] * V7X_SUBLANES
    for s in (range(V7X_SUBLANES - 1, -1, -1) if rev else range(V7X_SUBLANES)):
        enter[s] = carry
        carry = p[s:s + 1] * carry + h[s:s + 1]
    enter = jnp.concatenate(enter, axis=0)
    return jnp.concatenate([hi + pi * enter for hi, pi in zip(hs, ps)], axis=0), carry


def _lru_kernel(x_ref, cw_ref, cb_ref, w_ref, bias_ref, lam_ref, of_ref, ob_ref, *, n_lat, n_ctx):
    t, c = TM, x_ref.shape[-1]
    n_tiles = n_lat + n_ctx
    halo = 2 * V7X_SUBLANES

    def load_conv(tile):
        r0 = pl.multiple_of(tile * t, t)
        first = jnp.logical_or(tile == 0, tile == n_lat)
        last = jnp.logical_or(tile == n_lat - 1, tile == n_tiles - 1)
        x = x_ref[pl.ds(r0, t), :].astype(F32)
        p0 = pl.multiple_of(jnp.maximum(r0 - halo, 0), halo)
        n0 = pl.multiple_of(jnp.minimum(r0 + t, n_tiles * t - halo), halo)
        prev = jnp.where(first, 0.0, x_ref[pl.ds(p0, halo), :].astype(F32))
        nxt = jnp.where(last, 0.0, x_ref[pl.ds(n0, halo), :].astype(F32))
        before1, before2 = prev[halo - 1:halo], prev[V7X_SUBLANES - 1:V7X_SUBLANES]
        return r0, _conv4_perm(x, before1, before2, nxt[0:1], cw_ref[...]) + cb_ref[...]

    def direction(d, tile, h0, o_ref):
        r0, xc = load_conv(tile)
        z = _dot(xc.astype(BF16), w_ref[d]) + bias_ref[d]
        r, i = _sigmoid(z[:, :c]), _sigmoid(z[:, c:])
        la = (-LRU_C * _softplus(-lam_ref[d])) * r
        a = jnp.exp(la)
        bx = jnp.sqrt(jnp.tanh(-la) * (1.0 + a * a)) * (i * xc)
        h, carry = _tile_scan(a, bx, h0, d == 1)
        o_ref[pl.ds(r0, t), :] = h.astype(o_ref.dtype)
        return carry

    def step(s, carry):
        hf, hb = carry
        hf = direction(0, jnp.where(s < n_ctx, n_lat + s, s - n_ctx), hf, of_ref)
        hb = direction(1, n_tiles - 1 - s, hb, ob_ref)
        return hf, hb

    zero = jnp.zeros((1, c), F32)
    lax.fori_loop(0, n_tiles, step, (zero, zero))


def _lru(lru, conv_w, conv_b, w_gates, b_gates, lam, l, n_lat, n_ctx):
    b, lt, _ = lru.shape
    c = conv_w.shape[-1]
    seq = pl.BlockSpec((None, lt, c), lambda bi: (bi, 0, 0))
    return pl.pallas_call(
        functools.partial(_lru_kernel, n_lat=n_lat, n_ctx=n_ctx),
        grid=(b,),
        in_specs=[seq,
                  _const_spec((None, CONV_W, c), (l, 0, 0)),
                  _const_spec((None, 1, c), (l, 0, 0)),
                  _const_spec((None, 2, c, 2 * c), (l, 0, 0, 0)),
                  _const_spec((None, 2, 1, 2 * c), (l, 0, 0, 0)),
                  _const_spec((None, 2, 1, c), (l, 0, 0, 0))],
        out_specs=[seq, seq],
        out_shape=[jax.ShapeDtypeStruct((b, lt, c), BF16)] * 2,
        compiler_params=_params("parallel"),
        name="lru",
    )(lru, conv_w, conv_b, w_gates, b_gates, lam)


def _mlstm_kernel(qf_ref, kf_ref, vf_ref, qb_ref, kb_ref, vb_ref, grf_ref, grb_ref, br_ref,
                  hf_ref, hb_ref, c_ref, m_ref):
    @pl.when(pl.program_id(1) == 0)
    def _():
        c_ref[...] = jnp.zeros(c_ref.shape, F32)
        m_ref[...] = jnp.zeros(m_ref.shape, F32)

    nrow, tc, nqk = qf_ref.shape
    dk = nqk // ML_HEADS
    dv = vf_ref.shape[-1] // ML_HEADS
    nh = ML_HEADS
    ri = lax.broadcasted_iota(jnp.int32, (tc, tc), 0)
    ci = lax.broadcasted_iota(jnp.int32, (tc, tc), 1)
    causal = (ci <= ri, ci >= ri)
    lane_head = lax.broadcasted_iota(jnp.int32, (1, nqk), 1) // dk
    ones = jnp.ones((tc, dv), BF16)
    dirs = ((qf_ref, kf_ref, vf_ref, grf_ref), (qb_ref, kb_ref, vb_ref, grb_ref))
    seqs = [(r, d) for r in range(nrow) for d in range(2)]

    rows, v_rows, ws_rows, decs = [], [], [], []
    for r, d in seqs:
        rev = d == 1
        gr = dirs[d][3][r].T + br_ref[...]
        i_r = gr[nh * d:nh * (d + 1)]
        b_r = _cumsum_groups(-_softplus(-gr[nh * (2 + d):nh * (3 + d)]), 1, tc, rev)
        m_prev = m_ref[r, nh * d:nh * (d + 1), 0:1]
        m_t = b_r + jnp.maximum(m_prev, _cummax_groups(i_r - b_r, 1, tc, rev))
        b_end = b_r[:, 0:1] if rev else b_r[:, tc - 1:tc]
        lws = b_end - b_r + i_r
        m_new = jnp.maximum(b_end + m_prev, jnp.max(lws, axis=1, keepdims=True))
        rows += [b_r - m_t, jnp.exp(b_r + m_prev - m_t), jnp.exp(-m_t)]
        v_rows.append(b_r - i_r)
        ws_rows.append(jnp.exp(lws - m_new))
        decs.append(jnp.exp(b_end + m_prev - m_new))
        m_ref[r, nh * d:nh * (d + 1), :] = jnp.broadcast_to(m_new, (nh, m_ref.shape[-1]))
    cols = jnp.concatenate(rows, axis=0).T

    chains = [(s, h) for s in range(len(seqs)) for h in range(nh)]
    q_all = [dirs[d][0][r] * (dk ** -0.5) for r, d in seqs]
    k_all = [dirs[d][1][r] for r, d in seqs]
    v_all = [dirs[d][2][r] for r, d in seqs]
    kt_all = [k.astype(F32).T for k in k_all]
    c_all = [c_ref[r, d] for r, d in seqs]
    cb_all = [c.astype(BF16) for c in c_all]
    col = lambda s, j, h: cols[:, (3 * s + j) * nh + h:(3 * s + j) * nh + h + 1]
    qh = [jnp.where(lane_head == h, q_all[s], jnp.zeros_like(q_all[s])) for s, h in chains]
    vp = [jnp.concatenate([v_all[s][:, h * dv:(h + 1) * dv], ones], axis=1) for s, h in chains]
    s_raw = [_dot_nt(q, k_all[s]) for q, (s, h) in zip(qh, chains)]
    p = [(sr * jnp.exp(jnp.where(causal[seqs[s][1]], col(s, 0, h) - v_rows[s][h:h + 1], NEG))).astype(BF16)
         for sr, (s, h) in zip(s_raw, chains)]
    num = [_dot(pc, vc) + col(s, 1, h) * _dot(q, cb_all[s]) for pc, vc, q, (s, h) in zip(p, vp, qh, chains)]
    outs = [nm[:, :dv] / jnp.maximum(jnp.abs(nm[:, dv:]), col(s, 2, h)) for nm, (s, h) in zip(num, chains)]
    for vc, (s, h) in zip(vp, chains):
        r, d = seqs[s]
        kw = (kt_all[s][h * dk:(h + 1) * dk] * ws_rows[s][h:h + 1]).astype(BF16)
        c_ref[r, d, h * dk:(h + 1) * dk, :] = decs[s][h:h + 1] * c_all[s][h * dk:(h + 1) * dk] + _dot(kw, vc)
    for s, (r, d) in enumerate(seqs):
        o_ref = (hf_ref, hb_ref)[d]
        o_ref[r] = jnp.concatenate(outs[s * nh:(s + 1) * nh], axis=1).astype(o_ref.dtype)


def _mlstm(ml, gates, bias_r, l, n_lat, n_ctx):
    b, lt, _ = ml.shape
    ng = gates.shape[-1]
    n_tiles = n_lat + n_ctx
    dqk = ml.shape[-1] // 6
    fwd = lambda s: jnp.where(s < n_ctx, n_lat + s, s - n_ctx)
    bwd = lambda s: n_tiles - 1 - s
    rpb = ML_ROWS if b % ML_ROWS == 0 else 1

    def specs(tile):
        return [pl.BlockSpec((rpb, TM, dqk), lambda bi, s: (bi, tile(s), 0)),
                pl.BlockSpec((rpb, TM, dqk), lambda bi, s: (bi, tile(s), 1)),
                pl.BlockSpec((rpb, TM, 2 * dqk), lambda bi, s: (bi, tile(s), 1))]

    row = lambda tile: pl.BlockSpec((rpb, TM, ng), lambda bi, s: (bi, tile(s), 0))
    out = lambda tile: pl.BlockSpec((rpb, TM, 2 * dqk), lambda bi, s: (bi, tile(s), 0))
    return pl.pallas_call(
        _mlstm_kernel,
        grid=(b // rpb, n_tiles),
        in_specs=specs(fwd) + specs(bwd) + [row(fwd), row(bwd), _const_spec((None, ng, 1), (l, 0, 0))],
        out_specs=[out(fwd), out(bwd)],
        out_shape=[jax.ShapeDtypeStruct((b, lt, 2 * dqk), BF16)] * 2,
        scratch_shapes=[pltpu.VMEM((rpb, 2, dqk, 2 * (2 * dqk // ML_HEADS)), F32),
                        pltpu.VMEM((rpb, 2 * ML_HEADS, V7X_LANES), F32)],
        compiler_params=_params("parallel", "arbitrary"),
        name="mlstm",
    )(ml, ml, ml, ml, ml, ml, gates, gates, bias_r)


INV_BASE = 8


def _inv_masks(ri, ci):
    blk = lambda s: (ri // s) == (ci // s)
    as_bf = lambda m: jnp.where(m, 1.0, 0.0).astype(BF16)
    masks, s = [as_bf(blk(INV_BASE))], INV_BASE
    while s < CHUNK:
        masks.append(as_bf(jnp.logical_and(blk(2 * s), jnp.logical_not(blk(s)))))
        s *= 2
    return masks


def _tri_inv_many(a_list, eye, same, masks):
    nblk = a_list[0].shape[0] // CHUNK
    compact = lambda m: functools.reduce(lambda u, v: u + v, [m[i * CHUNK:(i + 1) * CHUNK] for i in range(nblk)])
    spread = lambda c: jnp.concatenate([c] * nblk, axis=0) * same
    eye_c = compact(eye)
    ps = [a * masks[0] for a in a_list]
    pcs = [compact(p) for p in ps]
    xcs = [eye_c - pc for pc in pcs]
    for _ in range(INV_BASE.bit_length() - 2):
        pcs = [_dot(pc, p).astype(BF16) for pc, p in zip(pcs, ps)]
        ps = [spread(pc) for pc in pcs]
        xcs = [_dot(xc, eye + p).astype(BF16) for xc, p in zip(xcs, ps)]
    for m in masks[1:]:
        ys = [_dot(xc, a * m).astype(BF16) for xc, a in zip(xcs, a_list)]
        xcs = [xc - _dot(y, spread(xc)).astype(BF16) for y, xc in zip(ys, xcs)]
    return [spread(xc) for xc in xcs]


def _dn_prep_kernel(xl_ref, xp_ref, xn_ref, xc_ref, grow_ref, cw_ref, arow_ref,
                    u_ref, w_ref, qe_ref, kd_ref, qk_ref, eg_ref, *, n_lat_groups):
    g = pl.program_id(1)
    is_ctx = g >= n_lat_groups
    width = xl_ref.shape[-1]
    dh = width // (3 * DN_HEADS)
    n = DN_HEADS * CHUNK
    halo = 2 * V7X_SUBLANES
    x = jnp.where(is_ctx, xc_ref[...], xl_ref[...]).reshape(DN_GROUP * CHUNK, width)
    no_prev = jnp.logical_or(is_ctx, g == 0)
    no_next = jnp.logical_or(is_ctx, g == n_lat_groups - 1)
    prev8 = jnp.where(no_prev, 0.0, xp_ref[CHUNK - halo:, :].astype(F32)[V7X_SUBLANES:])
    nxt8 = jnp.where(no_next, 0.0, xn_ref[:halo, :].astype(F32)[:V7X_SUBLANES])
    xc = _silu(_conv4(x, prev8, nxt8, cw_ref[...]))

    ri = lax.broadcasted_iota(jnp.int32, (n, n), 0)
    ci = lax.broadcasted_iota(jnp.int32, (n, n), 1)
    same = (ri // CHUNK) == (ci // CHUNK)
    eye = jnp.where(ri == ci, 1.0, 0.0).astype(BF16)
    incl =(jnp.logical_and(same, ci <= ri), jnp.logical_and(same, ci >= ri))
    strict = (jnp.logical_and(same, ci < ri), jnp.logical_and(same, ci > ri))
    masks = _inv_masks(ri, ci)

    qn, kn, v, kk, qk = [], [], [], [], []
    for i in range(DN_GROUP):
        xi = xc[i * CHUNK:(i + 1) * CHUNK]
        q, k, vi = (_stack_heads(xi[:, j * DN_HEADS * dh:(j + 1) * DN_HEADS * dh], DN_HEADS) for j in range(3))
        qn.append(q * lax.rsqrt(jnp.sum(q * q, axis=-1, keepdims=True) + EPS) * (dh ** -0.5))
        kn.append(k * lax.rsqrt(jnp.sum(k * k, axis=-1, keepdims=True) + EPS))
        v.append(vi)
        kb = kn[i].astype(BF16)
        kk.append(_dot_nt(kb, kb))
        qk.append(_dot_nt(qn[i].astype(BF16), kb))

    systems = [(i, d) for i in range(DN_GROUP) for d in range(2)]
    rows, cs_rows = [], []
    for i, d in systems:
        raw = grow_ref[i]
        beta = _sigmoid(raw[d:d + 1])
        g_r = -jnp.exp(arow_ref[d, 0:1, :]) * _softplus(raw[2 + d:3 + d] + arow_ref[d, 1:2, :])
        cs = _cumsum_groups(g_r, 1, CHUNK, d == 1)
        tot = cs + _cumsum_groups(g_r, 1, CHUNK, d == 0) - g_r
        eg = jnp.exp(cs)
        rows += [cs, beta, eg, jnp.exp(tot - cs), beta * eg]
        cs_rows.append(cs)
        eg_ref[d, i] = jnp.exp(tot)
    n_col = len(rows) // len(systems)
    cols = jnp.concatenate(rows, axis=0).T

    a_list, rhs = [], []
    for s, (i, d) in enumerate(systems):
        cs_c, beta_c, eg_c, ekd_c, beg_c = (cols[:, n_col * s + j:n_col * s + j + 1] for j in range(n_col))
        gam = jnp.exp(jnp.where(incl[d], cs_c - cs_rows[s], NEG))
        a_list.append((jnp.where(strict[d], kk[i], 0.0) * gam * beta_c).astype(BF16))
        rhs.append(jnp.concatenate([beta_c * v[i], beg_c * kn[i]], axis=1).astype(BF16))
        qk_ref[d, i] = (qk[i] * gam).astype(qk_ref.dtype)
        qe_ref[d, i] = (qn[i] * eg_c).astype(qe_ref.dtype)
        kd_ref[d, i] = (kn[i] * ekd_c).astype(kd_ref.dtype)
    same_bf = jnp.where(same, 1.0, 0.0).astype(BF16)
    for (i, d), tinv, r in zip(systems, _tri_inv_many(a_list, eye, same_bf, masks), rhs):
        sol = _dot(tinv, r)
        u_ref[d, i] = sol[:, :dh].astype(u_ref.dtype)
        w_ref[d, i] = sol[:, dh:].astype(w_ref.dtype)


def _dn_prep(x_cols, x_rows, grow, conv_w, arow, l):
    b, nc, _, width = x_rows.shape
    n_cols = x_cols.shape[1]
    n_groups = nc // DN_GROUP
    n_lat_groups = n_cols // DN_GROUP
    assert n_groups == n_lat_groups + 1
    n = DN_HEADS * CHUNK
    dh = width // (3 * DN_HEADS)
    lat_g = lambda g: jnp.minimum(g, n_lat_groups - 1)
    out = lambda w, dt: (jax.ShapeDtypeStruct((b, 2, nc, n, w), dt),
                         pl.BlockSpec((None, 2, DN_GROUP, n, w), lambda bi, g: (bi, 0, g, 0, 0)))
    outs = [out(dh, BF16)] * 4 + [out(n, BF16)]
    outs.append((jax.ShapeDtypeStruct((b, 2, nc, 1, n), F32),
                 pl.BlockSpec((None, 2, DN_GROUP, 1, n), lambda bi, g: (bi, 0, g, 0, 0))))
    chunk = lambda idx: pl.BlockSpec((None, None, CHUNK, width), lambda bi, g: (bi, idx(g), 0, 0))
    return pl.pallas_call(
        functools.partial(_dn_prep_kernel, n_lat_groups=n_lat_groups),
        grid=(b, n_groups),
        in_specs=[pl.BlockSpec((None, DN_GROUP, CHUNK, width), lambda bi, g: (bi, lat_g(g), 0, 0)),
                  chunk(lambda g: jnp.maximum(lat_g(g) * DN_GROUP - 1, 0)),
                  chunk(lambda g: jnp.minimum(lat_g(g) * DN_GROUP + DN_GROUP, n_cols - 1)),
                  pl.BlockSpec((None, DN_GROUP, CHUNK, width), lambda bi, g: (bi, n_groups - 1, 0, 0)),
                  pl.BlockSpec((None, DN_GROUP, 4, n), lambda bi, g: (bi, g, 0, 0)),
                  _const_spec((None, CONV_W, width), (l, 0, 0)),
                  _const_spec((None, 2, 2, n), (l, 0, 0, 0))],
        out_specs=[o[1] for o in outs],
        out_shape=[o[0] for o in outs],
        compiler_params=_params("parallel", "parallel"),
        name="dn_prep",
    )(x_cols, x_cols, x_cols, x_rows, grow, conv_w, arow)


def _dn_scan_kernel(*refs):
    ins, (of_ref, ob_ref, s_ref) = refs[:12], refs[12:]

    @pl.when(pl.program_id(1) == 0)
    def _():
        s_ref[...] = jnp.zeros(s_ref.shape, F32)

    rows = [slice(h * CHUNK, (h + 1) * CHUNK) for h in range(DN_HEADS)]
    seqs = [(r, d) for r in range(of_ref.shape[0]) for d in range(2)]
    get = lambda j, r, d: ins[6 * d + j][r]
    sb = [[s_ref[r, d, h].astype(BF16) for h in range(DN_HEADS)] for r, d in seqs]
    ws = [jnp.concatenate([_dot(get(1, r, d)[rows[h]], sb[i][h]) for h in range(DN_HEADS)], axis=0)
          for i, (r, d) in enumerate(seqs)]
    qs = [jnp.concatenate([_dot(get(2, r, d)[rows[h]], sb[i][h]) for h in range(DN_HEADS)], axis=0)
          for i, (r, d) in enumerate(seqs)]
    vnew = [(get(0, r, d).astype(F32) - w).astype(BF16) for w, (r, d) in zip(ws, seqs)]
    o = [q + _dot(get(4, r, d), vn) for q, vn, (r, d) in zip(qs, vnew, seqs)]
    for vn, (r, d) in zip(vnew, seqs):
        kd, eg = get(3, r, d), get(5, r, d)
        for h in range(DN_HEADS):
            s_ref[r, d, h] = (eg[:, h * CHUNK:h * CHUNK + 1] * s_ref[r, d, h]
                              + _dot_tn(kd[rows[h]], vn[rows[h]]))
    for oi, (r, d) in zip(o, seqs):
        o_ref = (of_ref, ob_ref)[d]
        o_ref[r] = _unstack_heads(oi, DN_HEADS).astype(o_ref.dtype)


def _dn_scan(prep, n_lat_chunks):
    b, _, nc, _, dh = prep[0].shape
    ow = DN_HEADS * dh
    n_ctx_chunks = nc - n_lat_chunks
    fwd = lambda s: jnp.where(s < n_ctx_chunks, n_lat_chunks + s, s - n_ctx_chunks)
    bwd = lambda s: nc - 1 - s

    rpb = DN_SCAN_ROWS if b % DN_SCAN_ROWS == 0 else 1

    def spec(a, d, chunk):
        return pl.BlockSpec((rpb, None, None) + a.shape[3:], lambda bi, s: (bi, d, chunk(s), 0, 0))

    out = lambda chunk: pl.BlockSpec((rpb, None, CHUNK, ow), lambda bi, s: (bi, chunk(s), 0, 0))
    return pl.pallas_call(
        _dn_scan_kernel,
        grid=(b // rpb, nc),
        in_specs=[spec(a, 0, fwd) for a in prep] + [spec(a, 1, bwd) for a in prep],
        out_specs=[out(fwd), out(bwd)],
        out_shape=[jax.ShapeDtypeStruct((b, nc, CHUNK, ow), BF16)] * 2,
        scratch_shapes=[pltpu.VMEM((rpb, 2, DN_HEADS, dh, dh), F32)],
        compiler_params=_params("parallel", "arbitrary"),
        name="dn_scan",
    )(*prep, *prep)


def _block_diag(w):
    n, i, j = w.shape
    return jnp.einsum('nij,nm->nimj', w, jnp.eye(n, dtype=w.dtype)).reshape(n * i, n * j)


def _dn_gate_rows(raw, n_lat):
    b = raw.shape[0]
    lat = raw[:, :n_lat].reshape(b, n_lat // CHUNK, CHUNK, 4, DN_HEADS)
    ctx = raw[:, n_lat:].reshape(b, -1, CHUNK, 4, DN_HEADS)
    row = jnp.concatenate([lat.transpose(0, 2, 3, 4, 1), ctx.transpose(0, 1, 3, 4, 2)], axis=1)
    return row.reshape(b, row.shape[1], 4, DN_HEADS * CHUNK)


def kernel(x, c, ctx, c_ctx, w_mod, b_mod, norm_g, ffn_w_gu, ffn_w_down, w_in, ml_gate_b, ml_norm_g,
           lru_conv_w, lru_conv_b, lru_w_a, lru_b_a, lru_w_x, lru_b_x, lru_lambda, dn_conv_w,
           dn_a_log, dn_dt_bias, dn_norm_g, w_branch, w_out):
    b, n_lat, d = x.shape
    n_ctx = ctx.shape[1]
    depth = w_mod.shape[0]
    bw = w_branch.shape[2]
    assert n_lat == CHUNK * CHUNK and n_lat % TM == 0 and n_ctx % TM == 0 and n_ctx == CHUNK * DN_GROUP
    n_lat_tiles, n_ctx_tiles = n_lat // TM, n_ctx // TM
    n_tiles = n_lat_tiles + n_ctx_tiles

    ctx_row = b
    n_rows = -(-(b + 1) // V7X_SUBLANES) * V7X_SUBLANES
    cc = jnp.zeros((n_rows, d), F32).at[:b].set(c).at[b].set(c_ctx)
    mod = _mod_table(cc, w_mod, b_mod).reshape(depth, n_rows, N_MOD, d)

    dqk = bw // 2
    edges = [0]
    for wdt in (dqk, dqk, bw, bw, N_GATES, bw, bw, bw, bw, bw, bw, N_GATES, N_BRANCH * d):
        edges.append(edges[-1] + wdt)
    piece = lambda i, j: w_in[:, :, edges[i]:edges[j]]
    layout = (((2 * bw, None), (bw, _sigmoid)),
              ((bw, "perm"), (bw, _gelu_tanh)),
              ((3 * bw, None),),
              ((bw, _silu),),
              ((N_BRANCH * d, _sigmoid),))
    w_main = jnp.concatenate([piece(0, 4), piece(5, 7), piece(7, 11), piece(12, 13)], axis=-1).astype(BF16)
    w_gate = jnp.concatenate([piece(4, 5), piece(11, 12)], axis=-1).astype(BF16)
    wgu = ffn_w_gu.astype(BF16)
    wdn = ffn_w_down.astype(BF16)
    wbr = w_branch.astype(BF16)
    wout = w_out.astype(BF16)
    ml_bias = jnp.concatenate([ml_gate_b.reshape(depth, N_GATES), jnp.zeros((depth, N_GATES), F32)], axis=-1)
    ml_bias_r = ml_bias[:, :, None]
    ml_g = ml_norm_g[:, None, :]
    dn_g = jnp.tile(dn_norm_g, (1, DN_HEADS))[:, None, :]
    lru_w = jnp.stack([jnp.concatenate([jax.vmap(_block_diag)(lru_w_a[:, dd]), jax.vmap(_block_diag)(lru_w_x[:, dd])],
                                       axis=-1) for dd in range(2)], axis=1).astype(BF16)
    lru_b = jnp.concatenate([lru_b_a, lru_b_x], axis=-1)[:, :, None, :]
    lru_lam = lru_lambda[:, :, None, :]
    lru_cb = lru_conv_b[:, None, :]
    dn_arow = jnp.repeat(jnp.stack([dn_a_log, dn_dt_bias], axis=2), CHUNK, axis=3)

    rows3 = lambda a: a.reshape(b, n_lat + n_ctx, a.shape[-1])
    for l in range(depth):
        full = _TilePlan(b, n_tiles, n_lat_tiles, n_tiles, ctx_row, l)
        if l == 0:
            xz = _ffn(full, _flat(x), mod, norm_g, wgu, wdn, 0, ctx=_flat(ctx))
        else:
            xz = _ffn(full, xz, mod, norm_g, wgu, wdn, 0)
        ml, lru, dnqkv, dnz, mg, gates = _inproj(full, xz, mod, norm_g, w_main, w_gate, layout)
        gates = rows3(gates)
        ml_hf, ml_hb = _mlstm(rows3(ml), gates, ml_bias_r, l, n_lat_tiles, n_ctx_tiles)
        lr_hf, lr_hb = _lru(rows3(lru), lru_conv_w, lru_cb, lru_w, lru_b, lru_lam, l, n_lat_tiles, n_ctx_tiles)
        n_cols = n_lat // CHUNK
        dq = rows3(dnqkv)
        x_cols = dq[:, :n_lat].reshape(b, n_cols, CHUNK, dq.shape[-1]).swapaxes(1, 2)
        x_rows = dq.reshape(b, -1, CHUNK, dq.shape[-1])
        prep = _dn_prep(x_cols, x_rows, _dn_gate_rows(gates[:, :, N_GATES:], n_lat), dn_conv_w, dn_arow, l)
        dn_all = _dn_scan(prep, n_cols)
        dn_lat = [_flat(o[:, :n_cols].swapaxes(1, 2)) for o in dn_all]
        xz = _merge(full, xz, mod, norm_g, _flat(ml_hf), _flat(ml_hb), ml, _flat(lr_hf), _flat(lr_hb), lru,
                    dn_lat, [_flat(o) for o in dn_all], dnz, mg, ml_g, dn_g, wbr, wout)
        visit = n_lat_tiles if l == depth - 1 else n_tiles
        xz = _ffn(_TilePlan(b, n_tiles, n_lat_tiles, visit, ctx_row, l), xz, mod, norm_g, wgu, wdn, 2)
    return xz.reshape(b, n_lat, d)
```

```python
import functools
import math

import jax
import jax.numpy as jnp
from jax import lax
from jax.experimental import pallas as pl
from jax.experimental.pallas import tpu as pltpu

F32 = jnp.float32
BF16 = jnp.bfloat16

EPS = 1e-6
N_MOD = 9
N_BRANCH = 3
CONV_W = 4
CONV_LEFT = 2
ML_HEADS = 4
LRU_C = 8.0
DN_HEADS = 4
CHUNK = 64
DN_GROUP = 4
DN_SCAN_ROWS = 4
ML_ROWS = 2
N_GATES = 16

V7X_SUBLANES = 8
V7X_LANES = 128
V7X_VMEM_BYTES = 64 * 1024 * 1024
VMEM_LIMIT = V7X_VMEM_BYTES - 8 * 1024 * 1024

TM = 256
NEG = -1e30


def _sigmoid(x):
    return 0.5 * jnp.tanh(0.5 * x) + 0.5


def _silu(x):
    return x * _sigmoid(x)


def _softplus(x):
    return jnp.maximum(x, 0.0) + jnp.log(1.0 + jnp.exp(-jnp.abs(x)))


def _rms(x):
    return x * lax.rsqrt(jnp.mean(x * x, axis=-1, keepdims=True) + EPS)


def _rms_heads(x, n_heads):
    hd = x.shape[-1] // n_heads
    return jnp.concatenate([_rms(x[:, h * hd:(h + 1) * hd]) for h in range(n_heads)], axis=-1)


def _gelu_tanh(x):
    return 0.5 * x * (1.0 + jnp.tanh(math.sqrt(2.0 / math.pi) * (x + 0.044715 * (x * x * x))))


def _dot(a, b):
    return jnp.dot(a, b, preferred_element_type=F32)


def _dot_nt(a, b):
    return lax.dot_general(a, b, (((1,), (1,)), ((), ())), preferred_element_type=F32)


def _dot_tn(a, b):
    return lax.dot_general(a, b, (((0,), (0,)), ((), ())), preferred_element_type=F32)


def _params(*sem):
    return pltpu.CompilerParams(dimension_semantics=sem, vmem_limit_bytes=VMEM_LIMIT)


def _const_spec(block, index):
    return pl.BlockSpec(block, lambda *_: index, pipeline_mode=pl.Buffered(1))


def _stack_heads(x, n):
    w = x.shape[1] // n
    return jnp.concatenate([x[:, h * w:(h + 1) * w] for h in range(n)], axis=0)


def _unstack_heads(x, n):
    t = x.shape[0] // n
    return jnp.concatenate([x[h * t:(h + 1) * t] for h in range(n)], axis=1)


def _cumsum_groups(x, axis, period, rev):
    n = x.shape[axis]
    idx = lax.broadcasted_iota(jnp.int32, x.shape, axis) % period
    sh = 1
    while sh < period:
        if rev:
            x = x + jnp.where(idx < period - sh, pltpu.roll(x, n - sh, axis=axis), 0.0)
        else:
            x = x + jnp.where(idx >= sh, pltpu.roll(x, sh, axis=axis), 0.0)
        sh *= 2
    return x


def _cummax_groups(x, axis, period, rev):
    n = x.shape[axis]
    idx = lax.broadcasted_iota(jnp.int32, x.shape, axis) % period
    sh = 1
    while sh < period:
        if rev:
            x = jnp.maximum(x, jnp.where(idx < period - sh, pltpu.roll(x, n - sh, axis=axis), NEG))
        else:
            x = jnp.maximum(x, jnp.where(idx >= sh, pltpu.roll(x, sh, axis=axis), NEG))
        sh *= 2
    return x


def _shift_rows(xb, prev8, nxt8, off):
    t = xb.shape[0]
    ri = lax.broadcasted_iota(jnp.int32, (t, t), 0)
    ci = lax.broadcasted_iota(jnp.int32, (t, t), 1)
    shifted = _dot(jnp.where(ci == ri + off, 1.0, 0.0).astype(BF16), xb)
    row8 = lax.broadcasted_iota(jnp.int32, (V7X_SUBLANES, 1), 0)
    if off < 0:
        edge = jnp.where(row8 < -off, pltpu.roll(prev8, -off, axis=0), shifted[:V7X_SUBLANES])
        return jnp.concatenate([edge, shifted[V7X_SUBLANES:]], axis=0)
    edge = jnp.where(row8 >= V7X_SUBLANES - off, pltpu.roll(nxt8, V7X_SUBLANES - off, axis=0),
                     shifted[t - V7X_SUBLANES:])
    return jnp.concatenate([shifted[:t - V7X_SUBLANES], edge], axis=0)


def _conv4(xb, prev8, nxt8, w):
    acc = xb.astype(F32) * w[CONV_LEFT:CONV_LEFT + 1]
    for j in range(CONV_W):
        if j != CONV_LEFT:
            acc = acc + _shift_rows(xb, prev8, nxt8, j - CONV_LEFT) * w[j:j + 1]
    return acc


def _mod_kernel(c_ref, w_ref, b_ref, o_ref):
    s = _silu(c_ref[...]).astype(BF16)
    o_ref[...] = _dot(s, w_ref[...].astype(BF16)) + b_ref[...]


def _mod_table(cc, w_mod, b_mod):
    depth, d, nd = w_mod.shape
    r = cc.shape[0]
    tn = nd // 4
    return pl.pallas_call(
        _mod_kernel,
        grid=(depth, nd // tn),
        in_specs=[pl.BlockSpec((r, d), lambda l, j: (0, 0)),
                  pl.BlockSpec((None, d, tn), lambda l, j: (l, 0, j)),
                  pl.BlockSpec((None, 1, tn), lambda l, j: (l, 0, j))],
        out_specs=pl.BlockSpec((None, r, tn), lambda l, j: (l, 0, j)),
        out_shape=jax.ShapeDtypeStruct((depth, r, nd), F32),
        compiler_params=_params("parallel", "parallel"),
        name="mod_table",
    )(cc, w_mod, b_mod.reshape(depth, 1, nd))


class _TilePlan:
    def __init__(self, b, n_tiles, n_lat_tiles, n_vis, ctx_row, layer):
        self.b, self.n_tiles, self.n_lat_tiles, self.n_vis = b, n_tiles, n_lat_tiles, n_vis
        self.ctx_row, self.layer = ctx_row, layer
        self.sub = 2 if (b * n_vis) % 2 == 0 else 1
        self.grid = (b * n_vis // self.sub,)

    def _tile(self, i, k):
        tid = i * self.sub + k
        return tid // self.n_vis, tid % self.n_vis

    def tok(self, k, width, col=0):
        def index(i):
            bi, t = self._tile(i, k)
            return bi * self.n_tiles + t, col
        return pl.BlockSpec((TM, width), index)

    def toks(self, width, col=0):
        return [self.tok(k, width, col) for k in range(self.sub)]

    def is_ctx(self, i, k):
        return self._tile(i, k)[1] >= self.n_lat_tiles

    def tok_ctx_rest(self, k, width):
        def index(i):
            bi, t = self._tile(i, k)
            return bi * self.n_tiles + jnp.maximum(t, self.n_lat_tiles), 0
        return pl.BlockSpec((TM, width), index)

    def tok_split(self, k, width, ctx):
        n_ctx_tiles = self.n_tiles - self.n_lat_tiles

        def index(i):
            bi, t = self._tile(i, k)
            if ctx:
                return bi * n_ctx_tiles + jnp.clip(t - self.n_lat_tiles, 0, n_ctx_tiles - 1), 0
            return bi * self.n_lat_tiles + jnp.minimum(t, self.n_lat_tiles - 1), 0
        return pl.BlockSpec((TM, width), index)

    def mods(self, d):
        def spec(k):
            def index(i):
                bi, t = self._tile(i, k)
                return self.layer, jnp.where(t >= self.n_lat_tiles, self.ctx_row, bi), 0, 0
            return pl.BlockSpec((None, None, N_MOD, d), index)
        return [spec(k) for k in range(self.sub)]

    def out(self, width):
        return pl.BlockSpec((self.sub * TM, width), lambda i: (i, 0))

    def out_shape(self, width, dtype):
        return jax.ShapeDtypeStruct((self.b * self.n_vis * TM, width), dtype)

    def rows(self, k):
        return slice(k * TM, (k + 1) * TM)


def _flat(a):
    return a.reshape(-1, a.shape[-1])


def _ffn_kernel(*refs, j, dff, sub, ctx_tiles):
    if ctx_tiles is not None:
        lat_refs, ctx_refs, refs = refs[:sub], refs[sub:2 * sub], refs[sub:]
        xs = [jnp.where(ctx_tiles(k), c[...], x[...]) for k, (x, c) in enumerate(zip(lat_refs, ctx_refs))]
    else:
        xs = [x[...] for x in refs[:sub]]
    mod_refs = refs[sub:2 * sub]
    g_ref, wgu_ref, wd_ref, o_ref = refs[2 * sub:]
    g_pre, g_post = g_ref[2 * j:2 * j + 1, :], g_ref[2 * j + 1:2 * j + 2, :]
    mods = [[m[3 * j + i:3 * j + i + 1, :] for i in range(3)] for m in mod_refs]
    hs = [(_rms(x) * g_pre * (1.0 + m[1]) + m[0]).astype(BF16) for x, m in zip(xs, mods)]
    gus = [_dot(h, wgu_ref[...]) for h in hs]
    acts = [(_silu(gu[:, :dff]) * gu[:, dff:]).astype(BF16) for gu in gus]
    ys = [_dot(a, wd_ref[...]) for a in acts]
    for k, (x, m, y) in enumerate(zip(xs, mods, ys)):
        o_ref[k * TM:(k + 1) * TM, :] = x + 0.5 * m[2] * (_rms(y) * g_post)


def _ffn(plan, xz, mod, norm_g, wgu, wd, j, ctx=None):
    d = xz.shape[-1]
    dff = wd.shape[2]
    l = plan.layer
    if ctx is None:
        tok_specs, tok_args, ctx_tiles = plan.toks(d), [xz] * plan.sub, None
    else:
        tok_specs = [plan.tok_split(k, d, False) for k in range(plan.sub)] \
                    + [plan.tok_split(k, d, True) for k in range(plan.sub)]
        tok_args = [xz] * plan.sub + [ctx] * plan.sub
        ctx_tiles = lambda k: plan.is_ctx(pl.program_id(0), k)
    return pl.pallas_call(
        functools.partial(_ffn_kernel, j=j, dff=dff, sub=plan.sub, ctx_tiles=ctx_tiles),
        grid=plan.grid,
        in_specs=tok_specs + plan.mods(d)
                 + [_const_spec((None,) + norm_g.shape[1:], (l, 0, 0)),
                    _const_spec((None, None, d, 2 * dff), (l, j // 2, 0, 0)),
                    _const_spec((None, None, dff, d), (l, j // 2, 0, 0))],
        out_specs=plan.out(d),
        out_shape=plan.out_shape(d, F32),
        compiler_params=_params("parallel"),
        name=f"ffn{j}",
    )(*tok_args, *([mod] * plan.sub), norm_g, wgu, wd)


def _inproj_kernel(*refs, sub, layout):
    x_refs, mod_refs = refs[:sub], refs[sub:2 * sub]
    g_ref, w_ref, wg_ref = refs[2 * sub:2 * sub + 3]
    o_refs = refs[2 * sub + 3:]
    hs = [(_rms(x[...]) * g_ref[2:3, :] * (1.0 + m[4:5, :]) + m[3:4, :]).astype(BF16)
          for x, m in zip(x_refs, mod_refs)]
    perm = _time_perm(False)
    col = 0
    for o_ref, segments in zip(o_refs[:-1], layout):
        n = o_ref.shape[-1]
        for k, h in enumerate(hs):
            res = _dot(h, w_ref[:, col:col + n])
            parts, c0 = [], 0
            for width, post in segments:
                seg = res[:, c0:c0 + width]
                if post == "perm":
                    seg = _dot(perm, seg.astype(BF16))
                elif post is not None:
                    seg = post(seg)
                parts.append(seg.astype(o_ref.dtype))
                c0 += width
            o_ref[k * TM:(k + 1) * TM, :] = jnp.concatenate(parts, axis=1)
        col += n
    for k, h in enumerate(hs):
        o_refs[-1][k * TM:(k + 1) * TM, :] = _dot(h, wg_ref[...])


def _inproj(plan, xz, mod, norm_g, w_main, w_gate, layout):
    d = xz.shape[-1]
    ng = w_gate.shape[-1]
    l = plan.layer
    widths = [sum(w for w, _ in segments) for segments in layout]
    return pl.pallas_call(
        functools.partial(_inproj_kernel, sub=plan.sub, layout=layout),
        grid=plan.grid,
        in_specs=plan.toks(d) + plan.mods(d)
                 + [_const_spec((None,) + norm_g.shape[1:], (l, 0, 0)),
                    _const_spec((None, d, w_main.shape[-1]), (l, 0, 0)),
                    _const_spec((None, d, ng), (l, 0, 0))],
        out_specs=[plan.out(w) for w in widths] + [plan.out(ng)],
        out_shape=[plan.out_shape(w, BF16) for w in widths] + [plan.out_shape(ng, F32)],
        compiler_params=_params("parallel"),
        name="inproj",
    )(*([xz] * plan.sub), *([mod] * plan.sub), norm_g, w_main, w_gate)


N_MERGE_STREAMS = 12


def _merge_kernel(*refs, sub, ctx_tiles):
    x_refs, mod_refs = refs[:sub], refs[sub:2 * sub]
    tok = refs[2 * sub:(2 + N_MERGE_STREAMS) * sub]
    g_ref, mlg_ref, dng_ref, wb_ref, wo_ref, o_ref = refs[(2 + N_MERGE_STREAMS) * sub:]
    d = o_ref.shape[-1]
    f32 = lambda r: r[...].astype(F32)
    unperm = _time_perm(True)
    mixes = []
    for k in range(sub):
        mlf, mlb, mlo, lrf, lrb, lry, dlf, dlb, dcf, dcb, dnz, mg = (
            tok[s * sub + k] for s in range(N_MERGE_STREAMS))
        y_ml = _rms_heads(f32(mlf) + f32(mlb), ML_HEADS) * mlg_ref[...] * f32(mlo)
        y_lr = (_dot(unperm, lrf[...]) + _dot(unperm, lrb[...])) * f32(lry)
        dn_h = jnp.where(ctx_tiles(k), f32(dcf) + f32(dcb), f32(dlf) + f32(dlb))
        y_dn = _rms_heads(dn_h, DN_HEADS) * dng_ref[...] * f32(dnz)
        mix = None
        for n, y in enumerate((y_ml, y_lr, y_dn)):
            term = mg[:, n * d:(n + 1) * d].astype(F32) * _dot(y.astype(BF16), wb_ref[n])
            mix = term if mix is None else mix + term
        mixes.append(mix.astype(BF16))
    outs = [_dot(mix, wo_ref[...]) for mix in mixes]
    for k, out in enumerate(outs):
        o_ref[k * TM:(k + 1) * TM, :] = x_refs[k][...] + mod_refs[k][5:6, :] * (_rms(out) * g_ref[3:4, :])


def _merge(plan, xz, mod, norm_g, ml_hf, ml_hb, ml, lr_hf, lr_hb, lru, dn_lat, dn_all, dnz, mg,
           ml_g, dn_g, w_branch, w_out):
    d = xz.shape[-1]
    bw = w_branch.shape[2]
    l = plan.layer
    sub = range(plan.sub)
    whole = lambda a, w, c=0: (a, plan.toks(w, c))
    streams = [whole(ml_hf, bw), whole(ml_hb, bw), whole(ml, bw, ml.shape[-1] // bw - 1),
               whole(lr_hf, bw), whole(lr_hb, bw), whole(lru, bw, lru.shape[-1] // bw - 1),
               (dn_lat[0], [plan.tok_split(k, bw, False) for k in sub]),
               (dn_lat[1], [plan.tok_split(k, bw, False) for k in sub]),
               (dn_all[0], [plan.tok_ctx_rest(k, bw) for k in sub]),
               (dn_all[1], [plan.tok_ctx_rest(k, bw) for k in sub]),
               whole(dnz, bw), whole(mg, N_BRANCH * d)]
    assert len(streams) == N_MERGE_STREAMS
    tok_specs = [sp for _, specs in streams for sp in specs]
    tok_args = [a for a, _ in streams for _ in sub]
    return pl.pallas_call(
        functools.partial(_merge_kernel, sub=plan.sub, ctx_tiles=lambda k: plan.is_ctx(pl.program_id(0), k)),
        grid=plan.grid,
        in_specs=plan.toks(d) + plan.mods(d) + tok_specs
                 + [_const_spec((None,) + norm_g.shape[1:], (l, 0, 0)),
                    _const_spec((None, 1, bw), (l, 0, 0)),
                    _const_spec((None, 1, bw), (l, 0, 0)),
                    _const_spec((None, N_BRANCH, bw, d), (l, 0, 0, 0)),
                    _const_spec((None, d, d), (l, 0, 0))],
        out_specs=plan.out(d),
        out_shape=plan.out_shape(d, F32),
        compiler_params=_params("parallel"),
        name="merge",
    )(*([xz] * plan.sub), *([mod] * plan.sub), *tok_args, norm_g, ml_g, dn_g, w_branch, w_out)


SEG = TM // V7X_SUBLANES


def _time_perm(inverse):
    ri = lax.broadcasted_iota(jnp.int32, (TM, TM), 0)
    ci = lax.broadcasted_iota(jnp.int32, (TM, TM), 1)
    r, t = (ci, ri) if inverse else (ri, ci)
    return jnp.where(t == (r % V7X_SUBLANES) * SEG + r // V7X_SUBLANES, 1.0, 0.0).astype(BF16)


def _conv4_perm(x, before1, before2, after1, w):
    sub = lax.broadcasted_iota(jnp.int32, (V7X_SUBLANES, 1), 0)
    vrow = lambda i: x[V7X_SUBLANES * i:V7X_SUBLANES * (i + 1)]
    m1_edge = jnp.where(sub == 0, before1, pltpu.roll(vrow(SEG - 1), 1, axis=0))
    m2_edge = jnp.where(sub == 0, before2, pltpu.roll(vrow(SEG - 2), 1, axis=0))
    p1_edge = jnp.where(sub == V7X_SUBLANES - 1, after1, pltpu.roll(vrow(0), V7X_SUBLANES - 1, axis=0))
    x_m1 = jnp.concatenate([m1_edge, x[:-V7X_SUBLANES]], axis=0)
    x_m2 = jnp.concatenate([m2_edge, m1_edge, x[:-2 * V7X_SUBLANES]], axis=0)
    x_p1 = jnp.concatenate([x[V7X_SUBLANES:], p1_edge], axis=0)
    taps = {-2: x_m2, -1: x_m1, 0: x, 1: x_p1}
    acc = None
    for j in range(CONV_W):
        term = taps[j - CONV_LEFT] * w[j:j + 1]
        acc = term if acc is None else acc + term
    return acc


def _tile_scan(a, b, h0, rev):
    vrow = lambda x, i: x[V7X_SUBLANES * i:V7X_SUBLANES * (i + 1)]
    h = jnp.zeros_like(vrow(a, 0))
    p = jnp.ones_like(h)
    hs, ps = [None] * SEG, [None] * SEG
    for i in (range(SEG - 1, -1, -1) if rev else range(SEG)):
        ai = vrow(a, i)
        h = ai * h + vrow(b, i)
        p = ai * p
        hs[i], ps[i] = h, p
    carry = h0
    enter = [None] * V7X_SUBLANES
    for s in (range(V7X_SUBLANES - 1, -1, -1) if rev else range(V7X_SUBLANES)):
        enter[s] = carry
        carry = p[s:s + 1] * carry + h[s:s + 1]
    enter = jnp.concatenate(enter, axis=0)
    return jnp.concatenate([hi + pi * enter for hi, pi in zip(hs, ps)], axis=0), carry


def _lru_kernel(x_ref, cw_ref, cb_ref, w_ref, bias_ref, lam_ref, of_ref, ob_ref, *, n_lat, n_ctx):
    t, c = TM, x_ref.shape[-1]
    n_tiles = n_lat + n_ctx
    halo = 2 * V7X_SUBLANES

    def load_conv(tile):
        r0 = pl.multiple_of(tile * t, t)
        first = jnp.logical_or(tile == 0, tile == n_lat)
        last = jnp.logical_or(tile == n_lat - 1, tile == n_tiles - 1)
        x = x_ref[pl.ds(r0, t), :].astype(F32)
        p0 = pl.multiple_of(jnp.maximum(r0 - halo, 0), halo)
        n0 = pl.multiple_of(jnp.minimum(r0 + t, n_tiles * t - halo), halo)
        prev = jnp.where(first, 0.0, x_ref[pl.ds(p0, halo), :].astype(F32))
        nxt = jnp.where(last, 0.0, x_ref[pl.ds(n0, halo), :].astype(F32))
        before1, before2 = prev[halo - 1:halo], prev[V7X_SUBLANES - 1:V7X_SUBLANES]
        return r0, _conv4_perm(x, before1, before2, nxt[0:1], cw_ref[...]) + cb_ref[...]

    def direction(d, tile, h0, o_ref):
        r0, xc = load_conv(tile)
        z = _dot(xc.astype(BF16), w_ref[d]) + bias_ref[d]
        r, i = _sigmoid(z[:, :c]), _sigmoid(z[:, c:])
        la = (-LRU_C * _softplus(-lam_ref[d])) * r
        a = jnp.exp(la)
        bx = jnp.sqrt(jnp.tanh(-la) * (1.0 + a * a)) * (i * xc)
        h, carry = _tile_scan(a, bx, h0, d == 1)
        o_ref[pl.ds(r0, t), :] = h.astype(o_ref.dtype)
        return carry

    def step(s, carry):
        hf, hb = carry
        hf = direction(0, jnp.where(s < n_ctx, n_lat + s, s - n_ctx), hf, of_ref)
        hb = direction(1, n_tiles - 1 - s, hb, ob_ref)
        return hf, hb

    zero = jnp.zeros((1, c), F32)
    lax.fori_loop(0, n_tiles, step, (zero, zero))


def _lru(lru, conv_w, conv_b, w_gates, b_gates, lam, l, n_lat, n_ctx):
    b, lt, _ = lru.shape
    c = conv_w.shape[-1]
    seq = pl.BlockSpec((None, lt, c), lambda bi: (bi, 0, 0))
    return pl.pallas_call(
        functools.partial(_lru_kernel, n_lat=n_lat, n_ctx=n_ctx),
        grid=(b,),
        in_specs=[seq,
                  _const_spec((None, CONV_W, c), (l, 0, 0)),
                  _const_spec((None, 1, c), (l, 0, 0)),
                  _const_spec((None, 2, c, 2 * c), (l, 0, 0, 0)),
                  _const_spec((None, 2, 1, 2 * c), (l, 0, 0, 0)),
                  _const_spec((None, 2, 1, c), (l, 0, 0, 0))],
        out_specs=[seq, seq],
        out_shape=[jax.ShapeDtypeStruct((b, lt, c), BF16)] * 2,
        compiler_params=_params("parallel"),
        name="lru",
    )(lru, conv_w, conv_b, w_gates, b_gates, lam)


def _mlstm_kernel(qf_ref, kf_ref, vf_ref, qb_ref, kb_ref, vb_ref, grf_ref, grb_ref, br_ref,
                  hf_ref, hb_ref, c_ref, m_ref):
    @pl.when(pl.program_id(1) == 0)
    def _():
        c_ref[...] = jnp.zeros(c_ref.shape, F32)
        m_ref[...] = jnp.zeros(m_ref.shape, F32)

    nrow, tc, nqk = qf_ref.shape
    dk = nqk // ML_HEADS
    dv = vf_ref.shape[-1] // ML_HEADS
    nh = ML_HEADS
    ri = lax.broadcasted_iota(jnp.int32, (tc, tc), 0)
    ci = lax.broadcasted_iota(jnp.int32, (tc, tc), 1)
    causal = (ci <= ri, ci >= ri)
    lane_head = lax.broadcasted_iota(jnp.int32, (1, nqk), 1) // dk
    ones = jnp.ones((tc, dv), BF16)
    dirs = ((qf_ref, kf_ref, vf_ref, grf_ref), (qb_ref, kb_ref, vb_ref, grb_ref))
    seqs = [(r, d) for r in range(nrow) for d in range(2)]

    rows, v_rows, ws_rows, decs = [], [], [], []
    for r, d in seqs:
        rev = d == 1
        gr = dirs[d][3][r].T + br_ref[...]
        i_r = gr[nh * d:nh * (d + 1)]
        b_r = _cumsum_groups(-_softplus(-gr[nh * (2 + d):nh * (3 + d)]), 1, tc, rev)
        m_prev = m_ref[r, nh * d:nh * (d + 1), 0:1]
        m_t = b_r + jnp.maximum(m_prev, _cummax_groups(i_r - b_r, 1, tc, rev))
        b_end = b_r[:, 0:1] if rev else b_r[:, tc - 1:tc]
        lws = b_end - b_r + i_r
        m_new = jnp.maximum(b_end + m_prev, jnp.max(lws, axis=1, keepdims=True))
        rows += [b_r - m_t, jnp.exp(b_r + m_prev - m_t), jnp.exp(-m_t)]
        v_rows.append(b_r - i_r)
        ws_rows.append(jnp.exp(lws - m_new))
        decs.append(jnp.exp(b_end + m_prev - m_new))
        m_ref[r, nh * d:nh * (d + 1), :] = jnp.broadcast_to(m_new, (nh, m_ref.shape[-1]))
    cols = jnp.concatenate(rows, axis=0).T

    chains = [(s, h) for s in range(len(seqs)) for h in range(nh)]
    q_all = [dirs[d][0][r] * (dk ** -0.5) for r, d in seqs]
    k_all = [dirs[d][1][r] for r, d in seqs]
    v_all = [dirs[d][2][r] for r, d in seqs]
    kt_all = [k.astype(F32).T for k in k_all]
    c_all = [c_ref[r, d] for r, d in seqs]
    cb_all = [c.astype(BF16) for c in c_all]
    col = lambda s, j, h: cols[:, (3 * s + j) * nh + h:(3 * s + j) * nh + h + 1]
    qh = [jnp.where(lane_head == h, q_all[s], jnp.zeros_like(q_all[s])) for s, h in chains]
    vp = [jnp.concatenate([v_all[s][:, h * dv:(h + 1) * dv], ones], axis=1) for s, h in chains]
    s_raw = [_dot_nt(q, k_all[s]) for q, (s, h) in zip(qh, chains)]
    p = [(sr * jnp.exp(jnp.where(causal[seqs[s][1]], col(s, 0, h) - v_rows[s][h:h + 1], NEG))).astype(BF16)
         for sr, (s, h) in zip(s_raw, chains)]
    num = [_dot(pc, vc) + col(s, 1, h) * _dot(q, cb_all[s]) for pc, vc, q, (s, h) in zip(p, vp, qh, chains)]
    outs = [nm[:, :dv] / jnp.maximum(jnp.abs(nm[:, dv:]), col(s, 2, h)) for nm, (s, h) in zip(num, chains)]
    for vc, (s, h) in zip(vp, chains):
        r, d = seqs[s]
        kw = (kt_all[s][h * dk:(h + 1) * dk] * ws_rows[s][h:h + 1]).astype(BF16)
        c_ref[r, d, h * dk:(h + 1) * dk, :] = decs[s][h:h + 1] * c_all[s][h * dk:(h + 1) * dk] + _dot(kw, vc)
    for s, (r, d) in enumerate(seqs):
        o_ref = (hf_ref, hb_ref)[d]
        o_ref[r] = jnp.concatenate(outs[s * nh:(s + 1) * nh], axis=1).astype(o_ref.dtype)


def _mlstm(ml, gates, bias_r, l, n_lat, n_ctx):
    b, lt, _ = ml.shape
    ng = gates.shape[-1]
    n_tiles = n_lat + n_ctx
    dqk = ml.shape[-1] // 6
    fwd = lambda s: jnp.where(s < n_ctx, n_lat + s, s - n_ctx)
    bwd = lambda s: n_tiles - 1 - s
    rpb = ML_ROWS if b % ML_ROWS == 0 else 1

    def specs(tile):
        return [pl.BlockSpec((rpb, TM, dqk), lambda bi, s: (bi, tile(s), 0)),
                pl.BlockSpec((rpb, TM, dqk), lambda bi, s: (bi, tile(s), 1)),
                pl.BlockSpec((rpb, TM, 2 * dqk), lambda bi, s: (bi, tile(s), 1))]

    row = lambda tile: pl.BlockSpec((rpb, TM, ng), lambda bi, s: (bi, tile(s), 0))
    out = lambda tile: pl.BlockSpec((rpb, TM, 2 * dqk), lambda bi, s: (bi, tile(s), 0))
    return pl.pallas_call(
        _mlstm_kernel,
        grid=(b // rpb, n_tiles),
        in_specs=specs(fwd) + specs(bwd) + [row(fwd), row(bwd), _const_spec((None, ng, 1), (l, 0, 0))],
        out_specs=[out(fwd), out(bwd)],
        out_shape=[jax.ShapeDtypeStruct((b, lt, 2 * dqk), BF16)] * 2,
        scratch_shapes=[pltpu.VMEM((rpb, 2, dqk, 2 * (2 * dqk // ML_HEADS)), F32),
                        pltpu.VMEM((rpb, 2 * ML_HEADS, V7X_LANES), F32)],
        compiler_params=_params("parallel", "arbitrary"),
        name="mlstm",
    )(ml, ml, ml, ml, ml, ml, gates, gates, bias_r)


INV_BASE = 8


def _inv_masks(ri, ci):
    blk = lambda s: (ri // s) == (ci // s)
    as_bf = lambda m: jnp.where(m, 1.0, 0.0).astype(BF16)
    masks, s = [as_bf(blk(INV_BASE))], INV_BASE
    while s < CHUNK:
        masks.append(as_bf(jnp.logical_and(blk(2 * s), jnp.logical_not(blk(s)))))
        s *= 2
    return masks


def _tri_inv_many(a_list, eye, same, masks):
    nblk = a_list[0].shape[0] // CHUNK
    compact = lambda m: functools.reduce(lambda u, v: u + v, [m[i * CHUNK:(i + 1) * CHUNK] for i in range(nblk)])
    spread = lambda c: jnp.concatenate([c] * nblk, axis=0) * same
    eye_c = compact(eye)
    ps = [a * masks[0] for a in a_list]
    pcs = [compact(p) for p in ps]
    xcs = [eye_c - pc for pc in pcs]
    for _ in range(INV_BASE.bit_length() - 2):
        pcs = [_dot(pc, p).astype(BF16) for pc, p in zip(pcs, ps)]
        ps = [spread(pc) for pc in pcs]
        xcs = [_dot(xc, eye + p).astype(BF16) for xc, p in zip(xcs, ps)]
    for m in masks[1:]:
        ys = [_dot(xc, a * m).astype(BF16) for xc, a in zip(xcs, a_list)]
        xcs = [xc - _dot(y, spread(xc)).astype(BF16) for y, xc in zip(ys, xcs)]
    return [spread(xc) for xc in xcs]


def _dn_prep_kernel(xl_ref, xp_ref, xn_ref, xc_ref, grow_ref, cw_ref, arow_ref,
                    u_ref, w_ref, qe_ref, kd_ref, qk_ref, eg_ref, *, n_lat_groups):
    g = pl.program_id(1)
    is_ctx = g >= n_lat_groups
    width = xl_ref.shape[-1]
    dh = width // (3 * DN_HEADS)
    n = DN_HEADS * CHUNK
    halo = 2 * V7X_SUBLANES
    x = jnp.where(is_ctx, xc_ref[...], xl_ref[...]).reshape(DN_GROUP * CHUNK, width)
    no_prev = jnp.logical_or(is_ctx, g == 0)
    no_next = jnp.logical_or(is_ctx, g == n_lat_groups - 1)
    prev8 = jnp.where(no_prev, 0.0, xp_ref[CHUNK - halo:, :].astype(F32)[V7X_SUBLANES:])
    nxt8 = jnp.where(no_next, 0.0, xn_ref[:halo, :].astype(F32)[:V7X_SUBLANES])
    xc = _silu(_conv4(x, prev8, nxt8, cw_ref[...]))

    ri = lax.broadcasted_iota(jnp.int32, (n, n), 0)
    ci = lax.broadcasted_iota(jnp.int32, (n, n), 1)
    same = (ri // CHUNK) == (ci // CHUNK)
    eye = jnp.where(ri == ci, 1.0, 0.0).astype(BF16)
    incl =(jnp.logical_and(same, ci <= ri), jnp.logical_and(same, ci >= ri))
    strict = (jnp.logical_and(same, ci < ri), jnp.logical_and(same, ci > ri))
    masks = _inv_masks(ri, ci)

    qn, kn, v, kk, qk = [], [], [], [], []
    for i in range(DN_GROUP):
        xi = xc[i * CHUNK:(i + 1) * CHUNK]
        q, k, vi = (_stack_heads(xi[:, j * DN_HEADS * dh:(j + 1) * DN_HEADS * dh], DN_HEADS) for j in range(3))
        qn.append(q * lax.rsqrt(jnp.sum(q * q, axis=-1, keepdims=True) + EPS) * (dh ** -0.5))
        kn.append(k * lax.rsqrt(jnp.sum(k * k, axis=-1, keepdims=True) + EPS))
        v.append(vi)
        kb = kn[i].astype(BF16)
        kk.append(_dot_nt(kb, kb))
        qk.append(_dot_nt(qn[i].astype(BF16), kb))

    systems = [(i, d) for i in range(DN_GROUP) for d in range(2)]
    rows, cs_rows = [], []
    for i, d in systems:
        raw = grow_ref[i]
        beta = _sigmoid(raw[d:d + 1])
        g_r = -jnp.exp(arow_ref[d, 0:1, :]) * _softplus(raw[2 + d:3 + d] + arow_ref[d, 1:2, :])
        cs = _cumsum_groups(g_r, 1, CHUNK, d == 1)
        tot = cs + _cumsum_groups(g_r, 1, CHUNK, d == 0) - g_r
        eg = jnp.exp(cs)
        rows += [cs, beta, eg, jnp.exp(tot - cs), beta * eg]
        cs_rows.append(cs)
        eg_ref[d, i] = jnp.exp(tot)
    n_col = len(rows) // len(systems)
    cols = jnp.concatenate(rows, axis=0).T

    a_list, rhs = [], []
    for s, (i, d) in enumerate(systems):
        cs_c, beta_c, eg_c, ekd_c, beg_c = (cols[:, n_col * s + j:n_col * s + j + 1] for j in range(n_col))
        gam = jnp.exp(jnp.where(incl[d], cs_c - cs_rows[s], NEG))
        a_list.append((jnp.where(strict[d], kk[i], 0.0) * gam * beta_c).astype(BF16))
        rhs.append(jnp.concatenate([beta_c * v[i], beg_c * kn[i]], axis=1).astype(BF16))
        qk_ref[d, i] = (qk[i] * gam).astype(qk_ref.dtype)
        qe_ref[d, i] = (qn[i] * eg_c).astype(qe_ref.dtype)
        kd_ref[d, i] = (kn[i] * ekd_c).astype(kd_ref.dtype)
    same_bf = jnp.where(same, 1.0, 0.0).astype(BF16)
    for (i, d), tinv, r in zip(systems, _tri_inv_many(a_list, eye, same_bf, masks), rhs):
        sol = _dot(tinv, r)
        u_ref[d, i] = sol[:, :dh].astype(u_ref.dtype)
        w_ref[d, i] = sol[:, dh:].astype(w_ref.dtype)


def _dn_prep(x_cols, x_rows, grow, conv_w, arow, l):
    b, nc, _, width = x_rows.shape
    n_cols = x_cols.shape[1]
    n_groups = nc // DN_GROUP
    n_lat_groups = n_cols // DN_GROUP
    assert n_groups == n_lat_groups + 1
    n = DN_HEADS * CHUNK
    dh = width // (3 * DN_HEADS)
    lat_g = lambda g: jnp.minimum(g, n_lat_groups - 1)
    out = lambda w, dt: (jax.ShapeDtypeStruct((b, 2, nc, n, w), dt),
                         pl.BlockSpec((None, 2, DN_GROUP, n, w), lambda bi, g: (bi, 0, g, 0, 0)))
    outs = [out(dh, BF16)] * 4 + [out(n, BF16)]
    outs.append((jax.ShapeDtypeStruct((b, 2, nc, 1, n), F32),
                 pl.BlockSpec((None, 2, DN_GROUP, 1, n), lambda bi, g: (bi, 0, g, 0, 0))))
    chunk = lambda idx: pl.BlockSpec((None, None, CHUNK, width), lambda bi, g: (bi, idx(g), 0, 0))
    return pl.pallas_call(
        functools.partial(_dn_prep_kernel, n_lat_groups=n_lat_groups),
        grid=(b, n_groups),
        in_specs=[pl.BlockSpec((None, DN_GROUP, CHUNK, width), lambda bi, g: (bi, lat_g(g), 0, 0)),
                  chunk(lambda g: jnp.maximum(lat_g(g) * DN_GROUP - 1, 0)),
                  chunk(lambda g: jnp.minimum(lat_g(g) * DN_GROUP + DN_GROUP, n_cols - 1)),
                  pl.BlockSpec((None, DN_GROUP, CHUNK, width), lambda bi, g: (bi, n_groups - 1, 0, 0)),
                  pl.BlockSpec((None, DN_GROUP, 4, n), lambda bi, g: (bi, g, 0, 0)),
                  _const_spec((None, CONV_W, width), (l, 0, 0)),
                  _const_spec((None, 2, 2, n), (l, 0, 0, 0))],
        out_specs=[o[1] for o in outs],
        out_shape=[o[0] for o in outs],
        compiler_params=_params("parallel", "parallel"),
        name="dn_prep",
    )(x_cols, x_cols, x_cols, x_rows, grow, conv_w, arow)


def _dn_scan_kernel(*refs):
    ins, (of_ref, ob_ref, s_ref) = refs[:12], refs[12:]

    @pl.when(pl.program_id(1) == 0)
    def _():
        s_ref[...] = jnp.zeros(s_ref.shape, F32)

    rows = [slice(h * CHUNK, (h + 1) * CHUNK) for h in range(DN_HEADS)]
    seqs = [(r, d) for r in range(of_ref.shape[0]) for d in range(2)]
    get = lambda j, r, d: ins[6 * d + j][r]
    sb = [[s_ref[r, d, h].astype(BF16) for h in range(DN_HEADS)] for r, d in seqs]
    ws = [jnp.concatenate([_dot(get(1, r, d)[rows[h]], sb[i][h]) for h in range(DN_HEADS)], axis=0)
          for i, (r, d) in enumerate(seqs)]
    qs = [jnp.concatenate([_dot(get(2, r, d)[rows[h]], sb[i][h]) for h in range(DN_HEADS)], axis=0)
          for i, (r, d) in enumerate(seqs)]
    vnew = [(get(0, r, d).astype(F32) - w).astype(BF16) for w, (r, d) in zip(ws, seqs)]
    o = [q + _dot(get(4, r, d), vn) for q, vn, (r, d) in zip(qs, vnew, seqs)]
    for vn, (r, d) in zip(vnew, seqs):
        kd, eg = get(3, r, d), get(5, r, d)
        for h in range(DN_HEADS):
            s_ref[r, d, h] = (eg[:, h * CHUNK:h * CHUNK + 1] * s_ref[r, d, h]
                              + _dot_tn(kd[rows[h]], vn[rows[h]]))
    for oi, (r, d) in zip(o, seqs):
        o_ref = (of_ref, ob_ref)[d]
        o_ref[r] = _unstack_heads(oi, DN_HEADS).astype(o_ref.dtype)


def _dn_scan(prep, n_lat_chunks):
    b, _, nc, _, dh = prep[0].shape
    ow = DN_HEADS * dh
    n_ctx_chunks = nc - n_lat_chunks
    fwd = lambda s: jnp.where(s < n_ctx_chunks, n_lat_chunks + s, s - n_ctx_chunks)
    bwd = lambda s: nc - 1 - s

    rpb = DN_SCAN_ROWS if b % DN_SCAN_ROWS == 0 else 1

    def spec(a, d, chunk):
        return pl.BlockSpec((rpb, None, None) + a.shape[3:], lambda bi, s: (bi, d, chunk(s), 0, 0))

    out = lambda chunk: pl.BlockSpec((rpb, None, CHUNK, ow), lambda bi, s: (bi, chunk(s), 0, 0))
    return pl.pallas_call(
        _dn_scan_kernel,
        grid=(b // rpb, nc),
        in_specs=[spec(a, 0, fwd) for a in prep] + [spec(a, 1, bwd) for a in prep],
        out_specs=[out(fwd), out(bwd)],
        out_shape=[jax.ShapeDtypeStruct((b, nc, CHUNK, ow), BF16)] * 2,
        scratch_shapes=[pltpu.VMEM((rpb, 2, DN_HEADS, dh, dh), F32)],
        compiler_params=_params("parallel", "arbitrary"),
        name="dn_scan",
    )(*prep, *prep)


def _block_diag(w):
    n, i, j = w.shape
    return jnp.einsum('nij,nm->nimj', w, jnp.eye(n, dtype=w.dtype)).reshape(n * i, n * j)


def _dn_gate_rows(raw, n_lat):
    b = raw.shape[0]
    lat = raw[:, :n_lat].reshape(b, n_lat // CHUNK, CHUNK, 4, DN_HEADS)
    ctx = raw[:, n_lat:].reshape(b, -1, CHUNK, 4, DN_HEADS)
    row = jnp.concatenate([lat.transpose(0, 2, 3, 4, 1), ctx.transpose(0, 1, 3, 4, 2)], axis=1)
    return row.reshape(b, row.shape[1], 4, DN_HEADS * CHUNK)


def kernel(x, c, ctx, c_ctx, w_mod, b_mod, norm_g, ffn_w_gu, ffn_w_down, w_in, ml_gate_b, ml_norm_g,
           lru_conv_w, lru_conv_b, lru_w_a, lru_b_a, lru_w_x, lru_b_x, lru_lambda, dn_conv_w,
           dn_a_log, dn_dt_bias, dn_norm_g, w_branch, w_out):
    b, n_lat, d = x.shape
    n_ctx = ctx.shape[1]
    depth = w_mod.shape[0]
    bw = w_branch.shape[2]
    assert n_lat == CHUNK * CHUNK and n_lat % TM == 0 and n_ctx % TM == 0 and n_ctx == CHUNK * DN_GROUP
    n_lat_tiles, n_ctx_tiles = n_lat // TM, n_ctx // TM
    n_tiles = n_lat_tiles + n_ctx_tiles

    ctx_row = b
    n_rows = -(-(b + 1) // V7X_SUBLANES) * V7X_SUBLANES
    cc = jnp.zeros((n_rows, d), F32).at[:b].set(c).at[b].set(c_ctx)
    mod = _mod_table(cc, w_mod, b_mod).reshape(depth, n_rows, N_MOD, d)

    dqk = bw // 2
    edges = [0]
    for wdt in (dqk, dqk, bw, bw, N_GATES, bw, bw, bw, bw, bw, bw, N_GATES, N_BRANCH * d):
        edges.append(edges[-1] + wdt)
    piece = lambda i, j: w_in[:, :, edges[i]:edges[j]]
    layout = (((2 * bw, None), (bw, _sigmoid)),
              ((bw, "perm"), (bw, _gelu_tanh)),
              ((3 * bw, None),),
              ((bw, _silu),),
              ((N_BRANCH * d, _sigmoid),))
    w_main = jnp.concatenate([piece(0, 4), piece(5, 7), piece(7, 11), piece(12, 13)], axis=-1).astype(BF16)
    w_gate = jnp.concatenate([piece(4, 5), piece(11, 12)], axis=-1).astype(BF16)
    wgu = ffn_w_gu.astype(BF16)
    wdn = ffn_w_down.astype(BF16)
    wbr = w_branch.astype(BF16)
    wout = w_out.astype(BF16)
    ml_bias = jnp.concatenate([ml_gate_b.reshape(depth, N_GATES), jnp.zeros((depth, N_GATES), F32)], axis=-1)
    ml_bias_r = ml_bias[:, :, None]
    ml_g = ml_norm_g[:, None, :]
    dn_g = jnp.tile(dn_norm_g, (1, DN_HEADS))[:, None, :]
    lru_w = jnp.stack([jnp.concatenate([jax.vmap(_block_diag)(lru_w_a[:, dd]), jax.vmap(_block_diag)(lru_w_x[:, dd])],
                                       axis=-1) for dd in range(2)], axis=1).astype(BF16)
    lru_b = jnp.concatenate([lru_b_a, lru_b_x], axis=-1)[:, :, None, :]
    lru_lam = lru_lambda[:, :, None, :]
    lru_cb = lru_conv_b[:, None, :]
    dn_arow = jnp.repeat(jnp.stack([dn_a_log, dn_dt_bias], axis=2), CHUNK, axis=3)

    rows3 = lambda a: a.reshape(b, n_lat + n_ctx, a.shape[-1])
    for l in range(depth):
        full = _TilePlan(b, n_tiles, n_lat_tiles, n_tiles, ctx_row, l)
        if l == 0:
            xz = _ffn(full, _flat(x), mod, norm_g, wgu, wdn, 0, ctx=_flat(ctx))
        else:
            xz = _ffn(full, xz, mod, norm_g, wgu, wdn, 0)
        ml, lru, dnqkv, dnz, mg, gates = _inproj(full, xz, mod, norm_g, w_main, w_gate, layout)
        gates = rows3(gates)
        ml_hf, ml_hb = _mlstm(rows3(ml), gates, ml_bias_r, l, n_lat_tiles, n_ctx_tiles)
        lr_hf, lr_hb = _lru(rows3(lru), lru_conv_w, lru_cb, lru_w, lru_b, lru_lam, l, n_lat_tiles, n_ctx_tiles)
        n_cols = n_lat // CHUNK
        dq = rows3(dnqkv)
        x_rows = dq.reshape(b, -1, CHUNK, dq.shape[-1])
        x_cols = x_rows.swapaxes(1, 2)
        prep = _dn_prep(x_cols, x_rows, _dn_gate_rows(gates[:, :, N_GATES:], n_lat), dn_conv_w, dn_arow, l)
        dn_all = _dn_scan(prep, n_cols)
        dn_lat = [_flat(o[:, :n_cols].swapaxes(1, 2)) for o in dn_all]
        xz = _merge(full, xz, mod, norm_g, _flat(ml_hf), _flat(ml_hb), ml, _flat(lr_hf), _flat(lr_hb), lru,
                    dn_lat, [_flat(o) for o in dn_all], dnz, mg, ml_g, dn_g, wbr, wout)
        visit = n_lat_tiles if l == depth - 1 else n_tiles
        xz = _ffn(_TilePlan(b, n_tiles, n_lat_tiles, visit, ctx_row, l), xz, mod, norm_g, wgu, wdn, 2)
    return xz.reshape(b, n_lat, d)
```

```python
import functools
import math

import jax
import jax.numpy as jnp
from jax import lax
from jax.experimental import pallas as pl
from jax.experimental.pallas import tpu as pltpu

F32 = jnp.float32
BF16 = jnp.bfloat16

EPS = 1e-6
N_MOD = 9
N_BRANCH = 3
CONV_W = 4
CONV_LEFT = 2
ML_HEADS = 4
LRU_C = 8.0
DN_HEADS = 4
CHUNK = 64
DN_GROUP = 4
DN_SCAN_ROWS = 4
ML_ROWS = 2
N_GATES = 16

V7X_SUBLANES = 8
V7X_LANES = 128
V7X_VMEM_BYTES = 64 * 1024 * 1024
VMEM_LIMIT = V7X_VMEM_BYTES - 8 * 1024 * 1024

TM = 256
NEG = -1e30


def _sigmoid(x):
    return 0.5 * jnp.tanh(0.5 * x) + 0.5


def _silu(x):
    return x * _sigmoid(x)


def _softplus(x):
    return jnp.maximum(x, 0.0) + jnp.log(1.0 + jnp.exp(-jnp.abs(x)))


def _rms(x):
    return x * lax.rsqrt(jnp.mean(x * x, axis=-1, keepdims=True) + EPS)


def _rms_heads(x, n_heads):
    hd = x.shape[-1] // n_heads
    return jnp.concatenate([_rms(x[:, h * hd:(h + 1) * hd]) for h in range(n_heads)], axis=-1)


def _gelu_tanh(x):
    return 0.5 * x * (1.0 + jnp.tanh(math.sqrt(2.0 / math.pi) * (x + 0.044715 * (x * x * x))))


def _dot(a, b):
    return jnp.dot(a, b, preferred_element_type=F32)


def _dot_nt(a, b):
    return lax.dot_general(a, b, (((1,), (1,)), ((), ())), preferred_element_type=F32)


def _dot_tn(a, b):
    return lax.dot_general(a, b, (((0,), (0,)), ((), ())), preferred_element_type=F32)


def _params(*sem):
    return pltpu.CompilerParams(dimension_semantics=sem, vmem_limit_bytes=VMEM_LIMIT)


def _const_spec(block, index):
    return pl.BlockSpec(block, lambda *_: index, pipeline_mode=pl.Buffered(1))


def _stack_heads(x, n):
    w = x.shape[1] // n
    return jnp.concatenate([x[:, h * w:(h + 1) * w] for h in range(n)], axis=0)


def _unstack_heads(x, n):
    t = x.shape[0] // n
    return jnp.concatenate([x[h * t:(h + 1) * t] for h in range(n)], axis=1)


def _cumsum_groups(x, axis, period, rev):
    n = x.shape[axis]
    idx = lax.broadcasted_iota(jnp.int32, x.shape, axis) % period
    sh = 1
    while sh < period:
        if rev:
            x = x + jnp.where(idx < period - sh, pltpu.roll(x, n - sh, axis=axis), 0.0)
        else:
            x = x + jnp.where(idx >= sh, pltpu.roll(x, sh, axis=axis), 0.0)
        sh *= 2
    return x


def _cummax_groups(x, axis, period, rev):
    n = x.shape[axis]
    idx = lax.broadcasted_iota(jnp.int32, x.shape, axis) % period
    sh = 1
    while sh < period:
        if rev:
            x = jnp.maximum(x, jnp.where(idx < period - sh, pltpu.roll(x, n - sh, axis=axis), NEG))
        else:
            x = jnp.maximum(x, jnp.where(idx >= sh, pltpu.roll(x, sh, axis=axis), NEG))
        sh *= 2
    return x


def _shift_rows(xb, prev8, nxt8, off):
    t = xb.shape[0]
    ri = lax.broadcasted_iota(jnp.int32, (t, t), 0)
    ci = lax.broadcasted_iota(jnp.int32, (t, t), 1)
    shifted = _dot(jnp.where(ci == ri + off, 1.0, 0.0).astype(BF16), xb)
    row8 = lax.broadcasted_iota(jnp.int32, (V7X_SUBLANES, 1), 0)
    if off < 0:
        edge = jnp.where(row8 < -off, pltpu.roll(prev8, -off, axis=0), shifted[:V7X_SUBLANES])
        return jnp.concatenate([edge, shifted[V7X_SUBLANES:]], axis=0)
    edge = jnp.where(row8 >= V7X_SUBLANES - off, pltpu.roll(nxt8, V7X_SUBLANES - off, axis=0),
                     shifted[t - V7X_SUBLANES:])
    return jnp.concatenate([shifted[:t - V7X_SUBLANES], edge], axis=0)


def _conv4(xb, prev8, nxt8, w):
    acc = xb.astype(F32) * w[CONV_LEFT:CONV_LEFT + 1]
    for j in range(CONV_W):
        if j != CONV_LEFT:
            acc = acc + _shift_rows(xb, prev8, nxt8, j - CONV_LEFT) * w[j:j + 1]
    return acc


CAST_ROWS = 512


def _cast_kernel(w_ref, o_ref):
    o_ref[...] = w_ref[...].astype(o_ref.dtype)


def _to_bf16(w):
    flat = w.reshape(-1, w.shape[-1])
    spec = pl.BlockSpec((CAST_ROWS, flat.shape[1]), lambda i: (i, 0))
    out = pl.pallas_call(
        _cast_kernel,
        grid=(flat.shape[0] // CAST_ROWS,),
        in_specs=[spec],
        out_specs=spec,
        out_shape=jax.ShapeDtypeStruct(flat.shape, BF16),
        compiler_params=_params("parallel"),
        name="cast_bf16",
    )(flat)
    return out.reshape(w.shape)


def _mod_kernel(c_ref, w_ref, b_ref, o_ref):
    s = _silu(c_ref[...]).astype(BF16)
    o_ref[...] = _dot(s, w_ref[...].astype(BF16)) + b_ref[...]


def _mod_table(cc, w_mod, b_mod):
    depth, d, nd = w_mod.shape
    r = cc.shape[0]
    tn = nd // 4
    return pl.pallas_call(
        _mod_kernel,
        grid=(depth, nd // tn),
        in_specs=[pl.BlockSpec((r, d), lambda l, j: (0, 0)),
                  pl.BlockSpec((None, d, tn), lambda l, j: (l, 0, j)),
                  pl.BlockSpec((None, 1, tn), lambda l, j: (l, 0, j))],
        out_specs=pl.BlockSpec((None, r, tn), lambda l, j: (l, 0, j)),
        out_shape=jax.ShapeDtypeStruct((depth, r, nd), F32),
        compiler_params=_params("parallel", "parallel"),
        name="mod_table",
    )(cc, w_mod, b_mod.reshape(depth, 1, nd))


class _TilePlan:
    def __init__(self, b, n_tiles, n_lat_tiles, n_vis, ctx_row, layer):
        self.b, self.n_tiles, self.n_lat_tiles, self.n_vis = b, n_tiles, n_lat_tiles, n_vis
        self.ctx_row, self.layer = ctx_row, layer
        self.sub = 2 if (b * n_vis) % 2 == 0 else 1
        self.grid = (b * n_vis // self.sub,)

    def _tile(self, i, k):
        tid = i * self.sub + k
        return tid // self.n_vis, tid % self.n_vis

    def tok(self, k, width, col=0):
        def index(i):
            bi, t = self._tile(i, k)
            return bi * self.n_tiles + t, col
        return pl.BlockSpec((TM, width), index)

    def toks(self, width, col=0):
        return [self.tok(k, width, col) for k in range(self.sub)]

    def is_ctx(self, i, k):
        return self._tile(i, k)[1] >= self.n_lat_tiles

    def tok_ctx_rest(self, k, width):
        def index(i):
            bi, t = self._tile(i, k)
            return bi * self.n_tiles + jnp.maximum(t, self.n_lat_tiles), 0
        return pl.BlockSpec((TM, width), index)

    def tok_split(self, k, width, ctx):
        n_ctx_tiles = self.n_tiles - self.n_lat_tiles

        def index(i):
            bi, t = self._tile(i, k)
            if ctx:
                return bi * n_ctx_tiles + jnp.clip(t - self.n_lat_tiles, 0, n_ctx_tiles - 1), 0
            return bi * self.n_lat_tiles + jnp.minimum(t, self.n_lat_tiles - 1), 0
        return pl.BlockSpec((TM, width), index)

    def mods(self, d):
        def spec(k):
            def index(i):
                bi, t = self._tile(i, k)
                return self.layer, jnp.where(t >= self.n_lat_tiles, self.ctx_row, bi), 0, 0
            return pl.BlockSpec((None, None, N_MOD, d), index)
        return [spec(k) for k in range(self.sub)]

    def out(self, width):
        return pl.BlockSpec((self.sub * TM, width), lambda i: (i, 0))

    def out_shape(self, width, dtype):
        return jax.ShapeDtypeStruct((self.b * self.n_vis * TM, width), dtype)

    def rows(self, k):
        return slice(k * TM, (k + 1) * TM)


def _flat(a):
    return a.reshape(-1, a.shape[-1])


def _ffn_kernel(*refs, j, dff, sub, ctx_tiles):
    if ctx_tiles is not None:
        lat_refs, ctx_refs, refs = refs[:sub], refs[sub:2 * sub], refs[sub:]
        xs = [jnp.where(ctx_tiles(k), c[...], x[...]) for k, (x, c) in enumerate(zip(lat_refs, ctx_refs))]
    else:
        xs = [x[...] for x in refs[:sub]]
    mod_refs = refs[sub:2 * sub]
    g_ref, wgu_ref, wd_ref, o_ref = refs[2 * sub:]
    g_pre, g_post = g_ref[2 * j:2 * j + 1, :], g_ref[2 * j + 1:2 * j + 2, :]
    mods = [[m[3 * j + i:3 * j + i + 1, :] for i in range(3)] for m in mod_refs]
    hs = [(_rms(x) * g_pre * (1.0 + m[1]) + m[0]).astype(BF16) for x, m in zip(xs, mods)]
    gus = [_dot(h, wgu_ref[...]) for h in hs]
    acts = [(_silu(gu[:, :dff]) * gu[:, dff:]).astype(BF16) for gu in gus]
    ys = [_dot(a, wd_ref[...]) for a in acts]
    for k, (x, m, y) in enumerate(zip(xs, mods, ys)):
        o_ref[k * TM:(k + 1) * TM, :] = x + 0.5 * m[2] * (_rms(y) * g_post)


def _ffn(plan, xz, mod, norm_g, wgu, wd, j, ctx=None):
    d = xz.shape[-1]
    dff = wd.shape[2]
    l = plan.layer
    if ctx is None:
        tok_specs, tok_args, ctx_tiles = plan.toks(d), [xz] * plan.sub, None
    else:
        tok_specs = [plan.tok_split(k, d, False) for k in range(plan.sub)] \
                    + [plan.tok_split(k, d, True) for k in range(plan.sub)]
        tok_args = [xz] * plan.sub + [ctx] * plan.sub
        ctx_tiles = lambda k: plan.is_ctx(pl.program_id(0), k)
    return pl.pallas_call(
        functools.partial(_ffn_kernel, j=j, dff=dff, sub=plan.sub, ctx_tiles=ctx_tiles),
        grid=plan.grid,
        in_specs=tok_specs + plan.mods(d)
                 + [_const_spec((None,) + norm_g.shape[1:], (l, 0, 0)),
                    _const_spec((None, None, d, 2 * dff), (l, j // 2, 0, 0)),
                    _const_spec((None, None, dff, d), (l, j // 2, 0, 0))],
        out_specs=plan.out(d),
        out_shape=plan.out_shape(d, F32),
        compiler_params=_params("parallel"),
        name=f"ffn{j}",
    )(*tok_args, *([mod] * plan.sub), norm_g, wgu, wd)


def _inproj_kernel(*refs, sub, layout):
    x_refs, mod_refs = refs[:sub], refs[sub:2 * sub]
    g_ref, w_ref, wg_ref = refs[2 * sub:2 * sub + 3]
    o_refs = refs[2 * sub + 3:]
    hs = [(_rms(x[...]) * g_ref[2:3, :] * (1.0 + m[4:5, :]) + m[3:4, :]).astype(BF16)
          for x, m in zip(x_refs, mod_refs)]
    perm = _time_perm(False)
    col = 0
    for o_ref, segments in zip(o_refs[:-1], layout):
        n = o_ref.shape[-1]
        for k, h in enumerate(hs):
            res = _dot(h, w_ref[:, col:col + n])
            parts, c0 = [], 0
            for width, post in segments:
                seg = res[:, c0:c0 + width]
                if post == "perm":
                    seg = _dot(perm, seg.astype(BF16))
                elif post is not None:
                    seg = post(seg)
                parts.append(seg.astype(o_ref.dtype))
                c0 += width
            o_ref[k * TM:(k + 1) * TM, :] = jnp.concatenate(parts, axis=1)
        col += n
    for k, h in enumerate(hs):
        o_refs[-1][k * TM:(k + 1) * TM, :] = _dot(h, wg_ref[...])


def _inproj(plan, xz, mod, norm_g, w_main, w_gate, layout):
    d = xz.shape[-1]
    ng = w_gate.shape[-1]
    l = plan.layer
    widths = [sum(w for w, _ in segments) for segments in layout]
    return pl.pallas_call(
        functools.partial(_inproj_kernel, sub=plan.sub, layout=layout),
        grid=plan.grid,
        in_specs=plan.toks(d) + plan.mods(d)
                 + [_const_spec((None,) + norm_g.shape[1:], (l, 0, 0)),
                    _const_spec((None, d, w_main.shape[-1]), (l, 0, 0)),
                    _const_spec((None, d, ng), (l, 0, 0))],
        out_specs=[plan.out(w) for w in widths] + [plan.out(ng)],
        out_shape=[plan.out_shape(w, BF16) for w in widths] + [plan.out_shape(ng, F32)],
        compiler_params=_params("parallel"),
        name="inproj",
    )(*([xz] * plan.sub), *([mod] * plan.sub), norm_g, w_main, w_gate)


N_MERGE_STREAMS = 12


def _merge_kernel(*refs, sub, ctx_tiles):
    x_refs, mod_refs = refs[:sub], refs[sub:2 * sub]
    tok = refs[2 * sub:(2 + N_MERGE_STREAMS) * sub]
    g_ref, mlg_ref, dng_ref, wb_ref, wo_ref, o_ref = refs[(2 + N_MERGE_STREAMS) * sub:]
    d = o_ref.shape[-1]
    f32 = lambda r: r[...].astype(F32)
    unperm = _time_perm(True)
    mixes = []
    for k in range(sub):
        mlf, mlb, mlo, lrf, lrb, lry, dlf, dlb, dcf, dcb, dnz, mg = (
            tok[s * sub + k] for s in range(N_MERGE_STREAMS))
        y_ml = _rms_heads(f32(mlf) + f32(mlb), ML_HEADS) * mlg_ref[...] * f32(mlo)
        y_lr = (_dot(unperm, lrf[...]) + _dot(unperm, lrb[...])) * f32(lry)
        dn_h = jnp.where(ctx_tiles(k), f32(dcf) + f32(dcb), f32(dlf) + f32(dlb))
        y_dn = _rms_heads(dn_h, DN_HEADS) * dng_ref[...] * f32(dnz)
        mix = None
        for n, y in enumerate((y_ml, y_lr, y_dn)):
            term = mg[:, n * d:(n + 1) * d].astype(F32) * _dot(y.astype(BF16), wb_ref[n])
            mix = term if mix is None else mix + term
        mixes.append(mix.astype(BF16))
    outs = [_dot(mix, wo_ref[...]) for mix in mixes]
    for k, out in enumerate(outs):
        o_ref[k * TM:(k + 1) * TM, :] = x_refs[k][...] + mod_refs[k][5:6, :] * (_rms(out) * g_ref[3:4, :])


def _merge(plan, xz, mod, norm_g, ml_hf, ml_hb, ml, lr_hf, lr_hb, lru, dn_lat, dn_all, dnz, mg,
           ml_g, dn_g, w_branch, w_out):
    d = xz.shape[-1]
    bw = w_branch.shape[2]
    l = plan.layer
    sub = range(plan.sub)
    whole = lambda a, w, c=0: (a, plan.toks(w, c))
    streams = [whole(ml_hf, bw), whole(ml_hb, bw), whole(ml, bw, ml.shape[-1] // bw - 1),
               whole(lr_hf, bw), whole(lr_hb, bw), whole(lru, bw, lru.shape[-1] // bw - 1),
               (dn_lat[0], [plan.tok_split(k, bw, False) for k in sub]),
               (dn_lat[1], [plan.tok_split(k, bw, False) for k in sub]),
               (dn_all[0], [plan.tok_ctx_rest(k, bw) for k in sub]),
               (dn_all[1], [plan.tok_ctx_rest(k, bw) for k in sub]),
               whole(dnz, bw), whole(mg, N_BRANCH * d)]
    assert len(streams) == N_MERGE_STREAMS
    tok_specs = [sp for _, specs in streams for sp in specs]
    tok_args = [a for a, _ in streams for _ in sub]
    return pl.pallas_call(
        functools.partial(_merge_kernel, sub=plan.sub, ctx_tiles=lambda k: plan.is_ctx(pl.program_id(0), k)),
        grid=plan.grid,
        in_specs=plan.toks(d) + plan.mods(d) + tok_specs
                 + [_const_spec((None,) + norm_g.shape[1:], (l, 0, 0)),
                    _const_spec((None, 1, bw), (l, 0, 0)),
                    _const_spec((None, 1, bw), (l, 0, 0)),
                    _const_spec((None, N_BRANCH, bw, d), (l, 0, 0, 0)),
                    _const_spec((None, d, d), (l, 0, 0))],
        out_specs=plan.out(d),
        out_shape=plan.out_shape(d, F32),
        compiler_params=_params("parallel"),
        name="merge",
    )(*([xz] * plan.sub), *([mod] * plan.sub), *tok_args, norm_g, ml_g, dn_g, w_branch, w_out)


SEG = TM // V7X_SUBLANES


def _time_perm(inverse):
    ri = lax.broadcasted_iota(jnp.int32, (TM, TM), 0)
    ci = lax.broadcasted_iota(jnp.int32, (TM, TM), 1)
    r, t = (ci, ri) if inverse else (ri, ci)
    return jnp.where(t == (r % V7X_SUBLANES) * SEG + r // V7X_SUBLANES, 1.0, 0.0).astype(BF16)


def _conv4_perm(x, before1, before2, after1, w):
    sub = lax.broadcasted_iota(jnp.int32, (V7X_SUBLANES, 1), 0)
    vrow = lambda i: x[V7X_SUBLANES * i:V7X_SUBLANES * (i + 1)]
    m1_edge = jnp.where(sub == 0, before1, pltpu.roll(vrow(SEG - 1), 1, axis=0))
    m2_edge = jnp.where(sub == 0, before2, pltpu.roll(vrow(SEG - 2), 1, axis=0))
    p1_edge = jnp.where(sub == V7X_SUBLANES - 1, after1, pltpu.roll(vrow(0), V7X_SUBLANES - 1, axis=0))
    x_m1 = jnp.concatenate([m1_edge, x[:-V7X_SUBLANES]], axis=0)
    x_m2 = jnp.concatenate([m2_edge, m1_edge, x[:-2 * V7X_SUBLANES]], axis=0)
    x_p1 = jnp.concatenate([x[V7X_SUBLANES:], p1_edge], axis=0)
    taps = {-2: x_m2, -1: x_m1, 0: x, 1: x_p1}
    acc = None
    for j in range(CONV_W):
        term = taps[j - CONV_LEFT] * w[j:j + 1]
        acc = term if acc is None else acc + term
    return acc


def _tile_scan(a, b, h0, rev):
    vrow = lambda x, i: x[V7X_SUBLANES * i:V7X_SUBLANES * (i + 1)]
    h = jnp.zeros_like(vrow(a, 0))
    p = jnp.ones_like(h)
    hs, ps = [None] * SEG, [None] * SEG
    for i in (range(SEG - 1, -1, -1) if rev else range(SEG)):
        ai = vrow(a, i)
        h = ai * h + vrow(b, i)
        p = ai * p
        hs[i], ps[i] = h, p
    carry = h0
    enter = [None] * V7X_SUBLANES
    for s in (range(V7X_SUBLANES - 1, -1, -1) if rev else range(V7X_SUBLANES)):
        enter[s] = carry
        carry = p[s:s + 1] * carry + h[s:s + 1]
    enter = jnp.concatenate(enter, axis=0)
    return jnp.concatenate([hi + pi * enter for hi, pi in zip(hs, ps)], axis=0), carry


def _lru_kernel(x_ref, cw_ref, cb_ref, w_ref, bias_ref, lam_ref, of_ref, ob_ref, *, n_lat, n_ctx):
    t, c = TM, x_ref.shape[-1]
    n_tiles = n_lat + n_ctx
    halo = 2 * V7X_SUBLANES

    def load_conv(tile):
        r0 = pl.multiple_of(tile * t, t)
        first = jnp.logical_or(tile == 0, tile == n_lat)
        last = jnp.logical_or(tile == n_lat - 1, tile == n_tiles - 1)
        x = x_ref[pl.ds(r0, t), :].astype(F32)
        p0 = pl.multiple_of(jnp.maximum(r0 - halo, 0), halo)
        n0 = pl.multiple_of(jnp.minimum(r0 + t, n_tiles * t - halo), halo)
        prev = jnp.where(first, 0.0, x_ref[pl.ds(p0, halo), :].astype(F32))
        nxt = jnp.where(last, 0.0, x_ref[pl.ds(n0, halo), :].astype(F32))
        before1, before2 = prev[halo - 1:halo], prev[V7X_SUBLANES - 1:V7X_SUBLANES]
        return r0, _conv4_perm(x, before1, before2, nxt[0:1], cw_ref[...]) + cb_ref[...]

    def direction(d, tile, h0, o_ref):
        r0, xc = load_conv(tile)
        z = _dot(xc.astype(BF16), w_ref[d]) + bias_ref[d]
        r, i = _sigmoid(z[:, :c]), _sigmoid(z[:, c:])
        la = (-LRU_C * _softplus(-lam_ref[d])) * r
        a = jnp.exp(la)
        bx = jnp.sqrt(jnp.tanh(-la) * (1.0 + a * a)) * (i * xc)
        h, carry = _tile_scan(a, bx, h0, d == 1)
        o_ref[pl.ds(r0, t), :] = h.astype(o_ref.dtype)
        return carry

    def step(s, carry):
        hf, hb = carry
        hf = direction(0, jnp.where(s < n_ctx, n_lat + s, s - n_ctx), hf, of_ref)
        hb = direction(1, n_tiles - 1 - s, hb, ob_ref)
        return hf, hb

    zero = jnp.zeros((1, c), F32)
    lax.fori_loop(0, n_tiles, step, (zero, zero))


def _lru(lru, conv_w, conv_b, w_gates, b_gates, lam, l, n_lat, n_ctx):
    b, lt, _ = lru.shape
    c = conv_w.shape[-1]
    seq = pl.BlockSpec((None, lt, c), lambda bi: (bi, 0, 0))
    return pl.pallas_call(
        functools.partial(_lru_kernel, n_lat=n_lat, n_ctx=n_ctx),
        grid=(b,),
        in_specs=[seq,
                  _const_spec((None, CONV_W, c), (l, 0, 0)),
                  _const_spec((None, 1, c), (l, 0, 0)),
                  _const_spec((None, 2, c, 2 * c), (l, 0, 0, 0)),
                  _const_spec((None, 2, 1, 2 * c), (l, 0, 0, 0)),
                  _const_spec((None, 2, 1, c), (l, 0, 0, 0))],
        out_specs=[seq, seq],
        out_shape=[jax.ShapeDtypeStruct((b, lt, c), BF16)] * 2,
        compiler_params=_params("parallel"),
        name="lru",
    )(lru, conv_w, conv_b, w_gates, b_gates, lam)


def _mlstm_kernel(qf_ref, kf_ref, vf_ref, qb_ref, kb_ref, vb_ref, grf_ref, grb_ref, br_ref,
                  hf_ref, hb_ref, c_ref, m_ref):
    @pl.when(pl.program_id(1) == 0)
    def _():
        c_ref[...] = jnp.zeros(c_ref.shape, F32)
        m_ref[...] = jnp.zeros(m_ref.shape, F32)

    nrow, tc, nqk = qf_ref.shape
    dk = nqk // ML_HEADS
    dv = vf_ref.shape[-1] // ML_HEADS
    nh = ML_HEADS
    ri = lax.broadcasted_iota(jnp.int32, (tc, tc), 0)
    ci = lax.broadcasted_iota(jnp.int32, (tc, tc), 1)
    causal = (ci <= ri, ci >= ri)
    lane_head = lax.broadcasted_iota(jnp.int32, (1, nqk), 1) // dk
    ones = jnp.ones((tc, dv), BF16)
    dirs = ((qf_ref, kf_ref, vf_ref, grf_ref), (qb_ref, kb_ref, vb_ref, grb_ref))
    seqs = [(r, d) for r in range(nrow) for d in range(2)]

    rows, v_rows, ws_rows, decs = [], [], [], []
    for r, d in seqs:
        rev = d == 1
        gr = dirs[d][3][r].T + br_ref[...]
        i_r = gr[nh * d:nh * (d + 1)]
        b_r = _cumsum_groups(-_softplus(-gr[nh * (2 + d):nh * (3 + d)]), 1, tc, rev)
        m_prev = m_ref[r, nh * d:nh * (d + 1), 0:1]
        m_t = b_r + jnp.maximum(m_prev, _cummax_groups(i_r - b_r, 1, tc, rev))
        b_end = b_r[:, 0:1] if rev else b_r[:, tc - 1:tc]
        lws = b_end - b_r + i_r
        m_new = jnp.maximum(b_end + m_prev, jnp.max(lws, axis=1, keepdims=True))
        rows += [b_r - m_t, jnp.exp(b_r + m_prev - m_t), jnp.exp(-m_t)]
        v_rows.append(b_r - i_r)
        ws_rows.append(jnp.exp(lws - m_new))
        decs.append(jnp.exp(b_end + m_prev - m_new))
        m_ref[r, nh * d:nh * (d + 1), :] = jnp.broadcast_to(m_new, (nh, m_ref.shape[-1]))
    cols = jnp.concatenate(rows, axis=0).T

    chains = [(s, h) for s in range(len(seqs)) for h in range(nh)]
    q_all = [dirs[d][0][r] * (dk ** -0.5) for r, d in seqs]
    k_all = [dirs[d][1][r] for r, d in seqs]
    v_all = [dirs[d][2][r] for r, d in seqs]
    kt_all = [k.astype(F32).T for k in k_all]
    c_all = [c_ref[r, d] for r, d in seqs]
    cb_all = [c.astype(BF16) for c in c_all]
    col = lambda s, j, h: cols[:, (3 * s + j) * nh + h:(3 * s + j) * nh + h + 1]
    qh = [jnp.where(lane_head == h, q_all[s], jnp.zeros_like(q_all[s])) for s, h in chains]
    vp = [jnp.concatenate([v_all[s][:, h * dv:(h + 1) * dv], ones], axis=1) for s, h in chains]
    s_raw = [_dot_nt(q, k_all[s]) for q, (s, h) in zip(qh, chains)]
    p = [(sr * jnp.exp(jnp.where(causal[seqs[s][1]], col(s, 0, h) - v_rows[s][h:h + 1], NEG))).astype(BF16)
         for sr, (s, h) in zip(s_raw, chains)]
    num = [_dot(pc, vc) + col(s, 1, h) * _dot(q, cb_all[s]) for pc, vc, q, (s, h) in zip(p, vp, qh, chains)]
    outs = [nm[:, :dv] / jnp.maximum(jnp.abs(nm[:, dv:]), col(s, 2, h)) for nm, (s, h) in zip(num, chains)]
    for vc, (s, h) in zip(vp, chains):
        r, d = seqs[s]
        kw = (kt_all[s][h * dk:(h + 1) * dk] * ws_rows[s][h:h + 1]).astype(BF16)
        c_ref[r, d, h * dk:(h + 1) * dk, :] = decs[s][h:h + 1] * c_all[s][h * dk:(h + 1) * dk] + _dot(kw, vc)
    for s, (r, d) in enumerate(seqs):
        o_ref = (hf_ref, hb_ref)[d]
        o_ref[r] = jnp.concatenate(outs[s * nh:(s + 1) * nh], axis=1).astype(o_ref.dtype)


def _mlstm(ml, gates, bias_r, l, n_lat, n_ctx):
    b, lt, _ = ml.shape
    ng = gates.shape[-1]
    n_tiles = n_lat + n_ctx
    dqk = ml.shape[-1] // 6
    fwd = lambda s: jnp.where(s < n_ctx, n_lat + s, s - n_ctx)
    bwd = lambda s: n_tiles - 1 - s
    rpb = ML_ROWS if b % ML_ROWS == 0 else 1

    def specs(tile):
        return [pl.BlockSpec((rpb, TM, dqk), lambda bi, s: (bi, tile(s), 0)),
                pl.BlockSpec((rpb, TM, dqk), lambda bi, s: (bi, tile(s), 1)),
                pl.BlockSpec((rpb, TM, 2 * dqk), lambda bi, s: (bi, tile(s), 1))]

    row = lambda tile: pl.BlockSpec((rpb, TM, ng), lambda bi, s: (bi, tile(s), 0))
    out = lambda tile: pl.BlockSpec((rpb, TM, 2 * dqk), lambda bi, s: (bi, tile(s), 0))
    return pl.pallas_call(
        _mlstm_kernel,
        grid=(b // rpb, n_tiles),
        in_specs=specs(fwd) + specs(bwd) + [row(fwd), row(bwd), _const_spec((None, ng, 1), (l, 0, 0))],
        out_specs=[out(fwd), out(bwd)],
        out_shape=[jax.ShapeDtypeStruct((b, lt, 2 * dqk), BF16)] * 2,
        scratch_shapes=[pltpu.VMEM((rpb, 2, dqk, 2 * (2 * dqk // ML_HEADS)), F32),
                        pltpu.VMEM((rpb, 2 * ML_HEADS, V7X_LANES), F32)],
        compiler_params=_params("parallel", "arbitrary"),
        name="mlstm",
    )(ml, ml, ml, ml, ml, ml, gates, gates, bias_r)


INV_BASE = 8


def _inv_masks(ri, ci):
    blk = lambda s: (ri // s) == (ci // s)
    as_bf = lambda m: jnp.where(m, 1.0, 0.0).astype(BF16)
    masks, s = [as_bf(blk(INV_BASE))], INV_BASE
    while s < CHUNK:
        masks.append(as_bf(jnp.logical_and(blk(2 * s), jnp.logical_not(blk(s)))))
        s *= 2
    return masks


def _tri_inv_many(a_list, eye, same, masks):
    nblk = a_list[0].shape[0] // CHUNK
    compact = lambda m: functools.reduce(lambda u, v: u + v, [m[i * CHUNK:(i + 1) * CHUNK] for i in range(nblk)])
    spread = lambda c: jnp.concatenate([c] * nblk, axis=0) * same
    eye_c = compact(eye)
    ps = [a * masks[0] for a in a_list]
    pcs = [compact(p) for p in ps]
    xcs = [eye_c - pc for pc in pcs]
    for _ in range(INV_BASE.bit_length() - 2):
        pcs = [_dot(pc, p).astype(BF16) for pc, p in zip(pcs, ps)]
        ps = [spread(pc) for pc in pcs]
        xcs = [_dot(xc, eye + p).astype(BF16) for xc, p in zip(xcs, ps)]
    for m in masks[1:]:
        ys = [_dot(xc, a * m).astype(BF16) for xc, a in zip(xcs, a_list)]
        xcs = [xc - _dot(y, spread(xc)).astype(BF16) for y, xc in zip(ys, xcs)]
    return [spread(xc) for xc in xcs]


def _dn_prep_kernel(xl_ref, xp_ref, xn_ref, xc_ref, grow_ref, cw_ref, arow_ref,
                    u_ref, w_ref, qe_ref, kd_ref, qk_ref, eg_ref, *, n_lat_groups):
    g = pl.program_id(1)
    is_ctx = g >= n_lat_groups
    width = xl_ref.shape[-1]
    dh = width // (3 * DN_HEADS)
    n = DN_HEADS * CHUNK
    halo = 2 * V7X_SUBLANES
    x = jnp.where(is_ctx, xc_ref[...], xl_ref[...]).reshape(DN_GROUP * CHUNK, width)
    no_prev = jnp.logical_or(is_ctx, g == 0)
    no_next = jnp.logical_or(is_ctx, g == n_lat_groups - 1)
    prev8 = jnp.where(no_prev, 0.0, xp_ref[CHUNK - halo:, :].astype(F32)[V7X_SUBLANES:])
    nxt8 = jnp.where(no_next, 0.0, xn_ref[:halo, :].astype(F32)[:V7X_SUBLANES])
    xc = _silu(_conv4(x, prev8, nxt8, cw_ref[...]))

    ri = lax.broadcasted_iota(jnp.int32, (n, n), 0)
    ci = lax.broadcasted_iota(jnp.int32, (n, n), 1)
    same = (ri // CHUNK) == (ci // CHUNK)
    eye = jnp.where(ri == ci, 1.0, 0.0).astype(BF16)
    incl =(jnp.logical_and(same, ci <= ri), jnp.logical_and(same, ci >= ri))
    strict = (jnp.logical_and(same, ci < ri), jnp.logical_and(same, ci > ri))
    masks = _inv_masks(ri, ci)

    qn, kn, v, kk, qk = [], [], [], [], []
    for i in range(DN_GROUP):
        xi = xc[i * CHUNK:(i + 1) * CHUNK]
        q, k, vi = (_stack_heads(xi[:, j * DN_HEADS * dh:(j + 1) * DN_HEADS * dh], DN_HEADS) for j in range(3))
        qn.append(q * lax.rsqrt(jnp.sum(q * q, axis=-1, keepdims=True) + EPS) * (dh ** -0.5))
        kn.append(k * lax.rsqrt(jnp.sum(k * k, axis=-1, keepdims=True) + EPS))
        v.append(vi)
        kb = kn[i].astype(BF16)
        kk.append(_dot_nt(kb, kb))
        qk.append(_dot_nt(qn[i].astype(BF16), kb))

    systems = [(i, d) for i in range(DN_GROUP) for d in range(2)]
    rows, cs_rows = [], []
    for i, d in systems:
        raw = grow_ref[i]
        beta = _sigmoid(raw[d:d + 1])
        g_r = -jnp.exp(arow_ref[d, 0:1, :]) * _softplus(raw[2 + d:3 + d] + arow_ref[d, 1:2, :])
        cs = _cumsum_groups(g_r, 1, CHUNK, d == 1)
        tot = cs + _cumsum_groups(g_r, 1, CHUNK, d == 0) - g_r
        eg = jnp.exp(cs)
        rows += [cs, beta, eg, jnp.exp(tot - cs), beta * eg]
        cs_rows.append(cs)
        eg_ref[d, i] = jnp.exp(tot)
    n_col = len(rows) // len(systems)
    cols = jnp.concatenate(rows, axis=0).T

    a_list, rhs = [], []
    for s, (i, d) in enumerate(systems):
        cs_c, beta_c, eg_c, ekd_c, beg_c = (cols[:, n_col * s + j:n_col * s + j + 1] for j in range(n_col))
        gam = jnp.exp(jnp.where(incl[d], cs_c - cs_rows[s], NEG))
        a_list.append((jnp.where(strict[d], kk[i], 0.0) * gam * beta_c).astype(BF16))
        rhs.append(jnp.concatenate([beta_c * v[i], beg_c * kn[i]], axis=1).astype(BF16))
        qk_ref[d, i] = (qk[i] * gam).astype(qk_ref.dtype)
        qe_ref[d, i] = (qn[i] * eg_c).astype(qe_ref.dtype)
        kd_ref[d, i] = (kn[i] * ekd_c).astype(kd_ref.dtype)
    same_bf = jnp.where(same, 1.0, 0.0).astype(BF16)
    for (i, d), tinv, r in zip(systems, _tri_inv_many(a_list, eye, same_bf, masks), rhs):
        sol = _dot(tinv, r)
        u_ref[d, i] = sol[:, :dh].astype(u_ref.dtype)
        w_ref[d, i] = sol[:, dh:].astype(w_ref.dtype)


def _dn_prep(x_cols, x_rows, grow, conv_w, arow, l):
    b, nc, _, width = x_rows.shape
    n_cols = x_cols.shape[1]
    n_groups = nc // DN_GROUP
    n_lat_groups = n_cols // DN_GROUP
    assert n_groups == n_lat_groups + 1
    n = DN_HEADS * CHUNK
    dh = width // (3 * DN_HEADS)
    lat_g = lambda g: jnp.minimum(g, n_lat_groups - 1)
    out = lambda w, dt: (jax.ShapeDtypeStruct((b, 2, nc, n, w), dt),
                         pl.BlockSpec((None, 2, DN_GROUP, n, w), lambda bi, g: (bi, 0, g, 0, 0)))
    outs = [out(dh, BF16)] * 4 + [out(n, BF16)]
    outs.append((jax.ShapeDtypeStruct((b, 2, nc, 1, n), F32),
                 pl.BlockSpec((None, 2, DN_GROUP, 1, n), lambda bi, g: (bi, 0, g, 0, 0))))
    chunk = lambda idx: pl.BlockSpec((None, None, CHUNK, width), lambda bi, g: (bi, idx(g), 0, 0))
    return pl.pallas_call(
        functools.partial(_dn_prep_kernel, n_lat_groups=n_lat_groups),
        grid=(b, n_groups),
        in_specs=[pl.BlockSpec((None, DN_GROUP, CHUNK, width), lambda bi, g: (bi, lat_g(g), 0, 0)),
                  chunk(lambda g: jnp.maximum(lat_g(g) * DN_GROUP - 1, 0)),
                  chunk(lambda g: jnp.minimum(lat_g(g) * DN_GROUP + DN_GROUP, n_cols - 1)),
                  pl.BlockSpec((None, DN_GROUP, CHUNK, width), lambda bi, g: (bi, n_groups - 1, 0, 0)),
                  pl.BlockSpec((None, DN_GROUP, 4, n), lambda bi, g: (bi, g, 0, 0)),
                  _const_spec((None, CONV_W, width), (l, 0, 0)),
                  _const_spec((None, 2, 2, n), (l, 0, 0, 0))],
        out_specs=[o[1] for o in outs],
        out_shape=[o[0] for o in outs],
        compiler_params=_params("parallel", "parallel"),
        name="dn_prep",
    )(x_cols, x_cols, x_cols, x_rows, grow, conv_w, arow)


def _dn_scan_kernel(*refs):
    ins, (of_ref, ob_ref, s_ref) = refs[:12], refs[12:]

    @pl.when(pl.program_id(1) == 0)
    def _():
        s_ref[...] = jnp.zeros(s_ref.shape, F32)

    rows = [slice(h * CHUNK, (h + 1) * CHUNK) for h in range(DN_HEADS)]
    seqs = [(r, d) for r in range(of_ref.shape[0]) for d in range(2)]
    get = lambda j, r, d: ins[6 * d + j][r]
    sb = [[s_ref[r, d, h].astype(BF16) for h in range(DN_HEADS)] for r, d in seqs]
    ws = [jnp.concatenate([_dot(get(1, r, d)[rows[h]], sb[i][h]) for h in range(DN_HEADS)], axis=0)
          for i, (r, d) in enumerate(seqs)]
    qs = [jnp.concatenate([_dot(get(2, r, d)[rows[h]], sb[i][h]) for h in range(DN_HEADS)], axis=0)
          for i, (r, d) in enumerate(seqs)]
    vnew = [(get(0, r, d).astype(F32) - w).astype(BF16) for w, (r, d) in zip(ws, seqs)]
    o = [q + _dot(get(4, r, d), vn) for q, vn, (r, d) in zip(qs, vnew, seqs)]
    for vn, (r, d) in zip(vnew, seqs):
        kd, eg = get(3, r, d), get(5, r, d)
        for h in range(DN_HEADS):
            s_ref[r, d, h] = (eg[:, h * CHUNK:h * CHUNK + 1] * s_ref[r, d, h]
                              + _dot_tn(kd[rows[h]], vn[rows[h]]))
    for oi, (r, d) in zip(o, seqs):
        o_ref = (of_ref, ob_ref)[d]
        o_ref[r] = _unstack_heads(oi, DN_HEADS).astype(o_ref.dtype)


def _dn_scan(prep, n_lat_chunks):
    b, _, nc, _, dh = prep[0].shape
    ow = DN_HEADS * dh
    n_ctx_chunks = nc - n_lat_chunks
    fwd = lambda s: jnp.where(s < n_ctx_chunks, n_lat_chunks + s, s - n_ctx_chunks)
    bwd = lambda s: nc - 1 - s

    rpb = DN_SCAN_ROWS if b % DN_SCAN_ROWS == 0 else 1

    def spec(a, d, chunk):
        return pl.BlockSpec((rpb, None, None) + a.shape[3:], lambda bi, s: (bi, d, chunk(s), 0, 0))

    out = lambda chunk: pl.BlockSpec((rpb, None, CHUNK, ow), lambda bi, s: (bi, chunk(s), 0, 0))
    return pl.pallas_call(
        _dn_scan_kernel,
        grid=(b // rpb, nc),
        in_specs=[spec(a, 0, fwd) for a in prep] + [spec(a, 1, bwd) for a in prep],
        out_specs=[out(fwd), out(bwd)],
        out_shape=[jax.ShapeDtypeStruct((b, nc, CHUNK, ow), BF16)] * 2,
        scratch_shapes=[pltpu.VMEM((rpb, 2, DN_HEADS, dh, dh), F32)],
        compiler_params=_params("parallel", "arbitrary"),
        name="dn_scan",
    )(*prep, *prep)


def _block_diag(w):
    n, i, j = w.shape
    return jnp.einsum('nij,nm->nimj', w, jnp.eye(n, dtype=w.dtype)).reshape(n * i, n * j)


def _dn_gate_rows(raw, n_lat):
    b = raw.shape[0]
    lat = raw[:, :n_lat].reshape(b, n_lat // CHUNK, CHUNK, 4, DN_HEADS)
    ctx = raw[:, n_lat:].reshape(b, -1, CHUNK, 4, DN_HEADS)
    row = jnp.concatenate([lat.transpose(0, 2, 3, 4, 1), ctx.transpose(0, 1, 3, 4, 2)], axis=1)
    return row.reshape(b, row.shape[1], 4, DN_HEADS * CHUNK)


def kernel(x, c, ctx, c_ctx, w_mod, b_mod, norm_g, ffn_w_gu, ffn_w_down, w_in, ml_gate_b, ml_norm_g,
           lru_conv_w, lru_conv_b, lru_w_a, lru_b_a, lru_w_x, lru_b_x, lru_lambda, dn_conv_w,
           dn_a_log, dn_dt_bias, dn_norm_g, w_branch, w_out):
    b, n_lat, d = x.shape
    n_ctx = ctx.shape[1]
    depth = w_mod.shape[0]
    bw = w_branch.shape[2]
    assert n_lat == CHUNK * CHUNK and n_lat % TM == 0 and n_ctx % TM == 0 and n_ctx == CHUNK * DN_GROUP
    n_lat_tiles, n_ctx_tiles = n_lat // TM, n_ctx // TM
    n_tiles = n_lat_tiles + n_ctx_tiles

    ctx_row = b
    n_rows = -(-(b + 1) // V7X_SUBLANES) * V7X_SUBLANES
    cc = jnp.zeros((n_rows, d), F32).at[:b].set(c).at[b].set(c_ctx)
    mod = _mod_table(cc, w_mod, b_mod).reshape(depth, n_rows, N_MOD, d)

    dqk = bw // 2
    edges = [0]
    for wdt in (dqk, dqk, bw, bw, N_GATES, bw, bw, bw, bw, bw, bw, N_GATES, N_BRANCH * d):
        edges.append(edges[-1] + wdt)
    piece = lambda i, j: w_in[:, :, edges[i]:edges[j]]
    layout = (((2 * bw, None), (bw, _sigmoid)),
              ((bw, "perm"), (bw, _gelu_tanh)),
              ((3 * bw, None),),
              ((bw, _silu),),
              ((N_BRANCH * d, _sigmoid),))
    w_main = jnp.concatenate([piece(0, 4), piece(5, 7), piece(7, 11), piece(12, 13)], axis=-1).astype(BF16)
    w_gate = jnp.concatenate([piece(4, 5), piece(11, 12)], axis=-1).astype(BF16)
    wgu, wdn, wbr, wout = (_to_bf16(w) for w in (ffn_w_gu, ffn_w_down, w_branch, w_out))
    ml_bias = jnp.concatenate([ml_gate_b.reshape(depth, N_GATES), jnp.zeros((depth, N_GATES), F32)], axis=-1)
    ml_bias_r = ml_bias[:, :, None]
    ml_g = ml_norm_g[:, None, :]
    dn_g = jnp.tile(dn_norm_g, (1, DN_HEADS))[:, None, :]
    lru_w = jnp.stack([jnp.concatenate([jax.vmap(_block_diag)(lru_w_a[:, dd]), jax.vmap(_block_diag)(lru_w_x[:, dd])],
                                       axis=-1) for dd in range(2)], axis=1).astype(BF16)
    lru_b = jnp.concatenate([lru_b_a, lru_b_x], axis=-1)[:, :, None, :]
    lru_lam = lru_lambda[:, :, None, :]
    lru_cb = lru_conv_b[:, None, :]
    dn_arow = jnp.repeat(jnp.stack([dn_a_log, dn_dt_bias], axis=2), CHUNK, axis=3)

    rows3 = lambda a: a.reshape(b, n_lat + n_ctx, a.shape[-1])
    for l in range(depth):
        full = _TilePlan(b, n_tiles, n_lat_tiles, n_tiles, ctx_row, l)
        if l == 0:
            xz = _ffn(full, _flat(x), mod, norm_g, wgu, wdn, 0, ctx=_flat(ctx))
        else:
            xz = _ffn(full, xz, mod, norm_g, wgu, wdn, 0)
        ml, lru, dnqkv, dnz, mg, gates = _inproj(full, xz, mod, norm_g, w_main, w_gate, layout)
        gates = rows3(gates)
        ml_hf, ml_hb = _mlstm(rows3(ml), gates, ml_bias_r, l, n_lat_tiles, n_ctx_tiles)
        lr_hf, lr_hb = _lru(rows3(lru), lru_conv_w, lru_cb, lru_w, lru_b, lru_lam, l, n_lat_tiles, n_ctx_tiles)
        n_cols = n_lat // CHUNK
        dq = rows3(dnqkv)
        x_rows = dq.reshape(b, -1, CHUNK, dq.shape[-1])
        x_cols = x_rows.swapaxes(1, 2)
        prep = _dn_prep(x_cols, x_rows, _dn_gate_rows(gates[:, :, N_GATES:], n_lat), dn_conv_w, dn_arow, l)
        dn_all = _dn_scan(prep, n_cols)
        dn_lat = [_flat(o[:, :n_cols].swapaxes(1, 2)) for o in dn_all]
        xz = _merge(full, xz, mod, norm_g, _flat(ml_hf), _flat(ml_hb), ml, _flat(lr_hf), _flat(lr_hb), lru,
                    dn_lat, [_flat(o) for o in dn_all], dnz, mg, ml_g, dn_g, wbr, wout)
        visit = n_lat_tiles if l == depth - 1 else n_tiles
        xz = _ffn(_TilePlan(b, n_tiles, n_lat_tiles, visit, ctx_row, l), xz, mod, norm_g, wgu, wdn, 2)
    return xz.reshape(b, n_lat, d)
```

```python
import functools
import math

import jax
import jax.numpy as jnp
from jax import lax
from jax.experimental import pallas as pl
from jax.experimental.pallas import tpu as pltpu

F32 = jnp.float32
BF16 = jnp.bfloat16

EPS = 1e-6
N_MOD = 9
N_BRANCH = 3
CONV_W = 4
CONV_LEFT = 2
ML_HEADS = 4
LRU_C = 8.0
DN_HEADS = 4
CHUNK = 64
DN_GROUP = 4
DN_SCAN_ROWS = 4
ML_ROWS = 2
N_GATES = 16

V7X_SUBLANES = 8
V7X_LANES = 128
V7X_VMEM_BYTES = 64 * 1024 * 1024
VMEM_LIMIT = V7X_VMEM_BYTES - 8 * 1024 * 1024

TM = 256
NEG = -1e30


def _sigmoid(x):
    return 0.5 * jnp.tanh(0.5 * x) + 0.5


def _silu(x):
    return x * _sigmoid(x)


def _softplus(x):
    return jnp.maximum(x, 0.0) + jnp.log(1.0 + jnp.exp(-jnp.abs(x)))


def _rms(x):
    return x * lax.rsqrt(jnp.mean(x * x, axis=-1, keepdims=True) + EPS)


def _rms_heads(x, n_heads):
    hd = x.shape[-1] // n_heads
    return jnp.concatenate([_rms(x[:, h * hd:(h + 1) * hd]) for h in range(n_heads)], axis=-1)


def _gelu_tanh(x):
    return 0.5 * x * (1.0 + jnp.tanh(math.sqrt(2.0 / math.pi) * (x + 0.044715 * (x * x * x))))


def _dot(a, b):
    return jnp.dot(a, b, preferred_element_type=F32)


def _dot_nt(a, b):
    return lax.dot_general(a, b, (((1,), (1,)), ((), ())), preferred_element_type=F32)


def _dot_tn(a, b):
    return lax.dot_general(a, b, (((0,), (0,)), ((), ())), preferred_element_type=F32)


def _params(*sem):
    return pltpu.CompilerParams(dimension_semantics=sem, vmem_limit_bytes=VMEM_LIMIT)


def _const_spec(block, index):
    return pl.BlockSpec(block, lambda *_: index, pipeline_mode=pl.Buffered(1))


def _stack_heads(x, n):
    w = x.shape[1] // n
    return jnp.concatenate([x[:, h * w:(h + 1) * w] for h in range(n)], axis=0)


def _unstack_heads(x, n):
    t = x.shape[0] // n
    return jnp.concatenate([x[h * t:(h + 1) * t] for h in range(n)], axis=1)


def _cumsum_groups(x, axis, period, rev):
    n = x.shape[axis]
    idx = lax.broadcasted_iota(jnp.int32, x.shape, axis) % period
    sh = 1
    while sh < period:
        if rev:
            x = x + jnp.where(idx < period - sh, pltpu.roll(x, n - sh, axis=axis), 0.0)
        else:
            x = x + jnp.where(idx >= sh, pltpu.roll(x, sh, axis=axis), 0.0)
        sh *= 2
    return x


def _cummax_groups(x, axis, period, rev):
    n = x.shape[axis]
    idx = lax.broadcasted_iota(jnp.int32, x.shape, axis) % period
    sh = 1
    while sh < period:
        if rev:
            x = jnp.maximum(x, jnp.where(idx < period - sh, pltpu.roll(x, n - sh, axis=axis), NEG))
        else:
            x = jnp.maximum(x, jnp.where(idx >= sh, pltpu.roll(x, sh, axis=axis), NEG))
        sh *= 2
    return x


def _shift_rows(xb, prev8, nxt8, off):
    t = xb.shape[0]
    ri = lax.broadcasted_iota(jnp.int32, (t, t), 0)
    ci = lax.broadcasted_iota(jnp.int32, (t, t), 1)
    shifted = _dot(jnp.where(ci == ri + off, 1.0, 0.0).astype(BF16), xb)
    row8 = lax.broadcasted_iota(jnp.int32, (V7X_SUBLANES, 1), 0)
    if off < 0:
        edge = jnp.where(row8 < -off, pltpu.roll(prev8, -off, axis=0), shifted[:V7X_SUBLANES])
        return jnp.concatenate([edge, shifted[V7X_SUBLANES:]], axis=0)
    edge = jnp.where(row8 >= V7X_SUBLANES - off, pltpu.roll(nxt8, V7X_SUBLANES - off, axis=0),
                     shifted[t - V7X_SUBLANES:])
    return jnp.concatenate([shifted[:t - V7X_SUBLANES], edge], axis=0)


def _conv4(xb, prev8, nxt8, w):
    acc = xb.astype(F32) * w[CONV_LEFT:CONV_LEFT + 1]
    for j in range(CONV_W):
        if j != CONV_LEFT:
            acc = acc + _shift_rows(xb, prev8, nxt8, j - CONV_LEFT) * w[j:j + 1]
    return acc


def _mod_kernel(c_ref, w_ref, b_ref, o_ref):
    s = _silu(c_ref[...]).astype(BF16)
    o_ref[...] = _dot(s, w_ref[...].astype(BF16)) + b_ref[...]


def _mod_table(cc, w_mod, b_mod):
    depth, d, nd = w_mod.shape
    r = cc.shape[0]
    tn = nd // 4
    return pl.pallas_call(
        _mod_kernel,
        grid=(depth, nd // tn),
        in_specs=[pl.BlockSpec((r, d), lambda l, j: (0, 0)),
                  pl.BlockSpec((None, d, tn), lambda l, j: (l, 0, j)),
                  pl.BlockSpec((None, 1, tn), lambda l, j: (l, 0, j))],
        out_specs=pl.BlockSpec((None, r, tn), lambda l, j: (l, 0, j)),
        out_shape=jax.ShapeDtypeStruct((depth, r, nd), F32),
        compiler_params=_params("parallel", "parallel"),
        name="mod_table",
    )(cc, w_mod, b_mod.reshape(depth, 1, nd))


class _TilePlan:
    def __init__(self, b, n_tiles, n_lat_tiles, n_vis, ctx_row, layer):
        self.b, self.n_tiles, self.n_lat_tiles, self.n_vis = b, n_tiles, n_lat_tiles, n_vis
        self.ctx_row, self.layer = ctx_row, layer
        self.sub = 2 if (b * n_vis) % 2 == 0 else 1
        self.grid = (b * n_vis // self.sub,)

    def _tile(self, i, k):
        tid = i * self.sub + k
        return tid // self.n_vis, tid % self.n_vis

    def tok(self, k, width, col=0):
        def index(i):
            bi, t = self._tile(i, k)
            return bi * self.n_tiles + t, col
        return pl.BlockSpec((TM, width), index)

    def toks(self, width, col=0):
        return [self.tok(k, width, col) for k in range(self.sub)]

    def is_ctx(self, i, k):
        return self._tile(i, k)[1] >= self.n_lat_tiles

    def tok_ctx_rest(self, k, width):
        def index(i):
            bi, t = self._tile(i, k)
            return bi * self.n_tiles + jnp.maximum(t, self.n_lat_tiles), 0
        return pl.BlockSpec((TM, width), index)

    def tok_split(self, k, width, ctx):
        n_ctx_tiles = self.n_tiles - self.n_lat_tiles

        def index(i):
            bi, t = self._tile(i, k)
            if ctx:
                return bi * n_ctx_tiles + jnp.clip(t - self.n_lat_tiles, 0, n_ctx_tiles - 1), 0
            return bi * self.n_lat_tiles + jnp.minimum(t, self.n_lat_tiles - 1), 0
        return pl.BlockSpec((TM, width), index)

    def mods(self, d):
        def spec(k):
            def index(i):
                bi, t = self._tile(i, k)
                return self.layer, jnp.where(t >= self.n_lat_tiles, self.ctx_row, bi), 0, 0
            return pl.BlockSpec((None, None, N_MOD, d), index)
        return [spec(k) for k in range(self.sub)]

    def out(self, width):
        return pl.BlockSpec((self.sub * TM, width), lambda i: (i, 0))

    def out_shape(self, width, dtype):
        return jax.ShapeDtypeStruct((self.b * self.n_vis * TM, width), dtype)

    def rows(self, k):
        return slice(k * TM, (k + 1) * TM)


def _flat(a):
    return a.reshape(-1, a.shape[-1])


def _ffn_kernel(*refs, j, dff, sub, ctx_tiles):
    if ctx_tiles is not None:
        lat_refs, ctx_refs, refs = refs[:sub], refs[sub:2 * sub], refs[sub:]
        xs = [jnp.where(ctx_tiles(k), c[...], x[...]) for k, (x, c) in enumerate(zip(lat_refs, ctx_refs))]
    else:
        xs = [x[...] for x in refs[:sub]]
    mod_refs = refs[sub:2 * sub]
    g_ref, wgu_ref, wd_ref, o_ref = refs[2 * sub:]
    g_pre, g_post = g_ref[2 * j:2 * j + 1, :], g_ref[2 * j + 1:2 * j + 2, :]
    mods = [[m[3 * j + i:3 * j + i + 1, :] for i in range(3)] for m in mod_refs]
    hs = [(_rms(x) * g_pre * (1.0 + m[1]) + m[0]).astype(BF16) for x, m in zip(xs, mods)]
    gus = [_dot(h, wgu_ref[...]) for h in hs]
    acts = [(_silu(gu[:, :dff]) * gu[:, dff:]).astype(BF16) for gu in gus]
    ys = [_dot(a, wd_ref[...]) for a in acts]
    for k, (x, m, y) in enumerate(zip(xs, mods, ys)):
        o_ref[k * TM:(k + 1) * TM, :] = x + 0.5 * m[2] * (_rms(y) * g_post)


def _ffn(plan, xz, mod, norm_g, wgu, wd, j, ctx=None):
    d = xz.shape[-1]
    dff = wd.shape[2]
    l = plan.layer
    if ctx is None:
        tok_specs, tok_args, ctx_tiles = plan.toks(d), [xz] * plan.sub, None
    else:
        tok_specs = [plan.tok_split(k, d, False) for k in range(plan.sub)] \
                    + [plan.tok_split(k, d, True) for k in range(plan.sub)]
        tok_args = [xz] * plan.sub + [ctx] * plan.sub
        ctx_tiles = lambda k: plan.is_ctx(pl.program_id(0), k)
    return pl.pallas_call(
        functools.partial(_ffn_kernel, j=j, dff=dff, sub=plan.sub, ctx_tiles=ctx_tiles),
        grid=plan.grid,
        in_specs=tok_specs + plan.mods(d)
                 + [_const_spec((None,) + norm_g.shape[1:], (l, 0, 0)),
                    _const_spec((None, None, d, 2 * dff), (l, j // 2, 0, 0)),
                    _const_spec((None, None, dff, d), (l, j // 2, 0, 0))],
        out_specs=plan.out(d),
        out_shape=plan.out_shape(d, F32),
        compiler_params=_params("parallel"),
        name=f"ffn{j}",
    )(*tok_args, *([mod] * plan.sub), norm_g, wgu, wd)


def _inproj_kernel(*refs, sub, layout):
    x_refs, mod_refs = refs[:sub], refs[sub:2 * sub]
    g_ref, w_ref, wg_ref = refs[2 * sub:2 * sub + 3]
    o_refs = refs[2 * sub + 3:]
    hs = [(_rms(x[...]) * g_ref[2:3, :] * (1.0 + m[4:5, :]) + m[3:4, :]).astype(BF16)
          for x, m in zip(x_refs, mod_refs)]
    perm = _time_perm(False)
    col = 0
    for o_ref, segments in zip(o_refs[:-1], layout):
        n = o_ref.shape[-1]
        for k, h in enumerate(hs):
            res = _dot(h, w_ref[:, col:col + n])
            parts, c0 = [], 0
            for width, post in segments:
                seg = res[:, c0:c0 + width]
                if post == "perm":
                    seg = _dot(perm, seg.astype(BF16))
                elif post is not None:
                    seg = post(seg)
                parts.append(seg.astype(o_ref.dtype))
                c0 += width
            o_ref[k * TM:(k + 1) * TM, :] = jnp.concatenate(parts, axis=1)
        col += n
    for k, h in enumerate(hs):
        o_refs[-1][k * TM:(k + 1) * TM, :] = _dot(h, wg_ref[...])


def _inproj(plan, xz, mod, norm_g, w_main, w_gate, layout):
    d = xz.shape[-1]
    ng = w_gate.shape[-1]
    l = plan.layer
    widths = [sum(w for w, _ in segments) for segments in layout]
    return pl.pallas_call(
        functools.partial(_inproj_kernel, sub=plan.sub, layout=layout),
        grid=plan.grid,
        in_specs=plan.toks(d) + plan.mods(d)
                 + [_const_spec((None,) + norm_g.shape[1:], (l, 0, 0)),
                    _const_spec((None, d, w_main.shape[-1]), (l, 0, 0)),
                    _const_spec((None, d, ng), (l, 0, 0))],
        out_specs=[plan.out(w) for w in widths] + [plan.out(ng)],
        out_shape=[plan.out_shape(w, BF16) for w in widths] + [plan.out_shape(ng, F32)],
        compiler_params=_params("parallel"),
        name="inproj",
    )(*([xz] * plan.sub), *([mod] * plan.sub), norm_g, w_main, w_gate)


N_MERGE_STREAMS = 12


def _merge_kernel(*refs, sub, ctx_tiles):
    x_refs, mod_refs = refs[:sub], refs[sub:2 * sub]
    tok = refs[2 * sub:(2 + N_MERGE_STREAMS) * sub]
    g_ref, mlg_ref, dng_ref, wb_ref, wo_ref, o_ref = refs[(2 + N_MERGE_STREAMS) * sub:]
    d = o_ref.shape[-1]
    f32 = lambda r: r[...].astype(F32)
    unperm = _time_perm(True)
    mixes = []
    for k in range(sub):
        mlf, mlb, mlo, lrf, lrb, lry, dlf, dlb, dcf, dcb, dnz, mg = (
            tok[s * sub + k] for s in range(N_MERGE_STREAMS))
        y_ml = _rms_heads(f32(mlf) + f32(mlb), ML_HEADS) * mlg_ref[...] * f32(mlo)
        y_lr = (_dot(unperm, lrf[...]) + _dot(unperm, lrb[...])) * f32(lry)
        dn_h = jnp.where(ctx_tiles(k), f32(dcf) + f32(dcb), f32(dlf) + f32(dlb))
        y_dn = _rms_heads(dn_h, DN_HEADS) * dng_ref[...] * f32(dnz)
        mix = None
        for n, y in enumerate((y_ml, y_lr, y_dn)):
            term = mg[:, n * d:(n + 1) * d].astype(F32) * _dot(y.astype(BF16), wb_ref[n])
            mix = term if mix is None else mix + term
        mixes.append(mix.astype(BF16))
    outs = [_dot(mix, wo_ref[...]) for mix in mixes]
    for k, out in enumerate(outs):
        o_ref[k * TM:(k + 1) * TM, :] = x_refs[k][...] + mod_refs[k][5:6, :] * (_rms(out) * g_ref[3:4, :])


def _merge(plan, xz, mod, norm_g, ml_hf, ml_hb, ml, lr_hf, lr_hb, lru, dn_lat, dn_all, dnz, mg,
           ml_g, dn_g, w_branch, w_out):
    d = xz.shape[-1]
    bw = w_branch.shape[2]
    l = plan.layer
    sub = range(plan.sub)
    whole = lambda a, w, c=0: (a, plan.toks(w, c))
    streams = [whole(ml_hf, bw), whole(ml_hb, bw), whole(ml, bw, ml.shape[-1] // bw - 1),
               whole(lr_hf, bw), whole(lr_hb, bw), whole(lru, bw, lru.shape[-1] // bw - 1),
               (dn_lat[0], [plan.tok_split(k, bw, False) for k in sub]),
               (dn_lat[1], [plan.tok_split(k, bw, False) for k in sub]),
               (dn_all[0], [plan.tok_ctx_rest(k, bw) for k in sub]),
               (dn_all[1], [plan.tok_ctx_rest(k, bw) for k in sub]),
               whole(dnz, bw), whole(mg, N_BRANCH * d)]
    assert len(streams) == N_MERGE_STREAMS
    tok_specs = [sp for _, specs in streams for sp in specs]
    tok_args = [a for a, _ in streams for _ in sub]
    return pl.pallas_call(
        functools.partial(_merge_kernel, sub=plan.sub, ctx_tiles=lambda k: plan.is_ctx(pl.program_id(0), k)),
        grid=plan.grid,
        in_specs=plan.toks(d) + plan.mods(d) + tok_specs
                 + [_const_spec((None,) + norm_g.shape[1:], (l, 0, 0)),
                    _const_spec((None, 1, bw), (l, 0, 0)),
                    _const_spec((None, 1, bw), (l, 0, 0)),
                    _const_spec((None, N_BRANCH, bw, d), (l, 0, 0, 0)),
                    _const_spec((None, d, d), (l, 0, 0))],
        out_specs=plan.out(d),
        out_shape=plan.out_shape(d, F32),
        compiler_params=_params("parallel"),
        name="merge",
    )(*([xz] * plan.sub), *([mod] * plan.sub), *tok_args, norm_g, ml_g, dn_g, w_branch, w_out)


SEG = TM // V7X_SUBLANES


def _time_perm(inverse):
    ri = lax.broadcasted_iota(jnp.int32, (TM, TM), 0)
    ci = lax.broadcasted_iota(jnp.int32, (TM, TM), 1)
    r, t = (ci, ri) if inverse else (ri, ci)
    return jnp.where(t == (r % V7X_SUBLANES) * SEG + r // V7X_SUBLANES, 1.0, 0.0).astype(BF16)


def _conv4_perm(x, before1, before2, after1, w):
    sub = lax.broadcasted_iota(jnp.int32, (V7X_SUBLANES, 1), 0)
    vrow = lambda i: x[V7X_SUBLANES * i:V7X_SUBLANES * (i + 1)]
    m1_edge = jnp.where(sub == 0, before1, pltpu.roll(vrow(SEG - 1), 1, axis=0))
    m2_edge = jnp.where(sub == 0, before2, pltpu.roll(vrow(SEG - 2), 1, axis=0))
    p1_edge = jnp.where(sub == V7X_SUBLANES - 1, after1, pltpu.roll(vrow(0), V7X_SUBLANES - 1, axis=0))
    x_m1 = jnp.concatenate([m1_edge, x[:-V7X_SUBLANES]], axis=0)
    x_m2 = jnp.concatenate([m2_edge, m1_edge, x[:-2 * V7X_SUBLANES]], axis=0)
    x_p1 = jnp.concatenate([x[V7X_SUBLANES:], p1_edge], axis=0)
    taps = {-2: x_m2, -1: x_m1, 0: x, 1: x_p1}
    acc = None
    for j in range(CONV_W):
        term = taps[j - CONV_LEFT] * w[j:j + 1]
        acc = term if acc is None else acc + term
    return acc


def _tile_scan(a, b, h0, rev):
    vrow = lambda x, i: x[V7X_SUBLANES * i:V7X_SUBLANES * (i + 1)]
    h = jnp.zeros_like(vrow(a, 0))
    p = jnp.ones_like(h)
    hs, ps = [None] * SEG, [None] * SEG
    for i in (range(SEG - 1, -1, -1) if rev else range(SEG)):
        ai = vrow(a, i)
        h = ai * h + vrow(b, i)
        p = ai * p
        hs[i], ps[i] = h, p
    carry = h0
    enter = [None] * V7X_SUBLANES
    for s in (range(V7X_SUBLANES - 1, -1, -1) if rev else range(V7X_SUBLANES)):
        enter[s] = carry
        carry = p[s:s + 1] * carry + h[s:s + 1]
    enter = jnp.concatenate(enter, axis=0)
    return jnp.concatenate([hi + pi * enter for hi, pi in zip(hs, ps)], axis=0), carry


def _lru_kernel(x_ref, cw_ref, cb_ref, w_ref, bias_ref, lam_ref, of_ref, ob_ref, *, n_lat, n_ctx):
    t, c = TM, x_ref.shape[-1]
    n_tiles = n_lat + n_ctx
    halo = 2 * V7X_SUBLANES

    def load_conv(tile):
        r0 = pl.multiple_of(tile * t, t)
        first = jnp.logical_or(tile == 0, tile == n_lat)
        last = jnp.logical_or(tile == n_lat - 1, tile == n_tiles - 1)
        x = x_ref[pl.ds(r0, t), :].astype(F32)
        p0 = pl.multiple_of(jnp.maximum(r0 - halo, 0), halo)
        n0 = pl.multiple_of(jnp.minimum(r0 + t, n_tiles * t - halo), halo)
        prev = jnp.where(first, 0.0, x_ref[pl.ds(p0, halo), :].astype(F32))
        nxt = jnp.where(last, 0.0, x_ref[pl.ds(n0, halo), :].astype(F32))
        before1, before2 = prev[halo - 1:halo], prev[V7X_SUBLANES - 1:V7X_SUBLANES]
        return r0, _conv4_perm(x, before1, before2, nxt[0:1], cw_ref[...]) + cb_ref[...]

    def direction(d, tile, h0, o_ref):
        r0, xc = load_conv(tile)
        z = _dot(xc.astype(BF16), w_ref[d]) + bias_ref[d]
        r, i = _sigmoid(z[:, :c]), _sigmoid(z[:, c:])
        la = (-LRU_C * _softplus(-lam_ref[d])) * r
        a = jnp.exp(la)
        bx = jnp.sqrt(jnp.tanh(-la) * (1.0 + a * a)) * (i * xc)
        h, carry = _tile_scan(a, bx, h0, d == 1)
        o_ref[pl.ds(r0, t), :] = h.astype(o_ref.dtype)
        return carry

    def step(s, carry):
        hf, hb = carry
        hf = direction(0, jnp.where(s < n_ctx, n_lat + s, s - n_ctx), hf, of_ref)
        hb = direction(1, n_tiles - 1 - s, hb, ob_ref)
        return hf, hb

    zero = jnp.zeros((1, c), F32)
    lax.fori_loop(0, n_tiles, step, (zero, zero))


def _lru(lru, conv_w, conv_b, w_gates, b_gates, lam, l, n_lat, n_ctx):
    b, lt, _ = lru.shape
    c = conv_w.shape[-1]
    seq = pl.BlockSpec((None, lt, c), lambda bi: (bi, 0, 0))
    return pl.pallas_call(
        functools.partial(_lru_kernel, n_lat=n_lat, n_ctx=n_ctx),
        grid=(b,),
        in_specs=[seq,
                  _const_spec((None, CONV_W, c), (l, 0, 0)),
                  _const_spec((None, 1, c), (l, 0, 0)),
                  _const_spec((None, 2, c, 2 * c), (l, 0, 0, 0)),
                  _const_spec((None, 2, 1, 2 * c), (l, 0, 0, 0)),
                  _const_spec((None, 2, 1, c), (l, 0, 0, 0))],
        out_specs=[seq, seq],
        out_shape=[jax.ShapeDtypeStruct((b, lt, c), BF16)] * 2,
        compiler_params=_params("parallel"),
        name="lru",
    )(lru, conv_w, conv_b, w_gates, b_gates, lam)


def _mlstm_kernel(qf_ref, kf_ref, vf_ref, qb_ref, kb_ref, vb_ref, grf_ref, grb_ref, br_ref,
                  hf_ref, hb_ref, c_ref, m_ref):
    @pl.when(pl.program_id(1) == 0)
    def _():
        c_ref[...] = jnp.zeros(c_ref.shape, F32)
        m_ref[...] = jnp.zeros(m_ref.shape, F32)

    nrow, tc, nqk = qf_ref.shape
    dk = nqk // ML_HEADS
    dv = vf_ref.shape[-1] // ML_HEADS
    nh = ML_HEADS
    ri = lax.broadcasted_iota(jnp.int32, (tc, tc), 0)
    ci = lax.broadcasted_iota(jnp.int32, (tc, tc), 1)
    causal = (ci <= ri, ci >= ri)
    lane_head = lax.broadcasted_iota(jnp.int32, (1, nqk), 1) // dk
    ones = jnp.ones((tc, dv), BF16)
    dirs = ((qf_ref, kf_ref, vf_ref, grf_ref), (qb_ref, kb_ref, vb_ref, grb_ref))
    seqs = [(r, d) for r in range(nrow) for d in range(2)]

    rows, v_rows, ws_rows, decs = [], [], [], []
    for r, d in seqs:
        rev = d == 1
        gr = dirs[d][3][r].T + br_ref[...]
        i_r = gr[nh * d:nh * (d + 1)]
        b_r = _cumsum_groups(-_softplus(-gr[nh * (2 + d):nh * (3 + d)]), 1, tc, rev)
        m_prev = m_ref[r, nh * d:nh * (d + 1), 0:1]
        m_t = b_r + jnp.maximum(m_prev, _cummax_groups(i_r - b_r, 1, tc, rev))
        b_end = b_r[:, 0:1] if rev else b_r[:, tc - 1:tc]
        lws = b_end - b_r + i_r
        m_new = jnp.maximum(b_end + m_prev, jnp.max(lws, axis=1, keepdims=True))
        rows += [b_r - m_t, jnp.exp(b_r + m_prev - m_t), jnp.exp(-m_t)]
        v_rows.append(b_r - i_r)
        ws_rows.append(jnp.exp(lws - m_new))
        decs.append(jnp.exp(b_end + m_prev - m_new))
        m_ref[r, nh * d:nh * (d + 1), :] = jnp.broadcast_to(m_new, (nh, m_ref.shape[-1]))
    cols = jnp.concatenate(rows, axis=0).T

    chains = [(s, h) for s in range(len(seqs)) for h in range(nh)]
    q_all = [dirs[d][0][r] * (dk ** -0.5) for r, d in seqs]
    k_all = [dirs[d][1][r] for r, d in seqs]
    v_all = [dirs[d][2][r] for r, d in seqs]
    kt_all = [k.astype(F32).T for k in k_all]
    c_all = [c_ref[r, d] for r, d in seqs]
    cb_all = [c.astype(BF16) for c in c_all]
    col = lambda s, j, h: cols[:, (3 * s + j) * nh + h:(3 * s + j) * nh + h + 1]
    qh = [jnp.where(lane_head == h, q_all[s], jnp.zeros_like(q_all[s])) for s, h in chains]
    vp = [jnp.concatenate([v_all[s][:, h * dv:(h + 1) * dv], ones], axis=1) for s, h in chains]
    s_raw = [_dot_nt(q, k_all[s]) for q, (s, h) in zip(qh, chains)]
    p = [(sr * jnp.exp(jnp.where(causal[seqs[s][1]], col(s, 0, h) - v_rows[s][h:h + 1], NEG))).astype(BF16)
         for sr, (s, h) in zip(s_raw, chains)]
    num = [_dot(pc, vc) + col(s, 1, h) * _dot(q, cb_all[s]) for pc, vc, q, (s, h) in zip(p, vp, qh, chains)]
    outs = [nm[:, :dv] / jnp.maximum(jnp.abs(nm[:, dv:]), col(s, 2, h)) for nm, (s, h) in zip(num, chains)]
    for vc, (s, h) in zip(vp, chains):
        r, d = seqs[s]
        kw = (kt_all[s][h * dk:(h + 1) * dk] * ws_rows[s][h:h + 1]).astype(BF16)
        c_ref[r, d, h * dk:(h + 1) * dk, :] = decs[s][h:h + 1] * c_all[s][h * dk:(h + 1) * dk] + _dot(kw, vc)
    for s, (r, d) in enumerate(seqs):
        o_ref = (hf_ref, hb_ref)[d]
        o_ref[r] = jnp.concatenate(outs[s * nh:(s + 1) * nh], axis=1).astype(o_ref.dtype)


def _mlstm(ml, gates, bias_r, l, n_lat, n_ctx):
    b, lt, _ = ml.shape
    ng = gates.shape[-1]
    n_tiles = n_lat + n_ctx
    dqk = ml.shape[-1] // 6
    fwd = lambda s: jnp.where(s < n_ctx, n_lat + s, s - n_ctx)
    bwd = lambda s: n_tiles - 1 - s
    rpb = ML_ROWS if b % ML_ROWS == 0 else 1

    def specs(tile):
        return [pl.BlockSpec((rpb, TM, dqk), lambda bi, s: (bi, tile(s), 0)),
                pl.BlockSpec((rpb, TM, dqk), lambda bi, s: (bi, tile(s), 1)),
                pl.BlockSpec((rpb, TM, 2 * dqk), lambda bi, s: (bi, tile(s), 1))]

    row = lambda tile: pl.BlockSpec((rpb, TM, ng), lambda bi, s: (bi, tile(s), 0))
    out = lambda tile: pl.BlockSpec((rpb, TM, 2 * dqk), lambda bi, s: (bi, tile(s), 0))
    return pl.pallas_call(
        _mlstm_kernel,
        grid=(b // rpb, n_tiles),
        in_specs=specs(fwd) + specs(bwd) + [row(fwd), row(bwd), _const_spec((None, ng, 1), (l, 0, 0))],
        out_specs=[out(fwd), out(bwd)],
        out_shape=[jax.ShapeDtypeStruct((b, lt, 2 * dqk), BF16)] * 2,
        scratch_shapes=[pltpu.VMEM((rpb, 2, dqk, 2 * (2 * dqk // ML_HEADS)), F32),
                        pltpu.VMEM((rpb, 2 * ML_HEADS, V7X_LANES), F32)],
        compiler_params=_params("parallel", "arbitrary"),
        name="mlstm",
    )(ml, ml, ml, ml, ml, ml, gates, gates, bias_r)


INV_BASE = 8


def _inv_masks(ri, ci):
    blk = lambda s: (ri // s) == (ci // s)
    as_bf = lambda m: jnp.where(m, 1.0, 0.0).astype(BF16)
    masks, s = [as_bf(blk(INV_BASE))], INV_BASE
    while s < CHUNK:
        masks.append(as_bf(jnp.logical_and(blk(2 * s), jnp.logical_not(blk(s)))))
        s *= 2
    return masks


def _tri_inv_many(a_list, eye, same, masks):
    nblk = a_list[0].shape[0] // CHUNK
    compact = lambda m: functools.reduce(lambda u, v: u + v, [m[i * CHUNK:(i + 1) * CHUNK] for i in range(nblk)])
    spread = lambda c: jnp.concatenate([c] * nblk, axis=0) * same
    eye_c = compact(eye)
    ps = [a * masks[0] for a in a_list]
    pcs = [compact(p) for p in ps]
    xcs = [eye_c - pc for pc in pcs]
    for _ in range(INV_BASE.bit_length() - 2):
        pcs = [_dot(pc, p).astype(BF16) for pc, p in zip(pcs, ps)]
        ps = [spread(pc) for pc in pcs]
        xcs = [_dot(xc, eye + p).astype(BF16) for xc, p in zip(xcs, ps)]
    for m in masks[1:]:
        ys = [_dot(xc, a * m).astype(BF16) for xc, a in zip(xcs, a_list)]
        xcs = [xc - _dot(y, spread(xc)).astype(BF16) for y, xc in zip(ys, xcs)]
    return [spread(xc) for xc in xcs]


def _dn_prep_kernel(xl_ref, xp_ref, xn_ref, xc_ref, grow_ref, cw_ref, arow_ref,
                    u_ref, w_ref, qe_ref, kd_ref, qk_ref, eg_ref, *, n_lat_groups):
    g = pl.program_id(1)
    is_ctx = g >= n_lat_groups
    width = xl_ref.shape[-1]
    dh = width // (3 * DN_HEADS)
    n = DN_HEADS * CHUNK
    halo = 2 * V7X_SUBLANES
    x = jnp.where(is_ctx, xc_ref[...], xl_ref[...]).reshape(DN_GROUP * CHUNK, width)
    no_prev = jnp.logical_or(is_ctx, g == 0)
    no_next = jnp.logical_or(is_ctx, g == n_lat_groups - 1)
    prev8 = jnp.where(no_prev, 0.0, xp_ref[CHUNK - halo:, :].astype(F32)[V7X_SUBLANES:])
    nxt8 = jnp.where(no_next, 0.0, xn_ref[:halo, :].astype(F32)[:V7X_SUBLANES])
    xc = _silu(_conv4(x, prev8, nxt8, cw_ref[...]))

    ri = lax.broadcasted_iota(jnp.int32, (n, n), 0)
    ci = lax.broadcasted_iota(jnp.int32, (n, n), 1)
    same = (ri // CHUNK) == (ci // CHUNK)
    eye = jnp.where(ri == ci, 1.0, 0.0).astype(BF16)
    incl =(jnp.logical_and(same, ci <= ri), jnp.logical_and(same, ci >= ri))
    strict = (jnp.logical_and(same, ci < ri), jnp.logical_and(same, ci > ri))
    masks = _inv_masks(ri, ci)

    qn, kn, v, kk, qk = [], [], [], [], []
    for i in range(DN_GROUP):
        xi = xc[i * CHUNK:(i + 1) * CHUNK]
        q, k, vi = (_stack_heads(xi[:, j * DN_HEADS * dh:(j + 1) * DN_HEADS * dh], DN_HEADS) for j in range(3))
        qn.append(q * lax.rsqrt(jnp.sum(q * q, axis=-1, keepdims=True) + EPS) * (dh ** -0.5))
        kn.append(k * lax.rsqrt(jnp.sum(k * k, axis=-1, keepdims=True) + EPS))
        v.append(vi)
        kb = kn[i].astype(BF16)
        kk.append(_dot_nt(kb, kb))
        qk.append(_dot_nt(qn[i].astype(BF16), kb))

    systems = [(i, d) for i in range(DN_GROUP) for d in range(2)]
    rows, cs_rows = [], []
    for i, d in systems:
        raw = grow_ref[i]
        beta = _sigmoid(raw[d:d + 1])
        g_r = -jnp.exp(arow_ref[d, 0:1, :]) * _softplus(raw[2 + d:3 + d] + arow_ref[d, 1:2, :])
        cs = _cumsum_groups(g_r, 1, CHUNK, d == 1)
        tot = cs + _cumsum_groups(g_r, 1, CHUNK, d == 0) - g_r
        eg = jnp.exp(cs)
        rows += [cs, beta, eg, jnp.exp(tot - cs), beta * eg]
        cs_rows.append(cs)
        eg_ref[d, i] = jnp.exp(tot)
    n_col = len(rows) // len(systems)
    cols = jnp.concatenate(rows, axis=0).T

    a_list, rhs = [], []
    for s, (i, d) in enumerate(systems):
        cs_c, beta_c, eg_c, ekd_c, beg_c = (cols[:, n_col * s + j:n_col * s + j + 1] for j in range(n_col))
        gam = jnp.exp(jnp.where(incl[d], cs_c - cs_rows[s], NEG))
        a_list.append((jnp.where(strict[d], kk[i], 0.0) * gam * beta_c).astype(BF16))
        rhs.append(jnp.concatenate([beta_c * v[i], beg_c * kn[i]], axis=1).astype(BF16))
        qkg = qk[i] * gam
        qk_ref[d, i] = functools.reduce(lambda a, c: a + c, [qkg[h * CHUNK:(h + 1) * CHUNK]
                                                            for h in range(DN_HEADS)]).astype(qk_ref.dtype)
        qe_ref[d, i] = (qn[i] * eg_c).astype(qe_ref.dtype)
        kd_ref[d, i] = (kn[i] * ekd_c).astype(kd_ref.dtype)
    same_bf = jnp.where(same, 1.0, 0.0).astype(BF16)
    for (i, d), tinv, r in zip(systems, _tri_inv_many(a_list, eye, same_bf, masks), rhs):
        sol = _dot(tinv, r)
        u_ref[d, i] = sol[:, :dh].astype(u_ref.dtype)
        w_ref[d, i] = sol[:, dh:].astype(w_ref.dtype)


def _dn_prep(x_cols, x_rows, grow, conv_w, arow, l):
    b, nc, _, width = x_rows.shape
    n_cols = x_cols.shape[1]
    n_groups = nc // DN_GROUP
    n_lat_groups = n_cols // DN_GROUP
    assert n_groups == n_lat_groups + 1
    n = DN_HEADS * CHUNK
    dh = width // (3 * DN_HEADS)
    lat_g = lambda g: jnp.minimum(g, n_lat_groups - 1)
    out = lambda w, dt, rows=n: (jax.ShapeDtypeStruct((b, 2, nc, rows, w), dt),
                                 pl.BlockSpec((None, 2, DN_GROUP, rows, w), lambda bi, g: (bi, 0, g, 0, 0)))
    outs = [out(dh, BF16)] * 4 + [out(n, BF16, CHUNK)]
    outs.append((jax.ShapeDtypeStruct((b, 2, nc, 1, n), F32),
                 pl.BlockSpec((None, 2, DN_GROUP, 1, n), lambda bi, g: (bi, 0, g, 0, 0))))
    chunk = lambda idx: pl.BlockSpec((None, None, CHUNK, width), lambda bi, g: (bi, idx(g), 0, 0))
    return pl.pallas_call(
        functools.partial(_dn_prep_kernel, n_lat_groups=n_lat_groups),
        grid=(b, n_groups),
        in_specs=[pl.BlockSpec((None, DN_GROUP, CHUNK, width), lambda bi, g: (bi, lat_g(g), 0, 0)),
                  chunk(lambda g: jnp.maximum(lat_g(g) * DN_GROUP - 1, 0)),
                  chunk(lambda g: jnp.minimum(lat_g(g) * DN_GROUP + DN_GROUP, n_cols - 1)),
                  pl.BlockSpec((None, DN_GROUP, CHUNK, width), lambda bi, g: (bi, n_groups - 1, 0, 0)),
                  pl.BlockSpec((None, DN_GROUP, 4, n), lambda bi, g: (bi, g, 0, 0)),
                  _const_spec((None, CONV_W, width), (l, 0, 0)),
                  _const_spec((None, 2, 2, n), (l, 0, 0, 0))],
        out_specs=[o[1] for o in outs],
        out_shape=[o[0] for o in outs],
        compiler_params=_params("parallel", "parallel"),
        name="dn_prep",
    )(x_cols, x_cols, x_cols, x_rows, grow, conv_w, arow)


def _dn_scan_kernel(*refs):
    ins, (of_ref, ob_ref, s_ref) = refs[:12], refs[12:]

    @pl.when(pl.program_id(1) == 0)
    def _():
        s_ref[...] = jnp.zeros(s_ref.shape, F32)

    rows = [slice(h * CHUNK, (h + 1) * CHUNK) for h in range(DN_HEADS)]
    seqs = [(r, d) for r in range(of_ref.shape[0]) for d in range(2)]
    get = lambda j, r, d: ins[6 * d + j][r]
    sb = [[s_ref[r, d, h].astype(BF16) for h in range(DN_HEADS)] for r, d in seqs]
    ws = [jnp.concatenate([_dot(get(1, r, d)[rows[h]], sb[i][h]) for h in range(DN_HEADS)], axis=0)
          for i, (r, d) in enumerate(seqs)]
    qs = [jnp.concatenate([_dot(get(2, r, d)[rows[h]], sb[i][h]) for h in range(DN_HEADS)], axis=0)
          for i, (r, d) in enumerate(seqs)]
    vnew = [(get(0, r, d).astype(F32) - w).astype(BF16) for w, (r, d) in zip(ws, seqs)]
    n = DN_HEADS * CHUNK
    same = jnp.where(lax.broadcasted_iota(jnp.int32, (n, n), 0) // CHUNK
                     == lax.broadcasted_iota(jnp.int32, (n, n), 1) // CHUNK, 1.0, 0.0).astype(BF16)
    spread = lambda c: jnp.concatenate([c] * DN_HEADS, axis=0) * same
    o = [q + _dot(spread(get(4, r, d)), vn) for q, vn, (r, d) in zip(qs, vnew, seqs)]
    for vn, (r, d) in zip(vnew, seqs):
        kd, eg = get(3, r, d), get(5, r, d)
        for h in range(DN_HEADS):
            s_ref[r, d, h] = (eg[:, h * CHUNK:h * CHUNK + 1] * s_ref[r, d, h]
                              + _dot_tn(kd[rows[h]], vn[rows[h]]))
    for oi, (r, d) in zip(o, seqs):
        o_ref = (of_ref, ob_ref)[d]
        o_ref[r] = _unstack_heads(oi, DN_HEADS).astype(o_ref.dtype)


def _dn_scan(prep, n_lat_chunks):
    b, _, nc, _, dh = prep[0].shape
    ow = DN_HEADS * dh
    n_ctx_chunks = nc - n_lat_chunks
    fwd = lambda s: jnp.where(s < n_ctx_chunks, n_lat_chunks + s, s - n_ctx_chunks)
    bwd = lambda s: nc - 1 - s

    rpb = DN_SCAN_ROWS if b % DN_SCAN_ROWS == 0 else 1

    def spec(a, d, chunk):
        return pl.BlockSpec((rpb, None, None) + a.shape[3:], lambda bi, s: (bi, d, chunk(s), 0, 0))

    out = lambda chunk: pl.BlockSpec((rpb, None, CHUNK, ow), lambda bi, s: (bi, chunk(s), 0, 0))
    return pl.pallas_call(
        _dn_scan_kernel,
        grid=(b // rpb, nc),
        in_specs=[spec(a, 0, fwd) for a in prep] + [spec(a, 1, bwd) for a in prep],
        out_specs=[out(fwd), out(bwd)],
        out_shape=[jax.ShapeDtypeStruct((b, nc, CHUNK, ow), BF16)] * 2,
        scratch_shapes=[pltpu.VMEM((rpb, 2, DN_HEADS, dh, dh), F32)],
        compiler_params=_params("parallel", "arbitrary"),
        name="dn_scan",
    )(*prep, *prep)


def _block_diag(w):
    n, i, j = w.shape
    return jnp.einsum('nij,nm->nimj', w, jnp.eye(n, dtype=w.dtype)).reshape(n * i, n * j)


def _dn_gate_rows(raw, n_lat):
    b = raw.shape[0]
    lat = raw[:, :n_lat].reshape(b, n_lat // CHUNK, CHUNK, 4, DN_HEADS)
    ctx = raw[:, n_lat:].reshape(b, -1, CHUNK, 4, DN_HEADS)
    row = jnp.concatenate([lat.transpose(0, 2, 3, 4, 1), ctx.transpose(0, 1, 3, 4, 2)], axis=1)
    return row.reshape(b, row.shape[1], 4, DN_HEADS * CHUNK)


def kernel(x, c, ctx, c_ctx, w_mod, b_mod, norm_g, ffn_w_gu, ffn_w_down, w_in, ml_gate_b, ml_norm_g,
           lru_conv_w, lru_conv_b, lru_w_a, lru_b_a, lru_w_x, lru_b_x, lru_lambda, dn_conv_w,
           dn_a_log, dn_dt_bias, dn_norm_g, w_branch, w_out):
    b, n_lat, d = x.shape
    n_ctx = ctx.shape[1]
    depth = w_mod.shape[0]
    bw = w_branch.shape[2]
    assert n_lat == CHUNK * CHUNK and n_lat % TM == 0 and n_ctx % TM == 0 and n_ctx == CHUNK * DN_GROUP
    n_lat_tiles, n_ctx_tiles = n_lat // TM, n_ctx // TM
    n_tiles = n_lat_tiles + n_ctx_tiles

    ctx_row = b
    n_rows = -(-(b + 1) // V7X_SUBLANES) * V7X_SUBLANES
    cc = jnp.zeros((n_rows, d), F32).at[:b].set(c).at[b].set(c_ctx)
    mod = _mod_table(cc, w_mod, b_mod).reshape(depth, n_rows, N_MOD, d)

    dqk = bw // 2
    edges = [0]
    for wdt in (dqk, dqk, bw, bw, N_GATES, bw, bw, bw, bw, bw, bw, N_GATES, N_BRANCH * d):
        edges.append(edges[-1] + wdt)
    piece = lambda i, j: w_in[:, :, edges[i]:edges[j]]
    layout = (((2 * bw, None), (bw, _sigmoid)),
              ((bw, "perm"), (bw, _gelu_tanh)),
              ((3 * bw, None),),
              ((bw, _silu),),
              ((N_BRANCH * d, _sigmoid),))
    w_main = jnp.concatenate([piece(0, 4), piece(5, 7), piece(7, 11), piece(12, 13)], axis=-1).astype(BF16)
    w_gate = jnp.concatenate([piece(4, 5), piece(11, 12)], axis=-1).astype(BF16)
    wgu = ffn_w_gu.astype(BF16)
    wdn = ffn_w_down.astype(BF16)
    wbr = w_branch.astype(BF16)
    wout = w_out.astype(BF16)
    ml_bias = jnp.concatenate([ml_gate_b.reshape(depth, N_GATES), jnp.zeros((depth, N_GATES), F32)], axis=-1)
    ml_bias_r = ml_bias[:, :, None]
    ml_g = ml_norm_g[:, None, :]
    dn_g = jnp.tile(dn_norm_g, (1, DN_HEADS))[:, None, :]
    lru_w = jnp.stack([jnp.concatenate([jax.vmap(_block_diag)(lru_w_a[:, dd]), jax.vmap(_block_diag)(lru_w_x[:, dd])],
                                       axis=-1) for dd in range(2)], axis=1).astype(BF16)
    lru_b = jnp.concatenate([lru_b_a, lru_b_x], axis=-1)[:, :, None, :]
    lru_lam = lru_lambda[:, :, None, :]
    lru_cb = lru_conv_b[:, None, :]
    dn_arow = jnp.repeat(jnp.stack([dn_a_log, dn_dt_bias], axis=2), CHUNK, axis=3)

    rows3 = lambda a: a.reshape(b, n_lat + n_ctx, a.shape[-1])
    for l in range(depth):
        full = _TilePlan(b, n_tiles, n_lat_tiles, n_tiles, ctx_row, l)
        if l == 0:
            xz = _ffn(full, _flat(x), mod, norm_g, wgu, wdn, 0, ctx=_flat(ctx))
        else:
            xz = _ffn(full, xz, mod, norm_g, wgu, wdn, 0)
        ml, lru, dnqkv, dnz, mg, gates = _inproj(full, xz, mod, norm_g, w_main, w_gate, layout)
        gates = rows3(gates)
        ml_hf, ml_hb = _mlstm(rows3(ml), gates, ml_bias_r, l, n_lat_tiles, n_ctx_tiles)
        lr_hf, lr_hb = _lru(rows3(lru), lru_conv_w, lru_cb, lru_w, lru_b, lru_lam, l, n_lat_tiles, n_ctx_tiles)
        n_cols = n_lat // CHUNK
        dq = rows3(dnqkv)
        x_rows = dq.reshape(b, -1, CHUNK, dq.shape[-1])
        x_cols = x_rows.swapaxes(1, 2)
        prep = _dn_prep(x_cols, x_rows, _dn_gate_rows(gates[:, :, N_GATES:], n_lat), dn_conv_w, dn_arow, l)
        dn_all = _dn_scan(prep, n_cols)
        dn_lat = [_flat(o[:, :n_cols].swapaxes(1, 2)) for o in dn_all]
        xz = _merge(full, xz, mod, norm_g, _flat(ml_hf), _flat(ml_hb), ml, _flat(lr_hf), _flat(lr_hb), lru,
                    dn_lat, [_flat(o) for o in dn_all], dnz, mg, ml_g, dn_g, wbr, wout)
        visit = n_lat_tiles if l == depth - 1 else n_tiles
        xz = _ffn(_TilePlan(b, n_tiles, n_lat_tiles, visit, ctx_row, l), xz, mod, norm_g, wgu, wdn, 2)
    return xz.reshape(b, n_lat, d)
```

```python
import functools
import math

import jax
import jax.numpy as jnp
from jax import lax
from jax.experimental import pallas as pl
from jax.experimental.pallas import tpu as pltpu

F32 = jnp.float32
BF16 = jnp.bfloat16

EPS = 1e-6
N_MOD = 9
N_BRANCH = 3
CONV_W = 4
CONV_LEFT = 2
ML_HEADS = 4
LRU_C = 8.0
DN_HEADS = 4
CHUNK = 64
DN_GROUP = 4
DN_SCAN_ROWS = 4
ML_ROWS = 2
N_GATES = 16

V7X_SUBLANES = 8
V7X_LANES = 128
V7X_VMEM_BYTES = 64 * 1024 * 1024
VMEM_LIMIT = V7X_VMEM_BYTES - 8 * 1024 * 1024

TM = 256
NEG = -1e30


def _sigmoid(x):
    return 0.5 * jnp.tanh(0.5 * x) + 0.5


def _silu(x):
    return x * _sigmoid(x)


def _softplus(x):
    return jnp.maximum(x, 0.0) + jnp.log(1.0 + jnp.exp(-jnp.abs(x)))


def _rms(x):
    return x * lax.rsqrt(jnp.mean(x * x, axis=-1, keepdims=True) + EPS)


def _rms_heads(x, n_heads):
    hd = x.shape[-1] // n_heads
    return jnp.concatenate([_rms(x[:, h * hd:(h + 1) * hd]) for h in range(n_heads)], axis=-1)


def _gelu_tanh(x):
    return 0.5 * x * (1.0 + jnp.tanh(math.sqrt(2.0 / math.pi) * (x + 0.044715 * (x * x * x))))


def _dot(a, b):
    return jnp.dot(a, b, preferred_element_type=F32)


def _dot_nt(a, b):
    return lax.dot_general(a, b, (((1,), (1,)), ((), ())), preferred_element_type=F32)


def _dot_tn(a, b):
    return lax.dot_general(a, b, (((0,), (0,)), ((), ())), preferred_element_type=F32)


def _params(*sem):
    return pltpu.CompilerParams(dimension_semantics=sem, vmem_limit_bytes=VMEM_LIMIT)


def _const_spec(block, index):
    return pl.BlockSpec(block, lambda *_: index, pipeline_mode=pl.Buffered(1))


def _stack_heads(x, n):
    w = x.shape[1] // n
    return jnp.concatenate([x[:, h * w:(h + 1) * w] for h in range(n)], axis=0)


def _unstack_heads(x, n):
    t = x.shape[0] // n
    return jnp.concatenate([x[h * t:(h + 1) * t] for h in range(n)], axis=1)


def _cumsum_groups(x, axis, period, rev):
    n = x.shape[axis]
    idx = lax.broadcasted_iota(jnp.int32, x.shape, axis) % period
    sh = 1
    while sh < period:
        if rev:
            x = x + jnp.where(idx < period - sh, pltpu.roll(x, n - sh, axis=axis), 0.0)
        else:
            x = x + jnp.where(idx >= sh, pltpu.roll(x, sh, axis=axis), 0.0)
        sh *= 2
    return x


def _cummax_groups(x, axis, period, rev):
    n = x.shape[axis]
    idx = lax.broadcasted_iota(jnp.int32, x.shape, axis) % period
    sh = 1
    while sh < period:
        if rev:
            x = jnp.maximum(x, jnp.where(idx < period - sh, pltpu.roll(x, n - sh, axis=axis), NEG))
        else:
            x = jnp.maximum(x, jnp.where(idx >= sh, pltpu.roll(x, sh, axis=axis), NEG))
        sh *= 2
    return x


def _shift_rows(xb, prev8, nxt8, off):
    t = xb.shape[0]
    ri = lax.broadcasted_iota(jnp.int32, (t, t), 0)
    ci = lax.broadcasted_iota(jnp.int32, (t, t), 1)
    shifted = _dot(jnp.where(ci == ri + off, 1.0, 0.0).astype(BF16), xb)
    row8 = lax.broadcasted_iota(jnp.int32, (V7X_SUBLANES, 1), 0)
    if off < 0:
        edge = jnp.where(row8 < -off, pltpu.roll(prev8, -off, axis=0), shifted[:V7X_SUBLANES])
        return jnp.concatenate([edge, shifted[V7X_SUBLANES:]], axis=0)
    edge = jnp.where(row8 >= V7X_SUBLANES - off, pltpu.roll(nxt8, V7X_SUBLANES - off, axis=0),
                     shifted[t - V7X_SUBLANES:])
    return jnp.concatenate([shifted[:t - V7X_SUBLANES], edge], axis=0)


def _conv4(xb, prev8, nxt8, w):
    acc = xb.astype(F32) * w[CONV_LEFT:CONV_LEFT + 1]
    for j in range(CONV_W):
        if j != CONV_LEFT:
            acc = acc + _shift_rows(xb, prev8, nxt8, j - CONV_LEFT) * w[j:j + 1]
    return acc


def _mod_kernel(c_ref, w_ref, b_ref, o_ref):
    s = _silu(c_ref[...]).astype(BF16)
    o_ref[...] = _dot(s, w_ref[...].astype(BF16)) + b_ref[...]


def _mod_table(cc, w_mod, b_mod):
    depth, d, nd = w_mod.shape
    r = cc.shape[0]
    tn = nd // 4
    return pl.pallas_call(
        _mod_kernel,
        grid=(depth, nd // tn),
        in_specs=[pl.BlockSpec((r, d), lambda l, j: (0, 0)),
                  pl.BlockSpec((None, d, tn), lambda l, j: (l, 0, j)),
                  pl.BlockSpec((None, 1, tn), lambda l, j: (l, 0, j))],
        out_specs=pl.BlockSpec((None, r, tn), lambda l, j: (l, 0, j)),
        out_shape=jax.ShapeDtypeStruct((depth, r, nd), F32),
        compiler_params=_params("parallel", "parallel"),
        name="mod_table",
    )(cc, w_mod, b_mod.reshape(depth, 1, nd))


class _TilePlan:
    def __init__(self, b, n_tiles, n_lat_tiles, n_vis, ctx_row, layer):
        self.b, self.n_tiles, self.n_lat_tiles, self.n_vis = b, n_tiles, n_lat_tiles, n_vis
        self.ctx_row, self.layer = ctx_row, layer
        self.sub = 2 if (b * n_vis) % 2 == 0 else 1
        self.grid = (b * n_vis // self.sub,)

    def _tile(self, i, k):
        tid = i * self.sub + k
        return tid // self.n_vis, tid % self.n_vis

    def tok(self, k, width, col=0):
        def index(i):
            bi, t = self._tile(i, k)
            return bi * self.n_tiles + t, col
        return pl.BlockSpec((TM, width), index)

    def toks(self, width, col=0):
        return [self.tok(k, width, col) for k in range(self.sub)]

    def is_ctx(self, i, k):
        return self._tile(i, k)[1] >= self.n_lat_tiles

    def tok_ctx_rest(self, k, width):
        def index(i):
            bi, t = self._tile(i, k)
            return bi * self.n_tiles + jnp.maximum(t, self.n_lat_tiles), 0
        return pl.BlockSpec((TM, width), index)

    def tok_split(self, k, width, ctx):
        n_ctx_tiles = self.n_tiles - self.n_lat_tiles

        def index(i):
            bi, t = self._tile(i, k)
            if ctx:
                return bi * n_ctx_tiles + jnp.clip(t - self.n_lat_tiles, 0, n_ctx_tiles - 1), 0
            return bi * self.n_lat_tiles + jnp.minimum(t, self.n_lat_tiles - 1), 0
        return pl.BlockSpec((TM, width), index)

    def mods(self, d):
        def spec(k):
            def index(i):
                bi, t = self._tile(i, k)
                return self.layer, jnp.where(t >= self.n_lat_tiles, self.ctx_row, bi), 0, 0
            return pl.BlockSpec((None, None, N_MOD, d), index)
        return [spec(k) for k in range(self.sub)]

    def out(self, width):
        return pl.BlockSpec((self.sub * TM, width), lambda i: (i, 0))

    def out_shape(self, width, dtype):
        return jax.ShapeDtypeStruct((self.b * self.n_vis * TM, width), dtype)

    def rows(self, k):
        return slice(k * TM, (k + 1) * TM)


def _flat(a):
    return a.reshape(-1, a.shape[-1])


def _ffn_kernel(*refs, j, dff, sub, ctx_tiles):
    if ctx_tiles is not None:
        lat_refs, ctx_refs, refs = refs[:sub], refs[sub:2 * sub], refs[sub:]
        xs = [jnp.where(ctx_tiles(k), c[...], x[...]) for k, (x, c) in enumerate(zip(lat_refs, ctx_refs))]
    else:
        xs = [x[...] for x in refs[:sub]]
    mod_refs = refs[sub:2 * sub]
    g_ref, wgu_ref, wd_ref, o_ref = refs[2 * sub:]
    g_pre, g_post = g_ref[2 * j:2 * j + 1, :], g_ref[2 * j + 1:2 * j + 2, :]
    mods = [[m[3 * j + i:3 * j + i + 1, :] for i in range(3)] for m in mod_refs]
    hs = [(_rms(x) * g_pre * (1.0 + m[1]) + m[0]).astype(BF16) for x, m in zip(xs, mods)]
    gus = [_dot(h, wgu_ref[...]) for h in hs]
    acts = [(_silu(gu[:, :dff]) * gu[:, dff:]).astype(BF16) for gu in gus]
    ys = [_dot(a, wd_ref[...]) for a in acts]
    for k, (x, m, y) in enumerate(zip(xs, mods, ys)):
        o_ref[k * TM:(k + 1) * TM, :] = x + 0.5 * m[2] * (_rms(y) * g_post)


def _ffn(plan, xz, mod, norm_g, wgu, wd, j, ctx=None):
    d = xz.shape[-1]
    dff = wd.shape[2]
    l = plan.layer
    if ctx is None:
        tok_specs, tok_args, ctx_tiles = plan.toks(d), [xz] * plan.sub, None
    else:
        tok_specs = [plan.tok_split(k, d, False) for k in range(plan.sub)] \
                    + [plan.tok_split(k, d, True) for k in range(plan.sub)]
        tok_args = [xz] * plan.sub + [ctx] * plan.sub
        ctx_tiles = lambda k: plan.is_ctx(pl.program_id(0), k)
    return pl.pallas_call(
        functools.partial(_ffn_kernel, j=j, dff=dff, sub=plan.sub, ctx_tiles=ctx_tiles),
        grid=plan.grid,
        in_specs=tok_specs + plan.mods(d)
                 + [_const_spec((None,) + norm_g.shape[1:], (l, 0, 0)),
                    _const_spec((None, None, d, 2 * dff), (l, j // 2, 0, 0)),
                    _const_spec((None, None, dff, d), (l, j // 2, 0, 0))],
        out_specs=plan.out(d),
        out_shape=plan.out_shape(d, F32),
        compiler_params=_params("parallel"),
        name=f"ffn{j}",
    )(*tok_args, *([mod] * plan.sub), norm_g, wgu, wd)


def _inproj_kernel(*refs, sub, layout):
    x_refs, mod_refs = refs[:sub], refs[sub:2 * sub]
    g_ref, w_ref, wg_ref = refs[2 * sub:2 * sub + 3]
    o_refs = refs[2 * sub + 3:]
    hs = [(_rms(x[...]) * g_ref[2:3, :] * (1.0 + m[4:5, :]) + m[3:4, :]).astype(BF16)
          for x, m in zip(x_refs, mod_refs)]
    perm = _time_perm(False)
    col = 0
    for o_ref, segments in zip(o_refs[:-1], layout):
        n = o_ref.shape[-1]
        for k, h in enumerate(hs):
            res = _dot(h, w_ref[:, col:col + n])
            parts, c0 = [], 0
            for width, post in segments:
                seg = res[:, c0:c0 + width]
                if post == "perm":
                    seg = _dot(perm, seg.astype(BF16))
                elif post is not None:
                    seg = post(seg)
                parts.append(seg.astype(o_ref.dtype))
                c0 += width
            o_ref[k * TM:(k + 1) * TM, :] = jnp.concatenate(parts, axis=1)
        col += n
    for k, h in enumerate(hs):
        o_refs[-1][k * TM:(k + 1) * TM, :] = _dot(h, wg_ref[...])


def _inproj(plan, xz, mod, norm_g, w_main, w_gate, layout):
    d = xz.shape[-1]
    ng = w_gate.shape[-1]
    l = plan.layer
    widths = [sum(w for w, _ in segments) for segments in layout]
    return pl.pallas_call(
        functools.partial(_inproj_kernel, sub=plan.sub, layout=layout),
        grid=plan.grid,
        in_specs=plan.toks(d) + plan.mods(d)
                 + [_const_spec((None,) + norm_g.shape[1:], (l, 0, 0)),
                    _const_spec((None, d, w_main.shape[-1]), (l, 0, 0)),
                    _const_spec((None, d, ng), (l, 0, 0))],
        out_specs=[plan.out(w) for w in widths] + [plan.out(ng)],
        out_shape=[plan.out_shape(w, BF16) for w in widths] + [plan.out_shape(ng, F32)],
        compiler_params=_params("parallel"),
        name="inproj",
    )(*([xz] * plan.sub), *([mod] * plan.sub), norm_g, w_main, w_gate)


N_MERGE_STREAMS = 12


def _merge_kernel(*refs, sub, ctx_tiles):
    x_refs, mod_refs = refs[:sub], refs[sub:2 * sub]
    tok = refs[2 * sub:(2 + N_MERGE_STREAMS) * sub]
    g_ref, mlg_ref, dng_ref, wb_ref, wo_ref, o_ref = refs[(2 + N_MERGE_STREAMS) * sub:]
    d = o_ref.shape[-1]
    f32 = lambda r: r[...].astype(F32)
    unperm = _time_perm(True)
    mixes = []
    for k in range(sub):
        mlf, mlb, mlo, lrf, lrb, lry, dlf, dlb, dcf, dcb, dnz, mg = (
            tok[s * sub + k] for s in range(N_MERGE_STREAMS))
        y_ml = _rms_heads(f32(mlf) + f32(mlb), ML_HEADS) * mlg_ref[...] * f32(mlo)
        y_lr = (_dot(unperm, lrf[...]) + _dot(unperm, lrb[...])) * f32(lry)
        dn_h = jnp.where(ctx_tiles(k), f32(dcf) + f32(dcb), f32(dlf) + f32(dlb))
        y_dn = _rms_heads(dn_h, DN_HEADS) * dng_ref[...] * f32(dnz)
        mix = None
        for n, y in enumerate((y_ml, y_lr, y_dn)):
            term = mg[:, n * d:(n + 1) * d].astype(F32) * _dot(y.astype(BF16), wb_ref[n])
            mix = term if mix is None else mix + term
        mixes.append(mix.astype(BF16))
    outs = [_dot(mix, wo_ref[...]) for mix in mixes]
    for k, out in enumerate(outs):
        o_ref[k * TM:(k + 1) * TM, :] = x_refs[k][...] + mod_refs[k][5:6, :] * (_rms(out) * g_ref[3:4, :])


def _merge(plan, xz, mod, norm_g, ml_hf, ml_hb, ml, lr_hf, lr_hb, lru, dn_lat, dn_all, dnz, mg,
           ml_g, dn_g, w_branch, w_out):
    d = xz.shape[-1]
    bw = w_branch.shape[2]
    l = plan.layer
    sub = range(plan.sub)
    whole = lambda a, w, c=0: (a, plan.toks(w, c))
    streams = [whole(ml_hf, bw), whole(ml_hb, bw), whole(ml, bw, ml.shape[-1] // bw - 1),
               whole(lr_hf, bw), whole(lr_hb, bw), whole(lru, bw, lru.shape[-1] // bw - 1),
               (dn_lat[0], [plan.tok_split(k, bw, False) for k in sub]),
               (dn_lat[1], [plan.tok_split(k, bw, False) for k in sub]),
               (dn_all[0], [plan.tok_ctx_rest(k, bw) for k in sub]),
               (dn_all[1], [plan.tok_ctx_rest(k, bw) for k in sub]),
               whole(dnz, bw), whole(mg, N_BRANCH * d)]
    assert len(streams) == N_MERGE_STREAMS
    tok_specs = [sp for _, specs in streams for sp in specs]
    tok_args = [a for a, _ in streams for _ in sub]
    return pl.pallas_call(
        functools.partial(_merge_kernel, sub=plan.sub, ctx_tiles=lambda k: plan.is_ctx(pl.program_id(0), k)),
        grid=plan.grid,
        in_specs=plan.toks(d) + plan.mods(d) + tok_specs
                 + [_const_spec((None,) + norm_g.shape[1:], (l, 0, 0)),
                    _const_spec((None, 1, bw), (l, 0, 0)),
                    _const_spec((None, 1, bw), (l, 0, 0)),
                    _const_spec((None, N_BRANCH, bw, d), (l, 0, 0, 0)),
                    _const_spec((None, d, d), (l, 0, 0))],
        out_specs=plan.out(d),
        out_shape=plan.out_shape(d, F32),
        compiler_params=_params("parallel"),
        name="merge",
    )(*([xz] * plan.sub), *([mod] * plan.sub), *tok_args, norm_g, ml_g, dn_g, w_branch, w_out)


SEG = TM // V7X_SUBLANES


def _time_perm(inverse):
    ri = lax.broadcasted_iota(jnp.int32, (TM, TM), 0)
    ci = lax.broadcasted_iota(jnp.int32, (TM, TM), 1)
    r, t = (ci, ri) if inverse else (ri, ci)
    return jnp.where(t == (r % V7X_SUBLANES) * SEG + r // V7X_SUBLANES, 1.0, 0.0).astype(BF16)


def _conv4_perm(x, before1, before2, after1, w):
    sub = lax.broadcasted_iota(jnp.int32, (V7X_SUBLANES, 1), 0)
    vrow = lambda i: x[V7X_SUBLANES * i:V7X_SUBLANES * (i + 1)]
    m1_edge = jnp.where(sub == 0, before1, pltpu.roll(vrow(SEG - 1), 1, axis=0))
    m2_edge = jnp.where(sub == 0, before2, pltpu.roll(vrow(SEG - 2), 1, axis=0))
    p1_edge = jnp.where(sub == V7X_SUBLANES - 1, after1, pltpu.roll(vrow(0), V7X_SUBLANES - 1, axis=0))
    x_m1 = jnp.concatenate([m1_edge, x[:-V7X_SUBLANES]], axis=0)
    x_m2 = jnp.concatenate([m2_edge, m1_edge, x[:-2 * V7X_SUBLANES]], axis=0)
    x_p1 = jnp.concatenate([x[V7X_SUBLANES:], p1_edge], axis=0)
    taps = {-2: x_m2, -1: x_m1, 0: x, 1: x_p1}
    acc = None
    for j in range(CONV_W):
        term = taps[j - CONV_LEFT] * w[j:j + 1]
        acc = term if acc is None else acc + term
    return acc


def _tile_scan(a, b, h0, rev):
    vrow = lambda x, i: x[V7X_SUBLANES * i:V7X_SUBLANES * (i + 1)]
    h = jnp.zeros_like(vrow(a, 0))
    p = jnp.ones_like(h)
    hs, ps = [None] * SEG, [None] * SEG
    for i in (range(SEG - 1, -1, -1) if rev else range(SEG)):
        ai = vrow(a, i)
        h = ai * h + vrow(b, i)
        p = ai * p
        hs[i], ps[i] = h, p
    carry = h0
    enter = [None] * V7X_SUBLANES
    for s in (range(V7X_SUBLANES - 1, -1, -1) if rev else range(V7X_SUBLANES)):
        enter[s] = carry
        carry = p[s:s + 1] * carry + h[s:s + 1]
    enter = jnp.concatenate(enter, axis=0)
    return jnp.concatenate([hi + pi * enter for hi, pi in zip(hs, ps)], axis=0), carry


def _lru_kernel(x_ref, cw_ref, cb_ref, w_ref, bias_ref, lam_ref, of_ref, ob_ref, *, n_lat, n_ctx):
    t, c = TM, x_ref.shape[-1]
    n_tiles = n_lat + n_ctx
    halo = 2 * V7X_SUBLANES

    def load_conv(tile):
        r0 = pl.multiple_of(tile * t, t)
        first = jnp.logical_or(tile == 0, tile == n_lat)
        last = jnp.logical_or(tile == n_lat - 1, tile == n_tiles - 1)
        x = x_ref[pl.ds(r0, t), :].astype(F32)
        p0 = pl.multiple_of(jnp.maximum(r0 - halo, 0), halo)
        n0 = pl.multiple_of(jnp.minimum(r0 + t, n_tiles * t - halo), halo)
        prev = jnp.where(first, 0.0, x_ref[pl.ds(p0, halo), :].astype(F32))
        nxt = jnp.where(last, 0.0, x_ref[pl.ds(n0, halo), :].astype(F32))
        before1, before2 = prev[halo - 1:halo], prev[V7X_SUBLANES - 1:V7X_SUBLANES]
        return r0, _conv4_perm(x, before1, before2, nxt[0:1], cw_ref[...]) + cb_ref[...]

    def direction(d, tile, h0, o_ref):
        r0, xc = load_conv(tile)
        z = _dot(xc.astype(BF16), w_ref[d]) + bias_ref[d]
        r, i = _sigmoid(z[:, :c]), _sigmoid(z[:, c:])
        la = (-LRU_C * _softplus(-lam_ref[d])) * r
        a = jnp.exp(la)
        bx = jnp.sqrt(jnp.tanh(-la) * (1.0 + a * a)) * (i * xc)
        h, carry = _tile_scan(a, bx, h0, d == 1)
        o_ref[pl.ds(r0, t), :] = h.astype(o_ref.dtype)
        return carry

    def step(s, carry):
        hf, hb = carry
        hf = direction(0, jnp.where(s < n_ctx, n_lat + s, s - n_ctx), hf, of_ref)
        hb = direction(1, n_tiles - 1 - s, hb, ob_ref)
        return hf, hb

    zero = jnp.zeros((1, c), F32)
    lax.fori_loop(0, n_tiles, step, (zero, zero))


def _lru(lru, conv_w, conv_b, w_gates, b_gates, lam, l, n_lat, n_ctx):
    b, lt, _ = lru.shape
    c = conv_w.shape[-1]
    seq = pl.BlockSpec((None, lt, c), lambda bi: (bi, 0, 0))
    return pl.pallas_call(
        functools.partial(_lru_kernel, n_lat=n_lat, n_ctx=n_ctx),
        grid=(b,),
        in_specs=[seq,
                  _const_spec((None, CONV_W, c), (l, 0, 0)),
                  _const_spec((None, 1, c), (l, 0, 0)),
                  _const_spec((None, 2, c, 2 * c), (l, 0, 0, 0)),
                  _const_spec((None, 2, 1, 2 * c), (l, 0, 0, 0)),
                  _const_spec((None, 2, 1, c), (l, 0, 0, 0))],
        out_specs=[seq, seq],
        out_shape=[jax.ShapeDtypeStruct((b, lt, c), BF16)] * 2,
        compiler_params=_params("parallel"),
        name="lru",
    )(lru, conv_w, conv_b, w_gates, b_gates, lam)


def _mlstm_kernel(qf_ref, kf_ref, vf_ref, qb_ref, kb_ref, vb_ref, grf_ref, grb_ref, br_ref,
                  hf_ref, hb_ref, c_ref, m_ref):
    @pl.when(pl.program_id(1) == 0)
    def _():
        c_ref[...] = jnp.zeros(c_ref.shape, F32)
        m_ref[...] = jnp.zeros(m_ref.shape, F32)

    nrow, tc, nqk = qf_ref.shape
    dk = nqk // ML_HEADS
    dv = vf_ref.shape[-1] // ML_HEADS
    nh = ML_HEADS
    ri = lax.broadcasted_iota(jnp.int32, (tc, tc), 0)
    ci = lax.broadcasted_iota(jnp.int32, (tc, tc), 1)
    causal = (ci <= ri, ci >= ri)
    lane_head = lax.broadcasted_iota(jnp.int32, (1, nqk), 1) // dk
    ones = jnp.ones((tc, dv), BF16)
    dirs = ((qf_ref, kf_ref, vf_ref, grf_ref), (qb_ref, kb_ref, vb_ref, grb_ref))
    seqs = [(r, d) for r in range(nrow) for d in range(2)]

    rows, v_rows, ws_rows, decs = [], [], [], []
    for r, d in seqs:
        rev = d == 1
        gr = dirs[d][3][r].T + br_ref[...]
        i_r = gr[nh * d:nh * (d + 1)]
        b_r = _cumsum_groups(-_softplus(-gr[nh * (2 + d):nh * (3 + d)]), 1, tc, rev)
        m_prev = m_ref[r, nh * d:nh * (d + 1), 0:1]
        m_t = b_r + jnp.maximum(m_prev, _cummax_groups(i_r - b_r, 1, tc, rev))
        b_end = b_r[:, 0:1] if rev else b_r[:, tc - 1:tc]
        lws = b_end - b_r + i_r
        m_new = jnp.maximum(b_end + m_prev, jnp.max(lws, axis=1, keepdims=True))
        rows += [b_r - m_t, jnp.exp(b_r + m_prev - m_t), jnp.exp(-m_t)]
        v_rows.append(b_r - i_r)
        ws_rows.append(jnp.exp(lws - m_new))
        decs.append(jnp.exp(b_end + m_prev - m_new))
        m_ref[r, nh * d:nh * (d + 1), :] = jnp.broadcast_to(m_new, (nh, m_ref.shape[-1]))
    cols = jnp.concatenate(rows, axis=0).T

    chains = [(s, h) for s in range(len(seqs)) for h in range(nh)]
    q_all = [dirs[d][0][r] * (dk ** -0.5) for r, d in seqs]
    k_all = [dirs[d][1][r] for r, d in seqs]
    v_all = [dirs[d][2][r] for r, d in seqs]
    kt_all = [k.astype(F32).T for k in k_all]
    c_all = [c_ref[r, d] for r, d in seqs]
    cb_all = [c.astype(BF16) for c in c_all]
    col = lambda s, j, h: cols[:, (3 * s + j) * nh + h:(3 * s + j) * nh + h + 1]
    qh = [jnp.where(lane_head == h, q_all[s], jnp.zeros_like(q_all[s])) for s, h in chains]
    vp = [jnp.concatenate([v_all[s][:, h * dv:(h + 1) * dv], ones], axis=1) for s, h in chains]
    s_raw = [_dot_nt(q, k_all[s]) for q, (s, h) in zip(qh, chains)]
    p = [(sr * jnp.exp(jnp.where(causal[seqs[s][1]], col(s, 0, h) - v_rows[s][h:h + 1], NEG))).astype(BF16)
         for sr, (s, h) in zip(s_raw, chains)]
    num = [_dot(pc, vc) + col(s, 1, h) * _dot(q, cb_all[s]) for pc, vc, q, (s, h) in zip(p, vp, qh, chains)]
    outs = [nm[:, :dv] / jnp.maximum(jnp.abs(nm[:, dv:]), col(s, 2, h)) for nm, (s, h) in zip(num, chains)]
    for vc, (s, h) in zip(vp, chains):
        r, d = seqs[s]
        kw = (kt_all[s][h * dk:(h + 1) * dk] * ws_rows[s][h:h + 1]).astype(BF16)
        c_ref[r, d, h * dk:(h + 1) * dk, :] = decs[s][h:h + 1] * c_all[s][h * dk:(h + 1) * dk] + _dot(kw, vc)
    for s, (r, d) in enumerate(seqs):
        o_ref = (hf_ref, hb_ref)[d]
        o_ref[r] = jnp.concatenate(outs[s * nh:(s + 1) * nh], axis=1).astype(o_ref.dtype)


def _mlstm(ml, gates, bias_r, l, n_lat, n_ctx):
    b, lt, _ = ml.shape
    ng = gates.shape[-1]
    n_tiles = n_lat + n_ctx
    dqk = ml.shape[-1] // 6
    fwd = lambda s: jnp.where(s < n_ctx, n_lat + s, s - n_ctx)
    bwd = lambda s: n_tiles - 1 - s
    rpb = ML_ROWS if b % ML_ROWS == 0 else 1

    def specs(tile):
        return [pl.BlockSpec((rpb, TM, dqk), lambda bi, s: (bi, tile(s), 0)),
                pl.BlockSpec((rpb, TM, dqk), lambda bi, s: (bi, tile(s), 1)),
                pl.BlockSpec((rpb, TM, 2 * dqk), lambda bi, s: (bi, tile(s), 1))]

    row = lambda tile: pl.BlockSpec((rpb, TM, ng), lambda bi, s: (bi, tile(s), 0))
    out = lambda tile: pl.BlockSpec((rpb, TM, 2 * dqk), lambda bi, s: (bi, tile(s), 0))
    return pl.pallas_call(
        _mlstm_kernel,
        grid=(b // rpb, n_tiles),
        in_specs=specs(fwd) + specs(bwd) + [row(fwd), row(bwd), _const_spec((None, ng, 1), (l, 0, 0))],
        out_specs=[out(fwd), out(bwd)],
        out_shape=[jax.ShapeDtypeStruct((b, lt, 2 * dqk), BF16)] * 2,
        scratch_shapes=[pltpu.VMEM((rpb, 2, dqk, 2 * (2 * dqk // ML_HEADS)), F32),
                        pltpu.VMEM((rpb, 2 * ML_HEADS, V7X_LANES), F32)],
        compiler_params=_params("parallel", "arbitrary"),
        name="mlstm",
    )(ml, ml, ml, ml, ml, ml, gates, gates, bias_r)


INV_BASE = 8


def _inv_masks(ri, ci):
    blk = lambda s: (ri // s) == (ci // s)
    as_bf = lambda m: jnp.where(m, 1.0, 0.0).astype(BF16)
    masks, s = [as_bf(blk(INV_BASE))], INV_BASE
    while s < CHUNK:
        masks.append(as_bf(jnp.logical_and(blk(2 * s), jnp.logical_not(blk(s)))))
        s *= 2
    return masks


def _tri_inv_many(a_list, eye, same, masks):
    nblk = a_list[0].shape[0] // CHUNK
    compact = lambda m: functools.reduce(lambda u, v: u + v, [m[i * CHUNK:(i + 1) * CHUNK] for i in range(nblk)])
    spread = lambda c: jnp.concatenate([c] * nblk, axis=0) * same
    eye_c = compact(eye)
    ps = [a * masks[0] for a in a_list]
    pcs = [compact(p) for p in ps]
    xcs = [eye_c - pc for pc in pcs]
    for _ in range(INV_BASE.bit_length() - 2):
        pcs = [_dot(pc, p).astype(BF16) for pc, p in zip(pcs, ps)]
        ps = [spread(pc) for pc in pcs]
        xcs = [_dot(xc, eye + p).astype(BF16) for xc, p in zip(xcs, ps)]
    for m in masks[1:]:
        ys = [_dot(xc, a * m).astype(BF16) for xc, a in zip(xcs, a_list)]
        xcs = [xc - _dot(y, spread(xc)).astype(BF16) for y, xc in zip(ys, xcs)]
    return [spread(xc) for xc in xcs]


def _dn_prep_kernel(xl_ref, xp_ref, xn_ref, xc_ref, grow_ref, cw_ref, arow_ref,
                    u_ref, w_ref, qe_ref, kd_ref, qk_ref, eg_ref, *, n_lat_groups):
    g = pl.program_id(1)
    is_ctx = g >= n_lat_groups
    width = xl_ref.shape[-1]
    dh = width // (3 * DN_HEADS)
    n = DN_HEADS * CHUNK
    halo = 2 * V7X_SUBLANES
    x = jnp.where(is_ctx, xc_ref[...], xl_ref[...]).reshape(DN_GROUP * CHUNK, width)
    no_prev = jnp.logical_or(is_ctx, g == 0)
    no_next = jnp.logical_or(is_ctx, g == n_lat_groups - 1)
    prev8 = jnp.where(no_prev, 0.0, xp_ref[CHUNK - halo:, :].astype(F32)[V7X_SUBLANES:])
    nxt8 = jnp.where(no_next, 0.0, xn_ref[:halo, :].astype(F32)[:V7X_SUBLANES])
    xc = _silu(_conv4(x, prev8, nxt8, cw_ref[...]))

    ri = lax.broadcasted_iota(jnp.int32, (n, n), 0)
    ci = lax.broadcasted_iota(jnp.int32, (n, n), 1)
    same = (ri // CHUNK) == (ci // CHUNK)
    eye = jnp.where(ri == ci, 1.0, 0.0).astype(BF16)
    incl =(jnp.logical_and(same, ci <= ri), jnp.logical_and(same, ci >= ri))
    strict = (jnp.logical_and(same, ci < ri), jnp.logical_and(same, ci > ri))
    masks = _inv_masks(ri, ci)

    qn, kn, v, kk, qk = [], [], [], [], []
    for i in range(DN_GROUP):
        xi = xc[i * CHUNK:(i + 1) * CHUNK]
        q, k, vi = (_stack_heads(xi[:, j * DN_HEADS * dh:(j + 1) * DN_HEADS * dh], DN_HEADS) for j in range(3))
        qn.append(q * lax.rsqrt(jnp.sum(q * q, axis=-1, keepdims=True) + EPS) * (dh ** -0.5))
        kn.append(k * lax.rsqrt(jnp.sum(k * k, axis=-1, keepdims=True) + EPS))
        v.append(vi)
        kb = kn[i].astype(BF16)
        kk.append(_dot_nt(kb, kb))
        qk.append(_dot_nt(qn[i].astype(BF16), kb))

    systems = [(i, d) for i in range(DN_GROUP) for d in range(2)]
    rows, cs_rows = [], []
    for i, d in systems:
        raw = grow_ref[i]
        beta = _sigmoid(raw[d:d + 1])
        g_r = -jnp.exp(arow_ref[d, 0:1, :]) * _softplus(raw[2 + d:3 + d] + arow_ref[d, 1:2, :])
        cs = _cumsum_groups(g_r, 1, CHUNK, d == 1)
        tot = cs + _cumsum_groups(g_r, 1, CHUNK, d == 0) - g_r
        eg = jnp.exp(cs)
        rows += [cs, beta, eg, jnp.exp(tot - cs), beta * eg]
        cs_rows.append(cs)
        eg_ref[d, i] = jnp.exp(tot)
    n_col = len(rows) // len(systems)
    cols = jnp.concatenate(rows, axis=0).T

    a_list, rhs = [], []
    for s, (i, d) in enumerate(systems):
        cs_c, beta_c, eg_c, ekd_c, beg_c = (cols[:, n_col * s + j:n_col * s + j + 1] for j in range(n_col))
        gam = jnp.exp(jnp.where(incl[d], cs_c - cs_rows[s], NEG))
        a_list.append((jnp.where(strict[d], kk[i], 0.0) * gam * beta_c).astype(BF16))
        rhs.append(jnp.concatenate([beta_c * v[i], beg_c * kn[i]], axis=1).astype(BF16))
        qk_ref[d, i] = (qk[i] * gam).astype(qk_ref.dtype)
        qe_ref[d, i] = (qn[i] * eg_c).astype(qe_ref.dtype)
        kd_ref[d, i] = (kn[i] * ekd_c).astype(kd_ref.dtype)
    same_bf = jnp.where(same, 1.0, 0.0).astype(BF16)
    for (i, d), tinv, r in zip(systems, _tri_inv_many(a_list, eye, same_bf, masks), rhs):
        sol = _dot(tinv, r)
        u_ref[d, i] = sol[:, :dh].astype(u_ref.dtype)
        w_ref[d, i] = sol[:, dh:].astype(w_ref.dtype)


def _dn_prep(x_cols, x_rows, grow, conv_w, arow, l):
    b, nc, _, width = x_rows.shape
    n_cols = x_cols.shape[1]
    n_groups = nc // DN_GROUP
    n_lat_groups = n_cols // DN_GROUP
    assert n_groups == n_lat_groups + 1
    n = DN_HEADS * CHUNK
    dh = width // (3 * DN_HEADS)
    lat_g = lambda g: jnp.minimum(g, n_lat_groups - 1)
    out = lambda w, dt: (jax.ShapeDtypeStruct((b, 2, nc, n, w), dt),
                         pl.BlockSpec((None, 2, DN_GROUP, n, w), lambda bi, g: (bi, 0, g, 0, 0)))
    outs = [out(dh, BF16)] * 4 + [out(n, BF16)]
    outs.append((jax.ShapeDtypeStruct((b, 2, nc, 1, n), F32),
                 pl.BlockSpec((None, 2, DN_GROUP, 1, n), lambda bi, g: (bi, 0, g, 0, 0))))
    chunk = lambda idx: pl.BlockSpec((None, None, CHUNK, width), lambda bi, g: (bi, idx(g), 0, 0))
    return pl.pallas_call(
        functools.partial(_dn_prep_kernel, n_lat_groups=n_lat_groups),
        grid=(b, n_groups),
        in_specs=[pl.BlockSpec((None, DN_GROUP, CHUNK, width), lambda bi, g: (bi, lat_g(g), 0, 0)),
                  chunk(lambda g: jnp.maximum(lat_g(g) * DN_GROUP - 1, 0)),
                  chunk(lambda g: jnp.minimum(lat_g(g) * DN_GROUP + DN_GROUP, n_cols - 1)),
                  pl.BlockSpec((None, DN_GROUP, CHUNK, width), lambda bi, g: (bi, n_groups - 1, 0, 0)),
                  pl.BlockSpec((None, DN_GROUP, 4, n), lambda bi, g: (bi, g, 0, 0)),
                  _const_spec((None, CONV_W, width), (l, 0, 0)),
                  _const_spec((None, 2, 2, n), (l, 0, 0, 0))],
        out_specs=[o[1] for o in outs],
        out_shape=[o[0] for o in outs],
        compiler_params=_params("parallel", "parallel"),
        name="dn_prep",
    )(x_cols, x_cols, x_cols, x_rows, grow, conv_w, arow)


def _dn_scan_kernel(*refs):
    ins, (of_ref, ob_ref, s_ref) = refs[:12], refs[12:]

    @pl.when(pl.program_id(1) == 0)
    def _():
        s_ref[...] = jnp.zeros(s_ref.shape, F32)

    rows = [slice(h * CHUNK, (h + 1) * CHUNK) for h in range(DN_HEADS)]
    seqs = [(r, d) for r in range(of_ref.shape[0]) for d in range(2)]
    cps = of_ref.shape[1]
    for step in range(cps):
        pos = lambda d: cps - 1 - step if d == 1 else step
        get = lambda j, r, d: ins[6 * d + j][r, pos(d)]
        sb = [[s_ref[r, d, h].astype(BF16) for h in range(DN_HEADS)] for r, d in seqs]
        ws = [jnp.concatenate([_dot(get(1, r, d)[rows[h]], sb[i][h]) for h in range(DN_HEADS)], axis=0)
              for i, (r, d) in enumerate(seqs)]
        qs = [jnp.concatenate([_dot(get(2, r, d)[rows[h]], sb[i][h]) for h in range(DN_HEADS)], axis=0)
              for i, (r, d) in enumerate(seqs)]
        vnew = [(get(0, r, d).astype(F32) - w).astype(BF16) for w, (r, d) in zip(ws, seqs)]
        o = [q + _dot(get(4, r, d), vn) for q, vn, (r, d) in zip(qs, vnew, seqs)]
        for vn, (r, d) in zip(vnew, seqs):
            kd, eg = get(3, r, d), get(5, r, d)
            for h in range(DN_HEADS):
                s_ref[r, d, h] = (eg[:, h * CHUNK:h * CHUNK + 1] * s_ref[r, d, h]
                                  + _dot_tn(kd[rows[h]], vn[rows[h]]))
        for oi, (r, d) in zip(o, seqs):
            o_ref = (of_ref, ob_ref)[d]
            o_ref[r, pos(d)] = _unstack_heads(oi, DN_HEADS).astype(o_ref.dtype)


def _dn_scan(prep, n_lat_chunks):
    b, _, nc, _, dh = prep[0].shape
    ow = DN_HEADS * dh
    cps = 2
    nblk, n_lat_blk = nc // cps, n_lat_chunks // cps
    n_ctx_blk = nblk - n_lat_blk
    fwd = lambda s: jnp.where(s < n_ctx_blk, n_lat_blk + s, s - n_ctx_blk)
    bwd = lambda s: nblk - 1 - s

    rpb = DN_SCAN_ROWS if b % DN_SCAN_ROWS == 0 else 1

    def spec(a, d, chunk):
        return pl.BlockSpec((rpb, None, cps) + a.shape[3:], lambda bi, s: (bi, d, chunk(s), 0, 0))

    out = lambda chunk: pl.BlockSpec((rpb, cps, CHUNK, ow), lambda bi, s: (bi, chunk(s), 0, 0))
    return pl.pallas_call(
        _dn_scan_kernel,
        grid=(b // rpb, nblk),
        in_specs=[spec(a, 0, fwd) for a in prep] + [spec(a, 1, bwd) for a in prep],
        out_specs=[out(fwd), out(bwd)],
        out_shape=[jax.ShapeDtypeStruct((b, nc, CHUNK, ow), BF16)] * 2,
        scratch_shapes=[pltpu.VMEM((rpb, 2, DN_HEADS, dh, dh), F32)],
        compiler_params=_params("parallel", "arbitrary"),
        name="dn_scan",
    )(*prep, *prep)


def _block_diag(w):
    n, i, j = w.shape
    return jnp.einsum('nij,nm->nimj', w, jnp.eye(n, dtype=w.dtype)).reshape(n * i, n * j)


def _dn_gate_rows(raw, n_lat):
    b = raw.shape[0]
    lat = raw[:, :n_lat].reshape(b, n_lat // CHUNK, CHUNK, 4, DN_HEADS)
    ctx = raw[:, n_lat:].reshape(b, -1, CHUNK, 4, DN_HEADS)
    row = jnp.concatenate([lat.transpose(0, 2, 3, 4, 1), ctx.transpose(0, 1, 3, 4, 2)], axis=1)
    return row.reshape(b, row.shape[1], 4, DN_HEADS * CHUNK)


def kernel(x, c, ctx, c_ctx, w_mod, b_mod, norm_g, ffn_w_gu, ffn_w_down, w_in, ml_gate_b, ml_norm_g,
           lru_conv_w, lru_conv_b, lru_w_a, lru_b_a, lru_w_x, lru_b_x, lru_lambda, dn_conv_w,
           dn_a_log, dn_dt_bias, dn_norm_g, w_branch, w_out):
    b, n_lat, d = x.shape
    n_ctx = ctx.shape[1]
    depth = w_mod.shape[0]
    bw = w_branch.shape[2]
    assert n_lat == CHUNK * CHUNK and n_lat % TM == 0 and n_ctx % TM == 0 and n_ctx == CHUNK * DN_GROUP
    n_lat_tiles, n_ctx_tiles = n_lat // TM, n_ctx // TM
    n_tiles = n_lat_tiles + n_ctx_tiles

    ctx_row = b
    n_rows = -(-(b + 1) // V7X_SUBLANES) * V7X_SUBLANES
    cc = jnp.zeros((n_rows, d), F32).at[:b].set(c).at[b].set(c_ctx)
    mod = _mod_table(cc, w_mod, b_mod).reshape(depth, n_rows, N_MOD, d)

    dqk = bw // 2
    edges = [0]
    for wdt in (dqk, dqk, bw, bw, N_GATES, bw, bw, bw, bw, bw, bw, N_GATES, N_BRANCH * d):
        edges.append(edges[-1] + wdt)
    piece = lambda i, j: w_in[:, :, edges[i]:edges[j]]
    layout = (((2 * bw, None), (bw, _sigmoid)),
              ((bw, "perm"), (bw, _gelu_tanh)),
              ((3 * bw, None),),
              ((bw, _silu),),
              ((N_BRANCH * d, _sigmoid),))
    w_main = jnp.concatenate([piece(0, 4), piece(5, 7), piece(7, 11), piece(12, 13)], axis=-1).astype(BF16)
    w_gate = jnp.concatenate([piece(4, 5), piece(11, 12)], axis=-1).astype(BF16)
    wgu = ffn_w_gu.astype(BF16)
    wdn = ffn_w_down.astype(BF16)
    wbr = w_branch.astype(BF16)
    wout = w_out.astype(BF16)
    ml_bias = jnp.concatenate([ml_gate_b.reshape(depth, N_GATES), jnp.zeros((depth, N_GATES), F32)], axis=-1)
    ml_bias_r = ml_bias[:, :, None]
    ml_g = ml_norm_g[:, None, :]
    dn_g = jnp.tile(dn_norm_g, (1, DN_HEADS))[:, None, :]
    lru_w = jnp.stack([jnp.concatenate([jax.vmap(_block_diag)(lru_w_a[:, dd]), jax.vmap(_block_diag)(lru_w_x[:, dd])],
                                       axis=-1) for dd in range(2)], axis=1).astype(BF16)
    lru_b = jnp.concatenate([lru_b_a, lru_b_x], axis=-1)[:, :, None, :]
    lru_lam = lru_lambda[:, :, None, :]
    lru_cb = lru_conv_b[:, None, :]
    dn_arow = jnp.repeat(jnp.stack([dn_a_log, dn_dt_bias], axis=2), CHUNK, axis=3)

    rows3 = lambda a: a.reshape(b, n_lat + n_ctx, a.shape[-1])
    for l in range(depth):
        full = _TilePlan(b, n_tiles, n_lat_tiles, n_tiles, ctx_row, l)
        if l == 0:
            xz = _ffn(full, _flat(x), mod, norm_g, wgu, wdn, 0, ctx=_flat(ctx))
        else:
            xz = _ffn(full, xz, mod, norm_g, wgu, wdn, 0)
        ml, lru, dnqkv, dnz, mg, gates = _inproj(full, xz, mod, norm_g, w_main, w_gate, layout)
        gates = rows3(gates)
        ml_hf, ml_hb = _mlstm(rows3(ml), gates, ml_bias_r, l, n_lat_tiles, n_ctx_tiles)
        lr_hf, lr_hb = _lru(rows3(lru), lru_conv_w, lru_cb, lru_w, lru_b, lru_lam, l, n_lat_tiles, n_ctx_tiles)
        n_cols = n_lat // CHUNK
        dq = rows3(dnqkv)
        x_rows = dq.reshape(b, -1, CHUNK, dq.shape[-1])
        x_cols = x_rows.swapaxes(1, 2)
        prep = _dn_prep(x_cols, x_rows, _dn_gate_rows(gates[:, :, N_GATES:], n_lat), dn_conv_w, dn_arow, l)
        dn_all = _dn_scan(prep, n_cols)
        dn_lat = [_flat(o[:, :n_cols].swapaxes(1, 2)) for o in dn_all]
        xz = _merge(full, xz, mod, norm_g, _flat(ml_hf), _flat(ml_hb), ml, _flat(lr_hf), _flat(lr_hb), lru,
                    dn_lat, [_flat(o) for o in dn_all], dnz, mg, ml_g, dn_g, wbr, wout)
        visit = n_lat_tiles if l == depth - 1 else n_tiles
        xz = _ffn(_TilePlan(b, n_tiles, n_lat_tiles, visit, ctx_row, l), xz, mod, norm_g, wgu, wdn, 2)
    return xz.reshape(b, n_lat, d)
```

```python
import functools
import math

import jax
import jax.numpy as jnp
from jax import lax
from jax.experimental import pallas as pl
from jax.experimental.pallas import tpu as pltpu

F32 = jnp.float32
BF16 = jnp.bfloat16

EPS = 1e-6
N_MOD = 9
N_BRANCH = 3
CONV_W = 4
CONV_LEFT = 2
ML_HEADS = 4
LRU_C = 8.0
DN_HEADS = 4
CHUNK = 64
DN_GROUP = 4
DN_SCAN_ROWS = 4
ML_ROWS = 2
N_GATES = 16

V7X_SUBLANES = 8
V7X_LANES = 128
V7X_VMEM_BYTES = 64 * 1024 * 1024
VMEM_LIMIT = V7X_VMEM_BYTES - 8 * 1024 * 1024

TM = 256
NEG = -1e30


def _sigmoid(x):
    return 0.5 * jnp.tanh(0.5 * x) + 0.5


def _silu(x):
    return x * _sigmoid(x)


def _softplus(x):
    return jnp.maximum(x, 0.0) + jnp.log(1.0 + jnp.exp(-jnp.abs(x)))


def _rms(x):
    return x * lax.rsqrt(jnp.mean(x * x, axis=-1, keepdims=True) + EPS)


def _rms_heads(x, n_heads):
    hd = x.shape[-1] // n_heads
    return jnp.concatenate([_rms(x[:, h * hd:(h + 1) * hd]) for h in range(n_heads)], axis=-1)


def _gelu_tanh(x):
    return 0.5 * x * (1.0 + jnp.tanh(math.sqrt(2.0 / math.pi) * (x + 0.044715 * (x * x * x))))


def _dot(a, b):
    return jnp.dot(a, b, preferred_element_type=F32)


def _dot_nt(a, b):
    return lax.dot_general(a, b, (((1,), (1,)), ((), ())), preferred_element_type=F32)


def _dot_tn(a, b):
    return lax.dot_general(a, b, (((0,), (0,)), ((), ())), preferred_element_type=F32)


def _params(*sem):
    return pltpu.CompilerParams(dimension_semantics=sem, vmem_limit_bytes=VMEM_LIMIT)


def _const_spec(block, index):
    return pl.BlockSpec(block, lambda *_: index, pipeline_mode=pl.Buffered(1))


def _stack_heads(x, n):
    w = x.shape[1] // n
    return jnp.concatenate([x[:, h * w:(h + 1) * w] for h in range(n)], axis=0)


def _unstack_heads(x, n):
    t = x.shape[0] // n
    return jnp.concatenate([x[h * t:(h + 1) * t] for h in range(n)], axis=1)


def _cumsum_groups(x, axis, period, rev):
    n = x.shape[axis]
    idx = lax.broadcasted_iota(jnp.int32, x.shape, axis) % period
    sh = 1
    while sh < period:
        if rev:
            x = x + jnp.where(idx < period - sh, pltpu.roll(x, n - sh, axis=axis), 0.0)
        else:
            x = x + jnp.where(idx >= sh, pltpu.roll(x, sh, axis=axis), 0.0)
        sh *= 2
    return x


def _cummax_groups(x, axis, period, rev):
    n = x.shape[axis]
    idx = lax.broadcasted_iota(jnp.int32, x.shape, axis) % period
    sh = 1
    while sh < period:
        if rev:
            x = jnp.maximum(x, jnp.where(idx < period - sh, pltpu.roll(x, n - sh, axis=axis), NEG))
        else:
            x = jnp.maximum(x, jnp.where(idx >= sh, pltpu.roll(x, sh, axis=axis), NEG))
        sh *= 2
    return x


def _shift_rows(xb, prev8, nxt8, off):
    t = xb.shape[0]
    ri = lax.broadcasted_iota(jnp.int32, (t, t), 0)
    ci = lax.broadcasted_iota(jnp.int32, (t, t), 1)
    shifted = _dot(jnp.where(ci == ri + off, 1.0, 0.0).astype(BF16), xb)
    row8 = lax.broadcasted_iota(jnp.int32, (V7X_SUBLANES, 1), 0)
    if off < 0:
        edge = jnp.where(row8 < -off, pltpu.roll(prev8, -off, axis=0), shifted[:V7X_SUBLANES])
        return jnp.concatenate([edge, shifted[V7X_SUBLANES:]], axis=0)
    edge = jnp.where(row8 >= V7X_SUBLANES - off, pltpu.roll(nxt8, V7X_SUBLANES - off, axis=0),
                     shifted[t - V7X_SUBLANES:])
    return jnp.concatenate([shifted[:t - V7X_SUBLANES], edge], axis=0)


def _conv4(xb, prev8, nxt8, w):
    acc = xb.astype(F32) * w[CONV_LEFT:CONV_LEFT + 1]
    for j in range(CONV_W):
        if j != CONV_LEFT:
            acc = acc + _shift_rows(xb, prev8, nxt8, j - CONV_LEFT) * w[j:j + 1]
    return acc


def _mod_kernel(c_ref, w_ref, b_ref, o_ref):
    s = _silu(c_ref[...]).astype(BF16)
    o_ref[...] = _dot(s, w_ref[...].astype(BF16)) + b_ref[...]


def _mod_table(cc, w_mod, b_mod):
    depth, d, nd = w_mod.shape
    r = cc.shape[0]
    tn = nd // 4
    return pl.pallas_call(
        _mod_kernel,
        grid=(depth, nd // tn),
        in_specs=[pl.BlockSpec((r, d), lambda l, j: (0, 0)),
                  pl.BlockSpec((None, d, tn), lambda l, j: (l, 0, j)),
                  pl.BlockSpec((None, 1, tn), lambda l, j: (l, 0, j))],
        out_specs=pl.BlockSpec((None, r, tn), lambda l, j: (l, 0, j)),
        out_shape=jax.ShapeDtypeStruct((depth, r, nd), F32),
        compiler_params=_params("parallel", "parallel"),
        name="mod_table",
    )(cc, w_mod, b_mod.reshape(depth, 1, nd))


class _TilePlan:
    def __init__(self, b, n_tiles, n_lat_tiles, n_vis, ctx_row, layer):
        self.b, self.n_tiles, self.n_lat_tiles, self.n_vis = b, n_tiles, n_lat_tiles, n_vis
        self.ctx_row, self.layer = ctx_row, layer
        self.sub = 2 if (b * n_vis) % 2 == 0 else 1
        self.grid = (b * n_vis // self.sub,)

    def _tile(self, i, k):
        tid = i * self.sub + k
        return tid // self.n_vis, tid % self.n_vis

    def tok(self, k, width, col=0):
        def index(i):
            bi, t = self._tile(i, k)
            return bi * self.n_tiles + t, col
        return pl.BlockSpec((TM, width), index)

    def toks(self, width, col=0):
        return [self.tok(k, width, col) for k in range(self.sub)]

    def is_ctx(self, i, k):
        return self._tile(i, k)[1] >= self.n_lat_tiles

    def tok_ctx_rest(self, k, width):
        def index(i):
            bi, t = self._tile(i, k)
            return bi * self.n_tiles + jnp.maximum(t, self.n_lat_tiles), 0
        return pl.BlockSpec((TM, width), index)

    def tok_split(self, k, width, ctx):
        n_ctx_tiles = self.n_tiles - self.n_lat_tiles

        def index(i):
            bi, t = self._tile(i, k)
            if ctx:
                return bi * n_ctx_tiles + jnp.clip(t - self.n_lat_tiles, 0, n_ctx_tiles - 1), 0
            return bi * self.n_lat_tiles + jnp.minimum(t, self.n_lat_tiles - 1), 0
        return pl.BlockSpec((TM, width), index)

    def mods(self, d):
        def spec(k):
            def index(i):
                bi, t = self._tile(i, k)
                return self.layer, jnp.where(t >= self.n_lat_tiles, self.ctx_row, bi), 0, 0
            return pl.BlockSpec((None, None, N_MOD, d), index)
        return [spec(k) for k in range(self.sub)]

    def out(self, width):
        return pl.BlockSpec((self.sub * TM, width), lambda i: (i, 0))

    def out_shape(self, width, dtype):
        return jax.ShapeDtypeStruct((self.b * self.n_vis * TM, width), dtype)

    def rows(self, k):
        return slice(k * TM, (k + 1) * TM)


def _flat(a):
    return a.reshape(-1, a.shape[-1])


def _ffn_kernel(*refs, j, dff, sub, ctx_tiles):
    if ctx_tiles is not None:
        lat_refs, ctx_refs, refs = refs[:sub], refs[sub:2 * sub], refs[sub:]
        xs = [jnp.where(ctx_tiles(k), c[...], x[...]) for k, (x, c) in enumerate(zip(lat_refs, ctx_refs))]
    else:
        xs = [x[...] for x in refs[:sub]]
    mod_refs = refs[sub:2 * sub]
    g_ref, wgu_ref, wd_ref, o_ref = refs[2 * sub:]
    g_pre, g_post = g_ref[2 * j:2 * j + 1, :], g_ref[2 * j + 1:2 * j + 2, :]
    mods = [[m[3 * j + i:3 * j + i + 1, :] for i in range(3)] for m in mod_refs]
    hs = [(_rms(x) * g_pre * (1.0 + m[1]) + m[0]).astype(BF16) for x, m in zip(xs, mods)]
    gus = [_dot(h, wgu_ref[...]) for h in hs]
    acts = [(_silu(gu[:, :dff]) * gu[:, dff:]).astype(BF16) for gu in gus]
    ys = [_dot(a, wd_ref[...]) for a in acts]
    for k, (x, m, y) in enumerate(zip(xs, mods, ys)):
        o_ref[k * TM:(k + 1) * TM, :] = x + 0.5 * m[2] * (_rms(y) * g_post)


def _ffn(plan, xz, mod, norm_g, wgu, wd, j, ctx=None):
    d = xz.shape[-1]
    dff = wd.shape[2]
    l = plan.layer
    if ctx is None:
        tok_specs, tok_args, ctx_tiles = plan.toks(d), [xz] * plan.sub, None
    else:
        tok_specs = [plan.tok_split(k, d, False) for k in range(plan.sub)] \
                    + [plan.tok_split(k, d, True) for k in range(plan.sub)]
        tok_args = [xz] * plan.sub + [ctx] * plan.sub
        ctx_tiles = lambda k: plan.is_ctx(pl.program_id(0), k)
    return pl.pallas_call(
        functools.partial(_ffn_kernel, j=j, dff=dff, sub=plan.sub, ctx_tiles=ctx_tiles),
        grid=plan.grid,
        in_specs=tok_specs + plan.mods(d)
                 + [_const_spec((None,) + norm_g.shape[1:], (l, 0, 0)),
                    _const_spec((None, None, d, 2 * dff), (l, j // 2, 0, 0)),
                    _const_spec((None, None, dff, d), (l, j // 2, 0, 0))],
        out_specs=plan.out(d),
        out_shape=plan.out_shape(d, F32),
        compiler_params=_params("parallel"),
        name=f"ffn{j}",
    )(*tok_args, *([mod] * plan.sub), norm_g, wgu, wd)


def _inproj_kernel(*refs, sub, layout):
    x_refs, mod_refs = refs[:sub], refs[sub:2 * sub]
    g_ref, w_ref, wg_ref = refs[2 * sub:2 * sub + 3]
    o_refs = refs[2 * sub + 3:]
    hs = [(_rms(x[...]) * g_ref[2:3, :] * (1.0 + m[4:5, :]) + m[3:4, :]).astype(BF16)
          for x, m in zip(x_refs, mod_refs)]
    perm = _time_perm(False)
    col = 0
    for o_ref, segments in zip(o_refs[:-1], layout):
        n = o_ref.shape[-1]
        for k, h in enumerate(hs):
            res = _dot(h, w_ref[:, col:col + n])
            parts, c0 = [], 0
            for width, post in segments:
                seg = res[:, c0:c0 + width]
                if post == "perm":
                    seg = _dot(perm, seg.astype(BF16))
                elif post is not None:
                    seg = post(seg)
                parts.append(seg.astype(o_ref.dtype))
                c0 += width
            o_ref[k * TM:(k + 1) * TM, :] = jnp.concatenate(parts, axis=1)
        col += n
    for k, h in enumerate(hs):
        o_refs[-1][k * TM:(k + 1) * TM, :] = _dot(h, wg_ref[...])


def _inproj(plan, xz, mod, norm_g, w_main, w_gate, layout):
    d = xz.shape[-1]
    ng = w_gate.shape[-1]
    l = plan.layer
    widths = [sum(w for w, _ in segments) for segments in layout]
    return pl.pallas_call(
        functools.partial(_inproj_kernel, sub=plan.sub, layout=layout),
        grid=plan.grid,
        in_specs=plan.toks(d) + plan.mods(d)
                 + [_const_spec((None,) + norm_g.shape[1:], (l, 0, 0)),
                    _const_spec((None, d, w_main.shape[-1]), (l, 0, 0)),
                    _const_spec((None, d, ng), (l, 0, 0))],
        out_specs=[plan.out(w) for w in widths] + [plan.out(ng)],
        out_shape=[plan.out_shape(w, BF16) for w in widths] + [plan.out_shape(ng, F32)],
        compiler_params=_params("parallel"),
        name="inproj",
    )(*([xz] * plan.sub), *([mod] * plan.sub), norm_g, w_main, w_gate)


N_MERGE_STREAMS = 12


def _merge_kernel(*refs, sub, ctx_tiles):
    x_refs, mod_refs = refs[:sub], refs[sub:2 * sub]
    tok = refs[2 * sub:(2 + N_MERGE_STREAMS) * sub]
    g_ref, mlg_ref, dng_ref, wb_ref, wo_ref, o_ref = refs[(2 + N_MERGE_STREAMS) * sub:]
    d = o_ref.shape[-1]
    f32 = lambda r: r[...].astype(F32)
    unperm = _time_perm(True)
    mixes = []
    for k in range(sub):
        mlf, mlb, mlo, lrf, lrb, lry, dlf, dlb, dcf, dcb, dnz, mg = (
            tok[s * sub + k] for s in range(N_MERGE_STREAMS))
        y_ml = _rms_heads(f32(mlf) + f32(mlb), ML_HEADS) * mlg_ref[...] * f32(mlo)
        y_lr = (_dot(unperm, lrf[...]) + _dot(unperm, lrb[...])) * f32(lry)
        dn_h = jnp.where(ctx_tiles(k), f32(dcf) + f32(dcb), f32(dlf) + f32(dlb))
        y_dn = _rms_heads(dn_h, DN_HEADS) * dng_ref[...] * f32(dnz)
        mix = None
        for n, y in enumerate((y_ml, y_lr, y_dn)):
            term = mg[:, n * d:(n + 1) * d].astype(F32) * _dot(y.astype(BF16), wb_ref[n])
            mix = term if mix is None else mix + term
        mixes.append(mix.astype(BF16))
    outs = [_dot(mix, wo_ref[...]) for mix in mixes]
    for k, out in enumerate(outs):
        o_ref[k * TM:(k + 1) * TM, :] = x_refs[k][...] + mod_refs[k][5:6, :] * (_rms(out) * g_ref[3:4, :])


def _merge(plan, xz, mod, norm_g, ml_hf, ml_hb, ml, lr_hf, lr_hb, lru, dn_lat, dn_all, dnz, mg,
           ml_g, dn_g, w_branch, w_out):
    d = xz.shape[-1]
    bw = w_branch.shape[2]
    l = plan.layer
    sub = range(plan.sub)
    whole = lambda a, w, c=0: (a, plan.toks(w, c))
    streams = [whole(ml_hf, bw), whole(ml_hb, bw), whole(ml, bw, ml.shape[-1] // bw - 1),
               whole(lr_hf, bw), whole(lr_hb, bw), whole(lru, bw, lru.shape[-1] // bw - 1),
               (dn_lat[0], [plan.tok_split(k, bw, False) for k in sub]),
               (dn_lat[1], [plan.tok_split(k, bw, False) for k in sub]),
               (dn_all[0], [plan.tok_ctx_rest(k, bw) for k in sub]),
               (dn_all[1], [plan.tok_ctx_rest(k, bw) for k in sub]),
               whole(dnz, bw), whole(mg, N_BRANCH * d)]
    assert len(streams) == N_MERGE_STREAMS
    tok_specs = [sp for _, specs in streams for sp in specs]
    tok_args = [a for a, _ in streams for _ in sub]
    return pl.pallas_call(
        functools.partial(_merge_kernel, sub=plan.sub, ctx_tiles=lambda k: plan.is_ctx(pl.program_id(0), k)),
        grid=plan.grid,
        in_specs=plan.toks(d) + plan.mods(d) + tok_specs
                 + [_const_spec((None,) + norm_g.shape[1:], (l, 0, 0)),
                    _const_spec((None, 1, bw), (l, 0, 0)),
                    _const_spec((None, 1, bw), (l, 0, 0)),
                    _const_spec((None, N_BRANCH, bw, d), (l, 0, 0, 0)),
                    _const_spec((None, d, d), (l, 0, 0))],
        out_specs=plan.out(d),
        out_shape=plan.out_shape(d, F32),
        compiler_params=_params("parallel"),
        name="merge",
    )(*([xz] * plan.sub), *([mod] * plan.sub), *tok_args, norm_g, ml_g, dn_g, w_branch, w_out)


SEG = TM // V7X_SUBLANES


def _time_perm(inverse):
    ri = lax.broadcasted_iota(jnp.int32, (TM, TM), 0)
    ci = lax.broadcasted_iota(jnp.int32, (TM, TM), 1)
    r, t = (ci, ri) if inverse else (ri, ci)
    return jnp.where(t == (r % V7X_SUBLANES) * SEG + r // V7X_SUBLANES, 1.0, 0.0).astype(BF16)


def _conv4_perm(x, before1, before2, after1, w):
    sub = lax.broadcasted_iota(jnp.int32, (V7X_SUBLANES, 1), 0)
    vrow = lambda i: x[V7X_SUBLANES * i:V7X_SUBLANES * (i + 1)]
    m1_edge = jnp.where(sub == 0, before1, pltpu.roll(vrow(SEG - 1), 1, axis=0))
    m2_edge = jnp.where(sub == 0, before2, pltpu.roll(vrow(SEG - 2), 1, axis=0))
    p1_edge = jnp.where(sub == V7X_SUBLANES - 1, after1, pltpu.roll(vrow(0), V7X_SUBLANES - 1, axis=0))
    x_m1 = jnp.concatenate([m1_edge, x[:-V7X_SUBLANES]], axis=0)
    x_m2 = jnp.concatenate([m2_edge, m1_edge, x[:-2 * V7X_SUBLANES]], axis=0)
    x_p1 = jnp.concatenate([x[V7X_SUBLANES:], p1_edge], axis=0)
    taps = {-2: x_m2, -1: x_m1, 0: x, 1: x_p1}
    acc = None
    for j in range(CONV_W):
        term = taps[j - CONV_LEFT] * w[j:j + 1]
        acc = term if acc is None else acc + term
    return acc


def _tile_scan(a, b, h0, rev):
    vrow = lambda x, i: x[V7X_SUBLANES * i:V7X_SUBLANES * (i + 1)]
    h = jnp.zeros_like(vrow(a, 0))
    p = jnp.ones_like(h)
    hs, ps = [None] * SEG, [None] * SEG
    for i in (range(SEG - 1, -1, -1) if rev else range(SEG)):
        ai = vrow(a, i)
        h = ai * h + vrow(b, i)
        p = ai * p
        hs[i], ps[i] = h, p
    carry = h0
    enter = [None] * V7X_SUBLANES
    for s in (range(V7X_SUBLANES - 1, -1, -1) if rev else range(V7X_SUBLANES)):
        enter[s] = carry
        carry = p[s:s + 1] * carry + h[s:s + 1]
    enter = jnp.concatenate(enter, axis=0)
    return jnp.concatenate([hi + pi * enter for hi, pi in zip(hs, ps)], axis=0), carry


def _lru_kernel(x_ref, cw_ref, cb_ref, w_ref, bias_ref, lam_ref, of_ref, ob_ref, *, n_lat, n_ctx):
    t, c = TM, x_ref.shape[-1]
    n_tiles = n_lat + n_ctx
    halo = 2 * V7X_SUBLANES

    def load_conv(tile):
        r0 = pl.multiple_of(tile * t, t)
        first = jnp.logical_or(tile == 0, tile == n_lat)
        last = jnp.logical_or(tile == n_lat - 1, tile == n_tiles - 1)
        x = x_ref[pl.ds(r0, t), :].astype(F32)
        p0 = pl.multiple_of(jnp.maximum(r0 - halo, 0), halo)
        n0 = pl.multiple_of(jnp.minimum(r0 + t, n_tiles * t - halo), halo)
        prev = jnp.where(first, 0.0, x_ref[pl.ds(p0, halo), :].astype(F32))
        nxt = jnp.where(last, 0.0, x_ref[pl.ds(n0, halo), :].astype(F32))
        before1, before2 = prev[halo - 1:halo], prev[V7X_SUBLANES - 1:V7X_SUBLANES]
        return r0, _conv4_perm(x, before1, before2, nxt[0:1], cw_ref[...]) + cb_ref[...]

    def direction(d, tile, h0, o_ref):
        r0, xc = load_conv(tile)
        z = _dot(xc.astype(BF16), w_ref[d]) + bias_ref[d]
        r, i = _sigmoid(z[:, :c]), _sigmoid(z[:, c:])
        la = (-LRU_C * _softplus(-lam_ref[d])) * r
        a = jnp.exp(la)
        bx = jnp.sqrt(jnp.tanh(-la) * (1.0 + a * a)) * (i * xc)
        h, carry = _tile_scan(a, bx, h0, d == 1)
        o_ref[pl.ds(r0, t), :] = h.astype(o_ref.dtype)
        return carry

    def step(s, carry):
        hf, hb = carry
        hf = direction(0, jnp.where(s < n_ctx, n_lat + s, s - n_ctx), hf, of_ref)
        hb = direction(1, n_tiles - 1 - s, hb, ob_ref)
        return hf, hb

    zero = jnp.zeros((1, c), F32)
    lax.fori_loop(0, n_tiles, step, (zero, zero))


def _lru(lru, conv_w, conv_b, w_gates, b_gates, lam, l, n_lat, n_ctx):
    b, lt, _ = lru.shape
    c = conv_w.shape[-1]
    seq = pl.BlockSpec((None, lt, c), lambda bi: (bi, 0, 0))
    return pl.pallas_call(
        functools.partial(_lru_kernel, n_lat=n_lat, n_ctx=n_ctx),
        grid=(b,),
        in_specs=[seq,
                  _const_spec((None, CONV_W, c), (l, 0, 0)),
                  _const_spec((None, 1, c), (l, 0, 0)),
                  _const_spec((None, 2, c, 2 * c), (l, 0, 0, 0)),
                  _const_spec((None, 2, 1, 2 * c), (l, 0, 0, 0)),
                  _const_spec((None, 2, 1, c), (l, 0, 0, 0))],
        out_specs=[seq, seq],
        out_shape=[jax.ShapeDtypeStruct((b, lt, c), BF16)] * 2,
        compiler_params=_params("parallel"),
        name="lru",
    )(lru, conv_w, conv_b, w_gates, b_gates, lam)


def _mlstm_kernel(qf_ref, kf_ref, vf_ref, qb_ref, kb_ref, vb_ref, grf_ref, grb_ref, br_ref,
                  hf_ref, hb_ref, c_ref, m_ref):
    @pl.when(pl.program_id(1) == 0)
    def _():
        c_ref[...] = jnp.zeros(c_ref.shape, F32)
        m_ref[...] = jnp.zeros(m_ref.shape, F32)

    nrow, tc, nqk = qf_ref.shape
    dk = nqk // ML_HEADS
    dv = vf_ref.shape[-1] // ML_HEADS
    nh = ML_HEADS
    ri = lax.broadcasted_iota(jnp.int32, (tc, tc), 0)
    ci = lax.broadcasted_iota(jnp.int32, (tc, tc), 1)
    causal = (ci <= ri, ci >= ri)
    lane_head = lax.broadcasted_iota(jnp.int32, (1, nqk), 1) // dk
    ones = jnp.ones((tc, dv), BF16)
    dirs = ((qf_ref, kf_ref, vf_ref, grf_ref), (qb_ref, kb_ref, vb_ref, grb_ref))
    seqs = [(r, d) for r in range(nrow) for d in range(2)]

    rows, v_rows, ws_rows, decs = [], [], [], []
    for r, d in seqs:
        rev = d == 1
        gr = dirs[d][3][r].T + br_ref[...]
        i_r = gr[nh * d:nh * (d + 1)]
        b_r = _cumsum_groups(-_softplus(-gr[nh * (2 + d):nh * (3 + d)]), 1, tc, rev)
        m_prev = m_ref[r, nh * d:nh * (d + 1), 0:1]
        m_t = b_r + jnp.maximum(m_prev, _cummax_groups(i_r - b_r, 1, tc, rev))
        b_end = b_r[:, 0:1] if rev else b_r[:, tc - 1:tc]
        lws = b_end - b_r + i_r
        m_new = jnp.maximum(b_end + m_prev, jnp.max(lws, axis=1, keepdims=True))
        rows += [b_r - m_t, jnp.exp(b_r + m_prev - m_t), jnp.exp(-m_t)]
        v_rows.append(b_r - i_r)
        ws_rows.append(jnp.exp(lws - m_new))
        decs.append(jnp.exp(b_end + m_prev - m_new))
        m_ref[r, nh * d:nh * (d + 1), :] = jnp.broadcast_to(m_new, (nh, m_ref.shape[-1]))
    cols = jnp.concatenate(rows, axis=0).T

    chains = [(s, h) for s in range(len(seqs)) for h in range(nh)]
    q_all = [dirs[d][0][r] * (dk ** -0.5) for r, d in seqs]
    k_all = [dirs[d][1][r] for r, d in seqs]
    v_all = [dirs[d][2][r] for r, d in seqs]
    kt_all = [k.astype(F32).T for k in k_all]
    c_all = [c_ref[r, d] for r, d in seqs]
    cb_all = [c.astype(BF16) for c in c_all]
    col = lambda s, j, h: cols[:, (3 * s + j) * nh + h:(3 * s + j) * nh + h + 1]
    qh = [jnp.where(lane_head == h, q_all[s], jnp.zeros_like(q_all[s])) for s, h in chains]
    vp = [jnp.concatenate([v_all[s][:, h * dv:(h + 1) * dv], ones], axis=1) for s, h in chains]
    s_raw = [_dot_nt(q, k_all[s]) for q, (s, h) in zip(qh, chains)]
    p = [(sr * jnp.exp(jnp.where(causal[seqs[s][1]], col(s, 0, h) - v_rows[s][h:h + 1], NEG))).astype(BF16)
         for sr, (s, h) in zip(s_raw, chains)]
    num = [_dot(pc, vc) + col(s, 1, h) * _dot(q, cb_all[s]) for pc, vc, q, (s, h) in zip(p, vp, qh, chains)]
    outs = [nm[:, :dv] / jnp.maximum(jnp.abs(nm[:, dv:]), col(s, 2, h)) for nm, (s, h) in zip(num, chains)]
    for vc, (s, h) in zip(vp, chains):
        r, d = seqs[s]
        kw = (kt_all[s][h * dk:(h + 1) * dk] * ws_rows[s][h:h + 1]).astype(BF16)
        c_ref[r, d, h * dk:(h + 1) * dk, :] = decs[s][h:h + 1] * c_all[s][h * dk:(h + 1) * dk] + _dot(kw, vc)
    for s, (r, d) in enumerate(seqs):
        o_ref = (hf_ref, hb_ref)[d]
        o_ref[r] = jnp.concatenate(outs[s * nh:(s + 1) * nh], axis=1).astype(o_ref.dtype)


def _mlstm(ml, gates, bias_r, l, n_lat, n_ctx):
    b, lt, _ = ml.shape
    ng = gates.shape[-1]
    n_tiles = n_lat + n_ctx
    dqk = ml.shape[-1] // 6
    fwd = lambda s: jnp.where(s < n_ctx, n_lat + s, s - n_ctx)
    bwd = lambda s: n_tiles - 1 - s
    rpb = ML_ROWS if b % ML_ROWS == 0 else 1

    def specs(tile):
        return [pl.BlockSpec((rpb, TM, dqk), lambda bi, s: (bi, tile(s), 0)),
                pl.BlockSpec((rpb, TM, dqk), lambda bi, s: (bi, tile(s), 1)),
                pl.BlockSpec((rpb, TM, 2 * dqk), lambda bi, s: (bi, tile(s), 1))]

    row = lambda tile: pl.BlockSpec((rpb, TM, ng), lambda bi, s: (bi, tile(s), 0))
    out = lambda tile: pl.BlockSpec((rpb, TM, 2 * dqk), lambda bi, s: (bi, tile(s), 0))
    return pl.pallas_call(
        _mlstm_kernel,
        grid=(b // rpb, n_tiles),
        in_specs=specs(fwd) + specs(bwd) + [row(fwd), row(bwd), _const_spec((None, ng, 1), (l, 0, 0))],
        out_specs=[out(fwd), out(bwd)],
        out_shape=[jax.ShapeDtypeStruct((b, lt, 2 * dqk), BF16)] * 2,
        scratch_shapes=[pltpu.VMEM((rpb, 2, dqk, 2 * (2 * dqk // ML_HEADS)), F32),
                        pltpu.VMEM((rpb, 2 * ML_HEADS, V7X_LANES), F32)],
        compiler_params=_params("parallel", "arbitrary"),
        name="mlstm",
    )(ml, ml, ml, ml, ml, ml, gates, gates, bias_r)


INV_BASE = 8


def _inv_masks(ri, ci):
    blk = lambda s: (ri // s) == (ci // s)
    as_bf = lambda m: jnp.where(m, 1.0, 0.0).astype(BF16)
    masks, s = [as_bf(blk(INV_BASE))], INV_BASE
    while s < CHUNK:
        masks.append(as_bf(jnp.logical_and(blk(2 * s), jnp.logical_not(blk(s)))))
        s *= 2
    return masks


def _tri_inv_many(a_list, eye, same, masks):
    nblk = a_list[0].shape[0] // CHUNK
    compact = lambda m: functools.reduce(lambda u, v: u + v, [m[i * CHUNK:(i + 1) * CHUNK] for i in range(nblk)])
    spread = lambda c: jnp.concatenate([c] * nblk, axis=0) * same
    eye_c = compact(eye)
    ps = [a * masks[0] for a in a_list]
    pcs = [compact(p) for p in ps]
    xcs = [eye_c - pc for pc in pcs]
    for _ in range(INV_BASE.bit_length() - 2):
        pcs = [_dot(pc, p).astype(BF16) for pc, p in zip(pcs, ps)]
        ps = [spread(pc) for pc in pcs]
        xcs = [_dot(xc, eye + p).astype(BF16) for xc, p in zip(xcs, ps)]
    for m in masks[1:]:
        ys = [_dot(xc, a * m).astype(BF16) for xc, a in zip(xcs, a_list)]
        xcs = [xc - _dot(y, spread(xc)).astype(BF16) for y, xc in zip(ys, xcs)]
    return [spread(xc) for xc in xcs]


def _dn_prep_kernel(xl_ref, xp_ref, xn_ref, xc_ref, grow_ref, cw_ref, arow_ref,
                    u_ref, w_ref, qe_ref, kd_ref, qk_ref, eg_ref, *, n_lat_groups):
    g = pl.program_id(1)
    is_ctx = g >= n_lat_groups
    width = xl_ref.shape[-1]
    dh = width // (3 * DN_HEADS)
    n = DN_HEADS * CHUNK
    halo = 2 * V7X_SUBLANES
    x = jnp.where(is_ctx, xc_ref[...], xl_ref[...]).reshape(DN_GROUP * CHUNK, width)
    no_prev = jnp.logical_or(is_ctx, g == 0)
    no_next = jnp.logical_or(is_ctx, g == n_lat_groups - 1)
    prev8 = jnp.where(no_prev, 0.0, xp_ref[CHUNK - halo:, :].astype(F32)[V7X_SUBLANES:])
    nxt8 = jnp.where(no_next, 0.0, xn_ref[:halo, :].astype(F32)[:V7X_SUBLANES])
    xc = _silu(_conv4(x, prev8, nxt8, cw_ref[...]))

    ri = lax.broadcasted_iota(jnp.int32, (n, n), 0)
    ci = lax.broadcasted_iota(jnp.int32, (n, n), 1)
    same = (ri // CHUNK) == (ci // CHUNK)
    eye = jnp.where(ri == ci, 1.0, 0.0).astype(BF16)
    incl =(jnp.logical_and(same, ci <= ri), jnp.logical_and(same, ci >= ri))
    strict = (jnp.logical_and(same, ci < ri), jnp.logical_and(same, ci > ri))
    masks = _inv_masks(ri, ci)

    qn, kn, v, kk, qk = [], [], [], [], []
    for i in range(DN_GROUP):
        xi = xc[i * CHUNK:(i + 1) * CHUNK]
        q, k, vi = (_stack_heads(xi[:, j * DN_HEADS * dh:(j + 1) * DN_HEADS * dh], DN_HEADS) for j in range(3))
        qn.append(q * lax.rsqrt(jnp.sum(q * q, axis=-1, keepdims=True) + EPS) * (dh ** -0.5))
        kn.append(k * lax.rsqrt(jnp.sum(k * k, axis=-1, keepdims=True) + EPS))
        v.append(vi)
        kb = kn[i].astype(BF16)
        kk.append(_dot_nt(kb, kb))
        qk.append(_dot_nt(qn[i].astype(BF16), kb))

    systems = [(i, d) for i in range(DN_GROUP) for d in range(2)]
    rows, cs_rows = [], []
    for i, d in systems:
        raw = grow_ref[i]
        beta = _sigmoid(raw[d:d + 1])
        g_r = -jnp.exp(arow_ref[d, 0:1, :]) * _softplus(raw[2 + d:3 + d] + arow_ref[d, 1:2, :])
        cs = _cumsum_groups(g_r, 1, CHUNK, d == 1)
        tot = cs + _cumsum_groups(g_r, 1, CHUNK, d == 0) - g_r
        eg = jnp.exp(cs)
        rows += [cs, beta, eg, jnp.exp(tot - cs), beta * eg]
        cs_rows.append(cs)
        eg_ref[d, i] = jnp.exp(tot)
    n_col = len(rows) // len(systems)
    cols = jnp.concatenate(rows, axis=0).T

    a_list, rhs = [], []
    for s, (i, d) in enumerate(systems):
        cs_c, beta_c, eg_c, ekd_c, beg_c = (cols[:, n_col * s + j:n_col * s + j + 1] for j in range(n_col))
        gam = jnp.exp(jnp.where(incl[d], cs_c - cs_rows[s], NEG))
        a_list.append((jnp.where(strict[d], kk[i], 0.0) * gam * beta_c).astype(BF16))
        rhs.append(jnp.concatenate([beta_c * v[i], beg_c * kn[i]], axis=1).astype(BF16))
        qk_ref[d, i] = (qk[i] * gam).astype(qk_ref.dtype)
        qe_ref[d, i] = (qn[i] * eg_c).astype(qe_ref.dtype)
        kd_ref[d, i] = (kn[i] * ekd_c).astype(kd_ref.dtype)
    same_bf = jnp.where(same, 1.0, 0.0).astype(BF16)
    for (i, d), tinv, r in zip(systems, _tri_inv_many(a_list, eye, same_bf, masks), rhs):
        sol = _dot(tinv, r)
        u_ref[d, i] = sol[:, :dh].astype(u_ref.dtype)
        w_ref[d, i] = sol[:, dh:].astype(w_ref.dtype)


def _dn_prep(x_cols, x_rows, grow, conv_w, arow, l):
    b, nc, _, width = x_rows.shape
    n_cols = x_cols.shape[1]
    n_groups = nc // DN_GROUP
    n_lat_groups = n_cols // DN_GROUP
    assert n_groups == n_lat_groups + 1
    n = DN_HEADS * CHUNK
    dh = width // (3 * DN_HEADS)
    lat_g = lambda g: jnp.minimum(g, n_lat_groups - 1)
    out = lambda w, dt: (jax.ShapeDtypeStruct((b, 2, nc, n, w), dt),
                         pl.BlockSpec((None, 2, DN_GROUP, n, w), lambda bi, g: (bi, 0, g, 0, 0)))
    outs = [out(dh, BF16)] * 4 + [out(n, BF16)]
    outs.append((jax.ShapeDtypeStruct((b, 2, nc, 1, n), F32),
                 pl.BlockSpec((None, 2, DN_GROUP, 1, n), lambda bi, g: (bi, 0, g, 0, 0))))
    chunk = lambda idx: pl.BlockSpec((None, None, CHUNK, width), lambda bi, g: (bi, idx(g), 0, 0))
    return pl.pallas_call(
        functools.partial(_dn_prep_kernel, n_lat_groups=n_lat_groups),
        grid=(b, n_groups),
        in_specs=[pl.BlockSpec((None, DN_GROUP, CHUNK, width), lambda bi, g: (bi, lat_g(g), 0, 0)),
                  chunk(lambda g: jnp.maximum(lat_g(g) * DN_GROUP - 1, 0)),
                  chunk(lambda g: jnp.minimum(lat_g(g) * DN_GROUP + DN_GROUP, n_cols - 1)),
                  pl.BlockSpec((None, DN_GROUP, CHUNK, width), lambda bi, g: (bi, n_groups - 1, 0, 0)),
                  pl.BlockSpec((None, DN_GROUP, 4, n), lambda bi, g: (bi, g, 0, 0)),
                  _const_spec((None, CONV_W, width), (l, 0, 0)),
                  _const_spec((None, 2, 2, n), (l, 0, 0, 0))],
        out_specs=[o[1] for o in outs],
        out_shape=[o[0] for o in outs],
        compiler_params=_params("parallel", "parallel"),
        name="dn_prep",
    )(x_cols, x_cols, x_cols, x_rows, grow, conv_w, arow)


def _dn_scan_kernel(*refs):
    ins, (of_ref, ob_ref, s_ref) = refs[:12], refs[12:]

    @pl.when(pl.program_id(1) == 0)
    def _():
        s_ref[...] = jnp.zeros(s_ref.shape, F32)

    rows = [slice(h * CHUNK, (h + 1) * CHUNK) for h in range(DN_HEADS)]
    seqs = [(r, d) for r in range(of_ref.shape[0]) for d in range(2)]
    cps = of_ref.shape[1]
    for step in range(cps):
        pos = lambda d: cps - 1 - step if d == 1 else step
        get = lambda j, r, d: ins[6 * d + j][r, pos(d)]
        sb = [[s_ref[r, d, h].astype(BF16) for h in range(DN_HEADS)] for r, d in seqs]
        ws = [jnp.concatenate([_dot(get(1, r, d)[rows[h]], sb[i][h]) for h in range(DN_HEADS)], axis=0)
              for i, (r, d) in enumerate(seqs)]
        qs = [jnp.concatenate([_dot(get(2, r, d)[rows[h]], sb[i][h]) for h in range(DN_HEADS)], axis=0)
              for i, (r, d) in enumerate(seqs)]
        vnew = [(get(0, r, d).astype(F32) - w).astype(BF16) for w, (r, d) in zip(ws, seqs)]
        o = [q + _dot(get(4, r, d), vn) for q, vn, (r, d) in zip(qs, vnew, seqs)]
        for vn, (r, d) in zip(vnew, seqs):
            kd, eg = get(3, r, d), get(5, r, d)
            for h in range(DN_HEADS):
                s_ref[r, d, h] = (eg[:, h * CHUNK:h * CHUNK + 1] * s_ref[r, d, h]
                                  + _dot_tn(kd[rows[h]], vn[rows[h]]))
        for oi, (r, d) in zip(o, seqs):
            o_ref = (of_ref, ob_ref)[d]
            o_ref[r, pos(d)] = _unstack_heads(oi, DN_HEADS).astype(o_ref.dtype)


def _dn_scan(prep, n_lat_chunks):
    b, _, nc, _, dh = prep[0].shape
    ow = DN_HEADS * dh
    cps = 4
    nblk, n_lat_blk = nc // cps, n_lat_chunks // cps
    n_ctx_blk = nblk - n_lat_blk
    fwd = lambda s: jnp.where(s < n_ctx_blk, n_lat_blk + s, s - n_ctx_blk)
    bwd = lambda s: nblk - 1 - s

    rpb = DN_SCAN_ROWS if b % DN_SCAN_ROWS == 0 else 1

    def spec(a, d, chunk):
        return pl.BlockSpec((rpb, None, cps) + a.shape[3:], lambda bi, s: (bi, d, chunk(s), 0, 0))

    out = lambda chunk: pl.BlockSpec((rpb, cps, CHUNK, ow), lambda bi, s: (bi, chunk(s), 0, 0))
    return pl.pallas_call(
        _dn_scan_kernel,
        grid=(b // rpb, nblk),
        in_specs=[spec(a, 0, fwd) for a in prep] + [spec(a, 1, bwd) for a in prep],
        out_specs=[out(fwd), out(bwd)],
        out_shape=[jax.ShapeDtypeStruct((b, nc, CHUNK, ow), BF16)] * 2,
        scratch_shapes=[pltpu.VMEM((rpb, 2, DN_HEADS, dh, dh), F32)],
        compiler_params=_params("parallel", "arbitrary"),
        name="dn_scan",
    )(*prep, *prep)


def _block_diag(w):
    n, i, j = w.shape
    return jnp.einsum('nij,nm->nimj', w, jnp.eye(n, dtype=w.dtype)).reshape(n * i, n * j)


def _dn_gate_rows(raw, n_lat):
    b = raw.shape[0]
    lat = raw[:, :n_lat].reshape(b, n_lat // CHUNK, CHUNK, 4, DN_HEADS)
    ctx = raw[:, n_lat:].reshape(b, -1, CHUNK, 4, DN_HEADS)
    row = jnp.concatenate([lat.transpose(0, 2, 3, 4, 1), ctx.transpose(0, 1, 3, 4, 2)], axis=1)
    return row.reshape(b, row.shape[1], 4, DN_HEADS * CHUNK)


def kernel(x, c, ctx, c_ctx, w_mod, b_mod, norm_g, ffn_w_gu, ffn_w_down, w_in, ml_gate_b, ml_norm_g,
           lru_conv_w, lru_conv_b, lru_w_a, lru_b_a, lru_w_x, lru_b_x, lru_lambda, dn_conv_w,
           dn_a_log, dn_dt_bias, dn_norm_g, w_branch, w_out):
    b, n_lat, d = x.shape
    n_ctx = ctx.shape[1]
    depth = w_mod.shape[0]
    bw = w_branch.shape[2]
    assert n_lat == CHUNK * CHUNK and n_lat % TM == 0 and n_ctx % TM == 0 and n_ctx == CHUNK * DN_GROUP
    n_lat_tiles, n_ctx_tiles = n_lat // TM, n_ctx // TM
    n_tiles = n_lat_tiles + n_ctx_tiles

    ctx_row = b
    n_rows = -(-(b + 1) // V7X_SUBLANES) * V7X_SUBLANES
    cc = jnp.zeros((n_rows, d), F32).at[:b].set(c).at[b].set(c_ctx)
    mod = _mod_table(cc, w_mod, b_mod).reshape(depth, n_rows, N_MOD, d)

    dqk = bw // 2
    edges = [0]
    for wdt in (dqk, dqk, bw, bw, N_GATES, bw, bw, bw, bw, bw, bw, N_GATES, N_BRANCH * d):
        edges.append(edges[-1] + wdt)
    piece = lambda i, j: w_in[:, :, edges[i]:edges[j]]
    layout = (((2 * bw, None), (bw, _sigmoid)),
              ((bw, "perm"), (bw, _gelu_tanh)),
              ((3 * bw, None),),
              ((bw, _silu),),
              ((N_BRANCH * d, _sigmoid),))
    w_main = jnp.concatenate([piece(0, 4), piece(5, 7), piece(7, 11), piece(12, 13)], axis=-1).astype(BF16)
    w_gate = jnp.concatenate([piece(4, 5), piece(11, 12)], axis=-1).astype(BF16)
    wgu = ffn_w_gu.astype(BF16)
    wdn = ffn_w_down.astype(BF16)
    wbr = w_branch.astype(BF16)
    wout = w_out.astype(BF16)
    ml_bias = jnp.concatenate([ml_gate_b.reshape(depth, N_GATES), jnp.zeros((depth, N_GATES), F32)], axis=-1)
    ml_bias_r = ml_bias[:, :, None]
    ml_g = ml_norm_g[:, None, :]
    dn_g = jnp.tile(dn_norm_g, (1, DN_HEADS))[:, None, :]
    lru_w = jnp.stack([jnp.concatenate([jax.vmap(_block_diag)(lru_w_a[:, dd]), jax.vmap(_block_diag)(lru_w_x[:, dd])],
                                       axis=-1) for dd in range(2)], axis=1).astype(BF16)
    lru_b = jnp.concatenate([lru_b_a, lru_b_x], axis=-1)[:, :, None, :]
    lru_lam = lru_lambda[:, :, None, :]
    lru_cb = lru_conv_b[:, None, :]
    dn_arow = jnp.repeat(jnp.stack([dn_a_log, dn_dt_bias], axis=2), CHUNK, axis=3)

    rows3 = lambda a: a.reshape(b, n_lat + n_ctx, a.shape[-1])
    for l in range(depth):
        full = _TilePlan(b, n_tiles, n_lat_tiles, n_tiles, ctx_row, l)
        if l == 0:
            xz = _ffn(full, _flat(x), mod, norm_g, wgu, wdn, 0, ctx=_flat(ctx))
        else:
            xz = _ffn(full, xz, mod, norm_g, wgu, wdn, 0)
        ml, lru, dnqkv, dnz, mg, gates = _inproj(full, xz, mod, norm_g, w_main, w_gate, layout)
        gates = rows3(gates)
        ml_hf, ml_hb = _mlstm(rows3(ml), gates, ml_bias_r, l, n_lat_tiles, n_ctx_tiles)
        lr_hf, lr_hb = _lru(rows3(lru), lru_conv_w, lru_cb, lru_w, lru_b, lru_lam, l, n_lat_tiles, n_ctx_tiles)
        n_cols = n_lat // CHUNK
        dq = rows3(dnqkv)
        x_rows = dq.reshape(b, -1, CHUNK, dq.shape[-1])
        x_cols = x_rows.swapaxes(1, 2)
        prep = _dn_prep(x_cols, x_rows, _dn_gate_rows(gates[:, :, N_GATES:], n_lat), dn_conv_w, dn_arow, l)
        dn_all = _dn_scan(prep, n_cols)
        dn_lat = [_flat(o[:, :n_cols].swapaxes(1, 2)) for o in dn_all]
        xz = _merge(full, xz, mod, norm_g, _flat(ml_hf), _flat(ml_hb), ml, _flat(lr_hf), _flat(lr_hb), lru,
                    dn_lat, [_flat(o) for o in dn_all], dnz, mg, ml_g, dn_g, wbr, wout)
        visit = n_lat_tiles if l == depth - 1 else n_tiles
        xz = _ffn(_TilePlan(b, n_tiles, n_lat_tiles, visit, ctx_row, l), xz, mod, norm_g, wgu, wdn, 2)
    return xz.reshape(b, n_lat, d)
```
